```python
import math
import jax
import jax.numpy as jnp
from jax import lax
import numpy as np

D_MODEL = 4096
BATCH = 1
SEQ = 16384
DEPTH = 4

GRID_W = 64
CTX_LEN = 256
N_MIXERS = 3
N_DN = (DEPTH + 2) // N_MIXERS
N_HG = (DEPTH + 1) // N_MIXERS
N_RW = DEPTH // N_MIXERS

EPS = 1e-6
ADA_RANK = D_MODEL // 8
D_FF = 5 * D_MODEL // 2

DN_DK = 128
DN_DV = 128
DN_HEADS = D_MODEL // DN_DK
DN_CONV = 3
DN_CHUNK = 64

HG_EXPAND = 128
HG_HEADS = D_MODEL // HG_EXPAND
HG_DV = D_MODEL // HG_HEADS
HG_CHUNK = 32

RW_HEAD = 64
RW_HEADS = D_MODEL // RW_HEAD
RW_DECAY_LORA = max(32, int(round(1.8 * D_MODEL ** 0.5 / 32)) * 32)
RW_AAA_LORA = max(32, int(round(1.8 * D_MODEL ** 0.5 / 32)) * 32)
RW_GATE_LORA = max(32, int(round(0.6 * D_MODEL ** 0.8 / 32)) * 32)
RW_LN_EPS = 64e-5
RW_CHUNK = 32

F32 = jnp.float32

kernel_name = 'hybrid_deltanet_hgrn2_rwkv7_flow_trunk'


def rmsnorm(x, g):
    xf = x.astype(F32)
    y = xf * lax.rsqrt(jnp.mean(xf * xf, -1, keepdims=True) + EPS) * g.astype(F32)
    return y.astype(x.dtype)


def head_rmsnorm(x, g, heads):
    B, T, D = x.shape
    xf = x.astype(F32).reshape(B, T, heads, D // heads)
    y = xf * lax.rsqrt(jnp.mean(xf * xf, -1, keepdims=True) + EPS) * g.astype(F32)
    return y.reshape(B, T, D).astype(x.dtype)


def group_norm(x, w, b, groups):
    B, T, D = x.shape
    xf = x.astype(F32).reshape(B, T, groups, D // groups)
    xc = xf - jnp.mean(xf, -1, keepdims=True)
    y = (xc * lax.rsqrt(jnp.mean(xc * xc, -1, keepdims=True) + RW_LN_EPS)).reshape(B, T, D)
    return (y * w.astype(F32) + b.astype(F32)).astype(x.dtype)


def l2norm(x):
    xf = x.astype(F32)
    return (xf * lax.rsqrt(jnp.sum(xf * xf, -1, keepdims=True) + EPS)).astype(x.dtype)


def split_heads(t, heads):
    B, T, D = t.shape
    return t.reshape(B, T, heads, D // heads).transpose(0, 2, 1, 3)


def merge_heads(t):
    B, H, T, d = t.shape
    return t.transpose(0, 2, 1, 3).reshape(B, T, H * d)


def dwconv1d(x, w):
    return lax.conv_general_dilated(x, w[:, None, :].astype(x.dtype), (1,), 'SAME',
                                    dimension_numbers=('NWC', 'WIO', 'NWC'),
                                    feature_group_count=x.shape[-1])


def dwconv2d(x, w, rows, width):
    B, T, C = x.shape
    y = lax.conv_general_dilated(x.reshape(B, rows, width, C), w[:, :, None, :].astype(x.dtype),
                                 (1, 1), 'SAME', dimension_numbers=('NHWC', 'HWIO', 'NHWC'),
                                 feature_group_count=C)
    return y.reshape(B, T, C)


def qshift(x, rows, width):
    B, T, C = x.shape
    g = x.reshape(B, rows, width, C)
    q = C // 4
    left = jnp.pad(g[:, :, :-1, :q], ((0, 0), (0, 0), (1, 0), (0, 0)))
    right = jnp.pad(g[:, :, 1:, q:2 * q], ((0, 0), (0, 0), (0, 1), (0, 0)))
    up = jnp.pad(g[:, :-1, :, 2 * q:3 * q], ((0, 0), (1, 0), (0, 0), (0, 0)))
    down = jnp.pad(g[:, 1:, :, 3 * q:], ((0, 0), (0, 1), (0, 0), (0, 0)))
    return jnp.concatenate([left, right, up, down], -1).reshape(B, T, C)


def _chunks(t, size):
    B, H, T = t.shape[:3]
    return t.astype(F32).reshape(B, H, T // size, size, *t.shape[3:])


def _masks(size):
    idx = jnp.arange(size)
    return idx[:, None] >= idx[None, :], idx[:, None] > idx[None, :]


def _scan_chunks(step, s0, xs):
    s, o = lax.scan(step, s0, tuple(jnp.moveaxis(t, 2, 0) for t in xs))
    o = jnp.moveaxis(o, 0, 2)
    B, H, N, C, d = o.shape
    return o.reshape(B, H, N * C, d), s


def _unit_lower_solve(a_strict, rhs):
    eye = jnp.eye(a_strict.shape[-1], dtype=a_strict.dtype)
    return lax.linalg.triangular_solve(eye + a_strict, rhs, left_side=True, lower=True,
                                       unit_diagonal=True)


def _pair(x, y):
    return jnp.einsum('bhnid,bhnjd->bhnij', x, y)


def gated_delta_chunk(q, k, v, beta, g, s0):
    out_dtype = v.dtype
    q, k, v, beta, g = (_chunks(t, DN_CHUNK) for t in (q, k, v, beta, g))
    incl, strict = _masks(DN_CHUNK)
    gc = jnp.cumsum(g, axis=-1)
    decay = jnp.exp(jnp.where(incl, gc[..., :, None] - gc[..., None, :], -jnp.inf))
    kb = k * beta[..., None]
    a_kk = jnp.where(strict, _pair(kb, k) * decay, 0.0)
    dk = k.shape[-1]
    sol = _unit_lower_solve(a_kk, jnp.concatenate([kb * jnp.exp(gc)[..., None],
                                                    v * beta[..., None]], -1))
    w, u = sol[..., :dk], sol[..., dk:]
    attn = _pair(q, k) * decay
    qg = q * jnp.exp(gc)[..., None]
    kd = k * jnp.exp(gc[..., -1:] - gc)[..., None]
    gl = jnp.exp(gc[..., -1])[..., None, None]

    def step(s, xs):
        w_n, u_n, attn_n, qg_n, kd_n, gl_n = xs
        v_new = u_n - w_n @ s
        o = qg_n @ s + attn_n @ v_new
        s = s * gl_n + jnp.swapaxes(kd_n, -1, -2) @ v_new
        return s, o

    o, s = _scan_chunks(step, s0, (w, u, attn, qg, kd, gl))
    return o.astype(out_dtype), s


def gla_chunk(q, k, v, logf, s0):
    out_dtype = v.dtype
    C = HG_CHUNK
    q, k, v, logf = (_chunks(t, C) for t in (q, k, v, logf))
    incl, _ = _masks(C)
    b = jnp.cumsum(logf, axis=-2)
    m = b[..., C // 2:C // 2 + 1, :]
    a_qk = jnp.where(incl, _pair(q * jnp.exp(b - m), k * jnp.exp(m - b)), 0.0)
    intra = a_qk @ v
    qs = q * jnp.exp(b)
    kd = k * jnp.exp(b[..., -1:, :] - b)
    gl = jnp.exp(b[..., -1, :])[..., None]

    def step(s, xs):
        intra_n, qs_n, kd_n, v_n, gl_n = xs
        o = qs_n @ s + intra_n
        s = gl_n * s + jnp.swapaxes(kd_n, -1, -2) @ v_n
        return s, o

    o, s = _scan_chunks(step, s0, (intra, qs, kd, v, gl))
    return o.astype(out_dtype), s


def dplr_chunk(r, k, v, a, b, logw, s0):
    out_dtype = v.dtype
    C = RW_CHUNK
    r, k, v, a, b, logw = (_chunks(t, C) for t in (r, k, v, a, b, logw))
    incl, strict = _masks(C)
    c = jnp.cumsum(logw, axis=-2)
    ce = c - logw
    m = c[..., C // 2:C // 2 + 1, :]
    am = a * jnp.exp(ce - m)
    bm = b * jnp.exp(m - c)
    km = k * jnp.exp(m - c)
    rm = r * jnp.exp(c - m)
    a_ab = jnp.where(strict, _pair(am, bm), 0.0)
    a_ak = jnp.where(strict, _pair(am, km), 0.0)
    a_rb = jnp.where(incl, _pair(rm, bm), 0.0)
    a_rk = jnp.where(incl, _pair(rm, km), 0.0)
    dk = k.shape[-1]
    sol = _unit_lower_solve(-a_ab, jnp.concatenate([a * jnp.exp(ce), a_ak @ v], -1))
    w_s, u_v = sol[..., :dk], sol[..., dk:]
    rg = r * jnp.exp(c)
    o_v = a_rk @ v
    dec = jnp.exp(c[..., -1:, :] - c)
    bd, kd = b * dec, k * dec
    gl = jnp.exp(c[..., -1, :])[..., None]

    def step(s, xs):
        w_n, uv_n, rg_n, arb_n, ov_n, bd_n, kd_n, v_n, gl_n = xs
        u = w_n @ s + uv_n
        o = rg_n @ s + arb_n @ u + ov_n
        s = gl_n * s + jnp.swapaxes(bd_n, -1, -2) @ u + jnp.swapaxes(kd_n, -1, -2) @ v_n
        return s, o

    o, s = _scan_chunks(step, s0, (w_s, u_v, rg, a_rb, o_v, bd, kd, v, gl))
    return o.astype(out_dtype), s


def bidirectional(chunk_fn, ctx_fwd, lat_fwd, ctx_bwd, lat_bwd, s0):
    def flip(ts):
        return [jnp.flip(t, axis=2) for t in ts]
    oc_f, s_f = chunk_fn(*ctx_fwd, s0)
    ol_f, _ = chunk_fn(*lat_fwd, s_f)
    oc_b, s_b = chunk_fn(*flip(ctx_bwd), s0)
    ol_b, _ = chunk_fn(*flip(lat_bwd), s_b)
    return oc_f + jnp.flip(oc_b, axis=2), ol_f + jnp.flip(ol_b, axis=2)


def gated_deltanet_mixer(hc, hl, w_in, conv_w, a_log, dt_bias, norm_g, w_out, ctx_out):
    D, H = D_MODEL, DN_HEADS

    def prep(h):
        p = h @ w_in
        q, k, v = jnp.split(jax.nn.silu(dwconv1d(p[..., :3 * D], conv_w)), 3, axis=-1)
        z = p[..., 3 * D:4 * D]
        a_f, a_b, b_f, b_b = jnp.split(p[..., 4 * D:].astype(F32), 4, axis=-1)
        q = l2norm(split_heads(q, H)) * DN_DK ** -0.5
        k = l2norm(split_heads(k, H))
        v = split_heads(v, H)

        def direction(a_raw, b_raw, d):
            beta = jax.nn.sigmoid(b_raw).transpose(0, 2, 1)
            g = -jnp.exp(a_log[d].astype(F32))[:, None] * jax.nn.softplus(
                a_raw + dt_bias[d]).transpose(0, 2, 1)
            return (q, k, v, beta, g)

        return direction(a_f, b_f, 0), direction(a_b, b_b, 1), z

    cf, cb, zc = prep(hc)
    lf, lb, zl = prep(hl)
    s0 = jnp.zeros((hl.shape[0], H, DN_DK, DN_DV), F32)
    oc, ol = bidirectional(gated_delta_chunk, cf, lf, cb, lb, s0)

    def out(o, z):
        return (head_rmsnorm(merge_heads(o), norm_g, H) * jax.nn.silu(z)) @ w_out

    return (out(oc, zc) if ctx_out else None), out(ol, zl)


def hgrn2_mixer(hc, hl, w_in, lower_bound, norm_g, w_out, ctx_out):
    H = HG_HEADS

    def prep(h):
        q, f_f, f_b, i, z = jnp.split(h @ w_in, 5, axis=-1)
        q = split_heads(jax.nn.silu(q), H)
        v = split_heads(i, H)

        def direction(f_raw):
            f = lower_bound + (1.0 - lower_bound) * jax.nn.sigmoid(f_raw.astype(F32))
            return (q, split_heads(1.0 - f, H), v, split_heads(jnp.log(f), H))

        return direction(f_f), direction(f_b), z

    cf, cb, zc = prep(hc)
    lf, lb, zl = prep(hl)
    s0 = jnp.zeros((hl.shape[0], H, HG_EXPAND, HG_DV), F32)
    oc, ol = bidirectional(gla_chunk, cf, lf, cb, lb, s0)
    gain = norm_g.reshape(H, HG_DV)

    def out(o, z):
        return (head_rmsnorm(merge_heads(o), gain, H) * jax.nn.silu(z)) @ w_out

    return (out(oc, zc) if ctx_out else None), out(ol, zl)


def rwkv7_mixer(hc, hl, rows, mu, w_rkv, w0, w1, w2, a0, a1, a2, g1, g2,
                k_k, k_a, r_k, lnx_w, lnx_b, w_out, ctx_out):
    H = RW_HEADS

    def prep(h, grid_rows, grid_w):
        xx = qshift(h, grid_rows, grid_w) - h
        xr, xw, xk, xv, xa, xg = (h + xx * mu[n] for n in range(6))
        r, k, v = xr @ w_rkv[0], xk @ w_rkv[1], xv @ w_rkv[2]
        g = jax.nn.sigmoid(xg @ g1) @ g2
        kk = l2norm(split_heads(k * k_k, H))
        rh, vh = split_heads(r, H), split_heads(v, H)
        dirs, keys = [], []
        for d in range(2):
            logw = -jnp.exp(-jax.nn.softplus(-(w0[d] + jnp.tanh(xw @ w1[d]) @ w2[d]).astype(F32)) - 0.5)
            a = jax.nn.sigmoid((a0[d] + (xa @ a1[d]) @ a2[d]).astype(F32))
            kd = k * (1.0 + (a - 1.0) * k_a)
            dirs.append((rh, split_heads(kd, H), vh, -kk, kk * split_heads(a, H),
                         split_heads(logw, H)))
            keys.append(kd)
        return dirs, (r, keys[0] + keys[1], v, g)

    dc, rest_c = prep(hc, 1, hc.shape[1])
    dl, rest_l = prep(hl, rows, GRID_W)
    s0 = jnp.zeros((hl.shape[0], H, RW_HEAD, RW_HEAD), F32)
    oc, ol = bidirectional(dplr_chunk, dc[0], dl[0], dc[1], dl[1], s0)

    def out(o, r, k_sum, v, g):
        B, T, D = r.shape
        y = group_norm(merge_heads(o), lnx_w, lnx_b, H)
        bonus = jnp.sum((r * k_sum * r_k.reshape(-1)).reshape(B, T, H, RW_HEAD), -1,
                        keepdims=True) * v.reshape(B, T, H, RW_HEAD)
        return ((y + bonus.reshape(B, T, D)) * g) @ w_out

    return (out(oc, *rest_c) if ctx_out else None), out(ol, *rest_l)


def conv_glu(h, rows, width, w_up, conv_w, conv_b, w_down):
    u, v = jnp.split(h @ w_up, 2, axis=-1)
    u = dwconv2d(u, conv_w, rows, width) + conv_b
    return (jax.nn.gelu(u) * v) @ w_down


def modulation(cond, down, up, bias):
    return jnp.split((jax.nn.silu(cond) @ down) @ up + bias, 6, axis=-1)


def setup_inputs(seed: int = 0) -> dict:
    key = jax.random.key(seed)
    keys = iter(jax.random.split(key, 48))
    D, F = D_MODEL, D_FF

    def nrm(shape, scale):
        return jax.random.normal(next(keys), shape, F32) * scale

    def gain(shape):
        return 1.0 + nrm(shape, 0.02)

    x = nrm((BATCH, SEQ, D), 1.0)
    c = nrm((BATCH, D), 1.0)
    ctx = nrm((BATCH, CTX_LEN, D), 1.0)
    c_ctx = nrm((D,), 1.0)
    dn_a_log = jnp.log(jax.random.uniform(next(keys), (N_DN, 2, DN_HEADS), F32, 1.0, 16.0))
    dt = jnp.exp(jax.random.uniform(next(keys), (N_DN, 2, DN_HEADS), F32,
                                    math.log(1e-3), math.log(1e-1)))
    dn_dt_bias = dt + jnp.log(-jnp.expm1(-dt))
    ratio = jnp.arange(D, dtype=F32) / (D - 1)
    rw_w0 = (-6.5 + 5.0 * ratio ** 1.5) + nrm((N_RW, 2, D), 0.1)
    return {
        'x': x, 'c': c, 'ctx': ctx, 'c_ctx': c_ctx,
        'ada_down': nrm((DEPTH, D, ADA_RANK), D ** -0.5),
        'ada_up': nrm((DEPTH, ADA_RANK, 6 * D), 0.5 * ADA_RANK ** -0.5),
        'ada_b': nrm((DEPTH, 6 * D), 0.02),
        'norm1_g': gain((DEPTH, D)),
        'norm2_g': gain((DEPTH, D)),
        'ffn_w_up': nrm((DEPTH, D, 2 * F), D ** -0.5),
        'ffn_conv_w': nrm((DEPTH, 3, 3, F), 1.0 / 3.0),
        'ffn_conv_b': nrm((DEPTH, F), 0.02),
        'ffn_w_down': nrm((DEPTH, F, D), F ** -0.5),
        'dn_w_in': nrm((N_DN, D, 4 * D + 4 * DN_HEADS), D ** -0.5),
        'dn_conv_w': nrm((N_DN, DN_CONV, 3 * D), DN_CONV ** -0.5),
        'dn_a_log': dn_a_log,
        'dn_dt_bias': dn_dt_bias,
        'dn_norm_g': gain((N_DN, DN_DV)),
        'dn_w_out': nrm((N_DN, D, D), D ** -0.5),
        'hg_w_in': nrm((N_HG, D, 5 * D), D ** -0.5),
        'hg_lower': nrm((DEPTH, D), 0.1),
        'hg_norm_g': gain((N_HG, D)),
        'hg_w_out': nrm((N_HG, D, D), D ** -0.5),
        'rw_mu': jax.random.uniform(next(keys), (N_RW, 6, D), F32),
        'rw_w_rkv': nrm((N_RW, 3, D, D), D ** -0.5),
        'rw_w0': rw_w0,
        'rw_w1': nrm((N_RW, 2, D, RW_DECAY_LORA), D ** -0.5),
        'rw_w2': nrm((N_RW, 2, RW_DECAY_LORA, D), 0.1 * RW_DECAY_LORA ** -0.5),
        'rw_a0': nrm((N_RW, 2, D), 0.1),
        'rw_a1': nrm((N_RW, 2, D, RW_AAA_LORA), D ** -0.5),
        'rw_a2': nrm((N_RW, 2, RW_AAA_LORA, D), 0.1 * RW_AAA_LORA ** -0.5),
        'rw_g1': nrm((N_RW, D, RW_GATE_LORA), D ** -0.5),
        'rw_g2': nrm((N_RW, RW_GATE_LORA, D), RW_GATE_LORA ** -0.5),
        'rw_k_k': 0.85 + nrm((N_RW, D), 0.02),
        'rw_k_a': gain((N_RW, D)),
        'rw_r_k': nrm((N_RW, RW_HEADS, RW_HEAD), 0.1),
        'rw_lnx_w': gain((N_RW, D)),
        'rw_lnx_b': nrm((N_RW, D), 0.02),
        'rw_w_out': nrm((N_RW, D, D), D ** -0.5),
        'final_g': gain((D,)),
    }


def reference(x, c, ctx, c_ctx, ada_down, ada_up, ada_b, norm1_g, norm2_g,
              ffn_w_up, ffn_conv_w, ffn_conv_b, ffn_w_down,
              dn_w_in, dn_conv_w, dn_a_log, dn_dt_bias, dn_norm_g, dn_w_out,
              hg_w_in, hg_lower, hg_norm_g, hg_w_out,
              rw_mu, rw_w_rkv, rw_w0, rw_w1, rw_w2, rw_a0, rw_a1, rw_a2, rw_g1, rw_g2,
              rw_k_k, rw_k_a, rw_r_k, rw_lnx_w, rw_lnx_b, rw_w_out, final_g):
    B, L, _ = x.shape
    rows = L // GRID_W
    n_ctx = ctx.shape[1]
    sm = jax.nn.softmax(hg_lower.astype(F32), axis=0)
    lower_bounds = jnp.cumsum(sm, axis=0) - sm[0]
    cond_l = c[:, None, :]
    cond_c = c_ctx[None, None, :]
    xl, xc = x, ctx
    for i in range(DEPTH):
        kind, j = i % N_MIXERS, i // N_MIXERS
        ctx_live = i < DEPTH - 1
        sh1l, sc1l, ga1l, sh2l, sc2l, ga2l = modulation(cond_l, ada_down[i], ada_up[i], ada_b[i])
        sh1c, sc1c, ga1c, sh2c, sc2c, ga2c = modulation(cond_c, ada_down[i], ada_up[i], ada_b[i])
        hl = rmsnorm(xl, norm1_g[i]) * (1 + sc1l) + sh1l
        hc = rmsnorm(xc, norm1_g[i]) * (1 + sc1c) + sh1c
        if kind == 0:
            yc, yl = gated_deltanet_mixer(hc, hl, dn_w_in[j], dn_conv_w[j], dn_a_log[j],
                                          dn_dt_bias[j], dn_norm_g[j], dn_w_out[j], ctx_live)
        elif kind == 1:
            yc, yl = hgrn2_mixer(hc, hl, hg_w_in[j], lower_bounds[i], hg_norm_g[j],
                                 hg_w_out[j], ctx_live)
        else:
            yc, yl = rwkv7_mixer(hc, hl, rows, rw_mu[j], rw_w_rkv[j], rw_w0[j], rw_w1[j],
                                 rw_w2[j], rw_a0[j], rw_a1[j], rw_a2[j], rw_g1[j], rw_g2[j],
                                 rw_k_k[j], rw_k_a[j], rw_r_k[j], rw_lnx_w[j], rw_lnx_b[j],
                                 rw_w_out[j], ctx_live)
        xl = xl + ga1l * yl
        xl = xl + ga2l * conv_glu(rmsnorm(xl, norm2_g[i]) * (1 + sc2l) + sh2l, rows, GRID_W,
                                  ffn_w_up[i], ffn_conv_w[i], ffn_conv_b[i], ffn_w_down[i])
        if ctx_live:
            xc = xc + ga1c * yc
            xc = xc + ga2c * conv_glu(rmsnorm(xc, norm2_g[i]) * (1 + sc2c) + sh2c, 1, n_ctx,
                                      ffn_w_up[i], ffn_conv_w[i], ffn_conv_b[i], ffn_w_down[i])
    return rmsnorm(xl, final_g)
```

```python
import functools
import math

import jax
import jax.numpy as jnp
from jax import lax
from jax.experimental import pallas as pl
from jax.experimental.pallas import tpu as pltpu

F32 = jnp.float32
BF16 = jnp.bfloat16

EPS = 1e-6
RW_LN_EPS = 64e-5
GRID_W = 64
DN_DK = 128
DN_CHUNK = 64
HG_DK = 128
HG_CHUNK = 32
RW_HEAD = 64
RW_CHUNK = 32
LANES = 128
VMEM_LIMIT = 56 * 1024 * 1024


def _fit(n, b, unit=LANES):
    if n <= b:
        return n
    for cand in range(b - b % unit, 0, -unit):
        if n % cand == 0:
            return cand
    raise ValueError((n, b, unit))


def _cparams(*sem):
    return pltpu.CompilerParams(dimension_semantics=sem, vmem_limit_bytes=VMEM_LIMIT)


def _sigmoid(x):
    return 1.0 / (1.0 + jnp.exp(-x))


def _silu(x):
    return x * _sigmoid(x)


def _softplus(x):
    return jnp.maximum(x, 0.0) + jnp.log(1.0 + jnp.exp(-jnp.abs(x)))


def _gelu_tanh(x):
    c = math.sqrt(2.0 / math.pi)
    return 0.5 * x * (1.0 + jnp.tanh(c * (x + 0.044715 * (x * x * x))))


def _dot(a, b, dims=(((1,), (0,)), ((), ()))):
    return lax.dot_general(a.astype(BF16), b.astype(BF16), dims, preferred_element_type=F32)


def _dot_nt(a, b):
    return _dot(a, b, (((1,), (1,)), ((), ())))


def _dot_tn(a, b):
    return _dot(a, b, (((0,), (0,)), ((), ())))


def _split2(x):
    hi = x.astype(BF16)
    lo = (x - hi.astype(F32)).astype(BF16)
    return hi, lo


def _dot3(a, b, dims=(((1,), (0,)), ((), ()))):
    ah, al = _split2(a)
    bh, bl = _split2(b)
    d = functools.partial(lax.dot_general, dimension_numbers=dims, preferred_element_type=F32)
    return d(ah, bh) + (d(ah, bl) + d(al, bh))


def _unit_tri_inverse(n):
    c = n.shape[0]
    ii = lax.broadcasted_iota(jnp.int32, (c, c), 0)
    jj = lax.broadcasted_iota(jnp.int32, (c, c), 1)
    x = jnp.where(ii == jj, 1.0, 0.0) + n
    p = n
    steps = max(int(math.log2(c)) - 1, 0)
    for _ in range(steps):
        p = _dot3(p, p)
        x = x + _dot3(x, p)
    return x


def _cumsum_rows(x, rev):
    c = x.shape[0]
    row = lax.broadcasted_iota(jnp.int32, x.shape, 0)
    s = 1
    while s < c:
        if rev:
            x = x + jnp.where(row < c - s, pltpu.roll(x, c - s, 0), 0.0)
        else:
            x = x + jnp.where(row >= s, pltpu.roll(x, s, 0), 0.0)
        s *= 2
    return x


def _epi_none(y):
    return y


def _epi_logw(y):
    return -jnp.exp(-_softplus(-y) - 0.5)


_EPILOGUES = {'none': _epi_none, 'tanh': jnp.tanh, 'sigmoid': _sigmoid, 'logw': _epi_logw}


def _mm_body(*refs, nk, epi, has_bias, has_gate, has_resid, precise):
    it = iter(refs)
    a_ref, w_ref = next(it), next(it)
    bias_ref = next(it) if has_bias else None
    gate_ref = next(it) if has_gate else None
    resid_ref = next(it) if has_resid else None
    o_ref = next(it)
    acc_ref = next(it) if nk > 1 else None

    def finish(y):
        if has_bias:
            y = y + bias_ref[...]
        y = _EPILOGUES[epi](y)
        if has_gate:
            y = y * gate_ref[...]
        if has_resid:
            y = resid_ref[...] + y
        o_ref[...] = y.astype(o_ref.dtype)

    if precise:
        part = _dot3(a_ref[...], w_ref[...])
    else:
        part = _dot(a_ref[...], w_ref[...])
    if nk == 1:
        finish(part)
    else:
        k = pl.program_id(2)

        @pl.when(k == 0)
        def _():
            acc_ref[...] = part

        @pl.when(k > 0)
        def _():
            acc_ref[...] += part

        @pl.when(k == nk - 1)
        def _():
            finish(acc_ref[...])


def matmul(a, w, *, bias=None, gate=None, resid=None, epi='none', out_dtype=F32,
           bm=1024, bn=1024, bk=None, a_koff=0, precise=False, name='matmul'):
    m = a.shape[0]
    k, n = w.shape
    bm, bn = _fit(m, bm, 8), _fit(n, bn)
    bk = k if bk is None else _fit(k, bk)
    nk = k // bk
    if nk == 1:
        grid = (n // bn, m // bm)
        a_spec = pl.BlockSpec((bm, bk), lambda j, i: (i, a_koff))
        w_spec = pl.BlockSpec((bk, bn), lambda j, i: (0, j))
        row_spec = pl.BlockSpec((1, bn), lambda j, i: (0, j))
        o_spec = pl.BlockSpec((bm, bn), lambda j, i: (i, j))
        sem = ('parallel', 'parallel')
        scratch = []
    else:
        grid = (n // bn, m // bm, nk)
        a_spec = pl.BlockSpec((bm, bk), lambda j, i, kk: (i, kk + a_koff * nk))
        w_spec = pl.BlockSpec((bk, bn), lambda j, i, kk: (kk, j))
        row_spec = pl.BlockSpec((1, bn), lambda j, i, kk: (0, j))
        o_spec = pl.BlockSpec((bm, bn), lambda j, i, kk: (i, j))
        sem = ('parallel', 'parallel', 'arbitrary')
        scratch = [pltpu.VMEM((bm, bn), F32)]
    args, specs = [a, w], [a_spec, w_spec]
    for extra in (bias, gate):
        if extra is not None:
            args.append(extra.reshape(1, n).astype(F32))
            specs.append(row_spec)
    if resid is not None:
        args.append(resid)
        specs.append(o_spec)
    body = functools.partial(_mm_body, nk=nk, epi=epi, has_bias=bias is not None,
                             has_gate=gate is not None, has_resid=resid is not None,
                             precise=precise)
    return pl.pallas_call(
        body, out_shape=jax.ShapeDtypeStruct((m, n), out_dtype), grid=grid,
        in_specs=specs, out_specs=o_spec, scratch_shapes=scratch,
        compiler_params=_cparams(*sem), name=name)(*args)


def _normmod_body(x_ref, g_ref, sc_ref, sh_ref, o_ref):
    x = x_ref[...]
    y = x * lax.rsqrt(jnp.mean(x * x, axis=-1, keepdims=True) + EPS) * g_ref[...]
    o_ref[...] = (y * (1.0 + sc_ref[...]) + sh_ref[...]).astype(o_ref.dtype)


def normmod(x, g, sc, sh, out_dtype, bt=256):
    m, d = x.shape
    bt = min(bt, m)
    row = pl.BlockSpec((1, d), lambda i: (0, 0))
    blk = pl.BlockSpec((bt, d), lambda i: (i, 0))
    return pl.pallas_call(
        _normmod_body, out_shape=jax.ShapeDtypeStruct((m, d), out_dtype), grid=(m // bt,),
        in_specs=[blk, row, row, row], out_specs=blk,
        compiler_params=_cparams('parallel'), name='normmod')(
            x, g.reshape(1, d), sc.reshape(1, d), sh.reshape(1, d))


def _dnconv_body(x_ref, prev_ref, next_ref, w_ref, o_ref, *, mode):
    i = pl.program_id(0)
    nb = pl.num_programs(0)
    x = x_ref[...]
    bt = x.shape[0]
    row = lax.broadcasted_iota(jnp.int32, x.shape, 0)
    prev_row = jnp.where(i > 0, prev_ref[7:8, :], 0.0)
    next_row = jnp.where(i < nb - 1, next_ref[0:1, :], 0.0)
    xm = jnp.where(row == 0, prev_row, pltpu.roll(x, 1, 0))
    xp = jnp.where(row == bt - 1, next_row, pltpu.roll(x, bt - 1, 0))
    y = w_ref[0:1, :] * xm + w_ref[1:2, :] * x + w_ref[2:3, :] * xp
    y = _silu(y)
    if mode == 'v':
        o_ref[...] = y.astype(o_ref.dtype)
        return
    scale = DN_DK ** -0.5 if mode == 'q' else 1.0
    for h in range(x.shape[1] // DN_DK):
        sl = slice(h * DN_DK, (h + 1) * DN_DK)
        yh = y[:, sl]
        inv = lax.rsqrt(jnp.sum(yh * yh, axis=-1, keepdims=True) + EPS)
        o_ref[:, sl] = (yh * (inv * scale)).astype(o_ref.dtype)


def dn_conv(p, conv_w, part, d, mode, bt=512, bc=512):
    m = p.shape[0]
    bt, bc = _fit(m, bt, 8), _fit(d, bc)
    off = part * d // bc
    last8 = m // 8 - 1
    r8 = bt // 8
    return pl.pallas_call(
        functools.partial(_dnconv_body, mode=mode),
        out_shape=jax.ShapeDtypeStruct((m, d), F32), grid=(m // bt, d // bc),
        in_specs=[pl.BlockSpec((bt, bc), lambda i, j: (i, off + j)),
                  pl.BlockSpec((8, bc), lambda i, j: (jnp.maximum(i * r8 - 1, 0), off + j)),
                  pl.BlockSpec((8, bc), lambda i, j: (jnp.minimum((i + 1) * r8, last8), off + j)),
                  pl.BlockSpec((3, bc), lambda i, j: (0, off + j))],
        out_specs=pl.BlockSpec((bt, bc), lambda i, j: (i, j)),
        compiler_params=_cparams('parallel', 'parallel'), name='dn_conv_' + mode)(p, p, p, conv_w)


def _dnab_body(x_ref, alog_ref, dtb_ref, o_ref):
    x = x_ref[...]
    lane = lax.broadcasted_iota(jnp.int32, x.shape, 1)
    g = -jnp.exp(alog_ref[...]) * _softplus(x + dtb_ref[...])
    o_ref[...] = jnp.where(lane < x.shape[1] // 2, g, _sigmoid(x))


def dn_gates(pab, a_log, dt_bias, bt=1024):
    m, w = pab.shape
    bt = min(bt, m)
    zeros = jnp.zeros((w // 2,), F32)
    alog = jnp.concatenate([a_log.reshape(-1), zeros]).reshape(1, w)
    dtb = jnp.concatenate([dt_bias.reshape(-1), zeros]).reshape(1, w)
    row = pl.BlockSpec((1, w), lambda i: (0, 0))
    blk = pl.BlockSpec((bt, w), lambda i: (i, 0))
    return pl.pallas_call(
        _dnab_body, out_shape=jax.ShapeDtypeStruct((m, w), F32), grid=(m // bt,),
        in_specs=[blk, row, row], out_specs=blk,
        compiler_params=_cparams('parallel'), name='dn_gates')(pab, alog, dtb)


def _dn_chunk(q, k, v, g_row, beta_row, s, masks, rev):
    eye, incl, strict = masks
    c = q.shape[0]
    g_bc = jnp.broadcast_to(g_row, (c, c))
    gc_col = jnp.sum(jnp.where(incl, g_bc, 0.0), axis=1, keepdims=True)
    gc_row = jnp.sum(jnp.where(eye, jnp.broadcast_to(gc_col, (c, c)), 0.0), axis=0, keepdims=True)
    beta_col = jnp.sum(jnp.where(eye, jnp.broadcast_to(beta_row, (c, c)), 0.0), axis=1,
                       keepdims=True)
    g_tot = jnp.sum(g_row, axis=1, keepdims=True)
    decay = jnp.where(incl, jnp.exp(jnp.where(incl, gc_col - gc_row, 0.0)), 0.0)
    kb = k * beta_col
    n = -jnp.where(strict, _dot_nt(kb, k) * decay, 0.0)
    t = _unit_tri_inverse(n)
    egc = jnp.exp(gc_col)
    sol = _dot3(t, jnp.concatenate([kb * egc, v * beta_col], axis=1))
    w, u = sol[:, :DN_DK], sol[:, DN_DK:]
    attn = jnp.where(incl, _dot_nt(q, k) * decay, 0.0)
    qg = q * egc
    kd = k * jnp.exp(g_tot - gc_col)
    ws = _dot(jnp.concatenate([w, qg], axis=0), s)
    v_new = u - ws[:c]
    o = ws[c:] + _dot(attn, v_new)
    s_new = s * jnp.exp(g_tot) + _dot_tn(kd, v_new)
    return o, s_new


def _tri_masks(c, rev):
    ii = lax.broadcasted_iota(jnp.int32, (c, c), 0)
    jj = lax.broadcasted_iota(jnp.int32, (c, c), 1)
    if rev:
        return ii == jj, ii <= jj, ii < jj
    return ii == jj, ii >= jj, ii > jj


def _dn_scan_body(q_ref, k_ref, v_ref, g_ref, b_ref, s0_ref, o_ref, sf_ref, s_scr, *, rev, nc, hg):
    j = pl.program_id(1)

    @pl.when(j == 0)
    def _():
        s_scr[...] = s0_ref[...]

    c = DN_CHUNK
    masks = _tri_masks(c, rev)

    def chunk(ci, carry):
        cc = (nc - 1 - ci) if rev else ci
        r0 = pl.multiple_of(cc * c, c)
        for h in range(hg):
            sl = slice(h * DN_DK, (h + 1) * DN_DK)
            o, s_new = _dn_chunk(q_ref[pl.ds(r0, c), sl], k_ref[pl.ds(r0, c), sl],
                                 v_ref[pl.ds(r0, c), sl], g_ref[h, pl.ds(cc, 1), :],
                                 b_ref[h, pl.ds(cc, 1), :], s_scr[h], masks, rev)
            o_ref[pl.ds(r0, c), sl] = o
            s_scr[h] = s_new
        return carry

    lax.fori_loop(0, nc, chunk, 0)

    @pl.when(j == pl.num_programs(1) - 1)
    def _():
        sf_ref[...] = s_scr[...]


def dn_scan(q, k, v, gb4, s0, direction, hg=4, nc=8):
    m, d = q.shape
    h = d // DN_DK
    hg = min(hg, h)
    ng = h // hg
    nc = min(nc, m // DN_CHUNK)
    bt = nc * DN_CHUNK
    nb = m // bt
    rev = direction == 1

    def tb(j):
        return (nb - 1 - j) if rev else j

    tok = pl.BlockSpec((bt, hg * DN_DK), lambda g, j: (tb(j), g))
    st = pl.BlockSpec((hg, DN_DK, DN_DK), lambda g, j: (g, 0, 0))
    g_spec = pl.BlockSpec((None, hg, nc, DN_CHUNK),
                          lambda g, j: (direction * ng + g, 0, tb(j), 0))
    b_spec = pl.BlockSpec((None, hg, nc, DN_CHUNK),
                          lambda g, j: ((2 + direction) * ng + g, 0, tb(j), 0))
    return pl.pallas_call(
        functools.partial(_dn_scan_body, rev=rev, nc=nc, hg=hg),
        out_shape=(jax.ShapeDtypeStruct((m, d), F32), jax.ShapeDtypeStruct(s0.shape, F32)),
        grid=(ng, nb), in_specs=[tok, tok, tok, g_spec, b_spec, st], out_specs=(tok, st),
        scratch_shapes=[pltpu.VMEM((hg, DN_DK, DN_DK), F32)],
        compiler_params=_cparams('parallel', 'arbitrary'),
        name='dn_scan_bwd' if rev else 'dn_scan_fwd')(q, k, v, gb4, gb4, s0)


def _gateout_body(of_ref, ob_ref, z_ref, g_ref, o_ref):
    o = of_ref[...] + ob_ref[...]
    z = z_ref[...]
    for h in range(o.shape[1] // LANES):
        sl = slice(h * LANES, (h + 1) * LANES)
        oh = o[:, sl]
        y = oh * lax.rsqrt(jnp.mean(oh * oh, axis=-1, keepdims=True) + EPS) * g_ref[:, sl]
        o_ref[:, sl] = (y * _silu(z[:, sl])).astype(o_ref.dtype)


def gate_out(o_f, o_b, p, z_part, gain, bt=512, bc=512):
    m, d = o_f.shape
    bt, bc = _fit(m, bt, 8), _fit(d, bc)
    off = z_part * d // bc
    blk = pl.BlockSpec((bt, bc), lambda i, j: (i, j))
    return pl.pallas_call(
        _gateout_body, out_shape=jax.ShapeDtypeStruct((m, d), BF16), grid=(m // bt, d // bc),
        in_specs=[blk, blk, pl.BlockSpec((bt, bc), lambda i, j: (i, off + j)),
                  pl.BlockSpec((1, bc), lambda i, j: (0, j))],
        out_specs=blk, compiler_params=_cparams('parallel', 'parallel'),
        name='gate_out')(o_f, o_b, p, gain.reshape(1, d))


def _hg_chunk(q_raw, f_raw, v, lb, st, masks, rev):
    _, incl, _ = masks
    c = q_raw.shape[0]
    q = _silu(q_raw)
    f = lb + (1.0 - lb) * _sigmoid(f_raw)
    k = 1.0 - f
    b = _cumsum_rows(jnp.log(f), rev)
    mid = (c - 1 - c // 2) if rev else c // 2
    last = 0 if rev else c - 1
    m = b[mid:mid + 1, :]
    b_last = b[last:last + 1, :]
    a_qk = jnp.where(incl, _dot_nt(q * jnp.exp(b - m), k * jnp.exp(m - b)), 0.0)
    kd = k * jnp.exp(b_last - b)
    o = _dot_nt(q * jnp.exp(b), st) + _dot(a_qk, v)
    st_new = st * jnp.exp(b_last) + _dot_tn(v, kd)
    return o, st_new


def _hg_scan_body(q_ref, f_ref, v_ref, lb_ref, s0_ref, o_ref, sf_ref, s_scr, *, rev, nc, hg):
    j = pl.program_id(1)

    @pl.when(j == 0)
    def _():
        s_scr[...] = s0_ref[...]

    c = HG_CHUNK
    masks = _tri_masks(c, rev)

    def chunk(ci, carry):
        cc = (nc - 1 - ci) if rev else ci
        r0 = pl.multiple_of(cc * c, c)
        for h in range(hg):
            sl = slice(h * HG_DK, (h + 1) * HG_DK)
            o, s_new = _hg_chunk(q_ref[pl.ds(r0, c), sl], f_ref[pl.ds(r0, c), sl],
                                 v_ref[pl.ds(r0, c), sl], lb_ref[:, sl], s_scr[h], masks, rev)
            o_ref[pl.ds(r0, c), sl] = o
            s_scr[h] = s_new
        return carry

    lax.fori_loop(0, nc, chunk, 0)

    @pl.when(j == pl.num_programs(1) - 1)
    def _():
        sf_ref[...] = s_scr[...]


def hg_scan(p, lower, s0, direction, d, hg=4, nc=8):
    m = p.shape[0]
    h = d // HG_DK
    hg = min(hg, h)
    ng = h // hg
    nc = min(nc, m // HG_CHUNK)
    bt = nc * HG_CHUNK
    nb = m // bt
    rev = direction == 1

    def tb(j):
        return (nb - 1 - j) if rev else j

    def part(pi):
        return pl.BlockSpec((bt, hg * HG_DK), lambda g, j: (tb(j), pi * ng + g))

    st = pl.BlockSpec((hg, HG_DK, HG_DK), lambda g, j: (g, 0, 0))
    return pl.pallas_call(
        functools.partial(_hg_scan_body, rev=rev, nc=nc, hg=hg),
        out_shape=(jax.ShapeDtypeStruct((m, d), F32), jax.ShapeDtypeStruct(s0.shape, F32)),
        grid=(ng, nb),
        in_specs=[part(0), part(1 + direction), part(3),
                  pl.BlockSpec((1, hg * HG_DK), lambda g, j: (0, g)), st],
        out_specs=(pl.BlockSpec((bt, hg * HG_DK), lambda g, j: (tb(j), g)), st),
        scratch_shapes=[pltpu.VMEM((hg, HG_DK, HG_DK), F32)],
        compiler_params=_cparams('parallel', 'arbitrary'),
        name='hg_scan_bwd' if rev else 'hg_scan_fwd')(p, p, p, lower.reshape(1, d), s0)


def _shiftmix_body(x_ref, prev_ref, next_ref, mu_ref, *o_refs, width):
    i = pl.program_id(0)
    nb = pl.num_programs(0)
    bt, d = x_ref.shape
    dq = d // 4
    col = lax.broadcasted_iota(jnp.int32, (bt, dq), 0) % width
    for qi in range(4):
        sl = slice(qi * dq, (qi + 1) * dq)
        x = x_ref[:, sl]
        if qi == 0:
            sh = jnp.where(col == 0, 0.0, pltpu.roll(x, 1, 0))
        elif qi == 1:
            sh = jnp.where(col == width - 1, 0.0, pltpu.roll(x, bt - 1, 0))
        elif qi == 2:
            edge = jnp.where(i > 0, prev_ref[:, sl], 0.0)
            sh = edge if bt == width else jnp.concatenate([edge, x[:bt - width]], axis=0)
        else:
            edge = jnp.where(i < nb - 1, next_ref[:, sl], 0.0)
            sh = edge if bt == width else jnp.concatenate([x[width:], edge], axis=0)
        xx = sh - x
        for n, o_ref in enumerate(o_refs):
            o_ref[:, sl] = (x + xx * mu_ref[n:n + 1, sl]).astype(o_ref.dtype)


def shift_mix(h, mu, width, bt=128):
    m, d = h.shape
    bt = max(min(bt, m), width)
    rw = bt // width
    lastw = m // width - 1
    blk = pl.BlockSpec((bt, d), lambda i: (i, 0))
    out = jax.ShapeDtypeStruct((m, d), BF16)
    return pl.pallas_call(
        functools.partial(_shiftmix_body, width=width),
        out_shape=(out,) * 6, grid=(m // bt,),
        in_specs=[blk,
                  pl.BlockSpec((width, d), lambda i: (jnp.maximum(i * rw - 1, 0), 0)),
                  pl.BlockSpec((width, d), lambda i: (jnp.minimum((i + 1) * rw, lastw), 0)),
                  pl.BlockSpec((6, d), lambda i: (0, 0))],
        out_specs=(blk,) * 6, compiler_params=_cparams('arbitrary'), name='shift_mix')(h, h, h, mu)


def _halves(x, lo_mask):
    return jnp.concatenate([jnp.where(lo_mask, x, 0.0), jnp.where(lo_mask, 0.0, x)], axis=0)


def _head_sum(x, lo_mask):
    s_lo = jnp.sum(jnp.where(lo_mask, x, 0.0), axis=-1, keepdims=True)
    s_hi = jnp.sum(jnp.where(lo_mask, 0.0, x), axis=-1, keepdims=True)
    return jnp.where(lo_mask, s_lo, s_hi)


def _rw_chunk(r, k, v, a_sig, logw, k_k, k_a, st, masks, rev):
    lo, strict2, incl_cat = masks
    c = r.shape[0]
    kq = k * k_k
    kk = kq * lax.rsqrt(_head_sum(kq * kq, lo) + EPS)
    a = -kk
    b = kk * a_sig
    kd = k * (1.0 + (a_sig - 1.0) * k_a)
    cs = _cumsum_rows(logw, rev)
    ce = cs - logw
    mid = (c - 1 - c // 2) if rev else c // 2
    last = 0 if rev else c - 1
    m = cs[mid:mid + 1, :]
    c_last = cs[last:last + 1, :]
    e_mc = jnp.exp(m - cs)
    am2 = _halves(a * jnp.exp(ce - m), lo)
    bm2 = _halves(b * e_mc, lo)
    km2 = _halves(kd * e_mc, lo)
    v2 = _halves(v, lo)
    rm = r * jnp.exp(cs - m)
    n = jnp.where(strict2, _dot_nt(am2, bm2), 0.0)
    a_ak = jnp.where(strict2, _dot_nt(am2, km2), 0.0)
    t = _unit_tri_inverse(n)
    rhs = jnp.concatenate([_halves(a * jnp.exp(ce), lo), _dot(a_ak, v2)], axis=1)
    sol = _dot3(t, rhs)
    w2, uv2 = sol[:, :LANES], sol[:, LANES:]
    a_rb = jnp.where(incl_cat, _dot_nt(rm, bm2), 0.0)
    a_rk = jnp.where(incl_cat, _dot_nt(rm, km2), 0.0)
    ws = _dot_nt(jnp.concatenate([w2, r * jnp.exp(cs)], axis=0), st)
    u2 = ws[:2 * c] + uv2
    o = ws[2 * c:] + _dot(a_rb, u2) + _dot(a_rk, v2)
    dec = jnp.exp(c_last - cs)
    bd2 = _halves(b * dec, lo)
    kd2 = _halves(kd * dec, lo)
    st_new = st * jnp.exp(c_last) + _dot_tn(jnp.concatenate([u2, v2], axis=0),
                                            jnp.concatenate([bd2, kd2], axis=0))
    return o, st_new


def _rw_masks(c, rev):
    lo = lax.broadcasted_iota(jnp.int32, (1, LANES), 1) < RW_HEAD
    ii = lax.broadcasted_iota(jnp.int32, (2 * c, 2 * c), 0)
    jj = lax.broadcasted_iota(jnp.int32, (2 * c, 2 * c), 1)
    same = (ii // c) == (jj // c)
    il, jl = ii % c, jj % c
    strict2 = same & ((il < jl) if rev else (il > jl))
    ic = lax.broadcasted_iota(jnp.int32, (c, 2 * c), 0)
    jc = lax.broadcasted_iota(jnp.int32, (c, 2 * c), 1) % c
    incl_cat = (ic <= jc) if rev else (ic >= jc)
    return lo, strict2, incl_cat


def _rw_scan_body(r_ref, k_ref, v_ref, a_ref, lw_ref, kk_ref, ka_ref, s0_ref, o_ref, sf_ref, s_scr,
                  *, rev, nc, hg):
    j = pl.program_id(1)

    @pl.when(j == 0)
    def _():
        s_scr[...] = s0_ref[...]

    c = RW_CHUNK
    masks = _rw_masks(c, rev)

    def chunk(ci, carry):
        cc = (nc - 1 - ci) if rev else ci
        r0 = pl.multiple_of(cc * c, c)
        for h in range(hg):
            sl = slice(h * LANES, (h + 1) * LANES)
            rows = pl.ds(r0, c)
            o, s_new = _rw_chunk(r_ref[rows, sl], k_ref[rows, sl], v_ref[rows, sl],
                                 a_ref[rows, sl], lw_ref[rows, sl], kk_ref[:, sl], ka_ref[:, sl],
                                 s_scr[h], masks, rev)
            o_ref[rows, sl] = o
            s_scr[h] = s_new
        return carry

    lax.fori_loop(0, nc, chunk, 0)

    @pl.when(j == pl.num_programs(1) - 1)
    def _():
        sf_ref[...] = s_scr[...]


def rw_scan(r, k, v, a_sig, logw, k_k, k_a, s0, direction, hg=4, nc=8):
    m, d = r.shape
    npair = d // LANES
    hg = min(hg, npair)
    ng = npair // hg
    nc = min(nc, m // RW_CHUNK)
    bt = nc * RW_CHUNK
    nb = m // bt
    rev = direction == 1

    def tb(j):
        return (nb - 1 - j) if rev else j

    tok = pl.BlockSpec((bt, hg * LANES), lambda g, j: (tb(j), g))
    row = pl.BlockSpec((1, hg * LANES), lambda g, j: (0, g))
    st = pl.BlockSpec((hg, LANES, LANES), lambda g, j: (g, 0, 0))
    return pl.pallas_call(
        functools.partial(_rw_scan_body, rev=rev, nc=nc, hg=hg),
        out_shape=(jax.ShapeDtypeStruct((m, d), F32), jax.ShapeDtypeStruct(s0.shape, F32)),
        grid=(ng, nb), in_specs=[tok, tok, tok, tok, tok, row, row, st], out_specs=(tok, st),
        scratch_shapes=[pltpu.VMEM((hg, LANES, LANES), F32)],
        compiler_params=_cparams('parallel', 'arbitrary'),
        name='rw_scan_bwd' if rev else 'rw_scan_fwd')(
            r, k, v, a_sig, logw, k_k.reshape(1, d), k_a.reshape(1, d), s0)


def _rwout_body(of_ref, ob_ref, r_ref, k_ref, v_ref, af_ref, ab_ref, g_ref,
                lw_ref, lb_ref, ka_ref, rk_ref, o_ref):
    lo = lax.broadcasted_iota(jnp.int32, (1, LANES), 1) < RW_HEAD
    inv_n = 1.0 / RW_HEAD
    for h in range(of_ref.shape[1] // LANES):
        sl = slice(h * LANES, (h + 1) * LANES)
        o = of_ref[:, sl] + ob_ref[:, sl]
        oc = o - _head_sum(o, lo) * inv_n
        y = oc * lax.rsqrt(_head_sum(oc * oc, lo) * inv_n + RW_LN_EPS)
        y = y * lw_ref[:, sl] + lb_ref[:, sl]
        k_sum = k_ref[:, sl] * (2.0 + (af_ref[:, sl] + ab_ref[:, sl] - 2.0) * ka_ref[:, sl])
        bonus = _head_sum(r_ref[:, sl] * k_sum * rk_ref[:, sl], lo) * v_ref[:, sl]
        o_ref[:, sl] = ((y + bonus) * g_ref[:, sl]).astype(o_ref.dtype)


def rw_out(o_f, o_b, r, k, v, a_f, a_b, g, lnx_w, lnx_b, k_a, r_k, bt=256, bc=512):
    m, d = o_f.shape
    bt, bc = _fit(m, bt, 8), _fit(d, bc)
    blk = pl.BlockSpec((bt, bc), lambda i, j: (i, j))
    row = pl.BlockSpec((1, bc), lambda i, j: (0, j))
    rows = [t.reshape(1, d) for t in (lnx_w, lnx_b, k_a, r_k)]
    return pl.pallas_call(
        _rwout_body, out_shape=jax.ShapeDtypeStruct((m, d), BF16), grid=(m // bt, d // bc),
        in_specs=[blk] * 8 + [row] * 4, out_specs=blk,
        compiler_params=_cparams('parallel', 'parallel'), name='rw_out')(
            o_f, o_b, r, k, v, a_f, a_b, g, *rows)


def _convglu_body(u_ref, prev_ref, next_ref, v_ref, cw_ref, cb_ref, o_ref, *, width):
    i = pl.program_id(0)
    nb = pl.num_programs(0)
    x = u_ref[...]
    bt = x.shape[0]
    up = jnp.where(i > 0, prev_ref[...], 0.0)
    dn = jnp.where(i < nb - 1, next_ref[...], 0.0)
    if bt > width:
        up = jnp.concatenate([up, x[:bt - width]], axis=0)
        dn = jnp.concatenate([x[width:], dn], axis=0)
    col = lax.broadcasted_iota(jnp.int32, x.shape, 0) % width

    def taps(xx, r):
        left = jnp.where(col == 0, 0.0, pltpu.roll(xx, 1, 0))
        right = jnp.where(col == width - 1, 0.0, pltpu.roll(xx, bt - 1, 0))
        return (cw_ref[3 * r:3 * r + 1, :] * left + cw_ref[3 * r + 1:3 * r + 2, :] * xx
                + cw_ref[3 * r + 2:3 * r + 3, :] * right)

    y = taps(up, 0) + taps(x, 1) + taps(dn, 2) + cb_ref[...]
    o_ref[...] = (_gelu_tanh(y) * v_ref[...]).astype(o_ref.dtype)


def conv_glu_gate(uv, conv_w, conv_b, width, f, bt=512, bc=512):
    m = uv.shape[0]
    bt = max(min(bt, m), width)
    bc = _fit(f, bc)
    rw = bt // width
    lastw = m // width - 1
    voff = f // bc
    return pl.pallas_call(
        functools.partial(_convglu_body, width=width),
        out_shape=jax.ShapeDtypeStruct((m, f), BF16), grid=(m // bt, f // bc),
        in_specs=[pl.BlockSpec((bt, bc), lambda i, j: (i, j)),
                  pl.BlockSpec((width, bc), lambda i, j: (jnp.maximum(i * rw - 1, 0), j)),
                  pl.BlockSpec((width, bc), lambda i, j: (jnp.minimum((i + 1) * rw, lastw), j)),
                  pl.BlockSpec((bt, bc), lambda i, j: (i, voff + j)),
                  pl.BlockSpec((9, bc), lambda i, j: (0, j)),
                  pl.BlockSpec((1, bc), lambda i, j: (0, j))],
        out_specs=pl.BlockSpec((bt, bc), lambda i, j: (i, j)),
        compiler_params=_cparams('parallel', 'parallel'), name='conv_glu')(
            uv, uv, uv, uv, conv_w.reshape(9, f), conv_b.reshape(1, f))


def _modulation(conds, down, up, bias):
    low = matmul(conds, down, precise=True, bm=8, bn=512, name='ada_down')
    return matmul(low, up, bias=bias, precise=True, bm=8, bn=2048, name='ada_up')


def _dn_mixer(streams, w, d, last):
    heads = d // DN_DK
    hg = min(4, heads)
    w_main = w['w_in'][:, :4 * d].astype(BF16)
    w_ab = w['w_in'][:, 4 * d:].astype(BF16)
    gain = jnp.tile(w['norm_g'], heads)
    prepped = []
    for h in streams:
        p = matmul(h, w_main, name='dn_in')
        pab = matmul(h, w_ab, bn=4 * heads, name='dn_in_ab')
        q = dn_conv(p, w['conv_w'], 0, d, 'q')
        k = dn_conv(p, w['conv_w'], 1, d, 'k')
        v = dn_conv(p, w['conv_w'], 2, d, 'v')
        gb = dn_gates(pab, w['a_log'], w['dt_bias'])
        gb4 = gb.T.reshape(4 * heads // hg, hg, h.shape[0] // DN_CHUNK, DN_CHUNK)
        prepped.append((p, q, k, v, gb4))
    states = [jnp.zeros((heads, DN_DK, DN_DK), F32)] * 2
    outs = []
    for si, (p, q, k, v, gb4) in enumerate(prepped):
        o = []
        for direction in range(2):
            od, states[direction] = dn_scan(q, k, v, gb4, states[direction], direction, hg=hg)
            o.append(od)
        if si == 0 and last:
            outs.append(None)
        else:
            outs.append(gate_out(o[0], o[1], p, 3, gain))
    return outs


def _hg_mixer(streams, w, d, last):
    heads = d // HG_DK
    w_in = w['w_in'].astype(BF16)
    states = [jnp.zeros((heads, HG_DK, HG_DK), F32)] * 2
    outs = []
    for si, h in enumerate(streams):
        p = matmul(h, w_in, name='hg_in')
        o = []
        for direction in range(2):
            od, states[direction] = hg_scan(p, w['lower'], states[direction], direction, d)
            o.append(od)
        if si == 0 and last:
            outs.append(None)
        else:
            outs.append(gate_out(o[0], o[1], p, 4, w['norm_g']))
    return outs


def _rw_mixer(streams, widths, w, d, last):
    w_rkv = w['w_rkv'].astype(BF16)
    w1 = jnp.concatenate([w['w1'][0], w['w1'][1]], axis=1).astype(BF16)
    a1 = jnp.concatenate([w['a1'][0], w['a1'][1]], axis=1).astype(BF16)
    w2 = w['w2'].astype(BF16)
    a2 = w['a2'].astype(BF16)
    lora = w['w1'].shape[-1]
    gl = w['g1'].shape[-1]
    glp = -(-gl // LANES) * LANES
    g1 = jnp.pad(w['g1'], ((0, 0), (0, glp - gl))).astype(BF16)
    g2 = jnp.pad(w['g2'], ((0, glp - gl), (0, 0))).astype(BF16)
    states = [jnp.zeros((d // LANES, LANES, LANES), F32)] * 2
    outs = []
    for si, (h, width) in enumerate(zip(streams, widths)):
        xr, xw, xk, xv, xa, xg = shift_mix(h, w['mu'], width)
        r = matmul(xr, w_rkv[0], name='rw_r')
        k = matmul(xk, w_rkv[1], name='rw_k')
        v = matmul(xv, w_rkv[2], name='rw_v')
        tw = matmul(xw, w1, epi='tanh', out_dtype=BF16, bn=2 * lora, name='rw_w1')
        ta = matmul(xa, a1, out_dtype=BF16, bn=2 * lora, name='rw_a1')
        o, a_sig = [], []
        for direction in range(2):
            logw = matmul(tw, w2[direction], bias=w['w0'][direction], epi='logw',
                          a_koff=direction, name='rw_w2')
            a_d = matmul(ta, a2[direction], bias=w['a0'][direction], epi='sigmoid',
                         a_koff=direction, name='rw_a2')
            od, states[direction] = rw_scan(r, k, v, a_d, logw, w['k_k'], w['k_a'],
                                            states[direction], direction)
            o.append(od)
            a_sig.append(a_d)
        if si == 0 and last:
            outs.append(None)
            continue
        tg = matmul(xg, g1, epi='sigmoid', out_dtype=BF16, name='rw_g1')
        g = matmul(tg, g2, name='rw_g2')
        outs.append(rw_out(o[0], o[1], r, k, v, a_sig[0], a_sig[1], g, w['lnx_w'], w['lnx_b'],
                           w['k_a'], w['r_k'].reshape(-1)))
    return outs


def kernel(x, c, ctx, c_ctx, ada_down, ada_up, ada_b, norm1_g, norm2_g, ffn_w_up, ffn_conv_w, ffn_conv_b, ffn_w_down, dn_w_in, dn_conv_w, dn_a_log, dn_dt_bias, dn_norm_g, dn_w_out, hg_w_in, hg_lower, hg_norm_g, hg_w_out, rw_mu, rw_w_rkv, rw_w0, rw_w1, rw_w2, rw_a0, rw_a1, rw_a2, rw_g1, rw_g2, rw_k_k, rw_k_a, rw_r_k, rw_lnx_w, rw_lnx_b, rw_w_out, final_g):
    _, seq, d = x.shape
    depth = ada_down.shape[0]
    n_ctx = ctx.shape[1]
    f = ffn_w_down.shape[1]
    xl, xc = x[0], ctx[0]
    sm = jax.nn.softmax(hg_lower.astype(F32), axis=0)
    lower_bounds = jnp.cumsum(sm, axis=0) - sm[0]
    conds = jnp.zeros((8, d), F32).at[0].set(jax.nn.silu(c[0])).at[1].set(jax.nn.silu(c_ctx))
    widths = (n_ctx, GRID_W)
    for i in range(depth):
        kind, j = i % 3, i // 3
        last = i == depth - 1
        mod = _modulation(conds, ada_down[i], ada_up[i], ada_b[i])
        mods = [[mod[row, n * d:(n + 1) * d] for n in range(6)] for row in (1, 0)]
        xs = [xc, xl]
        h_dtype = F32 if kind == 2 else BF16
        hs = [normmod(t, norm1_g[i], mm[1], mm[0], h_dtype) for t, mm in zip(xs, mods)]
        if kind == 0:
            w = dict(w_in=dn_w_in[j], conv_w=dn_conv_w[j], a_log=dn_a_log[j],
                     dt_bias=dn_dt_bias[j], norm_g=dn_norm_g[j])
            ys = _dn_mixer(hs, w, d, last)
            w_out = dn_w_out[j]
        elif kind == 1:
            w = dict(w_in=hg_w_in[j], lower=lower_bounds[i], norm_g=hg_norm_g[j])
            ys = _hg_mixer(hs, w, d, last)
            w_out = hg_w_out[j]
        else:
            w = dict(mu=rw_mu[j], w_rkv=rw_w_rkv[j], w0=rw_w0[j], w1=rw_w1[j], w2=rw_w2[j],
                     a0=rw_a0[j], a1=rw_a1[j], a2=rw_a2[j], g1=rw_g1[j], g2=rw_g2[j],
                     k_k=rw_k_k[j], k_a=rw_k_a[j], r_k=rw_r_k[j], lnx_w=rw_lnx_w[j],
                     lnx_b=rw_lnx_b[j])
            ys = _rw_mixer(hs, widths, w, d, last)
            w_out = rw_w_out[j]
        w_out = w_out.astype(BF16)
        w_up = ffn_w_up[i].astype(BF16)
        w_down = ffn_w_down[i].astype(BF16)
        new = []
        for t, y, mm, width in zip(xs, ys, mods, widths):
            if y is None:
                new.append(t)
                continue
            t = matmul(y, w_out, gate=mm[2], resid=t, name='mix_out')
            h2 = normmod(t, norm2_g[i], mm[4], mm[3], BF16)
            uv = matmul(h2, w_up, name='ffn_up')
            gated = conv_glu_gate(uv, ffn_conv_w[i], ffn_conv_b[i], width, f)
            t = matmul(gated, w_down, gate=mm[5], resid=t, bk=2048, name='ffn_down')
            new.append(t)
        xc, xl = new
    zeros = jnp.zeros((d,), F32)
    return normmod(xl, final_g, zeros, zeros, F32)[None]
```

```python
import functools
import math

import jax
import jax.numpy as jnp
from jax import lax
from jax.experimental import pallas as pl
from jax.experimental.pallas import tpu as pltpu

F32 = jnp.float32
BF16 = jnp.bfloat16

EPS = 1e-6
RW_LN_EPS = 64e-5
GRID_W = 64
DN_DK = 128
DN_CHUNK = 64
HG_DK = 128
HG_CHUNK = 32
RW_HEAD = 64
RW_CHUNK = 32
LANES = 128
VMEM_LIMIT = 56 * 1024 * 1024


def _fit(n, b, unit=LANES):
    if n <= b:
        return n
    for cand in range(b - b % unit, 0, -unit):
        if n % cand == 0:
            return cand
    raise ValueError((n, b, unit))


def _cparams(*sem):
    return pltpu.CompilerParams(dimension_semantics=sem, vmem_limit_bytes=VMEM_LIMIT)


def _sigmoid(x):
    return 1.0 / (1.0 + jnp.exp(-x))


def _silu(x):
    return x * _sigmoid(x)


def _softplus(x):
    return jnp.maximum(x, 0.0) + jnp.log(1.0 + jnp.exp(-jnp.abs(x)))


def _gelu_tanh(x):
    c = math.sqrt(2.0 / math.pi)
    return 0.5 * x * (1.0 + jnp.tanh(c * (x + 0.044715 * (x * x * x))))


def _dot(a, b, dims=(((1,), (0,)), ((), ()))):
    return lax.dot_general(a.astype(BF16), b.astype(BF16), dims, preferred_element_type=F32)


def _dot_nt(a, b):
    return _dot(a, b, (((1,), (1,)), ((), ())))


def _dot_tn(a, b):
    return _dot(a, b, (((0,), (0,)), ((), ())))


def _split2(x):
    hi = x.astype(BF16)
    lo = (x - hi.astype(F32)).astype(BF16)
    return hi, lo


def _dot3(a, b, dims=(((1,), (0,)), ((), ()))):
    ah, al = _split2(a)
    bh, bl = _split2(b)
    d = functools.partial(lax.dot_general, dimension_numbers=dims, preferred_element_type=F32)
    return d(ah, bh) + (d(ah, bl) + d(al, bh))


def _unit_tri_inverse_many(ns, nilpotency=None):
    c = ns[0].shape[0]
    ii = lax.broadcasted_iota(jnp.int32, (c, c), 0)
    jj = lax.broadcasted_iota(jnp.int32, (c, c), 1)
    eye = jnp.where(ii == jj, 1.0, 0.0)
    xs = [eye + n for n in ns]
    ps = list(ns)
    steps = max(int(math.log2(nilpotency or c)) - 1, 0)
    for _ in range(steps):
        ps = [_dot3(p, p) for p in ps]
        xs = [x + _dot3(x, p) for x, p in zip(xs, ps)]
    return xs


def _unit_tri_inverse(n, nilpotency=None):
    return _unit_tri_inverse_many([n], nilpotency)[0]


def _cumsum_rows(x, rev):
    c = x.shape[0]
    row = lax.broadcasted_iota(jnp.int32, x.shape, 0)
    s = 1
    while s < c:
        if rev:
            x = x + jnp.where(row < c - s, pltpu.roll(x, c - s, 0), 0.0)
        else:
            x = x + jnp.where(row >= s, pltpu.roll(x, s, 0), 0.0)
        s *= 2
    return x


def _epi_none(y):
    return y


def _epi_logw(y):
    return -jnp.exp(-_softplus(-y) - 0.5)


_EPILOGUES = {'none': _epi_none, 'tanh': jnp.tanh, 'sigmoid': _sigmoid, 'logw': _epi_logw}


def _mm_body(*refs, nk, epi, has_bias, has_gate, has_resid, precise):
    it = iter(refs)
    a_ref, w_ref = next(it), next(it)
    bias_ref = next(it) if has_bias else None
    gate_ref = next(it) if has_gate else None
    resid_ref = next(it) if has_resid else None
    o_ref = next(it)
    acc_ref = next(it) if nk > 1 else None

    def finish(y):
        if has_bias:
            y = y + bias_ref[...]
        y = _EPILOGUES[epi](y)
        if has_gate:
            y = y * gate_ref[...]
        if has_resid:
            y = resid_ref[...] + y
        o_ref[...] = y.astype(o_ref.dtype)

    if precise:
        part = _dot3(a_ref[...], w_ref[...])
    else:
        part = _dot(a_ref[...], w_ref[...])
    if nk == 1:
        finish(part)
    else:
        k = pl.program_id(2)

        @pl.when(k == 0)
        def _():
            acc_ref[...] = part

        @pl.when(k > 0)
        def _():
            acc_ref[...] += part

        @pl.when(k == nk - 1)
        def _():
            finish(acc_ref[...])


def matmul(a, w, *, bias=None, gate=None, resid=None, epi='none', out_dtype=F32,
           bm=1024, bn=1024, bk=None, a_koff=0, precise=False, name='matmul'):
    m = a.shape[0]
    k, n = w.shape
    bm, bn = _fit(m, bm, 8), _fit(n, bn)
    bk = k if bk is None else _fit(k, bk)
    nk = k // bk
    if nk == 1:
        grid = (n // bn, m // bm)
        a_spec = pl.BlockSpec((bm, bk), lambda j, i: (i, a_koff))
        w_spec = pl.BlockSpec((bk, bn), lambda j, i: (0, j))
        row_spec = pl.BlockSpec((1, bn), lambda j, i: (0, j))
        o_spec = pl.BlockSpec((bm, bn), lambda j, i: (i, j))
        sem = ('parallel', 'parallel')
        scratch = []
    else:
        grid = (n // bn, m // bm, nk)
        a_spec = pl.BlockSpec((bm, bk), lambda j, i, kk: (i, kk + a_koff * nk))
        w_spec = pl.BlockSpec((bk, bn), lambda j, i, kk: (kk, j))
        row_spec = pl.BlockSpec((1, bn), lambda j, i, kk: (0, j))
        o_spec = pl.BlockSpec((bm, bn), lambda j, i, kk: (i, j))
        sem = ('parallel', 'parallel', 'arbitrary')
        scratch = [pltpu.VMEM((bm, bn), F32)]
    args, specs = [a, w], [a_spec, w_spec]
    for extra in (bias, gate):
        if extra is not None:
            args.append(extra.reshape(1, n).astype(F32))
            specs.append(row_spec)
    if resid is not None:
        args.append(resid)
        specs.append(o_spec)
    body = functools.partial(_mm_body, nk=nk, epi=epi, has_bias=bias is not None,
                             has_gate=gate is not None, has_resid=resid is not None,
                             precise=precise)
    return pl.pallas_call(
        body, out_shape=jax.ShapeDtypeStruct((m, n), out_dtype), grid=grid,
        in_specs=specs, out_specs=o_spec, scratch_shapes=scratch,
        compiler_params=_cparams(*sem), name=name)(*args)


def _normmod_body(x_ref, g_ref, sc_ref, sh_ref, o_ref):
    x = x_ref[...]
    y = x * lax.rsqrt(jnp.mean(x * x, axis=-1, keepdims=True) + EPS) * g_ref[...]
    o_ref[...] = (y * (1.0 + sc_ref[...]) + sh_ref[...]).astype(o_ref.dtype)


def normmod(x, g, sc, sh, out_dtype, bt=256):
    m, d = x.shape
    bt = min(bt, m)
    row = pl.BlockSpec((1, d), lambda i: (0, 0))
    blk = pl.BlockSpec((bt, d), lambda i: (i, 0))
    return pl.pallas_call(
        _normmod_body, out_shape=jax.ShapeDtypeStruct((m, d), out_dtype), grid=(m // bt,),
        in_specs=[blk, row, row, row], out_specs=blk,
        compiler_params=_cparams('parallel'), name='normmod')(
            x, g.reshape(1, d), sc.reshape(1, d), sh.reshape(1, d))


def _dnconv_body(x_ref, prev_ref, next_ref, w_ref, o_ref, *, mode):
    i = pl.program_id(0)
    nb = pl.num_programs(0)
    x = x_ref[...]
    bt = x.shape[0]
    row = lax.broadcasted_iota(jnp.int32, x.shape, 0)
    prev_row = jnp.where(i > 0, prev_ref[7:8, :], 0.0)
    next_row = jnp.where(i < nb - 1, next_ref[0:1, :], 0.0)
    xm = jnp.where(row == 0, prev_row, pltpu.roll(x, 1, 0))
    xp = jnp.where(row == bt - 1, next_row, pltpu.roll(x, bt - 1, 0))
    y = w_ref[0:1, :] * xm + w_ref[1:2, :] * x + w_ref[2:3, :] * xp
    y = _silu(y)
    if mode == 'v':
        o_ref[...] = y.astype(o_ref.dtype)
        return
    scale = DN_DK ** -0.5 if mode == 'q' else 1.0
    for h in range(x.shape[1] // DN_DK):
        sl = slice(h * DN_DK, (h + 1) * DN_DK)
        yh = y[:, sl]
        inv = lax.rsqrt(jnp.sum(yh * yh, axis=-1, keepdims=True) + EPS)
        o_ref[:, sl] = (yh * (inv * scale)).astype(o_ref.dtype)


def dn_conv(p, conv_w, part, d, mode, bt=512, bc=512):
    m = p.shape[0]
    bt, bc = _fit(m, bt, 8), _fit(d, bc)
    off = part * d // bc
    last8 = m // 8 - 1
    r8 = bt // 8
    return pl.pallas_call(
        functools.partial(_dnconv_body, mode=mode),
        out_shape=jax.ShapeDtypeStruct((m, d), F32), grid=(m // bt, d // bc),
        in_specs=[pl.BlockSpec((bt, bc), lambda i, j: (i, off + j)),
                  pl.BlockSpec((8, bc), lambda i, j: (jnp.maximum(i * r8 - 1, 0), off + j)),
                  pl.BlockSpec((8, bc), lambda i, j: (jnp.minimum((i + 1) * r8, last8), off + j)),
                  pl.BlockSpec((3, bc), lambda i, j: (0, off + j))],
        out_specs=pl.BlockSpec((bt, bc), lambda i, j: (i, j)),
        compiler_params=_cparams('parallel', 'parallel'), name='dn_conv_' + mode)(p, p, p, conv_w)


def _dnab_body(x_ref, alog_ref, dtb_ref, o_ref):
    x = x_ref[...]
    lane = lax.broadcasted_iota(jnp.int32, x.shape, 1)
    g = -jnp.exp(alog_ref[...]) * _softplus(x + dtb_ref[...])
    o_ref[...] = jnp.where(lane < x.shape[1] // 2, g, _sigmoid(x))


def dn_gates(pab, a_log, dt_bias, bt=1024):
    m, w = pab.shape
    bt = min(bt, m)
    zeros = jnp.zeros((w // 2,), F32)
    alog = jnp.concatenate([a_log.reshape(-1), zeros]).reshape(1, w)
    dtb = jnp.concatenate([dt_bias.reshape(-1), zeros]).reshape(1, w)
    row = pl.BlockSpec((1, w), lambda i: (0, 0))
    blk = pl.BlockSpec((bt, w), lambda i: (i, 0))
    return pl.pallas_call(
        _dnab_body, out_shape=jax.ShapeDtypeStruct((m, w), F32), grid=(m // bt,),
        in_specs=[blk, row, row], out_specs=blk,
        compiler_params=_cparams('parallel'), name='dn_gates')(pab, alog, dtb)


def _tri_masks(c, rev):
    ii = lax.broadcasted_iota(jnp.int32, (c, c), 0)
    jj = lax.broadcasted_iota(jnp.int32, (c, c), 1)
    if rev:
        return ii == jj, ii <= jj, ii < jj
    return ii == jj, ii >= jj, ii > jj


def _dn_intra(q, k, v, g_row, beta_row, masks):
    eye, incl, strict = masks
    c = q[0].shape[0]
    heads = range(len(q))
    gc_col, beta_col, g_tot, decay = [], [], [], []
    for h in heads:
        g_bc = jnp.broadcast_to(g_row[h], (c, c))
        gcc = jnp.sum(jnp.where(incl, g_bc, 0.0), axis=1, keepdims=True)
        gcr = jnp.sum(jnp.where(eye, jnp.broadcast_to(gcc, (c, c)), 0.0), axis=0, keepdims=True)
        gc_col.append(gcc)
        beta_col.append(jnp.sum(jnp.where(eye, jnp.broadcast_to(beta_row[h], (c, c)), 0.0),
                                axis=1, keepdims=True))
        g_tot.append(jnp.sum(g_row[h], axis=1, keepdims=True))
        decay.append(jnp.where(incl, jnp.exp(jnp.where(incl, gcc - gcr, 0.0)), 0.0))
    kb = [k[h] * beta_col[h] for h in heads]
    kk = [_dot_nt(kb[h], k[h]) for h in heads]
    qk = [_dot_nt(q[h], k[h]) for h in heads]
    t = _unit_tri_inverse_many([-jnp.where(strict, kk[h] * decay[h], 0.0) for h in heads])
    egc = [jnp.exp(gc_col[h]) for h in heads]
    rhs = [jnp.concatenate([kb[h] * egc[h], v[h] * beta_col[h]], axis=1) for h in heads]
    sol = [_dot3(t[h], rhs[h]) for h in heads]
    wq = [jnp.concatenate([sol[h][:, :DN_DK], q[h] * egc[h]], axis=0) for h in heads]
    u = [sol[h][:, DN_DK:] for h in heads]
    attn = [jnp.where(incl, qk[h] * decay[h], 0.0) for h in heads]
    kd = [k[h] * jnp.exp(g_tot[h] - gc_col[h]) for h in heads]
    gl = [jnp.exp(g_tot[h]) for h in heads]
    return wq, u, attn, kd, gl


def _dn_scan_body(q_ref, k_ref, v_ref, g_ref, b_ref, s0_ref, o_ref, sf_ref,
                  s_scr, wq_scr, u_scr, at_scr, kd_scr, gl_scr, *, rev, nc, hg):
    j = pl.program_id(1)

    @pl.when(j == 0)
    def _():
        s_scr[...] = s0_ref[...]

    c = DN_CHUNK
    masks = _tri_masks(c, rev)
    heads = range(hg)
    lanes = [slice(h * DN_DK, (h + 1) * DN_DK) for h in heads]

    def intra(ci, carry):
        rows = pl.ds(pl.multiple_of(ci * c, c), c)
        wq, u, attn, kd, gl = _dn_intra(
            [q_ref[rows, sl] for sl in lanes], [k_ref[rows, sl] for sl in lanes],
            [v_ref[rows, sl] for sl in lanes], [g_ref[ci, h:h + 1, :] for h in heads],
            [b_ref[ci, h:h + 1, :] for h in heads], masks)
        for h in heads:
            wq_scr[ci, h] = wq[h].astype(BF16)
            u_scr[ci, h] = u[h]
            at_scr[ci, h] = attn[h].astype(BF16)
            kd_scr[ci, h] = kd[h].astype(BF16)
            gl_scr[ci, h] = jnp.broadcast_to(gl[h], (1, DN_DK))
        return carry

    lax.fori_loop(0, nc, intra, 0)

    def inter(ci, carry):
        cc = (nc - 1 - ci) if rev else ci
        rows = pl.ds(pl.multiple_of(cc * c, c), c)
        s = [s_scr[h] for h in heads]
        ws = [_dot(wq_scr[cc, h], s[h]) for h in heads]
        v_new = [u_scr[cc, h] - ws[h][:c] for h in heads]
        av = [_dot(at_scr[cc, h], v_new[h]) for h in heads]
        kv = [_dot_tn(kd_scr[cc, h], v_new[h]) for h in heads]
        for h in heads:
            o_ref[rows, lanes[h]] = ws[h][c:] + av[h]
            s_scr[h] = s[h] * gl_scr[cc, h] + kv[h]
        return carry

    lax.fori_loop(0, nc, inter, 0)

    @pl.when(j == pl.num_programs(1) - 1)
    def _():
        sf_ref[...] = s_scr[...]


def dn_scan(q, k, v, gbt, s0, direction, hg=16, nc=4):
    m, d = q.shape
    h = d // DN_DK
    hg = min(hg, h)
    ng = h // hg
    nc = min(nc, m // DN_CHUNK)
    bt = nc * DN_CHUNK
    nb = m // bt
    rev = direction == 1
    gb4 = gbt.reshape(4 * ng, hg, m // DN_CHUNK, DN_CHUNK).transpose(0, 2, 1, 3)

    def tb(j):
        return (nb - 1 - j) if rev else j

    tok = pl.BlockSpec((bt, hg * DN_DK), lambda g, j: (tb(j), g))
    st = pl.BlockSpec((hg, DN_DK, DN_DK), lambda g, j: (g, 0, 0))
    g_spec = pl.BlockSpec((None, nc, hg, DN_CHUNK),
                          lambda g, j: (direction * ng + g, tb(j), 0, 0))
    b_spec = pl.BlockSpec((None, nc, hg, DN_CHUNK),
                          lambda g, j: ((2 + direction) * ng + g, tb(j), 0, 0))
    return pl.pallas_call(
        functools.partial(_dn_scan_body, rev=rev, nc=nc, hg=hg),
        out_shape=(jax.ShapeDtypeStruct((m, d), F32), jax.ShapeDtypeStruct(s0.shape, F32)),
        grid=(ng, nb), in_specs=[tok, tok, tok, g_spec, b_spec, st], out_specs=(tok, st),
        scratch_shapes=[pltpu.VMEM((hg, DN_DK, DN_DK), F32),
                        pltpu.VMEM((nc, hg, 2 * DN_CHUNK, DN_DK), BF16),
                        pltpu.VMEM((nc, hg, DN_CHUNK, DN_DK), F32),
                        pltpu.VMEM((nc, hg, DN_CHUNK, DN_CHUNK), BF16),
                        pltpu.VMEM((nc, hg, DN_CHUNK, DN_DK), BF16),
                        pltpu.VMEM((nc, hg, 1, DN_DK), F32)],
        compiler_params=_cparams('parallel', 'arbitrary'),
        name='dn_scan_bwd' if rev else 'dn_scan_fwd')(q, k, v, gb4, gb4, s0)


def _gateout_body(of_ref, ob_ref, z_ref, g_ref, o_ref):
    o = of_ref[...] + ob_ref[...]
    z = z_ref[...]
    for h in range(o.shape[1] // LANES):
        sl = slice(h * LANES, (h + 1) * LANES)
        oh = o[:, sl]
        y = oh * lax.rsqrt(jnp.mean(oh * oh, axis=-1, keepdims=True) + EPS) * g_ref[:, sl]
        o_ref[:, sl] = (y * _silu(z[:, sl])).astype(o_ref.dtype)


def gate_out(o_f, o_b, p, z_part, gain, bt=512, bc=512):
    m, d = o_f.shape
    bt, bc = _fit(m, bt, 8), _fit(d, bc)
    off = z_part * d // bc
    blk = pl.BlockSpec((bt, bc), lambda i, j: (i, j))
    return pl.pallas_call(
        _gateout_body, out_shape=jax.ShapeDtypeStruct((m, d), BF16), grid=(m // bt, d // bc),
        in_specs=[blk, blk, pl.BlockSpec((bt, bc), lambda i, j: (i, off + j)),
                  pl.BlockSpec((1, bc), lambda i, j: (0, j))],
        out_specs=blk, compiler_params=_cparams('parallel', 'parallel'),
        name='gate_out')(o_f, o_b, p, gain.reshape(1, d))


def _hg_chunk(q_raw, f_raw, v, lb, st, masks, rev):
    _, incl, _ = masks
    c = q_raw[0].shape[0]
    heads = range(len(q_raw))
    mid = (c - 1 - c // 2) if rev else c // 2
    last = 0 if rev else c - 1
    qm, km, qs, kd, gl = [], [], [], [], []
    for h in heads:
        q = _silu(q_raw[h])
        f = lb[h] + (1.0 - lb[h]) * _sigmoid(f_raw[h])
        k = 1.0 - f
        b = _cumsum_rows(jnp.log(f), rev)
        m = b[mid:mid + 1, :]
        b_last = b[last:last + 1, :]
        qm.append(q * jnp.exp(b - m))
        km.append(k * jnp.exp(m - b))
        qs.append(q * jnp.exp(b))
        kd.append(k * jnp.exp(b_last - b))
        gl.append(jnp.exp(b_last))
    a_qk = [jnp.where(incl, _dot_nt(qm[h], km[h]), 0.0) for h in heads]
    inter = [_dot_nt(qs[h], st[h]) for h in heads]
    kv = [_dot_tn(v[h], kd[h]) for h in heads]
    intra = [_dot(a_qk[h], v[h]) for h in heads]
    o = [inter[h] + intra[h] for h in heads]
    st_new = [st[h] * gl[h] + kv[h] for h in heads]
    return o, st_new


def _hg_scan_body(q_ref, f_ref, v_ref, lb_ref, s0_ref, o_ref, sf_ref, s_scr, *, rev, nc, hg):
    j = pl.program_id(1)

    @pl.when(j == 0)
    def _():
        s_scr[...] = s0_ref[...]

    c = HG_CHUNK
    masks = _tri_masks(c, rev)

    heads = range(hg)
    lanes = [slice(h * HG_DK, (h + 1) * HG_DK) for h in heads]

    def chunk(ci, carry):
        cc = (nc - 1 - ci) if rev else ci
        rows = pl.ds(pl.multiple_of(cc * c, c), c)
        o, s_new = _hg_chunk([q_ref[rows, sl] for sl in lanes], [f_ref[rows, sl] for sl in lanes],
                             [v_ref[rows, sl] for sl in lanes], [lb_ref[:, sl] for sl in lanes],
                             [s_scr[h] for h in heads], masks, rev)
        for h in heads:
            o_ref[rows, lanes[h]] = o[h]
            s_scr[h] = s_new[h]
        return carry

    lax.fori_loop(0, nc, chunk, 0)

    @pl.when(j == pl.num_programs(1) - 1)
    def _():
        sf_ref[...] = s_scr[...]


def hg_scan(p, lower, s0, direction, d, hg=16, nc=8):
    m = p.shape[0]
    h = d // HG_DK
    hg = min(hg, h)
    ng = h // hg
    nc = min(nc, m // HG_CHUNK)
    bt = nc * HG_CHUNK
    nb = m // bt
    rev = direction == 1

    def tb(j):
        return (nb - 1 - j) if rev else j

    def part(pi):
        return pl.BlockSpec((bt, hg * HG_DK), lambda g, j: (tb(j), pi * ng + g))

    st = pl.BlockSpec((hg, HG_DK, HG_DK), lambda g, j: (g, 0, 0))
    return pl.pallas_call(
        functools.partial(_hg_scan_body, rev=rev, nc=nc, hg=hg),
        out_shape=(jax.ShapeDtypeStruct((m, d), F32), jax.ShapeDtypeStruct(s0.shape, F32)),
        grid=(ng, nb),
        in_specs=[part(0), part(1 + direction), part(3),
                  pl.BlockSpec((1, hg * HG_DK), lambda g, j: (0, g)), st],
        out_specs=(pl.BlockSpec((bt, hg * HG_DK), lambda g, j: (tb(j), g)), st),
        scratch_shapes=[pltpu.VMEM((hg, HG_DK, HG_DK), F32)],
        compiler_params=_cparams('parallel', 'arbitrary'),
        name='hg_scan_bwd' if rev else 'hg_scan_fwd')(p, p, p, lower.reshape(1, d), s0)


def _shiftmix_body(x_ref, prev_ref, next_ref, mu_ref, *o_refs, width):
    i = pl.program_id(0)
    nb = pl.num_programs(0)
    bt, d = x_ref.shape
    dq = d // 4
    col = lax.broadcasted_iota(jnp.int32, (bt, dq), 0) % width
    for qi in range(4):
        sl = slice(qi * dq, (qi + 1) * dq)
        x = x_ref[:, sl]
        if qi == 0:
            sh = jnp.where(col == 0, 0.0, pltpu.roll(x, 1, 0))
        elif qi == 1:
            sh = jnp.where(col == width - 1, 0.0, pltpu.roll(x, bt - 1, 0))
        elif qi == 2:
            edge = jnp.where(i > 0, prev_ref[:, sl], 0.0)
            sh = edge if bt == width else jnp.concatenate([edge, x[:bt - width]], axis=0)
        else:
            edge = jnp.where(i < nb - 1, next_ref[:, sl], 0.0)
            sh = edge if bt == width else jnp.concatenate([x[width:], edge], axis=0)
        xx = sh - x
        for n, o_ref in enumerate(o_refs):
            o_ref[:, sl] = (x + xx * mu_ref[n:n + 1, sl]).astype(o_ref.dtype)


def shift_mix(h, mu, width, bt=128):
    m, d = h.shape
    bt = max(min(bt, m), width)
    rw = bt // width
    lastw = m // width - 1
    blk = pl.BlockSpec((bt, d), lambda i: (i, 0))
    out = jax.ShapeDtypeStruct((m, d), BF16)
    return pl.pallas_call(
        functools.partial(_shiftmix_body, width=width),
        out_shape=(out,) * 6, grid=(m // bt,),
        in_specs=[blk,
                  pl.BlockSpec((width, d), lambda i: (jnp.maximum(i * rw - 1, 0), 0)),
                  pl.BlockSpec((width, d), lambda i: (jnp.minimum((i + 1) * rw, lastw), 0)),
                  pl.BlockSpec((6, d), lambda i: (0, 0))],
        out_specs=(blk,) * 6, compiler_params=_cparams('arbitrary'), name='shift_mix')(h, h, h, mu)


def _halves(x, lo_mask):
    return jnp.concatenate([jnp.where(lo_mask, x, 0.0), jnp.where(lo_mask, 0.0, x)], axis=0)


def _head_sum(x, lo_mask):
    s_lo = jnp.sum(jnp.where(lo_mask, x, 0.0), axis=-1, keepdims=True)
    s_hi = jnp.sum(jnp.where(lo_mask, 0.0, x), axis=-1, keepdims=True)
    return jnp.where(lo_mask, s_lo, s_hi)


def _rw_intra(r, k, v, a_sig, logw, k_k, k_a, masks, rev):
    lo, strict2, incl_cat = masks
    c = r[0].shape[0]
    groups = range(len(r))
    mid = (c - 1 - c // 2) if rev else c // 2
    last = 0 if rev else c - 1
    am2, bm2, km2, v2, rm, ae2, rg, bd2, kd2, gl = ([] for _ in range(10))
    for g in groups:
        kq = k[g] * k_k[g]
        kk = kq * lax.rsqrt(_head_sum(kq * kq, lo) + EPS)
        b = kk * a_sig[g]
        kd = k[g] * (1.0 + (a_sig[g] - 1.0) * k_a[g])
        cs = _cumsum_rows(logw[g], rev)
        ce = cs - logw[g]
        m = cs[mid:mid + 1, :]
        c_last = cs[last:last + 1, :]
        e_mc = jnp.exp(m - cs)
        dec = jnp.exp(c_last - cs)
        am2.append(_halves(-kk * jnp.exp(ce - m), lo))
        bm2.append(_halves(b * e_mc, lo))
        km2.append(_halves(kd * e_mc, lo))
        v2.append(_halves(v[g], lo))
        rm.append(r[g] * jnp.exp(cs - m))
        ae2.append(_halves(-kk * jnp.exp(ce), lo))
        rg.append(r[g] * jnp.exp(cs))
        bd2.append(_halves(b * dec, lo))
        kd2.append(_halves(kd * dec, lo))
        gl.append(jnp.exp(c_last))
    n = [jnp.where(strict2, _dot_nt(am2[g], bm2[g]), 0.0) for g in groups]
    a_ak = [jnp.where(strict2, _dot_nt(am2[g], km2[g]), 0.0) for g in groups]
    a_rb = [jnp.where(incl_cat, _dot_nt(rm[g], bm2[g]), 0.0) for g in groups]
    a_rk = [jnp.where(incl_cat, _dot_nt(rm[g], km2[g]), 0.0) for g in groups]
    t = _unit_tri_inverse_many(n, nilpotency=c)
    akv = [_dot(a_ak[g], v2[g]) for g in groups]
    ov = [_dot(a_rk[g], v2[g]) for g in groups]
    kv = [_dot_tn(v2[g], kd2[g]) for g in groups]
    sol = [_dot3(t[g], jnp.concatenate([ae2[g], akv[g]], axis=1)) for g in groups]
    wr = [jnp.concatenate([sol[g][:, :LANES], rg[g]], axis=0) for g in groups]
    uv2 = [sol[g][:, LANES:] for g in groups]
    return wr, uv2, a_rb, ov, bd2, kv, gl


def _rw_masks(c, rev):
    lo = lax.broadcasted_iota(jnp.int32, (1, LANES), 1) < RW_HEAD
    ii = lax.broadcasted_iota(jnp.int32, (2 * c, 2 * c), 0)
    jj = lax.broadcasted_iota(jnp.int32, (2 * c, 2 * c), 1)
    same = (ii // c) == (jj // c)
    il, jl = ii % c, jj % c
    strict2 = same & ((il < jl) if rev else (il > jl))
    ic = lax.broadcasted_iota(jnp.int32, (c, 2 * c), 0)
    jc = lax.broadcasted_iota(jnp.int32, (c, 2 * c), 1) % c
    incl_cat = (ic <= jc) if rev else (ic >= jc)
    return lo, strict2, incl_cat


def _rw_scan_body(r_ref, k_ref, v_ref, a_ref, lw_ref, kk_ref, ka_ref, s0_ref, o_ref, sf_ref,
                  s_scr, wr_scr, uv_scr, arb_scr, ov_scr, bd_scr, kv_scr, gl_scr, *, rev, nc, hg):
    j = pl.program_id(1)

    @pl.when(j == 0)
    def _():
        s_scr[...] = s0_ref[...]

    c = RW_CHUNK
    masks = _rw_masks(c, rev)
    groups = range(hg)
    lanes = [slice(g * LANES, (g + 1) * LANES) for g in groups]

    def intra(ci, carry):
        rows = pl.ds(pl.multiple_of(ci * c, c), c)
        wr, uv2, a_rb, ov, bd2, kv, gl = _rw_intra(
            [r_ref[rows, sl] for sl in lanes], [k_ref[rows, sl] for sl in lanes],
            [v_ref[rows, sl] for sl in lanes], [a_ref[rows, sl] for sl in lanes],
            [lw_ref[rows, sl] for sl in lanes], [kk_ref[:, sl] for sl in lanes],
            [ka_ref[:, sl] for sl in lanes], masks, rev)
        for g in groups:
            wr_scr[ci, g] = wr[g].astype(BF16)
            uv_scr[ci, g] = uv2[g]
            arb_scr[ci, g] = a_rb[g].astype(BF16)
            ov_scr[ci, g] = ov[g]
            bd_scr[ci, g] = bd2[g].astype(BF16)
            kv_scr[ci, g] = kv[g]
            gl_scr[ci, g] = gl[g]
        return carry

    lax.fori_loop(0, nc, intra, 0)

    def inter(ci, carry):
        cc = (nc - 1 - ci) if rev else ci
        rows = pl.ds(pl.multiple_of(cc * c, c), c)
        st = [s_scr[g] for g in groups]
        ws = [_dot_nt(wr_scr[cc, g], st[g]) for g in groups]
        u2 = [ws[g][:2 * c] + uv_scr[cc, g] for g in groups]
        au = [_dot(arb_scr[cc, g], u2[g]) for g in groups]
        ub = [_dot_tn(u2[g], bd_scr[cc, g]) for g in groups]
        for g in groups:
            o_ref[rows, lanes[g]] = ws[g][2 * c:] + au[g] + ov_scr[cc, g]
            s_scr[g] = st[g] * gl_scr[cc, g] + kv_scr[cc, g] + ub[g]
        return carry

    lax.fori_loop(0, nc, inter, 0)

    @pl.when(j == pl.num_programs(1) - 1)
    def _():
        sf_ref[...] = s_scr[...]


def rw_scan(r, k, v, a_sig, logw, k_k, k_a, s0, direction, hg=16, nc=4):
    m, d = r.shape
    npair = d // LANES
    hg = min(hg, npair)
    ng = npair // hg
    nc = min(nc, m // RW_CHUNK)
    bt = nc * RW_CHUNK
    nb = m // bt
    rev = direction == 1

    def tb(j):
        return (nb - 1 - j) if rev else j

    tok = pl.BlockSpec((bt, hg * LANES), lambda g, j: (tb(j), g))
    row = pl.BlockSpec((1, hg * LANES), lambda g, j: (0, g))
    st = pl.BlockSpec((hg, LANES, LANES), lambda g, j: (g, 0, 0))
    return pl.pallas_call(
        functools.partial(_rw_scan_body, rev=rev, nc=nc, hg=hg),
        out_shape=(jax.ShapeDtypeStruct((m, d), F32), jax.ShapeDtypeStruct(s0.shape, F32)),
        grid=(ng, nb), in_specs=[tok, tok, tok, tok, tok, row, row, st], out_specs=(tok, st),
        scratch_shapes=[pltpu.VMEM((hg, LANES, LANES), F32),
                        pltpu.VMEM((nc, hg, 3 * RW_CHUNK, LANES), BF16),
                        pltpu.VMEM((nc, hg, 2 * RW_CHUNK, LANES), F32),
                        pltpu.VMEM((nc, hg, RW_CHUNK, 2 * RW_CHUNK), BF16),
                        pltpu.VMEM((nc, hg, RW_CHUNK, LANES), F32),
                        pltpu.VMEM((nc, hg, 2 * RW_CHUNK, LANES), BF16),
                        pltpu.VMEM((nc, hg, LANES, LANES), F32),
                        pltpu.VMEM((nc, hg, 1, LANES), F32)],
        compiler_params=_cparams('parallel', 'arbitrary'),
        name='rw_scan_bwd' if rev else 'rw_scan_fwd')(
            r, k, v, a_sig, logw, k_k.reshape(1, d), k_a.reshape(1, d), s0)


def _rwout_body(of_ref, ob_ref, r_ref, k_ref, v_ref, af_ref, ab_ref, g_ref,
                lw_ref, lb_ref, ka_ref, rk_ref, o_ref):
    lo = lax.broadcasted_iota(jnp.int32, (1, LANES), 1) < RW_HEAD
    inv_n = 1.0 / RW_HEAD
    for h in range(of_ref.shape[1] // LANES):
        sl = slice(h * LANES, (h + 1) * LANES)
        o = of_ref[:, sl] + ob_ref[:, sl]
        oc = o - _head_sum(o, lo) * inv_n
        y = oc * lax.rsqrt(_head_sum(oc * oc, lo) * inv_n + RW_LN_EPS)
        y = y * lw_ref[:, sl] + lb_ref[:, sl]
        k_sum = k_ref[:, sl] * (2.0 + (af_ref[:, sl] + ab_ref[:, sl] - 2.0) * ka_ref[:, sl])
        bonus = _head_sum(r_ref[:, sl] * k_sum * rk_ref[:, sl], lo) * v_ref[:, sl]
        o_ref[:, sl] = ((y + bonus) * g_ref[:, sl]).astype(o_ref.dtype)


def rw_out(o_f, o_b, r, k, v, a_f, a_b, g, lnx_w, lnx_b, k_a, r_k, bt=256, bc=512):
    m, d = o_f.shape
    bt, bc = _fit(m, bt, 8), _fit(d, bc)
    blk = pl.BlockSpec((bt, bc), lambda i, j: (i, j))
    row = pl.BlockSpec((1, bc), lambda i, j: (0, j))
    rows = [t.reshape(1, d) for t in (lnx_w, lnx_b, k_a, r_k)]
    return pl.pallas_call(
        _rwout_body, out_shape=jax.ShapeDtypeStruct((m, d), BF16), grid=(m // bt, d // bc),
        in_specs=[blk] * 8 + [row] * 4, out_specs=blk,
        compiler_params=_cparams('parallel', 'parallel'), name='rw_out')(
            o_f, o_b, r, k, v, a_f, a_b, g, *rows)


def _convglu_body(u_ref, prev_ref, next_ref, v_ref, cw_ref, cb_ref, o_ref, *, width):
    i = pl.program_id(0)
    nb = pl.num_programs(0)
    x = u_ref[...]
    bt = x.shape[0]
    up = jnp.where(i > 0, prev_ref[...], 0.0)
    dn = jnp.where(i < nb - 1, next_ref[...], 0.0)
    if bt > width:
        up = jnp.concatenate([up, x[:bt - width]], axis=0)
        dn = jnp.concatenate([x[width:], dn], axis=0)
    col = lax.broadcasted_iota(jnp.int32, x.shape, 0) % width

    def taps(xx, r):
        left = jnp.where(col == 0, 0.0, pltpu.roll(xx, 1, 0))
        right = jnp.where(col == width - 1, 0.0, pltpu.roll(xx, bt - 1, 0))
        return (cw_ref[3 * r:3 * r + 1, :] * left + cw_ref[3 * r + 1:3 * r + 2, :] * xx
                + cw_ref[3 * r + 2:3 * r + 3, :] * right)

    y = taps(up, 0) + taps(x, 1) + taps(dn, 2) + cb_ref[...]
    o_ref[...] = (_gelu_tanh(y) * v_ref[...]).astype(o_ref.dtype)


def conv_glu_gate(uv, conv_w, conv_b, width, f, bt=512, bc=512):
    m = uv.shape[0]
    bt = max(min(bt, m), width)
    bc = _fit(f, bc)
    rw = bt // width
    lastw = m // width - 1
    voff = f // bc
    return pl.pallas_call(
        functools.partial(_convglu_body, width=width),
        out_shape=jax.ShapeDtypeStruct((m, f), BF16), grid=(m // bt, f // bc),
        in_specs=[pl.BlockSpec((bt, bc), lambda i, j: (i, j)),
                  pl.BlockSpec((width, bc), lambda i, j: (jnp.maximum(i * rw - 1, 0), j)),
                  pl.BlockSpec((width, bc), lambda i, j: (jnp.minimum((i + 1) * rw, lastw), j)),
                  pl.BlockSpec((bt, bc), lambda i, j: (i, voff + j)),
                  pl.BlockSpec((9, bc), lambda i, j: (0, j)),
                  pl.BlockSpec((1, bc), lambda i, j: (0, j))],
        out_specs=pl.BlockSpec((bt, bc), lambda i, j: (i, j)),
        compiler_params=_cparams('parallel', 'parallel'), name='conv_glu')(
            uv, uv, uv, uv, conv_w.reshape(9, f), conv_b.reshape(1, f))


def _modulation(conds, down, up, bias):
    low = matmul(conds, down, precise=True, bm=8, bn=512, name='ada_down')
    return matmul(low, up, bias=bias, precise=True, bm=8, bn=2048, name='ada_up')


def _dn_mixer(streams, w, d, last):
    heads = d // DN_DK
    w_main = w['w_in'][:, :4 * d].astype(BF16)
    w_ab = w['w_in'][:, 4 * d:].astype(BF16)
    gain = jnp.tile(w['norm_g'], heads)
    prepped = []
    for h in streams:
        p = matmul(h, w_main, name='dn_in')
        pab = matmul(h, w_ab, bn=4 * heads, name='dn_in_ab')
        q = dn_conv(p, w['conv_w'], 0, d, 'q')
        k = dn_conv(p, w['conv_w'], 1, d, 'k')
        v = dn_conv(p, w['conv_w'], 2, d, 'v')
        gb = dn_gates(pab, w['a_log'], w['dt_bias'])
        prepped.append((p, q, k, v, gb.T))
    states = [jnp.zeros((heads, DN_DK, DN_DK), F32)] * 2
    outs = []
    for si, (p, q, k, v, gbt) in enumerate(prepped):
        o = []
        for direction in range(2):
            od, states[direction] = dn_scan(q, k, v, gbt, states[direction], direction)
            o.append(od)
        if si == 0 and last:
            outs.append(None)
        else:
            outs.append(gate_out(o[0], o[1], p, 3, gain))
    return outs


def _hg_mixer(streams, w, d, last):
    heads = d // HG_DK
    w_in = w['w_in'].astype(BF16)
    states = [jnp.zeros((heads, HG_DK, HG_DK), F32)] * 2
    outs = []
    for si, h in enumerate(streams):
        p = matmul(h, w_in, name='hg_in')
        o = []
        for direction in range(2):
            od, states[direction] = hg_scan(p, w['lower'], states[direction], direction, d)
            o.append(od)
        if si == 0 and last:
            outs.append(None)
        else:
            outs.append(gate_out(o[0], o[1], p, 4, w['norm_g']))
    return outs


def _rw_mixer(streams, widths, w, d, last):
    w_rkv = w['w_rkv'].astype(BF16)
    w1 = jnp.concatenate([w['w1'][0], w['w1'][1]], axis=1).astype(BF16)
    a1 = jnp.concatenate([w['a1'][0], w['a1'][1]], axis=1).astype(BF16)
    w2 = w['w2'].astype(BF16)
    a2 = w['a2'].astype(BF16)
    lora = w['w1'].shape[-1]
    gl = w['g1'].shape[-1]
    glp = -(-gl // LANES) * LANES
    g1 = jnp.pad(w['g1'], ((0, 0), (0, glp - gl))).astype(BF16)
    g2 = jnp.pad(w['g2'], ((0, glp - gl), (0, 0))).astype(BF16)
    states = [jnp.zeros((d // LANES, LANES, LANES), F32)] * 2
    outs = []
    for si, (h, width) in enumerate(zip(streams, widths)):
        xr, xw, xk, xv, xa, xg = shift_mix(h, w['mu'], width)
        r = matmul(xr, w_rkv[0], name='rw_r')
        k = matmul(xk, w_rkv[1], name='rw_k')
        v = matmul(xv, w_rkv[2], name='rw_v')
        tw = matmul(xw, w1, epi='tanh', out_dtype=BF16, bn=2 * lora, name='rw_w1')
        ta = matmul(xa, a1, out_dtype=BF16, bn=2 * lora, name='rw_a1')
        o, a_sig = [], []
        for direction in range(2):
            logw = matmul(tw, w2[direction], bias=w['w0'][direction], epi='logw',
                          a_koff=direction, name='rw_w2')
            a_d = matmul(ta, a2[direction], bias=w['a0'][direction], epi='sigmoid',
                         a_koff=direction, name='rw_a2')
            od, states[direction] = rw_scan(r, k, v, a_d, logw, w['k_k'], w['k_a'],
                                            states[direction], direction)
            o.append(od)
            a_sig.append(a_d)
        if si == 0 and last:
            outs.append(None)
            continue
        tg = matmul(xg, g1, epi='sigmoid', out_dtype=BF16, name='rw_g1')
        g = matmul(tg, g2, name='rw_g2')
        outs.append(rw_out(o[0], o[1], r, k, v, a_sig[0], a_sig[1], g, w['lnx_w'], w['lnx_b'],
                           w['k_a'], w['r_k'].reshape(-1)))
    return outs


def kernel(x, c, ctx, c_ctx, ada_down, ada_up, ada_b, norm1_g, norm2_g, ffn_w_up, ffn_conv_w, ffn_conv_b, ffn_w_down, dn_w_in, dn_conv_w, dn_a_log, dn_dt_bias, dn_norm_g, dn_w_out, hg_w_in, hg_lower, hg_norm_g, hg_w_out, rw_mu, rw_w_rkv, rw_w0, rw_w1, rw_w2, rw_a0, rw_a1, rw_a2, rw_g1, rw_g2, rw_k_k, rw_k_a, rw_r_k, rw_lnx_w, rw_lnx_b, rw_w_out, final_g):
    _, seq, d = x.shape
    depth = ada_down.shape[0]
    n_ctx = ctx.shape[1]
    f = ffn_w_down.shape[1]
    xl, xc = x[0], ctx[0]
    sm = jax.nn.softmax(hg_lower.astype(F32), axis=0)
    lower_bounds = jnp.cumsum(sm, axis=0) - sm[0]
    conds = jnp.zeros((8, d), F32).at[0].set(jax.nn.silu(c[0])).at[1].set(jax.nn.silu(c_ctx))
    widths = (n_ctx, GRID_W)
    for i in range(depth):
        kind, j = i % 3, i // 3
        last = i == depth - 1
        mod = _modulation(conds, ada_down[i], ada_up[i], ada_b[i])
        mods = [[mod[row, n * d:(n + 1) * d] for n in range(6)] for row in (1, 0)]
        xs = [xc, xl]
        h_dtype = F32 if kind == 2 else BF16
        hs = [normmod(t, norm1_g[i], mm[1], mm[0], h_dtype) for t, mm in zip(xs, mods)]
        if kind == 0:
            w = dict(w_in=dn_w_in[j], conv_w=dn_conv_w[j], a_log=dn_a_log[j],
                     dt_bias=dn_dt_bias[j], norm_g=dn_norm_g[j])
            ys = _dn_mixer(hs, w, d, last)
            w_out = dn_w_out[j]
        elif kind == 1:
            w = dict(w_in=hg_w_in[j], lower=lower_bounds[i], norm_g=hg_norm_g[j])
            ys = _hg_mixer(hs, w, d, last)
            w_out = hg_w_out[j]
        else:
            w = dict(mu=rw_mu[j], w_rkv=rw_w_rkv[j], w0=rw_w0[j], w1=rw_w1[j], w2=rw_w2[j],
                     a0=rw_a0[j], a1=rw_a1[j], a2=rw_a2[j], g1=rw_g1[j], g2=rw_g2[j],
                     k_k=rw_k_k[j], k_a=rw_k_a[j], r_k=rw_r_k[j], lnx_w=rw_lnx_w[j],
                     lnx_b=rw_lnx_b[j])
            ys = _rw_mixer(hs, widths, w, d, last)
            w_out = rw_w_out[j]
        w_out = w_out.astype(BF16)
        w_up = ffn_w_up[i].astype(BF16)
        w_down = ffn_w_down[i].astype(BF16)
        new = []
        for t, y, mm, width in zip(xs, ys, mods, widths):
            if y is None:
                new.append(t)
                continue
            t = matmul(y, w_out, gate=mm[2], resid=t, name='mix_out')
            h2 = normmod(t, norm2_g[i], mm[4], mm[3], BF16)
            uv = matmul(h2, w_up, name='ffn_up')
            gated = conv_glu_gate(uv, ffn_conv_w[i], ffn_conv_b[i], width, f)
            t = matmul(gated, w_down, gate=mm[5], resid=t, bk=2048, name='ffn_down')
            new.append(t)
        xc, xl = new
    zeros = jnp.zeros((d,), F32)
    return normmod(xl, final_g, zeros, zeros, F32)[None]
```

```python
import functools
import math

import jax
import jax.numpy as jnp
from jax import lax
from jax.experimental import pallas as pl
from jax.experimental.pallas import tpu as pltpu

F32 = jnp.float32
BF16 = jnp.bfloat16

EPS = 1e-6
RW_LN_EPS = 64e-5
GRID_W = 64
DN_DK = 128
DN_CHUNK = 64
HG_DK = 128
HG_CHUNK = 32
RW_HEAD = 64
RW_CHUNK = 32
LANES = 128
VMEM_LIMIT = 56 * 1024 * 1024


def _fit(n, b, unit=LANES):
    if n <= b:
        return n
    for cand in range(b - b % unit, 0, -unit):
        if n % cand == 0:
            return cand
    raise ValueError((n, b, unit))


def _cparams(*sem):
    return pltpu.CompilerParams(dimension_semantics=sem, vmem_limit_bytes=VMEM_LIMIT)


def _sigmoid(x):
    return 1.0 / (1.0 + jnp.exp(-x))


def _silu(x):
    return x * _sigmoid(x)


def _softplus(x):
    return jnp.maximum(x, 0.0) + jnp.log(1.0 + jnp.exp(-jnp.abs(x)))


def _gelu_tanh(x):
    c = math.sqrt(2.0 / math.pi)
    return 0.5 * x * (1.0 + jnp.tanh(c * (x + 0.044715 * (x * x * x))))


def _dot(a, b, dims=(((1,), (0,)), ((), ()))):
    return lax.dot_general(a.astype(BF16), b.astype(BF16), dims, preferred_element_type=F32)


def _dot_nt(a, b):
    return _dot(a, b, (((1,), (1,)), ((), ())))


def _dot_tn(a, b):
    return _dot(a, b, (((0,), (0,)), ((), ())))


def _split2(x):
    hi = x.astype(BF16)
    lo = (x - hi.astype(F32)).astype(BF16)
    return hi, lo


def _dot3(a, b, dims=(((1,), (0,)), ((), ()))):
    ah, al = _split2(a)
    bh, bl = _split2(b)
    d = functools.partial(lax.dot_general, dimension_numbers=dims, preferred_element_type=F32)
    return d(ah, bh) + (d(ah, bl) + d(al, bh))


def _tri_levels(c, size):
    ii = lax.broadcasted_iota(jnp.int32, (c, c), 0)
    jj = lax.broadcasted_iota(jnp.int32, (c, c), 1)
    levels = []
    s = 1
    while (2 << s) <= size * 2 and (1 << s) < size:
        lo = (ii >> s) != (jj >> s)
        hi = (ii >> (s + 1)) == (jj >> (s + 1))
        levels.append(lo & hi)
        s += 1
    return (ii >> 1) == (jj >> 1), ii == jj, levels


def _unit_tri_solve_many(ns, rhss, tri):
    pair, eye, levels = tri
    units = range(len(ns))
    ts = [jnp.where(pair, jnp.where(eye, 1.0, n), 0.0) for n in ns]
    for lmask in levels:
        ls = [jnp.where(lmask, n, 0.0) for n in ns]
        lt = [_dot(ls[i], ts[i]) for i in units]
        ts = [ts[i] + _dot(ts[i], lt[i]) for i in units]
    tb = [t.astype(BF16) for t in ts]
    x0 = [_dot(tb[i], rhss[i]) for i in units]
    nx = [_dot3(ns[i], x0[i]) for i in units]
    res = [rhss[i] - x0[i] + nx[i] for i in units]
    dx = [_dot(tb[i], res[i]) for i in units]
    return [x0[i] + dx[i] for i in units]


def _cumsum_rows(x, rev):
    c = x.shape[0]
    row = lax.broadcasted_iota(jnp.int32, x.shape, 0)
    s = 1
    while s < c:
        if rev:
            x = x + jnp.where(row < c - s, pltpu.roll(x, c - s, 0), 0.0)
        else:
            x = x + jnp.where(row >= s, pltpu.roll(x, s, 0), 0.0)
        s *= 2
    return x


def _epi_none(y):
    return y


def _epi_logw(y):
    return -jnp.exp(-_softplus(-y) - 0.5)


_EPILOGUES = {'none': _epi_none, 'tanh': jnp.tanh, 'sigmoid': _sigmoid, 'logw': _epi_logw}


def _mm_body(*refs, nk, epi, has_bias, has_gate, has_resid, precise):
    it = iter(refs)
    a_ref, w_ref = next(it), next(it)
    bias_ref = next(it) if has_bias else None
    gate_ref = next(it) if has_gate else None
    resid_ref = next(it) if has_resid else None
    o_ref = next(it)
    acc_ref = next(it) if nk > 1 else None

    def finish(y):
        if has_bias:
            y = y + bias_ref[...]
        y = _EPILOGUES[epi](y)
        if has_gate:
            y = y * gate_ref[...]
        if has_resid:
            y = resid_ref[...] + y
        o_ref[...] = y.astype(o_ref.dtype)

    if precise:
        part = _dot3(a_ref[...], w_ref[...])
    else:
        part = _dot(a_ref[...], w_ref[...])
    if nk == 1:
        finish(part)
    else:
        k = pl.program_id(2)

        @pl.when(k == 0)
        def _():
            acc_ref[...] = part

        @pl.when(k > 0)
        def _():
            acc_ref[...] += part

        @pl.when(k == nk - 1)
        def _():
            finish(acc_ref[...])


def matmul(a, w, *, bias=None, gate=None, resid=None, epi='none', out_dtype=F32,
           bm=1024, bn=1024, bk=None, a_koff=0, precise=False, name='matmul'):
    m = a.shape[0]
    k, n = w.shape
    bm, bn = _fit(m, bm, 8), _fit(n, bn)
    bk = k if bk is None else _fit(k, bk)
    nk = k // bk
    if nk == 1:
        grid = (n // bn, m // bm)
        a_spec = pl.BlockSpec((bm, bk), lambda j, i: (i, a_koff))
        w_spec = pl.BlockSpec((bk, bn), lambda j, i: (0, j))
        row_spec = pl.BlockSpec((1, bn), lambda j, i: (0, j))
        o_spec = pl.BlockSpec((bm, bn), lambda j, i: (i, j))
        sem = ('parallel', 'parallel')
        scratch = []
    else:
        grid = (n // bn, m // bm, nk)
        a_spec = pl.BlockSpec((bm, bk), lambda j, i, kk: (i, kk + a_koff * nk))
        w_spec = pl.BlockSpec((bk, bn), lambda j, i, kk: (kk, j))
        row_spec = pl.BlockSpec((1, bn), lambda j, i, kk: (0, j))
        o_spec = pl.BlockSpec((bm, bn), lambda j, i, kk: (i, j))
        sem = ('parallel', 'parallel', 'arbitrary')
        scratch = [pltpu.VMEM((bm, bn), F32)]
    args, specs = [a, w], [a_spec, w_spec]
    for extra in (bias, gate):
        if extra is not None:
            args.append(extra.reshape(1, n).astype(F32))
            specs.append(row_spec)
    if resid is not None:
        args.append(resid)
        specs.append(o_spec)
    body = functools.partial(_mm_body, nk=nk, epi=epi, has_bias=bias is not None,
                             has_gate=gate is not None, has_resid=resid is not None,
                             precise=precise)
    return pl.pallas_call(
        body, out_shape=jax.ShapeDtypeStruct((m, n), out_dtype), grid=grid,
        in_specs=specs, out_specs=o_spec, scratch_shapes=scratch,
        compiler_params=_cparams(*sem), name=name)(*args)


def _normmod_body(x_ref, g_ref, sc_ref, sh_ref, o_ref):
    x = x_ref[...]
    y = x * lax.rsqrt(jnp.mean(x * x, axis=-1, keepdims=True) + EPS) * g_ref[...]
    o_ref[...] = (y * (1.0 + sc_ref[...]) + sh_ref[...]).astype(o_ref.dtype)


def normmod(x, g, sc, sh, out_dtype, bt=256):
    m, d = x.shape
    bt = min(bt, m)
    row = pl.BlockSpec((1, d), lambda i: (0, 0))
    blk = pl.BlockSpec((bt, d), lambda i: (i, 0))
    return pl.pallas_call(
        _normmod_body, out_shape=jax.ShapeDtypeStruct((m, d), out_dtype), grid=(m // bt,),
        in_specs=[blk, row, row, row], out_specs=blk,
        compiler_params=_cparams('parallel'), name='normmod')(
            x, g.reshape(1, d), sc.reshape(1, d), sh.reshape(1, d))


def _dnconv_body(x_ref, prev_ref, next_ref, w_ref, o_ref, *, mode):
    i = pl.program_id(0)
    nb = pl.num_programs(0)
    x = x_ref[...]
    bt = x.shape[0]
    row = lax.broadcasted_iota(jnp.int32, x.shape, 0)
    prev_row = jnp.where(i > 0, prev_ref[7:8, :], 0.0)
    next_row = jnp.where(i < nb - 1, next_ref[0:1, :], 0.0)
    xm = jnp.where(row == 0, prev_row, pltpu.roll(x, 1, 0))
    xp = jnp.where(row == bt - 1, next_row, pltpu.roll(x, bt - 1, 0))
    y = w_ref[0:1, :] * xm + w_ref[1:2, :] * x + w_ref[2:3, :] * xp
    y = _silu(y)
    if mode == 'v':
        o_ref[...] = y.astype(o_ref.dtype)
        return
    scale = DN_DK ** -0.5 if mode == 'q' else 1.0
    for h in range(x.shape[1] // DN_DK):
        sl = slice(h * DN_DK, (h + 1) * DN_DK)
        yh = y[:, sl]
        inv = lax.rsqrt(jnp.sum(yh * yh, axis=-1, keepdims=True) + EPS)
        o_ref[:, sl] = (yh * (inv * scale)).astype(o_ref.dtype)


def dn_conv(p, conv_w, part, d, mode, bt=512, bc=512):
    m = p.shape[0]
    bt, bc = _fit(m, bt, 8), _fit(d, bc)
    off = part * d // bc
    last8 = m // 8 - 1
    r8 = bt // 8
    return pl.pallas_call(
        functools.partial(_dnconv_body, mode=mode),
        out_shape=jax.ShapeDtypeStruct((m, d), F32), grid=(m // bt, d // bc),
        in_specs=[pl.BlockSpec((bt, bc), lambda i, j: (i, off + j)),
                  pl.BlockSpec((8, bc), lambda i, j: (jnp.maximum(i * r8 - 1, 0), off + j)),
                  pl.BlockSpec((8, bc), lambda i, j: (jnp.minimum((i + 1) * r8, last8), off + j)),
                  pl.BlockSpec((3, bc), lambda i, j: (0, off + j))],
        out_specs=pl.BlockSpec((bt, bc), lambda i, j: (i, j)),
        compiler_params=_cparams('parallel', 'parallel'), name='dn_conv_' + mode)(p, p, p, conv_w)


def _dnab_body(x_ref, alog_ref, dtb_ref, o_ref):
    x = x_ref[...]
    lane = lax.broadcasted_iota(jnp.int32, x.shape, 1)
    g = -jnp.exp(alog_ref[...]) * _softplus(x + dtb_ref[...])
    o_ref[...] = jnp.where(lane < x.shape[1] // 2, g, _sigmoid(x))


def dn_gates(pab, a_log, dt_bias, bt=1024):
    m, w = pab.shape
    bt = min(bt, m)
    zeros = jnp.zeros((w // 2,), F32)
    alog = jnp.concatenate([a_log.reshape(-1), zeros]).reshape(1, w)
    dtb = jnp.concatenate([dt_bias.reshape(-1), zeros]).reshape(1, w)
    row = pl.BlockSpec((1, w), lambda i: (0, 0))
    blk = pl.BlockSpec((bt, w), lambda i: (i, 0))
    return pl.pallas_call(
        _dnab_body, out_shape=jax.ShapeDtypeStruct((m, w), F32), grid=(m // bt,),
        in_specs=[blk, row, row], out_specs=blk,
        compiler_params=_cparams('parallel'), name='dn_gates')(pab, alog, dtb)


def _tri_masks(c, rev):
    ii = lax.broadcasted_iota(jnp.int32, (c, c), 0)
    jj = lax.broadcasted_iota(jnp.int32, (c, c), 1)
    if rev:
        return ii == jj, ii <= jj, ii < jj
    return ii == jj, ii >= jj, ii > jj


def _dn_intra(q, k, v, g_row, beta_row, masks, tri):
    eye, incl, strict = masks
    c = q[0].shape[0]
    heads = range(len(q))
    gc_col, beta_col, g_tot, decay = [], [], [], []
    for h in heads:
        g_bc = jnp.broadcast_to(g_row[h], (c, c))
        gcc = jnp.sum(jnp.where(incl, g_bc, 0.0), axis=1, keepdims=True)
        gcr = jnp.sum(jnp.where(eye, jnp.broadcast_to(gcc, (c, c)), 0.0), axis=0, keepdims=True)
        gc_col.append(gcc)
        beta_col.append(jnp.sum(jnp.where(eye, jnp.broadcast_to(beta_row[h], (c, c)), 0.0),
                                axis=1, keepdims=True))
        g_tot.append(jnp.sum(g_row[h], axis=1, keepdims=True))
        decay.append(jnp.where(incl, jnp.exp(jnp.where(incl, gcc - gcr, 0.0)), 0.0))
    kb = [k[h] * beta_col[h] for h in heads]
    kk = [_dot_nt(kb[h], k[h]) for h in heads]
    qk = [_dot_nt(q[h], k[h]) for h in heads]
    n = [-jnp.where(strict, kk[h] * decay[h], 0.0) for h in heads]
    egc = [jnp.exp(gc_col[h]) for h in heads]
    rhs = [jnp.concatenate([kb[h] * egc[h], v[h] * beta_col[h]], axis=1) for h in heads]
    sol = _unit_tri_solve_many(n, rhs, tri)
    wq = [jnp.concatenate([sol[h][:, :DN_DK], q[h] * egc[h]], axis=0) for h in heads]
    u = [sol[h][:, DN_DK:] for h in heads]
    attn = [jnp.where(incl, qk[h] * decay[h], 0.0) for h in heads]
    kd = [k[h] * jnp.exp(g_tot[h] - gc_col[h]) for h in heads]
    gl = [jnp.exp(g_tot[h]) for h in heads]
    return wq, u, attn, kd, gl


def _dn_scan_body(q_ref, k_ref, v_ref, g_ref, b_ref, s0_ref, o_ref, sf_ref,
                  s_scr, wq_scr, u_scr, at_scr, kd_scr, gl_scr, *, rev, nc, hg):
    j = pl.program_id(1)

    @pl.when(j == 0)
    def _():
        s_scr[...] = s0_ref[...]

    c = DN_CHUNK
    masks = _tri_masks(c, rev)
    tri = _tri_levels(c, c)
    heads = range(hg)
    lanes = [slice(h * DN_DK, (h + 1) * DN_DK) for h in heads]

    def intra(ci, carry):
        rows = pl.ds(pl.multiple_of(ci * c, c), c)
        wq, u, attn, kd, gl = _dn_intra(
            [q_ref[rows, sl] for sl in lanes], [k_ref[rows, sl] for sl in lanes],
            [v_ref[rows, sl] for sl in lanes], [g_ref[ci, h:h + 1, :] for h in heads],
            [b_ref[ci, h:h + 1, :] for h in heads], masks, tri)
        for h in heads:
            wq_scr[ci, h] = wq[h].astype(BF16)
            u_scr[ci, h] = u[h]
            at_scr[ci, h] = attn[h].astype(BF16)
            kd_scr[ci, h] = kd[h].astype(BF16)
            gl_scr[ci, h] = jnp.broadcast_to(gl[h], (1, DN_DK))
        return carry

    lax.fori_loop(0, nc, intra, 0)

    def inter(ci, carry):
        cc = (nc - 1 - ci) if rev else ci
        rows = pl.ds(pl.multiple_of(cc * c, c), c)
        s = [s_scr[h] for h in heads]
        ws = [_dot(wq_scr[cc, h], s[h]) for h in heads]
        v_new = [u_scr[cc, h] - ws[h][:c] for h in heads]
        av = [_dot(at_scr[cc, h], v_new[h]) for h in heads]
        kv = [_dot_tn(kd_scr[cc, h], v_new[h]) for h in heads]
        for h in heads:
            o_ref[rows, lanes[h]] = ws[h][c:] + av[h]
            s_scr[h] = s[h] * gl_scr[cc, h] + kv[h]
        return carry

    lax.fori_loop(0, nc, inter, 0)

    @pl.when(j == pl.num_programs(1) - 1)
    def _():
        sf_ref[...] = s_scr[...]


def dn_scan(q, k, v, gbt, s0, direction, hg=16, nc=4):
    m, d = q.shape
    h = d // DN_DK
    hg = min(hg, h)
    ng = h // hg
    nc = min(nc, m // DN_CHUNK)
    bt = nc * DN_CHUNK
    nb = m // bt
    rev = direction == 1
    gb4 = gbt.reshape(4 * ng, hg, m // DN_CHUNK, DN_CHUNK).transpose(0, 2, 1, 3)

    def tb(j):
        return (nb - 1 - j) if rev else j

    tok = pl.BlockSpec((bt, hg * DN_DK), lambda g, j: (tb(j), g))
    st = pl.BlockSpec((hg, DN_DK, DN_DK), lambda g, j: (g, 0, 0))
    g_spec = pl.BlockSpec((None, nc, hg, DN_CHUNK),
                          lambda g, j: (direction * ng + g, tb(j), 0, 0))
    b_spec = pl.BlockSpec((None, nc, hg, DN_CHUNK),
                          lambda g, j: ((2 + direction) * ng + g, tb(j), 0, 0))
    return pl.pallas_call(
        functools.partial(_dn_scan_body, rev=rev, nc=nc, hg=hg),
        out_shape=(jax.ShapeDtypeStruct((m, d), F32), jax.ShapeDtypeStruct(s0.shape, F32)),
        grid=(ng, nb), in_specs=[tok, tok, tok, g_spec, b_spec, st], out_specs=(tok, st),
        scratch_shapes=[pltpu.VMEM((hg, DN_DK, DN_DK), F32),
                        pltpu.VMEM((nc, hg, 2 * DN_CHUNK, DN_DK), BF16),
                        pltpu.VMEM((nc, hg, DN_CHUNK, DN_DK), F32),
                        pltpu.VMEM((nc, hg, DN_CHUNK, DN_CHUNK), BF16),
                        pltpu.VMEM((nc, hg, DN_CHUNK, DN_DK), BF16),
                        pltpu.VMEM((nc, hg, 1, DN_DK), F32)],
        compiler_params=_cparams('parallel', 'arbitrary'),
        name='dn_scan_bwd' if rev else 'dn_scan_fwd')(q, k, v, gb4, gb4, s0)


def _gateout_body(of_ref, ob_ref, z_ref, g_ref, o_ref):
    o = of_ref[...] + ob_ref[...]
    z = z_ref[...]
    for h in range(o.shape[1] // LANES):
        sl = slice(h * LANES, (h + 1) * LANES)
        oh = o[:, sl]
        y = oh * lax.rsqrt(jnp.mean(oh * oh, axis=-1, keepdims=True) + EPS) * g_ref[:, sl]
        o_ref[:, sl] = (y * _silu(z[:, sl])).astype(o_ref.dtype)


def gate_out(o_f, o_b, p, z_part, gain, bt=512, bc=512):
    m, d = o_f.shape
    bt, bc = _fit(m, bt, 8), _fit(d, bc)
    off = z_part * d // bc
    blk = pl.BlockSpec((bt, bc), lambda i, j: (i, j))
    return pl.pallas_call(
        _gateout_body, out_shape=jax.ShapeDtypeStruct((m, d), BF16), grid=(m // bt, d // bc),
        in_specs=[blk, blk, pl.BlockSpec((bt, bc), lambda i, j: (i, off + j)),
                  pl.BlockSpec((1, bc), lambda i, j: (0, j))],
        out_specs=blk, compiler_params=_cparams('parallel', 'parallel'),
        name='gate_out')(o_f, o_b, p, gain.reshape(1, d))


def _hg_chunk(q_raw, f_raw, v, lb, st, masks, rev):
    _, incl, _ = masks
    c = q_raw[0].shape[0]
    heads = range(len(q_raw))
    mid = (c - 1 - c // 2) if rev else c // 2
    last = 0 if rev else c - 1
    qm, km, qs, kd, gl = [], [], [], [], []
    for h in heads:
        q = _silu(q_raw[h])
        f = lb[h] + (1.0 - lb[h]) * _sigmoid(f_raw[h])
        k = 1.0 - f
        b = _cumsum_rows(jnp.log(f), rev)
        m = b[mid:mid + 1, :]
        b_last = b[last:last + 1, :]
        qm.append(q * jnp.exp(b - m))
        km.append(k * jnp.exp(m - b))
        qs.append(q * jnp.exp(b))
        kd.append(k * jnp.exp(b_last - b))
        gl.append(jnp.exp(b_last))
    a_qk = [jnp.where(incl, _dot_nt(qm[h], km[h]), 0.0) for h in heads]
    inter = [_dot_nt(qs[h], st[h]) for h in heads]
    kv = [_dot_tn(v[h], kd[h]) for h in heads]
    intra = [_dot(a_qk[h], v[h]) for h in heads]
    o = [inter[h] + intra[h] for h in heads]
    st_new = [st[h] * gl[h] + kv[h] for h in heads]
    return o, st_new


def _hg_scan_body(q_ref, f_ref, v_ref, lb_ref, s0_ref, o_ref, sf_ref, s_scr, *, rev, nc, hg):
    j = pl.program_id(1)

    @pl.when(j == 0)
    def _():
        s_scr[...] = s0_ref[...]

    c = HG_CHUNK
    masks = _tri_masks(c, rev)

    heads = range(hg)
    lanes = [slice(h * HG_DK, (h + 1) * HG_DK) for h in heads]

    def chunk(ci, carry):
        cc = (nc - 1 - ci) if rev else ci
        rows = pl.ds(pl.multiple_of(cc * c, c), c)
        o, s_new = _hg_chunk([q_ref[rows, sl] for sl in lanes], [f_ref[rows, sl] for sl in lanes],
                             [v_ref[rows, sl] for sl in lanes], [lb_ref[:, sl] for sl in lanes],
                             [s_scr[h] for h in heads], masks, rev)
        for h in heads:
            o_ref[rows, lanes[h]] = o[h]
            s_scr[h] = s_new[h]
        return carry

    lax.fori_loop(0, nc, chunk, 0)

    @pl.when(j == pl.num_programs(1) - 1)
    def _():
        sf_ref[...] = s_scr[...]


def hg_scan(p, lower, s0, direction, d, hg=16, nc=8):
    m = p.shape[0]
    h = d // HG_DK
    hg = min(hg, h)
    ng = h // hg
    nc = min(nc, m // HG_CHUNK)
    bt = nc * HG_CHUNK
    nb = m // bt
    rev = direction == 1

    def tb(j):
        return (nb - 1 - j) if rev else j

    def part(pi):
        return pl.BlockSpec((bt, hg * HG_DK), lambda g, j: (tb(j), pi * ng + g))

    st = pl.BlockSpec((hg, HG_DK, HG_DK), lambda g, j: (g, 0, 0))
    return pl.pallas_call(
        functools.partial(_hg_scan_body, rev=rev, nc=nc, hg=hg),
        out_shape=(jax.ShapeDtypeStruct((m, d), F32), jax.ShapeDtypeStruct(s0.shape, F32)),
        grid=(ng, nb),
        in_specs=[part(0), part(1 + direction), part(3),
                  pl.BlockSpec((1, hg * HG_DK), lambda g, j: (0, g)), st],
        out_specs=(pl.BlockSpec((bt, hg * HG_DK), lambda g, j: (tb(j), g)), st),
        scratch_shapes=[pltpu.VMEM((hg, HG_DK, HG_DK), F32)],
        compiler_params=_cparams('parallel', 'arbitrary'),
        name='hg_scan_bwd' if rev else 'hg_scan_fwd')(p, p, p, lower.reshape(1, d), s0)


def _shiftmix_body(x_ref, prev_ref, next_ref, mu_ref, *o_refs, width):
    i = pl.program_id(0)
    nb = pl.num_programs(0)
    bt, d = x_ref.shape
    dq = d // 4
    col = lax.broadcasted_iota(jnp.int32, (bt, dq), 0) % width
    for qi in range(4):
        sl = slice(qi * dq, (qi + 1) * dq)
        x = x_ref[:, sl]
        if qi == 0:
            sh = jnp.where(col == 0, 0.0, pltpu.roll(x, 1, 0))
        elif qi == 1:
            sh = jnp.where(col == width - 1, 0.0, pltpu.roll(x, bt - 1, 0))
        elif qi == 2:
            edge = jnp.where(i > 0, prev_ref[:, sl], 0.0)
            sh = edge if bt == width else jnp.concatenate([edge, x[:bt - width]], axis=0)
        else:
            edge = jnp.where(i < nb - 1, next_ref[:, sl], 0.0)
            sh = edge if bt == width else jnp.concatenate([x[width:], edge], axis=0)
        xx = sh - x
        for n, o_ref in enumerate(o_refs):
            o_ref[:, sl] = (x + xx * mu_ref[n:n + 1, sl]).astype(o_ref.dtype)


def shift_mix(h, mu, width, bt=128):
    m, d = h.shape
    bt = max(min(bt, m), width)
    rw = bt // width
    lastw = m // width - 1
    blk = pl.BlockSpec((bt, d), lambda i: (i, 0))
    out = jax.ShapeDtypeStruct((m, d), BF16)
    return pl.pallas_call(
        functools.partial(_shiftmix_body, width=width),
        out_shape=(out,) * 6, grid=(m // bt,),
        in_specs=[blk,
                  pl.BlockSpec((width, d), lambda i: (jnp.maximum(i * rw - 1, 0), 0)),
                  pl.BlockSpec((width, d), lambda i: (jnp.minimum((i + 1) * rw, lastw), 0)),
                  pl.BlockSpec((6, d), lambda i: (0, 0))],
        out_specs=(blk,) * 6, compiler_params=_cparams('arbitrary'), name='shift_mix')(h, h, h, mu)


def _halves(x, lo_mask):
    return jnp.concatenate([jnp.where(lo_mask, x, 0.0), jnp.where(lo_mask, 0.0, x)], axis=0)


def _head_sum(x, lo_mask):
    s_lo = jnp.sum(jnp.where(lo_mask, x, 0.0), axis=-1, keepdims=True)
    s_hi = jnp.sum(jnp.where(lo_mask, 0.0, x), axis=-1, keepdims=True)
    return jnp.where(lo_mask, s_lo, s_hi)


def _rw_intra(r, k, v, a_sig, logw, k_k, k_a, masks, tri, rev):
    lo, strict2, incl_cat = masks
    c = r[0].shape[0]
    groups = range(len(r))
    mid = (c - 1 - c // 2) if rev else c // 2
    last = 0 if rev else c - 1
    am2, bm2, km2, v2, rm, ae2, rg, bd2, kd2, gl = ([] for _ in range(10))
    for g in groups:
        kq = k[g] * k_k[g]
        kk = kq * lax.rsqrt(_head_sum(kq * kq, lo) + EPS)
        b = kk * a_sig[g]
        kd = k[g] * (1.0 + (a_sig[g] - 1.0) * k_a[g])
        cs = _cumsum_rows(logw[g], rev)
        ce = cs - logw[g]
        m = cs[mid:mid + 1, :]
        c_last = cs[last:last + 1, :]
        e_mc = jnp.exp(m - cs)
        dec = jnp.exp(c_last - cs)
        am2.append(_halves(-kk * jnp.exp(ce - m), lo))
        bm2.append(_halves(b * e_mc, lo))
        km2.append(_halves(kd * e_mc, lo))
        v2.append(_halves(v[g], lo))
        rm.append(r[g] * jnp.exp(cs - m))
        ae2.append(_halves(-kk * jnp.exp(ce), lo))
        rg.append(r[g] * jnp.exp(cs))
        bd2.append(_halves(b * dec, lo))
        kd2.append(_halves(kd * dec, lo))
        gl.append(jnp.exp(c_last))
    n = [jnp.where(strict2, _dot_nt(am2[g], bm2[g]), 0.0) for g in groups]
    a_ak = [jnp.where(strict2, _dot_nt(am2[g], km2[g]), 0.0) for g in groups]
    a_rb = [jnp.where(incl_cat, _dot_nt(rm[g], bm2[g]), 0.0) for g in groups]
    a_rk = [jnp.where(incl_cat, _dot_nt(rm[g], km2[g]), 0.0) for g in groups]
    akv = [_dot(a_ak[g], v2[g]) for g in groups]
    ov = [_dot(a_rk[g], v2[g]) for g in groups]
    kv = [_dot_tn(v2[g], kd2[g]) for g in groups]
    sol = _unit_tri_solve_many(n, [jnp.concatenate([ae2[g], akv[g]], axis=1) for g in groups],
                               tri)
    wr = [jnp.concatenate([sol[g][:, :LANES], rg[g]], axis=0) for g in groups]
    uv2 = [sol[g][:, LANES:] for g in groups]
    return wr, uv2, a_rb, ov, bd2, kv, gl


def _rw_masks(c, rev):
    lo = lax.broadcasted_iota(jnp.int32, (1, LANES), 1) < RW_HEAD
    ii = lax.broadcasted_iota(jnp.int32, (2 * c, 2 * c), 0)
    jj = lax.broadcasted_iota(jnp.int32, (2 * c, 2 * c), 1)
    same = (ii // c) == (jj // c)
    il, jl = ii % c, jj % c
    strict2 = same & ((il < jl) if rev else (il > jl))
    ic = lax.broadcasted_iota(jnp.int32, (c, 2 * c), 0)
    jc = lax.broadcasted_iota(jnp.int32, (c, 2 * c), 1) % c
    incl_cat = (ic <= jc) if rev else (ic >= jc)
    return lo, strict2, incl_cat


def _rw_scan_body(r_ref, k_ref, v_ref, a_ref, lw_ref, kk_ref, ka_ref, s0_ref, o_ref, sf_ref,
                  s_scr, wr_scr, uv_scr, arb_scr, ov_scr, bd_scr, kv_scr, gl_scr, *, rev, nc, hg):
    j = pl.program_id(1)

    @pl.when(j == 0)
    def _():
        s_scr[...] = s0_ref[...]

    c = RW_CHUNK
    masks = _rw_masks(c, rev)
    tri = _tri_levels(2 * c, c)
    groups = range(hg)
    lanes = [slice(g * LANES, (g + 1) * LANES) for g in groups]

    def intra(ci, carry):
        rows = pl.ds(pl.multiple_of(ci * c, c), c)
        wr, uv2, a_rb, ov, bd2, kv, gl = _rw_intra(
            [r_ref[rows, sl] for sl in lanes], [k_ref[rows, sl] for sl in lanes],
            [v_ref[rows, sl] for sl in lanes], [a_ref[rows, sl] for sl in lanes],
            [lw_ref[rows, sl] for sl in lanes], [kk_ref[:, sl] for sl in lanes],
            [ka_ref[:, sl] for sl in lanes], masks, tri, rev)
        for g in groups:
            wr_scr[ci, g] = wr[g].astype(BF16)
            uv_scr[ci, g] = uv2[g]
            arb_scr[ci, g] = a_rb[g].astype(BF16)
            ov_scr[ci, g] = ov[g]
            bd_scr[ci, g] = bd2[g].astype(BF16)
            kv_scr[ci, g] = kv[g]
            gl_scr[ci, g] = gl[g]
        return carry

    lax.fori_loop(0, nc, intra, 0)

    def inter(ci, carry):
        cc = (nc - 1 - ci) if rev else ci
        rows = pl.ds(pl.multiple_of(cc * c, c), c)
        st = [s_scr[g] for g in groups]
        ws = [_dot_nt(wr_scr[cc, g], st[g]) for g in groups]
        u2 = [ws[g][:2 * c] + uv_scr[cc, g] for g in groups]
        au = [_dot(arb_scr[cc, g], u2[g]) for g in groups]
        ub = [_dot_tn(u2[g], bd_scr[cc, g]) for g in groups]
        for g in groups:
            o_ref[rows, lanes[g]] = ws[g][2 * c:] + au[g] + ov_scr[cc, g]
            s_scr[g] = st[g] * gl_scr[cc, g] + kv_scr[cc, g] + ub[g]
        return carry

    lax.fori_loop(0, nc, inter, 0)

    @pl.when(j == pl.num_programs(1) - 1)
    def _():
        sf_ref[...] = s_scr[...]


def rw_scan(r, k, v, a_sig, logw, k_k, k_a, s0, direction, hg=16, nc=4):
    m, d = r.shape
    npair = d // LANES
    hg = min(hg, npair)
    ng = npair // hg
    nc = min(nc, m // RW_CHUNK)
    bt = nc * RW_CHUNK
    nb = m // bt
    rev = direction == 1

    def tb(j):
        return (nb - 1 - j) if rev else j

    tok = pl.BlockSpec((bt, hg * LANES), lambda g, j: (tb(j), g))
    row = pl.BlockSpec((1, hg * LANES), lambda g, j: (0, g))
    st = pl.BlockSpec((hg, LANES, LANES), lambda g, j: (g, 0, 0))
    return pl.pallas_call(
        functools.partial(_rw_scan_body, rev=rev, nc=nc, hg=hg),
        out_shape=(jax.ShapeDtypeStruct((m, d), F32), jax.ShapeDtypeStruct(s0.shape, F32)),
        grid=(ng, nb), in_specs=[tok, tok, tok, tok, tok, row, row, st], out_specs=(tok, st),
        scratch_shapes=[pltpu.VMEM((hg, LANES, LANES), F32),
                        pltpu.VMEM((nc, hg, 3 * RW_CHUNK, LANES), BF16),
                        pltpu.VMEM((nc, hg, 2 * RW_CHUNK, LANES), F32),
                        pltpu.VMEM((nc, hg, RW_CHUNK, 2 * RW_CHUNK), BF16),
                        pltpu.VMEM((nc, hg, RW_CHUNK, LANES), F32),
                        pltpu.VMEM((nc, hg, 2 * RW_CHUNK, LANES), BF16),
                        pltpu.VMEM((nc, hg, LANES, LANES), F32),
                        pltpu.VMEM((nc, hg, 1, LANES), F32)],
        compiler_params=_cparams('parallel', 'arbitrary'),
        name='rw_scan_bwd' if rev else 'rw_scan_fwd')(
            r, k, v, a_sig, logw, k_k.reshape(1, d), k_a.reshape(1, d), s0)


def _rwout_body(of_ref, ob_ref, r_ref, k_ref, v_ref, af_ref, ab_ref, g_ref,
                lw_ref, lb_ref, ka_ref, rk_ref, o_ref):
    lo = lax.broadcasted_iota(jnp.int32, (1, LANES), 1) < RW_HEAD
    inv_n = 1.0 / RW_HEAD
    for h in range(of_ref.shape[1] // LANES):
        sl = slice(h * LANES, (h + 1) * LANES)
        o = of_ref[:, sl] + ob_ref[:, sl]
        oc = o - _head_sum(o, lo) * inv_n
        y = oc * lax.rsqrt(_head_sum(oc * oc, lo) * inv_n + RW_LN_EPS)
        y = y * lw_ref[:, sl] + lb_ref[:, sl]
        k_sum = k_ref[:, sl] * (2.0 + (af_ref[:, sl] + ab_ref[:, sl] - 2.0) * ka_ref[:, sl])
        bonus = _head_sum(r_ref[:, sl] * k_sum * rk_ref[:, sl], lo) * v_ref[:, sl]
        o_ref[:, sl] = ((y + bonus) * g_ref[:, sl]).astype(o_ref.dtype)


def rw_out(o_f, o_b, r, k, v, a_f, a_b, g, lnx_w, lnx_b, k_a, r_k, bt=256, bc=512):
    m, d = o_f.shape
    bt, bc = _fit(m, bt, 8), _fit(d, bc)
    blk = pl.BlockSpec((bt, bc), lambda i, j: (i, j))
    row = pl.BlockSpec((1, bc), lambda i, j: (0, j))
    rows = [t.reshape(1, d) for t in (lnx_w, lnx_b, k_a, r_k)]
    return pl.pallas_call(
        _rwout_body, out_shape=jax.ShapeDtypeStruct((m, d), BF16), grid=(m // bt, d // bc),
        in_specs=[blk] * 8 + [row] * 4, out_specs=blk,
        compiler_params=_cparams('parallel', 'parallel'), name='rw_out')(
            o_f, o_b, r, k, v, a_f, a_b, g, *rows)


def _convglu_body(u_ref, prev_ref, next_ref, v_ref, cw_ref, cb_ref, o_ref, *, width):
    i = pl.program_id(0)
    nb = pl.num_programs(0)
    x = u_ref[...].astype(F32)
    bt = x.shape[0]
    up = jnp.where(i > 0, prev_ref[...].astype(F32), 0.0)
    dn = jnp.where(i < nb - 1, next_ref[...].astype(F32), 0.0)
    if bt > width:
        up = jnp.concatenate([up, x[:bt - width]], axis=0)
        dn = jnp.concatenate([x[width:], dn], axis=0)
    col = lax.broadcasted_iota(jnp.int32, x.shape, 0) % width
    rows = (up, x, dn)

    def column(kw):
        return sum(cw_ref[3 * r + kw:3 * r + kw + 1, :] * rows[r] for r in range(3))

    left = jnp.where(col == 0, 0.0, pltpu.roll(column(0), 1, 0))
    right = jnp.where(col == width - 1, 0.0, pltpu.roll(column(2), bt - 1, 0))
    y = column(1) + left + right + cb_ref[...]
    o_ref[...] = (_gelu_tanh(y) * v_ref[...].astype(F32)).astype(o_ref.dtype)


def conv_glu_gate(uv, conv_w, conv_b, width, f, bt=512, bc=512):
    m = uv.shape[0]
    bt = max(min(bt, m), width)
    bc = _fit(f, bc)
    rw = bt // width
    lastw = m // width - 1
    voff = f // bc
    return pl.pallas_call(
        functools.partial(_convglu_body, width=width),
        out_shape=jax.ShapeDtypeStruct((m, f), BF16), grid=(m // bt, f // bc),
        in_specs=[pl.BlockSpec((bt, bc), lambda i, j: (i, j)),
                  pl.BlockSpec((width, bc), lambda i, j: (jnp.maximum(i * rw - 1, 0), j)),
                  pl.BlockSpec((width, bc), lambda i, j: (jnp.minimum((i + 1) * rw, lastw), j)),
                  pl.BlockSpec((bt, bc), lambda i, j: (i, voff + j)),
                  pl.BlockSpec((9, bc), lambda i, j: (0, j)),
                  pl.BlockSpec((1, bc), lambda i, j: (0, j))],
        out_specs=pl.BlockSpec((bt, bc), lambda i, j: (i, j)),
        compiler_params=_cparams('parallel', 'parallel'), name='conv_glu')(
            uv, uv, uv, uv, conv_w.reshape(9, f), conv_b.reshape(1, f))


def _modulation(conds, down, up, bias):
    low = matmul(conds, down, precise=True, bm=8, bn=512, name='ada_down')
    return matmul(low, up, bias=bias, precise=True, bm=8, bn=2048, name='ada_up')


def _dn_mixer(streams, w, d, last):
    heads = d // DN_DK
    w_main = w['w_in'][:, :4 * d].astype(BF16)
    w_ab = w['w_in'][:, 4 * d:].astype(BF16)
    gain = jnp.tile(w['norm_g'], heads)
    prepped = []
    for h in streams:
        p = matmul(h, w_main, name='dn_in')
        pab = matmul(h, w_ab, bn=4 * heads, name='dn_in_ab')
        q = dn_conv(p, w['conv_w'], 0, d, 'q')
        k = dn_conv(p, w['conv_w'], 1, d, 'k')
        v = dn_conv(p, w['conv_w'], 2, d, 'v')
        gb = dn_gates(pab, w['a_log'], w['dt_bias'])
        prepped.append((p, q, k, v, gb.T))
    states = [jnp.zeros((heads, DN_DK, DN_DK), F32)] * 2
    outs = []
    for si, (p, q, k, v, gbt) in enumerate(prepped):
        o = []
        for direction in range(2):
            od, states[direction] = dn_scan(q, k, v, gbt, states[direction], direction)
            o.append(od)
        if si == 0 and last:
            outs.append(None)
        else:
            outs.append(gate_out(o[0], o[1], p, 3, gain))
    return outs


def _hg_mixer(streams, w, d, last):
    heads = d // HG_DK
    w_in = w['w_in'].astype(BF16)
    states = [jnp.zeros((heads, HG_DK, HG_DK), F32)] * 2
    outs = []
    for si, h in enumerate(streams):
        p = matmul(h, w_in, name='hg_in')
        o = []
        for direction in range(2):
            od, states[direction] = hg_scan(p, w['lower'], states[direction], direction, d)
            o.append(od)
        if si == 0 and last:
            outs.append(None)
        else:
            outs.append(gate_out(o[0], o[1], p, 4, w['norm_g']))
    return outs


def _rw_mixer(streams, widths, w, d, last):
    w_rkv = w['w_rkv'].astype(BF16)
    w1 = jnp.concatenate([w['w1'][0], w['w1'][1]], axis=1).astype(BF16)
    a1 = jnp.concatenate([w['a1'][0], w['a1'][1]], axis=1).astype(BF16)
    w2 = w['w2'].astype(BF16)
    a2 = w['a2'].astype(BF16)
    lora = w['w1'].shape[-1]
    gl = w['g1'].shape[-1]
    glp = -(-gl // LANES) * LANES
    g1 = jnp.pad(w['g1'], ((0, 0), (0, glp - gl))).astype(BF16)
    g2 = jnp.pad(w['g2'], ((0, glp - gl), (0, 0))).astype(BF16)
    states = [jnp.zeros((d // LANES, LANES, LANES), F32)] * 2
    outs = []
    for si, (h, width) in enumerate(zip(streams, widths)):
        xr, xw, xk, xv, xa, xg = shift_mix(h, w['mu'], width)
        r = matmul(xr, w_rkv[0], name='rw_r')
        k = matmul(xk, w_rkv[1], name='rw_k')
        v = matmul(xv, w_rkv[2], name='rw_v')
        tw = matmul(xw, w1, epi='tanh', out_dtype=BF16, bn=2 * lora, name='rw_w1')
        ta = matmul(xa, a1, out_dtype=BF16, bn=2 * lora, name='rw_a1')
        o, a_sig = [], []
        for direction in range(2):
            logw = matmul(tw, w2[direction], bias=w['w0'][direction], epi='logw',
                          a_koff=direction, name='rw_w2')
            a_d = matmul(ta, a2[direction], bias=w['a0'][direction], epi='sigmoid',
                         a_koff=direction, name='rw_a2')
            od, states[direction] = rw_scan(r, k, v, a_d, logw, w['k_k'], w['k_a'],
                                            states[direction], direction)
            o.append(od)
            a_sig.append(a_d)
        if si == 0 and last:
            outs.append(None)
            continue
        tg = matmul(xg, g1, epi='sigmoid', out_dtype=BF16, name='rw_g1')
        g = matmul(tg, g2, name='rw_g2')
        outs.append(rw_out(o[0], o[1], r, k, v, a_sig[0], a_sig[1], g, w['lnx_w'], w['lnx_b'],
                           w['k_a'], w['r_k'].reshape(-1)))
    return outs


def kernel(x, c, ctx, c_ctx, ada_down, ada_up, ada_b, norm1_g, norm2_g, ffn_w_up, ffn_conv_w, ffn_conv_b, ffn_w_down, dn_w_in, dn_conv_w, dn_a_log, dn_dt_bias, dn_norm_g, dn_w_out, hg_w_in, hg_lower, hg_norm_g, hg_w_out, rw_mu, rw_w_rkv, rw_w0, rw_w1, rw_w2, rw_a0, rw_a1, rw_a2, rw_g1, rw_g2, rw_k_k, rw_k_a, rw_r_k, rw_lnx_w, rw_lnx_b, rw_w_out, final_g):
    _, seq, d = x.shape
    depth = ada_down.shape[0]
    n_ctx = ctx.shape[1]
    f = ffn_w_down.shape[1]
    xl, xc = x[0], ctx[0]
    sm = jax.nn.softmax(hg_lower.astype(F32), axis=0)
    lower_bounds = jnp.cumsum(sm, axis=0) - sm[0]
    conds = jnp.zeros((8, d), F32).at[0].set(jax.nn.silu(c[0])).at[1].set(jax.nn.silu(c_ctx))
    widths = (n_ctx, GRID_W)
    for i in range(depth):
        kind, j = i % 3, i // 3
        last = i == depth - 1
        mod = _modulation(conds, ada_down[i], ada_up[i], ada_b[i])
        mods = [[mod[row, n * d:(n + 1) * d] for n in range(6)] for row in (1, 0)]
        xs = [xc, xl]
        h_dtype = F32 if kind == 2 else BF16
        hs = [normmod(t, norm1_g[i], mm[1], mm[0], h_dtype) for t, mm in zip(xs, mods)]
        if kind == 0:
            w = dict(w_in=dn_w_in[j], conv_w=dn_conv_w[j], a_log=dn_a_log[j],
                     dt_bias=dn_dt_bias[j], norm_g=dn_norm_g[j])
            ys = _dn_mixer(hs, w, d, last)
            w_out = dn_w_out[j]
        elif kind == 1:
            w = dict(w_in=hg_w_in[j], lower=lower_bounds[i], norm_g=hg_norm_g[j])
            ys = _hg_mixer(hs, w, d, last)
            w_out = hg_w_out[j]
        else:
            w = dict(mu=rw_mu[j], w_rkv=rw_w_rkv[j], w0=rw_w0[j], w1=rw_w1[j], w2=rw_w2[j],
                     a0=rw_a0[j], a1=rw_a1[j], a2=rw_a2[j], g1=rw_g1[j], g2=rw_g2[j],
                     k_k=rw_k_k[j], k_a=rw_k_a[j], r_k=rw_r_k[j], lnx_w=rw_lnx_w[j],
                     lnx_b=rw_lnx_b[j])
            ys = _rw_mixer(hs, widths, w, d, last)
            w_out = rw_w_out[j]
        w_out = w_out.astype(BF16)
        w_up = ffn_w_up[i].astype(BF16)
        w_down = ffn_w_down[i].astype(BF16)
        new = []
        for t, y, mm, width in zip(xs, ys, mods, widths):
            if y is None:
                new.append(t)
                continue
            t = matmul(y, w_out, gate=mm[2], resid=t, name='mix_out')
            h2 = normmod(t, norm2_g[i], mm[4], mm[3], BF16)
            uv = matmul(h2, w_up, out_dtype=BF16, name='ffn_up')
            gated = conv_glu_gate(uv, ffn_conv_w[i], ffn_conv_b[i], width, f)
            t = matmul(gated, w_down, gate=mm[5], resid=t, bk=2048, name='ffn_down')
            new.append(t)
        xc, xl = new
    zeros = jnp.zeros((d,), F32)
    return normmod(xl, final_g, zeros, zeros, F32)[None]
```

```python
import functools
import math

import jax
import jax.numpy as jnp
from jax import lax
from jax.experimental import pallas as pl
from jax.experimental.pallas import tpu as pltpu

F32 = jnp.float32
BF16 = jnp.bfloat16

EPS = 1e-6
RW_LN_EPS = 64e-5
GRID_W = 64
DN_DK = 128
DN_CHUNK = 128
HG_DK = 128
HG_CHUNK = 32
RW_HEAD = 64
RW_CHUNK = 64
LANES = 128
VMEM_LIMIT = 56 * 1024 * 1024


def _fit(n, b, unit=LANES):
    if n <= b:
        return n
    for cand in range(b - b % unit, 0, -unit):
        if n % cand == 0:
            return cand
    raise ValueError((n, b, unit))


def _cparams(*sem):
    return pltpu.CompilerParams(dimension_semantics=sem, vmem_limit_bytes=VMEM_LIMIT)


def _sigmoid(x):
    return 1.0 / (1.0 + jnp.exp(-x))


def _silu(x):
    return x * _sigmoid(x)


def _softplus(x):
    return jnp.maximum(x, 0.0) + jnp.log(1.0 + jnp.exp(-jnp.abs(x)))


def _gelu_tanh(x):
    c = math.sqrt(2.0 / math.pi)
    return 0.5 * x * (1.0 + jnp.tanh(c * (x + 0.044715 * (x * x * x))))


def _dot(a, b, dims=(((1,), (0,)), ((), ()))):
    return lax.dot_general(a.astype(BF16), b.astype(BF16), dims, preferred_element_type=F32)


def _dot_nt(a, b):
    return _dot(a, b, (((1,), (1,)), ((), ())))


def _dot_tn(a, b):
    return _dot(a, b, (((0,), (0,)), ((), ())))


def _split2(x):
    hi = x.astype(BF16)
    lo = (x - hi.astype(F32)).astype(BF16)
    return hi, lo


def _dot3(a, b, dims=(((1,), (0,)), ((), ()))):
    ah, al = _split2(a)
    bh, bl = _split2(b)
    d = functools.partial(lax.dot_general, dimension_numbers=dims, preferred_element_type=F32)
    return d(ah, bh) + (d(ah, bl) + d(al, bh))


def _tri_levels(c, size):
    ii = lax.broadcasted_iota(jnp.int32, (c, c), 0)
    jj = lax.broadcasted_iota(jnp.int32, (c, c), 1)
    levels = []
    s = 1
    while (2 << s) <= size * 2 and (1 << s) < size:
        lo = (ii >> s) != (jj >> s)
        hi = (ii >> (s + 1)) == (jj >> (s + 1))
        levels.append(lo & hi)
        s += 1
    return (ii >> 1) == (jj >> 1), ii == jj, levels


def _unit_tri_solve_many(ns, rhss, tri):
    pair, eye, levels = tri
    units = range(len(ns))
    ts = [jnp.where(pair, jnp.where(eye, 1.0, n), 0.0) for n in ns]
    for lmask in levels:
        ls = [jnp.where(lmask, n, 0.0) for n in ns]
        lt = [_dot(ls[i], ts[i]) for i in units]
        ts = [ts[i] + _dot(ts[i], lt[i]) for i in units]
    return [_dot(ts[i], rhss[i]) for i in units]


def _cumsum_rows(x, rev):
    c = x.shape[0]
    row = lax.broadcasted_iota(jnp.int32, x.shape, 0)
    s = 1
    while s < c:
        if rev:
            x = x + jnp.where(row < c - s, pltpu.roll(x, c - s, 0), 0.0)
        else:
            x = x + jnp.where(row >= s, pltpu.roll(x, s, 0), 0.0)
        s *= 2
    return x


def _epi_none(y):
    return y


def _epi_logw(y):
    return -jnp.exp(-_softplus(-y) - 0.5)


_EPILOGUES = {'none': _epi_none, 'tanh': jnp.tanh, 'sigmoid': _sigmoid, 'logw': _epi_logw}


def _mm_body(*refs, nk, epi, has_bias, has_gate, has_resid, precise):
    it = iter(refs)
    a_ref, w_ref = next(it), next(it)
    bias_ref = next(it) if has_bias else None
    gate_ref = next(it) if has_gate else None
    resid_ref = next(it) if has_resid else None
    o_ref = next(it)
    acc_ref = next(it) if nk > 1 else None

    def finish(y):
        if has_bias:
            y = y + bias_ref[...]
        y = _EPILOGUES[epi](y)
        if has_gate:
            y = y * gate_ref[...]
        if has_resid:
            y = resid_ref[...] + y
        o_ref[...] = y.astype(o_ref.dtype)

    if precise:
        part = _dot3(a_ref[...], w_ref[...])
    else:
        part = _dot(a_ref[...], w_ref[...])
    if nk == 1:
        finish(part)
    else:
        k = pl.program_id(2)

        @pl.when(k == 0)
        def _():
            acc_ref[...] = part

        @pl.when(k > 0)
        def _():
            acc_ref[...] += part

        @pl.when(k == nk - 1)
        def _():
            finish(acc_ref[...])


def matmul(a, w, *, bias=None, gate=None, resid=None, epi='none', out_dtype=F32,
           bm=1024, bn=1024, bk=None, a_koff=0, precise=False, name='matmul'):
    m = a.shape[0]
    k, n = w.shape
    bm, bn = _fit(m, bm, 8), _fit(n, bn)
    bk = k if bk is None else _fit(k, bk)
    nk = k // bk
    if nk == 1:
        grid = (n // bn, m // bm)
        a_spec = pl.BlockSpec((bm, bk), lambda j, i: (i, a_koff))
        w_spec = pl.BlockSpec((bk, bn), lambda j, i: (0, j))
        row_spec = pl.BlockSpec((1, bn), lambda j, i: (0, j))
        o_spec = pl.BlockSpec((bm, bn), lambda j, i: (i, j))
        sem = ('parallel', 'parallel')
        scratch = []
    else:
        grid = (n // bn, m // bm, nk)
        a_spec = pl.BlockSpec((bm, bk), lambda j, i, kk: (i, kk + a_koff * nk))
        w_spec = pl.BlockSpec((bk, bn), lambda j, i, kk: (kk, j))
        row_spec = pl.BlockSpec((1, bn), lambda j, i, kk: (0, j))
        o_spec = pl.BlockSpec((bm, bn), lambda j, i, kk: (i, j))
        sem = ('parallel', 'parallel', 'arbitrary')
        scratch = [pltpu.VMEM((bm, bn), F32)]
    args, specs = [a, w], [a_spec, w_spec]
    for extra in (bias, gate):
        if extra is not None:
            args.append(extra.reshape(1, n).astype(F32))
            specs.append(row_spec)
    if resid is not None:
        args.append(resid)
        specs.append(o_spec)
    body = functools.partial(_mm_body, nk=nk, epi=epi, has_bias=bias is not None,
                             has_gate=gate is not None, has_resid=resid is not None,
                             precise=precise)
    return pl.pallas_call(
        body, out_shape=jax.ShapeDtypeStruct((m, n), out_dtype), grid=grid,
        in_specs=specs, out_specs=o_spec, scratch_shapes=scratch,
        compiler_params=_cparams(*sem), name=name)(*args)


def _normmod_body(x_ref, g_ref, sc_ref, sh_ref, o_ref):
    x = x_ref[...]
    y = x * lax.rsqrt(jnp.mean(x * x, axis=-1, keepdims=True) + EPS) * g_ref[...]
    o_ref[...] = (y * (1.0 + sc_ref[...]) + sh_ref[...]).astype(o_ref.dtype)


def normmod(x, g, sc, sh, out_dtype, bt=256):
    m, d = x.shape
    bt = min(bt, m)
    row = pl.BlockSpec((1, d), lambda i: (0, 0))
    blk = pl.BlockSpec((bt, d), lambda i: (i, 0))
    return pl.pallas_call(
        _normmod_body, out_shape=jax.ShapeDtypeStruct((m, d), out_dtype), grid=(m // bt,),
        in_specs=[blk, row, row, row], out_specs=blk,
        compiler_params=_cparams('parallel'), name='normmod')(
            x, g.reshape(1, d), sc.reshape(1, d), sh.reshape(1, d))


def _dnconv_body(x_ref, prev_ref, next_ref, w_ref, o_ref, *, mode):
    i = pl.program_id(0)
    nb = pl.num_programs(0)
    x = x_ref[...]
    bt = x.shape[0]
    row = lax.broadcasted_iota(jnp.int32, x.shape, 0)
    prev_row = jnp.where(i > 0, prev_ref[7:8, :], 0.0)
    next_row = jnp.where(i < nb - 1, next_ref[0:1, :], 0.0)
    xm = jnp.where(row == 0, prev_row, pltpu.roll(x, 1, 0))
    xp = jnp.where(row == bt - 1, next_row, pltpu.roll(x, bt - 1, 0))
    y = w_ref[0:1, :] * xm + w_ref[1:2, :] * x + w_ref[2:3, :] * xp
    y = _silu(y)
    if mode == 'v':
        o_ref[...] = y.astype(o_ref.dtype)
        return
    scale = DN_DK ** -0.5 if mode == 'q' else 1.0
    for h in range(x.shape[1] // DN_DK):
        sl = slice(h * DN_DK, (h + 1) * DN_DK)
        yh = y[:, sl]
        inv = lax.rsqrt(jnp.sum(yh * yh, axis=-1, keepdims=True) + EPS)
        o_ref[:, sl] = (yh * (inv * scale)).astype(o_ref.dtype)


def dn_conv(p, conv_w, part, d, mode, bt=512, bc=512):
    m = p.shape[0]
    bt, bc = _fit(m, bt, 8), _fit(d, bc)
    off = part * d // bc
    last8 = m // 8 - 1
    r8 = bt // 8
    return pl.pallas_call(
        functools.partial(_dnconv_body, mode=mode),
        out_shape=jax.ShapeDtypeStruct((m, d), F32), grid=(m // bt, d // bc),
        in_specs=[pl.BlockSpec((bt, bc), lambda i, j: (i, off + j)),
                  pl.BlockSpec((8, bc), lambda i, j: (jnp.maximum(i * r8 - 1, 0), off + j)),
                  pl.BlockSpec((8, bc), lambda i, j: (jnp.minimum((i + 1) * r8, last8), off + j)),
                  pl.BlockSpec((3, bc), lambda i, j: (0, off + j))],
        out_specs=pl.BlockSpec((bt, bc), lambda i, j: (i, j)),
        compiler_params=_cparams('parallel', 'parallel'), name='dn_conv_' + mode)(p, p, p, conv_w)


def _dnab_body(x_ref, alog_ref, dtb_ref, o_ref):
    x = x_ref[...]
    lane = lax.broadcasted_iota(jnp.int32, x.shape, 1)
    g = -jnp.exp(alog_ref[...]) * _softplus(x + dtb_ref[...])
    o_ref[...] = jnp.where(lane < x.shape[1] // 2, g, _sigmoid(x))


def dn_gates(pab, a_log, dt_bias, bt=1024):
    m, w = pab.shape
    bt = min(bt, m)
    zeros = jnp.zeros((w // 2,), F32)
    alog = jnp.concatenate([a_log.reshape(-1), zeros]).reshape(1, w)
    dtb = jnp.concatenate([dt_bias.reshape(-1), zeros]).reshape(1, w)
    row = pl.BlockSpec((1, w), lambda i: (0, 0))
    blk = pl.BlockSpec((bt, w), lambda i: (i, 0))
    return pl.pallas_call(
        _dnab_body, out_shape=jax.ShapeDtypeStruct((m, w), F32), grid=(m // bt,),
        in_specs=[blk, row, row], out_specs=blk,
        compiler_params=_cparams('parallel'), name='dn_gates')(pab, alog, dtb)


def _tri_masks(c, rev):
    ii = lax.broadcasted_iota(jnp.int32, (c, c), 0)
    jj = lax.broadcasted_iota(jnp.int32, (c, c), 1)
    if rev:
        return ii == jj, ii <= jj, ii < jj
    return ii == jj, ii >= jj, ii > jj


def _dn_intra(q, k, v, g_row, beta_row, masks, tri):
    eye, incl, strict = masks
    c = q[0].shape[0]
    heads = range(len(q))
    gc_col, beta_col, g_tot, decay = [], [], [], []
    for h in heads:
        g_bc = jnp.broadcast_to(g_row[h], (c, c))
        gcc = jnp.sum(jnp.where(incl, g_bc, 0.0), axis=1, keepdims=True)
        gcr = jnp.sum(jnp.where(eye, jnp.broadcast_to(gcc, (c, c)), 0.0), axis=0, keepdims=True)
        gc_col.append(gcc)
        beta_col.append(jnp.sum(jnp.where(eye, jnp.broadcast_to(beta_row[h], (c, c)), 0.0),
                                axis=1, keepdims=True))
        g_tot.append(jnp.sum(g_row[h], axis=1, keepdims=True))
        decay.append(jnp.where(incl, jnp.exp(jnp.where(incl, gcc - gcr, 0.0)), 0.0))
    kb = [k[h] * beta_col[h] for h in heads]
    kk = [_dot_nt(kb[h], k[h]) for h in heads]
    qk = [_dot_nt(q[h], k[h]) for h in heads]
    n = [-jnp.where(strict, kk[h] * decay[h], 0.0) for h in heads]
    egc = [jnp.exp(gc_col[h]) for h in heads]
    rhs = [jnp.concatenate([kb[h] * egc[h], v[h] * beta_col[h]], axis=1) for h in heads]
    sol = _unit_tri_solve_many(n, rhs, tri)
    wq = [jnp.concatenate([sol[h][:, :DN_DK], q[h] * egc[h]], axis=0) for h in heads]
    u = [sol[h][:, DN_DK:] for h in heads]
    attn = [jnp.where(incl, qk[h] * decay[h], 0.0) for h in heads]
    kd = [k[h] * jnp.exp(g_tot[h] - gc_col[h]) for h in heads]
    gl = [jnp.exp(g_tot[h]) for h in heads]
    return wq, u, attn, kd, gl


def _dn_scan_body(q_ref, k_ref, v_ref, g_ref, b_ref, s0_ref, o_ref, sf_ref,
                  s_scr, wq_scr, u_scr, at_scr, kd_scr, gl_scr, *, rev, nc, hg):
    j = pl.program_id(1)

    @pl.when(j == 0)
    def _():
        s_scr[...] = s0_ref[...]

    c = DN_CHUNK
    masks = _tri_masks(c, rev)
    tri = _tri_levels(c, c)
    heads = range(hg)
    lanes = [slice(h * DN_DK, (h + 1) * DN_DK) for h in heads]

    def intra(ci, carry):
        rows = pl.ds(pl.multiple_of(ci * c, c), c)
        wq, u, attn, kd, gl = _dn_intra(
            [q_ref[rows, sl] for sl in lanes], [k_ref[rows, sl] for sl in lanes],
            [v_ref[rows, sl] for sl in lanes], [g_ref[ci, h:h + 1, :] for h in heads],
            [b_ref[ci, h:h + 1, :] for h in heads], masks, tri)
        for h in heads:
            wq_scr[ci, h] = wq[h].astype(BF16)
            u_scr[ci, h] = u[h]
            at_scr[ci, h] = attn[h].astype(BF16)
            kd_scr[ci, h] = kd[h].astype(BF16)
            gl_scr[ci, h] = jnp.broadcast_to(gl[h], (1, DN_DK))
        return carry

    lax.fori_loop(0, nc, intra, 0)

    def inter(ci, carry):
        cc = (nc - 1 - ci) if rev else ci
        rows = pl.ds(pl.multiple_of(cc * c, c), c)
        s = [s_scr[h] for h in heads]
        ws = [_dot(wq_scr[cc, h], s[h]) for h in heads]
        v_new = [u_scr[cc, h] - ws[h][:c] for h in heads]
        av = [_dot(at_scr[cc, h], v_new[h]) for h in heads]
        kv = [_dot_tn(kd_scr[cc, h], v_new[h]) for h in heads]
        for h in heads:
            o_ref[rows, lanes[h]] = ws[h][c:] + av[h]
            s_scr[h] = s[h] * gl_scr[cc, h] + kv[h]
        return carry

    lax.fori_loop(0, nc, inter, 0)

    @pl.when(j == pl.num_programs(1) - 1)
    def _():
        sf_ref[...] = s_scr[...]


def dn_scan(q, k, v, gbt, s0, direction, hg=16, nc=2):
    m, d = q.shape
    h = d // DN_DK
    hg = min(hg, h)
    ng = h // hg
    nc = min(nc, m // DN_CHUNK)
    bt = nc * DN_CHUNK
    nb = m // bt
    rev = direction == 1
    gb4 = gbt.reshape(4 * ng, hg, m // DN_CHUNK, DN_CHUNK).transpose(0, 2, 1, 3)

    def tb(j):
        return (nb - 1 - j) if rev else j

    tok = pl.BlockSpec((bt, hg * DN_DK), lambda g, j: (tb(j), g))
    st = pl.BlockSpec((hg, DN_DK, DN_DK), lambda g, j: (g, 0, 0))
    g_spec = pl.BlockSpec((None, nc, hg, DN_CHUNK),
                          lambda g, j: (direction * ng + g, tb(j), 0, 0))
    b_spec = pl.BlockSpec((None, nc, hg, DN_CHUNK),
                          lambda g, j: ((2 + direction) * ng + g, tb(j), 0, 0))
    return pl.pallas_call(
        functools.partial(_dn_scan_body, rev=rev, nc=nc, hg=hg),
        out_shape=(jax.ShapeDtypeStruct((m, d), F32), jax.ShapeDtypeStruct(s0.shape, F32)),
        grid=(ng, nb), in_specs=[tok, tok, tok, g_spec, b_spec, st], out_specs=(tok, st),
        scratch_shapes=[pltpu.VMEM((hg, DN_DK, DN_DK), F32),
                        pltpu.VMEM((nc, hg, 2 * DN_CHUNK, DN_DK), BF16),
                        pltpu.VMEM((nc, hg, DN_CHUNK, DN_DK), F32),
                        pltpu.VMEM((nc, hg, DN_CHUNK, DN_CHUNK), BF16),
                        pltpu.VMEM((nc, hg, DN_CHUNK, DN_DK), BF16),
                        pltpu.VMEM((nc, hg, 1, DN_DK), F32)],
        compiler_params=_cparams('parallel', 'arbitrary'),
        name='dn_scan_bwd' if rev else 'dn_scan_fwd')(q, k, v, gb4, gb4, s0)


def _gateout_body(of_ref, ob_ref, z_ref, g_ref, o_ref):
    o = of_ref[...] + ob_ref[...]
    z = z_ref[...]
    for h in range(o.shape[1] // LANES):
        sl = slice(h * LANES, (h + 1) * LANES)
        oh = o[:, sl]
        y = oh * lax.rsqrt(jnp.mean(oh * oh, axis=-1, keepdims=True) + EPS) * g_ref[:, sl]
        o_ref[:, sl] = (y * _silu(z[:, sl])).astype(o_ref.dtype)


def gate_out(o_f, o_b, p, z_part, gain, bt=512, bc=512):
    m, d = o_f.shape
    bt, bc = _fit(m, bt, 8), _fit(d, bc)
    off = z_part * d // bc
    blk = pl.BlockSpec((bt, bc), lambda i, j: (i, j))
    return pl.pallas_call(
        _gateout_body, out_shape=jax.ShapeDtypeStruct((m, d), BF16), grid=(m // bt, d // bc),
        in_specs=[blk, blk, pl.BlockSpec((bt, bc), lambda i, j: (i, off + j)),
                  pl.BlockSpec((1, bc), lambda i, j: (0, j))],
        out_specs=blk, compiler_params=_cparams('parallel', 'parallel'),
        name='gate_out')(o_f, o_b, p, gain.reshape(1, d))


def _hg_chunk(q_raw, f_raw, v, lb, st, masks, rev):
    _, incl, _ = masks
    c = q_raw[0].shape[0]
    heads = range(len(q_raw))
    mid = (c - 1 - c // 2) if rev else c // 2
    last = 0 if rev else c - 1
    qm, km, qs, kd, gl = [], [], [], [], []
    for h in heads:
        q = _silu(q_raw[h])
        f = lb[h] + (1.0 - lb[h]) * _sigmoid(f_raw[h])
        k = 1.0 - f
        b = _cumsum_rows(jnp.log(f), rev)
        m = b[mid:mid + 1, :]
        b_last = b[last:last + 1, :]
        qm.append(q * jnp.exp(b - m))
        km.append(k * jnp.exp(m - b))
        qs.append(q * jnp.exp(b))
        kd.append(k * jnp.exp(b_last - b))
        gl.append(jnp.exp(b_last))
    a_qk = [jnp.where(incl, _dot_nt(qm[h], km[h]), 0.0) for h in heads]
    inter = [_dot_nt(qs[h], st[h]) for h in heads]
    kv = [_dot_tn(v[h], kd[h]) for h in heads]
    intra = [_dot(a_qk[h], v[h]) for h in heads]
    o = [inter[h] + intra[h] for h in heads]
    st_new = [st[h] * gl[h] + kv[h] for h in heads]
    return o, st_new


def _hg_scan_body(q_ref, f_ref, v_ref, lb_ref, s0_ref, o_ref, sf_ref, s_scr, *, rev, nc, hg):
    j = pl.program_id(1)

    @pl.when(j == 0)
    def _():
        s_scr[...] = s0_ref[...]

    c = HG_CHUNK
    masks = _tri_masks(c, rev)

    heads = range(hg)
    lanes = [slice(h * HG_DK, (h + 1) * HG_DK) for h in heads]

    def chunk(ci, carry):
        cc = (nc - 1 - ci) if rev else ci
        rows = pl.ds(pl.multiple_of(cc * c, c), c)
        o, s_new = _hg_chunk([q_ref[rows, sl] for sl in lanes], [f_ref[rows, sl] for sl in lanes],
                             [v_ref[rows, sl] for sl in lanes], [lb_ref[:, sl] for sl in lanes],
                             [s_scr[h] for h in heads], masks, rev)
        for h in heads:
            o_ref[rows, lanes[h]] = o[h]
            s_scr[h] = s_new[h]
        return carry

    lax.fori_loop(0, nc, chunk, 0)

    @pl.when(j == pl.num_programs(1) - 1)
    def _():
        sf_ref[...] = s_scr[...]


def hg_scan(p, lower, s0, direction, d, hg=16, nc=8):
    m = p.shape[0]
    h = d // HG_DK
    hg = min(hg, h)
    ng = h // hg
    nc = min(nc, m // HG_CHUNK)
    bt = nc * HG_CHUNK
    nb = m // bt
    rev = direction == 1

    def tb(j):
        return (nb - 1 - j) if rev else j

    def part(pi):
        return pl.BlockSpec((bt, hg * HG_DK), lambda g, j: (tb(j), pi * ng + g))

    st = pl.BlockSpec((hg, HG_DK, HG_DK), lambda g, j: (g, 0, 0))
    return pl.pallas_call(
        functools.partial(_hg_scan_body, rev=rev, nc=nc, hg=hg),
        out_shape=(jax.ShapeDtypeStruct((m, d), F32), jax.ShapeDtypeStruct(s0.shape, F32)),
        grid=(ng, nb),
        in_specs=[part(0), part(1 + direction), part(3),
                  pl.BlockSpec((1, hg * HG_DK), lambda g, j: (0, g)), st],
        out_specs=(pl.BlockSpec((bt, hg * HG_DK), lambda g, j: (tb(j), g)), st),
        scratch_shapes=[pltpu.VMEM((hg, HG_DK, HG_DK), F32)],
        compiler_params=_cparams('parallel', 'arbitrary'),
        name='hg_scan_bwd' if rev else 'hg_scan_fwd')(p, p, p, lower.reshape(1, d), s0)


def _shiftmix_body(x_ref, prev_ref, next_ref, mu_ref, *o_refs, width):
    i = pl.program_id(0)
    nb = pl.num_programs(0)
    bt, d = x_ref.shape
    dq = d // 4
    col = lax.broadcasted_iota(jnp.int32, (bt, dq), 0) % width
    for qi in range(4):
        sl = slice(qi * dq, (qi + 1) * dq)
        x = x_ref[:, sl]
        if qi == 0:
            sh = jnp.where(col == 0, 0.0, pltpu.roll(x, 1, 0))
        elif qi == 1:
            sh = jnp.where(col == width - 1, 0.0, pltpu.roll(x, bt - 1, 0))
        elif qi == 2:
            edge = jnp.where(i > 0, prev_ref[:, sl], 0.0)
            sh = edge if bt == width else jnp.concatenate([edge, x[:bt - width]], axis=0)
        else:
            edge = jnp.where(i < nb - 1, next_ref[:, sl], 0.0)
            sh = edge if bt == width else jnp.concatenate([x[width:], edge], axis=0)
        xx = sh - x
        for n, o_ref in enumerate(o_refs):
            o_ref[:, sl] = (x + xx * mu_ref[n:n + 1, sl]).astype(o_ref.dtype)


def shift_mix(h, mu, width, bt=128):
    m, d = h.shape
    bt = max(min(bt, m), width)
    rw = bt // width
    lastw = m // width - 1
    blk = pl.BlockSpec((bt, d), lambda i: (i, 0))
    out = jax.ShapeDtypeStruct((m, d), BF16)
    return pl.pallas_call(
        functools.partial(_shiftmix_body, width=width),
        out_shape=(out,) * 6, grid=(m // bt,),
        in_specs=[blk,
                  pl.BlockSpec((width, d), lambda i: (jnp.maximum(i * rw - 1, 0), 0)),
                  pl.BlockSpec((width, d), lambda i: (jnp.minimum((i + 1) * rw, lastw), 0)),
                  pl.BlockSpec((6, d), lambda i: (0, 0))],
        out_specs=(blk,) * 6, compiler_params=_cparams('arbitrary'), name='shift_mix')(h, h, h, mu)


def _halves(x, lo_mask):
    return jnp.concatenate([jnp.where(lo_mask, x, 0.0), jnp.where(lo_mask, 0.0, x)], axis=0)


def _head_sum(x, lo_mask):
    s_lo = jnp.sum(jnp.where(lo_mask, x, 0.0), axis=-1, keepdims=True)
    s_hi = jnp.sum(jnp.where(lo_mask, 0.0, x), axis=-1, keepdims=True)
    return jnp.where(lo_mask, s_lo, s_hi)


def _rw_intra(r, k, v, a_sig, logw, k_k, k_a, masks, tri, rev):
    lo, strict2, incl_cat = masks
    c = r[0].shape[0]
    groups = range(len(r))
    mid = (c - 1 - c // 2) if rev else c // 2
    last = 0 if rev else c - 1
    am2, bm2, km2, v2, rm, ae2, rg, bd2, kd2, gl = ([] for _ in range(10))
    for g in groups:
        kq = k[g] * k_k[g]
        kk = kq * lax.rsqrt(_head_sum(kq * kq, lo) + EPS)
        b = kk * a_sig[g]
        kd = k[g] * (1.0 + (a_sig[g] - 1.0) * k_a[g])
        cs = _cumsum_rows(logw[g], rev)
        ce = cs - logw[g]
        m = cs[mid:mid + 1, :]
        c_last = cs[last:last + 1, :]
        e_mc = jnp.exp(m - cs)
        dec = jnp.exp(c_last - cs)
        am2.append(_halves(-kk * jnp.exp(ce - m), lo))
        bm2.append(_halves(b * e_mc, lo))
        km2.append(_halves(kd * e_mc, lo))
        v2.append(_halves(v[g], lo))
        rm.append(r[g] * jnp.exp(cs - m))
        ae2.append(_halves(-kk * jnp.exp(ce), lo))
        rg.append(r[g] * jnp.exp(cs))
        bd2.append(_halves(b * dec, lo))
        kd2.append(_halves(kd * dec, lo))
        gl.append(jnp.exp(c_last))
    n = [jnp.where(strict2, _dot_nt(am2[g], bm2[g]), 0.0) for g in groups]
    a_ak = [jnp.where(strict2, _dot_nt(am2[g], km2[g]), 0.0) for g in groups]
    a_rb = [jnp.where(incl_cat, _dot_nt(rm[g], bm2[g]), 0.0) for g in groups]
    a_rk = [jnp.where(incl_cat, _dot_nt(rm[g], km2[g]), 0.0) for g in groups]
    akv = [_dot(a_ak[g], v2[g]) for g in groups]
    ov = [_dot(a_rk[g], v2[g]) for g in groups]
    kv = [_dot_tn(v2[g], kd2[g]) for g in groups]
    sol = _unit_tri_solve_many(n, [jnp.concatenate([ae2[g], akv[g]], axis=1) for g in groups],
                               tri)
    wr = [jnp.concatenate([sol[g][:, :LANES], rg[g]], axis=0) for g in groups]
    uv2 = [sol[g][:, LANES:] for g in groups]
    return wr, uv2, a_rb, ov, bd2, kv, gl


def _rw_masks(c, rev):
    lo = lax.broadcasted_iota(jnp.int32, (1, LANES), 1) < RW_HEAD
    ii = lax.broadcasted_iota(jnp.int32, (2 * c, 2 * c), 0)
    jj = lax.broadcasted_iota(jnp.int32, (2 * c, 2 * c), 1)
    same = (ii // c) == (jj // c)
    il, jl = ii % c, jj % c
    strict2 = same & ((il < jl) if rev else (il > jl))
    ic = lax.broadcasted_iota(jnp.int32, (c, 2 * c), 0)
    jc = lax.broadcasted_iota(jnp.int32, (c, 2 * c), 1) % c
    incl_cat = (ic <= jc) if rev else (ic >= jc)
    return lo, strict2, incl_cat


def _rw_scan_body(r_ref, k_ref, v_ref, a_ref, lw_ref, kk_ref, ka_ref, s0_ref, o_ref, sf_ref,
                  s_scr, wr_scr, uv_scr, arb_scr, ov_scr, bd_scr, kv_scr, gl_scr, *, rev, nc, hg):
    j = pl.program_id(1)

    @pl.when(j == 0)
    def _():
        s_scr[...] = s0_ref[...]

    c = RW_CHUNK
    masks = _rw_masks(c, rev)
    tri = _tri_levels(2 * c, c)
    groups = range(hg)
    lanes = [slice(g * LANES, (g + 1) * LANES) for g in groups]

    def intra(ci, carry):
        rows = pl.ds(pl.multiple_of(ci * c, c), c)
        wr, uv2, a_rb, ov, bd2, kv, gl = _rw_intra(
            [r_ref[rows, sl] for sl in lanes], [k_ref[rows, sl] for sl in lanes],
            [v_ref[rows, sl] for sl in lanes], [a_ref[rows, sl] for sl in lanes],
            [lw_ref[rows, sl] for sl in lanes], [kk_ref[:, sl] for sl in lanes],
            [ka_ref[:, sl] for sl in lanes], masks, tri, rev)
        for g in groups:
            wr_scr[ci, g] = wr[g].astype(BF16)
            uv_scr[ci, g] = uv2[g]
            arb_scr[ci, g] = a_rb[g].astype(BF16)
            ov_scr[ci, g] = ov[g]
            bd_scr[ci, g] = bd2[g].astype(BF16)
            kv_scr[ci, g] = kv[g]
            gl_scr[ci, g] = gl[g]
        return carry

    lax.fori_loop(0, nc, intra, 0)

    def inter(ci, carry):
        cc = (nc - 1 - ci) if rev else ci
        rows = pl.ds(pl.multiple_of(cc * c, c), c)
        st = [s_scr[g] for g in groups]
        ws = [_dot_nt(wr_scr[cc, g], st[g]) for g in groups]
        u2 = [ws[g][:2 * c] + uv_scr[cc, g] for g in groups]
        au = [_dot(arb_scr[cc, g], u2[g]) for g in groups]
        ub = [_dot_tn(u2[g], bd_scr[cc, g]) for g in groups]
        for g in groups:
            o_ref[rows, lanes[g]] = ws[g][2 * c:] + au[g] + ov_scr[cc, g]
            s_scr[g] = st[g] * gl_scr[cc, g] + kv_scr[cc, g] + ub[g]
        return carry

    lax.fori_loop(0, nc, inter, 0)

    @pl.when(j == pl.num_programs(1) - 1)
    def _():
        sf_ref[...] = s_scr[...]


def rw_scan(r, k, v, a_sig, logw, k_k, k_a, s0, direction, hg=16, nc=2):
    m, d = r.shape
    npair = d // LANES
    hg = min(hg, npair)
    ng = npair // hg
    nc = min(nc, m // RW_CHUNK)
    bt = nc * RW_CHUNK
    nb = m // bt
    rev = direction == 1

    def tb(j):
        return (nb - 1 - j) if rev else j

    tok = pl.BlockSpec((bt, hg * LANES), lambda g, j: (tb(j), g))
    row = pl.BlockSpec((1, hg * LANES), lambda g, j: (0, g))
    st = pl.BlockSpec((hg, LANES, LANES), lambda g, j: (g, 0, 0))
    return pl.pallas_call(
        functools.partial(_rw_scan_body, rev=rev, nc=nc, hg=hg),
        out_shape=(jax.ShapeDtypeStruct((m, d), F32), jax.ShapeDtypeStruct(s0.shape, F32)),
        grid=(ng, nb), in_specs=[tok, tok, tok, tok, tok, row, row, st], out_specs=(tok, st),
        scratch_shapes=[pltpu.VMEM((hg, LANES, LANES), F32),
                        pltpu.VMEM((nc, hg, 3 * RW_CHUNK, LANES), BF16),
                        pltpu.VMEM((nc, hg, 2 * RW_CHUNK, LANES), F32),
                        pltpu.VMEM((nc, hg, RW_CHUNK, 2 * RW_CHUNK), BF16),
                        pltpu.VMEM((nc, hg, RW_CHUNK, LANES), F32),
                        pltpu.VMEM((nc, hg, 2 * RW_CHUNK, LANES), BF16),
                        pltpu.VMEM((nc, hg, LANES, LANES), F32),
                        pltpu.VMEM((nc, hg, 1, LANES), F32)],
        compiler_params=_cparams('parallel', 'arbitrary'),
        name='rw_scan_bwd' if rev else 'rw_scan_fwd')(
            r, k, v, a_sig, logw, k_k.reshape(1, d), k_a.reshape(1, d), s0)


def _rwout_body(of_ref, ob_ref, r_ref, k_ref, v_ref, af_ref, ab_ref, g_ref,
                lw_ref, lb_ref, ka_ref, rk_ref, o_ref):
    lo = lax.broadcasted_iota(jnp.int32, (1, LANES), 1) < RW_HEAD
    inv_n = 1.0 / RW_HEAD
    for h in range(of_ref.shape[1] // LANES):
        sl = slice(h * LANES, (h + 1) * LANES)
        o = of_ref[:, sl] + ob_ref[:, sl]
        oc = o - _head_sum(o, lo) * inv_n
        y = oc * lax.rsqrt(_head_sum(oc * oc, lo) * inv_n + RW_LN_EPS)
        y = y * lw_ref[:, sl] + lb_ref[:, sl]
        k_sum = k_ref[:, sl] * (2.0 + (af_ref[:, sl] + ab_ref[:, sl] - 2.0) * ka_ref[:, sl])
        bonus = _head_sum(r_ref[:, sl] * k_sum * rk_ref[:, sl], lo) * v_ref[:, sl]
        o_ref[:, sl] = ((y + bonus) * g_ref[:, sl]).astype(o_ref.dtype)


def rw_out(o_f, o_b, r, k, v, a_f, a_b, g, lnx_w, lnx_b, k_a, r_k, bt=256, bc=512):
    m, d = o_f.shape
    bt, bc = _fit(m, bt, 8), _fit(d, bc)
    blk = pl.BlockSpec((bt, bc), lambda i, j: (i, j))
    row = pl.BlockSpec((1, bc), lambda i, j: (0, j))
    rows = [t.reshape(1, d) for t in (lnx_w, lnx_b, k_a, r_k)]
    return pl.pallas_call(
        _rwout_body, out_shape=jax.ShapeDtypeStruct((m, d), BF16), grid=(m // bt, d // bc),
        in_specs=[blk] * 8 + [row] * 4, out_specs=blk,
        compiler_params=_cparams('parallel', 'parallel'), name='rw_out')(
            o_f, o_b, r, k, v, a_f, a_b, g, *rows)


def _convglu_body(u_ref, prev_ref, next_ref, v_ref, cw_ref, cb_ref, o_ref, *, width):
    i = pl.program_id(0)
    nb = pl.num_programs(0)
    x = u_ref[...].astype(F32)
    bt = x.shape[0]
    up = jnp.where(i > 0, prev_ref[...].astype(F32), 0.0)
    dn = jnp.where(i < nb - 1, next_ref[...].astype(F32), 0.0)
    if bt > width:
        up = jnp.concatenate([up, x[:bt - width]], axis=0)
        dn = jnp.concatenate([x[width:], dn], axis=0)
    col = lax.broadcasted_iota(jnp.int32, x.shape, 0) % width
    rows = (up, x, dn)

    def column(kw):
        return sum(cw_ref[3 * r + kw:3 * r + kw + 1, :] * rows[r] for r in range(3))

    left = jnp.where(col == 0, 0.0, pltpu.roll(column(0), 1, 0))
    right = jnp.where(col == width - 1, 0.0, pltpu.roll(column(2), bt - 1, 0))
    y = column(1) + left + right + cb_ref[...]
    o_ref[...] = (_gelu_tanh(y) * v_ref[...].astype(F32)).astype(o_ref.dtype)


def conv_glu_gate(uv, conv_w, conv_b, width, f, bt=512, bc=512):
    m = uv.shape[0]
    bt = max(min(bt, m), width)
    bc = _fit(f, bc)
    rw = bt // width
    lastw = m // width - 1
    voff = f // bc
    return pl.pallas_call(
        functools.partial(_convglu_body, width=width),
        out_shape=jax.ShapeDtypeStruct((m, f), BF16), grid=(m // bt, f // bc),
        in_specs=[pl.BlockSpec((bt, bc), lambda i, j: (i, j)),
                  pl.BlockSpec((width, bc), lambda i, j: (jnp.maximum(i * rw - 1, 0), j)),
                  pl.BlockSpec((width, bc), lambda i, j: (jnp.minimum((i + 1) * rw, lastw), j)),
                  pl.BlockSpec((bt, bc), lambda i, j: (i, voff + j)),
                  pl.BlockSpec((9, bc), lambda i, j: (0, j)),
                  pl.BlockSpec((1, bc), lambda i, j: (0, j))],
        out_specs=pl.BlockSpec((bt, bc), lambda i, j: (i, j)),
        compiler_params=_cparams('parallel', 'parallel'), name='conv_glu')(
            uv, uv, uv, uv, conv_w.reshape(9, f), conv_b.reshape(1, f))


def _modulation(conds, down, up, bias):
    low = matmul(conds, down, precise=True, bm=8, bn=512, name='ada_down')
    return matmul(low, up, bias=bias, precise=True, bm=8, bn=2048, name='ada_up')


def _dn_mixer(streams, w, d, last):
    heads = d // DN_DK
    w_main = w['w_in'][:, :4 * d].astype(BF16)
    w_ab = w['w_in'][:, 4 * d:].astype(BF16)
    gain = jnp.tile(w['norm_g'], heads)
    prepped = []
    for h in streams:
        p = matmul(h, w_main, name='dn_in')
        pab = matmul(h, w_ab, bn=4 * heads, name='dn_in_ab')
        q = dn_conv(p, w['conv_w'], 0, d, 'q')
        k = dn_conv(p, w['conv_w'], 1, d, 'k')
        v = dn_conv(p, w['conv_w'], 2, d, 'v')
        gb = dn_gates(pab, w['a_log'], w['dt_bias'])
        prepped.append((p, q, k, v, gb.T))
    states = [jnp.zeros((heads, DN_DK, DN_DK), F32)] * 2
    outs = []
    for si, (p, q, k, v, gbt) in enumerate(prepped):
        o = []
        for direction in range(2):
            od, states[direction] = dn_scan(q, k, v, gbt, states[direction], direction)
            o.append(od)
        if si == 0 and last:
            outs.append(None)
        else:
            outs.append(gate_out(o[0], o[1], p, 3, gain))
    return outs


def _hg_mixer(streams, w, d, last):
    heads = d // HG_DK
    w_in = w['w_in'].astype(BF16)
    states = [jnp.zeros((heads, HG_DK, HG_DK), F32)] * 2
    outs = []
    for si, h in enumerate(streams):
        p = matmul(h, w_in, name='hg_in')
        o = []
        for direction in range(2):
            od, states[direction] = hg_scan(p, w['lower'], states[direction], direction, d)
            o.append(od)
        if si == 0 and last:
            outs.append(None)
        else:
            outs.append(gate_out(o[0], o[1], p, 4, w['norm_g']))
    return outs


def _rw_mixer(streams, widths, w, d, last):
    w_rkv = w['w_rkv'].astype(BF16)
    w1 = jnp.concatenate([w['w1'][0], w['w1'][1]], axis=1).astype(BF16)
    a1 = jnp.concatenate([w['a1'][0], w['a1'][1]], axis=1).astype(BF16)
    w2 = w['w2'].astype(BF16)
    a2 = w['a2'].astype(BF16)
    lora = w['w1'].shape[-1]
    gl = w['g1'].shape[-1]
    glp = -(-gl // LANES) * LANES
    g1 = jnp.pad(w['g1'], ((0, 0), (0, glp - gl))).astype(BF16)
    g2 = jnp.pad(w['g2'], ((0, glp - gl), (0, 0))).astype(BF16)
    states = [jnp.zeros((d // LANES, LANES, LANES), F32)] * 2
    outs = []
    for si, (h, width) in enumerate(zip(streams, widths)):
        xr, xw, xk, xv, xa, xg = shift_mix(h, w['mu'], width)
        r = matmul(xr, w_rkv[0], name='rw_r')
        k = matmul(xk, w_rkv[1], name='rw_k')
        v = matmul(xv, w_rkv[2], name='rw_v')
        tw = matmul(xw, w1, epi='tanh', out_dtype=BF16, bn=2 * lora, name='rw_w1')
        ta = matmul(xa, a1, out_dtype=BF16, bn=2 * lora, name='rw_a1')
        o, a_sig = [], []
        for direction in range(2):
            logw = matmul(tw, w2[direction], bias=w['w0'][direction], epi='logw',
                          a_koff=direction, name='rw_w2')
            a_d = matmul(ta, a2[direction], bias=w['a0'][direction], epi='sigmoid',
                         a_koff=direction, name='rw_a2')
            od, states[direction] = rw_scan(r, k, v, a_d, logw, w['k_k'], w['k_a'],
                                            states[direction], direction)
            o.append(od)
            a_sig.append(a_d)
        if si == 0 and last:
            outs.append(None)
            continue
        tg = matmul(xg, g1, epi='sigmoid', out_dtype=BF16, name='rw_g1')
        g = matmul(tg, g2, name='rw_g2')
        outs.append(rw_out(o[0], o[1], r, k, v, a_sig[0], a_sig[1], g, w['lnx_w'], w['lnx_b'],
                           w['k_a'], w['r_k'].reshape(-1)))
    return outs


def kernel(x, c, ctx, c_ctx, ada_down, ada_up, ada_b, norm1_g, norm2_g, ffn_w_up, ffn_conv_w, ffn_conv_b, ffn_w_down, dn_w_in, dn_conv_w, dn_a_log, dn_dt_bias, dn_norm_g, dn_w_out, hg_w_in, hg_lower, hg_norm_g, hg_w_out, rw_mu, rw_w_rkv, rw_w0, rw_w1, rw_w2, rw_a0, rw_a1, rw_a2, rw_g1, rw_g2, rw_k_k, rw_k_a, rw_r_k, rw_lnx_w, rw_lnx_b, rw_w_out, final_g):
    _, seq, d = x.shape
    depth = ada_down.shape[0]
    n_ctx = ctx.shape[1]
    f = ffn_w_down.shape[1]
    xl, xc = x[0], ctx[0]
    sm = jax.nn.softmax(hg_lower.astype(F32), axis=0)
    lower_bounds = jnp.cumsum(sm, axis=0) - sm[0]
    conds = jnp.zeros((8, d), F32).at[0].set(jax.nn.silu(c[0])).at[1].set(jax.nn.silu(c_ctx))
    widths = (n_ctx, GRID_W)
    for i in range(depth):
        kind, j = i % 3, i // 3
        last = i == depth - 1
        mod = _modulation(conds, ada_down[i], ada_up[i], ada_b[i])
        mods = [[mod[row, n * d:(n + 1) * d] for n in range(6)] for row in (1, 0)]
        xs = [xc, xl]
        h_dtype = F32 if kind == 2 else BF16
        hs = [normmod(t, norm1_g[i], mm[1], mm[0], h_dtype) for t, mm in zip(xs, mods)]
        if kind == 0:
            w = dict(w_in=dn_w_in[j], conv_w=dn_conv_w[j], a_log=dn_a_log[j],
                     dt_bias=dn_dt_bias[j], norm_g=dn_norm_g[j])
            ys = _dn_mixer(hs, w, d, last)
            w_out = dn_w_out[j]
        elif kind == 1:
            w = dict(w_in=hg_w_in[j], lower=lower_bounds[i], norm_g=hg_norm_g[j])
            ys = _hg_mixer(hs, w, d, last)
            w_out = hg_w_out[j]
        else:
            w = dict(mu=rw_mu[j], w_rkv=rw_w_rkv[j], w0=rw_w0[j], w1=rw_w1[j], w2=rw_w2[j],
                     a0=rw_a0[j], a1=rw_a1[j], a2=rw_a2[j], g1=rw_g1[j], g2=rw_g2[j],
                     k_k=rw_k_k[j], k_a=rw_k_a[j], r_k=rw_r_k[j], lnx_w=rw_lnx_w[j],
                     lnx_b=rw_lnx_b[j])
            ys = _rw_mixer(hs, widths, w, d, last)
            w_out = rw_w_out[j]
        w_out = w_out.astype(BF16)
        w_up = ffn_w_up[i].astype(BF16)
        w_down = ffn_w_down[i].astype(BF16)
        new = []
        for t, y, mm, width in zip(xs, ys, mods, widths):
            if y is None:
                new.append(t)
                continue
            t = matmul(y, w_out, gate=mm[2], resid=t, name='mix_out')
            h2 = normmod(t, norm2_g[i], mm[4], mm[3], BF16)
            uv = matmul(h2, w_up, out_dtype=BF16, name='ffn_up')
            gated = conv_glu_gate(uv, ffn_conv_w[i], ffn_conv_b[i], width, f)
            t = matmul(gated, w_down, gate=mm[5], resid=t, bk=2048, name='ffn_down')
            new.append(t)
        xc, xl = new
    zeros = jnp.zeros((d,), F32)
    return normmod(xl, final_g, zeros, zeros, F32)[None]
```

```python
import functools
import math

import jax
import jax.numpy as jnp
from jax import lax
from jax.experimental import pallas as pl
from jax.experimental.pallas import tpu as pltpu

F32 = jnp.float32
BF16 = jnp.bfloat16

EPS = 1e-6
RW_LN_EPS = 64e-5
GRID_W = 64
DN_DK = 128
DN_CHUNK = 128
HG_DK = 128
HG_CHUNK = 32
RW_HEAD = 64
RW_CHUNK = 64
LANES = 128
VMEM_LIMIT = 56 * 1024 * 1024


def _fit(n, b, unit=LANES):
    if n <= b:
        return n
    for cand in range(b - b % unit, 0, -unit):
        if n % cand == 0:
            return cand
    raise ValueError((n, b, unit))


def _cparams(*sem):
    return pltpu.CompilerParams(dimension_semantics=sem, vmem_limit_bytes=VMEM_LIMIT)


def _sigmoid(x):
    return 1.0 / (1.0 + jnp.exp(-x))


def _silu(x):
    return x * _sigmoid(x)


def _softplus(x):
    return jnp.maximum(x, 0.0) + jnp.log(1.0 + jnp.exp(-jnp.abs(x)))


def _gelu_tanh(x):
    c = math.sqrt(2.0 / math.pi)
    return 0.5 * x * (1.0 + jnp.tanh(c * (x + 0.044715 * (x * x * x))))


def _dot(a, b, dims=(((1,), (0,)), ((), ()))):
    return lax.dot_general(a.astype(BF16), b.astype(BF16), dims, preferred_element_type=F32)


def _dot_nt(a, b):
    return _dot(a, b, (((1,), (1,)), ((), ())))


def _dot_tn(a, b):
    return _dot(a, b, (((0,), (0,)), ((), ())))


def _split2(x):
    hi = x.astype(BF16)
    lo = (x - hi.astype(F32)).astype(BF16)
    return hi, lo


def _dot3(a, b, dims=(((1,), (0,)), ((), ()))):
    ah, al = _split2(a)
    bh, bl = _split2(b)
    d = functools.partial(lax.dot_general, dimension_numbers=dims, preferred_element_type=F32)
    return d(ah, bh) + (d(ah, bl) + d(al, bh))


def _tri_levels(c, size):
    ii = lax.broadcasted_iota(jnp.int32, (c, c), 0)
    jj = lax.broadcasted_iota(jnp.int32, (c, c), 1)
    levels = []
    s = 1
    while (1 << s) < size:
        lo = (ii >> s) != (jj >> s)
        hi = (ii >> (s + 1)) == (jj >> (s + 1))
        levels.append(lo & hi)
        s += 1
    return (ii >> 1) == (jj >> 1), ii == jj, levels


def _unit_tri_solve_many(ns, rhss, tri):
    pair, eye, levels = tri
    units = range(len(ns))
    ts = [jnp.where(pair, jnp.where(eye, 1.0, n), 0.0) for n in ns]
    for lmask in levels:
        ls = [jnp.where(lmask, n, 0.0) for n in ns]
        lt = [_dot(ls[i], ts[i]) for i in units]
        ts = [ts[i] + _dot(ts[i], lt[i]) for i in units]
    return [_dot(ts[i], rhss[i]) for i in units]


def _cumsum_rows(x, rev):
    c = x.shape[0]
    row = lax.broadcasted_iota(jnp.int32, x.shape, 0)
    s = 1
    while s < c:
        if rev:
            x = x + jnp.where(row < c - s, pltpu.roll(x, c - s, 0), 0.0)
        else:
            x = x + jnp.where(row >= s, pltpu.roll(x, s, 0), 0.0)
        s *= 2
    return x


def _epi_none(y):
    return y


def _epi_logw(y):
    return -jnp.exp(-_softplus(-y) - 0.5)


_EPILOGUES = {'none': _epi_none, 'tanh': jnp.tanh, 'sigmoid': _sigmoid, 'logw': _epi_logw}


def _mm_body(*refs, nk, epi, has_bias, has_gate, has_resid, precise):
    it = iter(refs)
    a_ref, w_ref = next(it), next(it)
    bias_ref = next(it) if has_bias else None
    gate_ref = next(it) if has_gate else None
    resid_ref = next(it) if has_resid else None
    o_ref = next(it)
    acc_ref = next(it) if nk > 1 else None

    def finish(y):
        if has_bias:
            y = y + bias_ref[...]
        y = _EPILOGUES[epi](y)
        if has_gate:
            y = y * gate_ref[...]
        if has_resid:
            y = resid_ref[...] + y
        o_ref[...] = y.astype(o_ref.dtype)

    if precise:
        part = _dot3(a_ref[...], w_ref[...])
    else:
        part = _dot(a_ref[...], w_ref[...])
    if nk == 1:
        finish(part)
    else:
        k = pl.program_id(2)

        @pl.when(k == 0)
        def _():
            acc_ref[...] = part

        @pl.when(k > 0)
        def _():
            acc_ref[...] += part

        @pl.when(k == nk - 1)
        def _():
            finish(acc_ref[...])


def matmul(a, w, *, bias=None, gate=None, resid=None, epi='none', out_dtype=F32,
           bm=1024, bn=1024, bk=None, a_koff=0, precise=False, name='matmul'):
    m = a.shape[0]
    k, n = w.shape
    bm, bn = _fit(m, bm, 8), _fit(n, bn)
    bk = k if bk is None else _fit(k, bk)
    nk = k // bk
    if nk == 1:
        grid = (n // bn, m // bm)
        a_spec = pl.BlockSpec((bm, bk), lambda j, i: (i, a_koff))
        w_spec = pl.BlockSpec((bk, bn), lambda j, i: (0, j))
        row_spec = pl.BlockSpec((1, bn), lambda j, i: (0, j))
        o_spec = pl.BlockSpec((bm, bn), lambda j, i: (i, j))
        sem = ('parallel', 'parallel')
        scratch = []
    else:
        grid = (n // bn, m // bm, nk)
        a_spec = pl.BlockSpec((bm, bk), lambda j, i, kk: (i, kk + a_koff * nk))
        w_spec = pl.BlockSpec((bk, bn), lambda j, i, kk: (kk, j))
        row_spec = pl.BlockSpec((1, bn), lambda j, i, kk: (0, j))
        o_spec = pl.BlockSpec((bm, bn), lambda j, i, kk: (i, j))
        sem = ('parallel', 'parallel', 'arbitrary')
        scratch = [pltpu.VMEM((bm, bn), F32)]
    args, specs = [a, w], [a_spec, w_spec]
    for extra in (bias, gate):
        if extra is not None:
            args.append(extra.reshape(1, n).astype(F32))
            specs.append(row_spec)
    if resid is not None:
        args.append(resid)
        specs.append(o_spec)
    body = functools.partial(_mm_body, nk=nk, epi=epi, has_bias=bias is not None,
                             has_gate=gate is not None, has_resid=resid is not None,
                             precise=precise)
    return pl.pallas_call(
        body, out_shape=jax.ShapeDtypeStruct((m, n), out_dtype), grid=grid,
        in_specs=specs, out_specs=o_spec, scratch_shapes=scratch,
        compiler_params=_cparams(*sem), name=name)(*args)


def _normmod_body(x_ref, g_ref, sc_ref, sh_ref, o_ref):
    x = x_ref[...]
    y = x * lax.rsqrt(jnp.mean(x * x, axis=-1, keepdims=True) + EPS) * g_ref[...]
    o_ref[...] = (y * (1.0 + sc_ref[...]) + sh_ref[...]).astype(o_ref.dtype)


def normmod(x, g, sc, sh, out_dtype, bt=256):
    m, d = x.shape
    bt = min(bt, m)
    row = pl.BlockSpec((1, d), lambda i: (0, 0))
    blk = pl.BlockSpec((bt, d), lambda i: (i, 0))
    return pl.pallas_call(
        _normmod_body, out_shape=jax.ShapeDtypeStruct((m, d), out_dtype), grid=(m // bt,),
        in_specs=[blk, row, row, row], out_specs=blk,
        compiler_params=_cparams('parallel'), name='normmod')(
            x, g.reshape(1, d), sc.reshape(1, d), sh.reshape(1, d))


def _dnconv_body(x_ref, prev_ref, next_ref, w_ref, o_ref, *, mode):
    i = pl.program_id(0)
    nb = pl.num_programs(0)
    x = x_ref[...]
    bt = x.shape[0]
    row = lax.broadcasted_iota(jnp.int32, x.shape, 0)
    prev_row = jnp.where(i > 0, prev_ref[7:8, :], 0.0)
    next_row = jnp.where(i < nb - 1, next_ref[0:1, :], 0.0)
    xm = jnp.where(row == 0, prev_row, pltpu.roll(x, 1, 0))
    xp = jnp.where(row == bt - 1, next_row, pltpu.roll(x, bt - 1, 0))
    y = w_ref[0:1, :] * xm + w_ref[1:2, :] * x + w_ref[2:3, :] * xp
    y = _silu(y)
    if mode == 'v':
        o_ref[...] = y.astype(o_ref.dtype)
        return
    scale = DN_DK ** -0.5 if mode == 'q' else 1.0
    for h in range(x.shape[1] // DN_DK):
        sl = slice(h * DN_DK, (h + 1) * DN_DK)
        yh = y[:, sl]
        inv = lax.rsqrt(jnp.sum(yh * yh, axis=-1, keepdims=True) + EPS)
        o_ref[:, sl] = (yh * (inv * scale)).astype(o_ref.dtype)


def dn_conv(p, conv_w, part, d, mode, bt=512, bc=512):
    m = p.shape[0]
    bt, bc = _fit(m, bt, 8), _fit(d, bc)
    off = part * d // bc
    last8 = m // 8 - 1
    r8 = bt // 8
    return pl.pallas_call(
        functools.partial(_dnconv_body, mode=mode),
        out_shape=jax.ShapeDtypeStruct((m, d), F32), grid=(m // bt, d // bc),
        in_specs=[pl.BlockSpec((bt, bc), lambda i, j: (i, off + j)),
                  pl.BlockSpec((8, bc), lambda i, j: (jnp.maximum(i * r8 - 1, 0), off + j)),
                  pl.BlockSpec((8, bc), lambda i, j: (jnp.minimum((i + 1) * r8, last8), off + j)),
                  pl.BlockSpec((3, bc), lambda i, j: (0, off + j))],
        out_specs=pl.BlockSpec((bt, bc), lambda i, j: (i, j)),
        compiler_params=_cparams('parallel', 'parallel'), name='dn_conv_' + mode)(p, p, p, conv_w)


def _dnab_body(x_ref, alog_ref, dtb_ref, o_ref):
    x = x_ref[...]
    lane = lax.broadcasted_iota(jnp.int32, x.shape, 1)
    g = -jnp.exp(alog_ref[...]) * _softplus(x + dtb_ref[...])
    o_ref[...] = jnp.where(lane < x.shape[1] // 2, g, _sigmoid(x))


def dn_gates(pab, a_log, dt_bias, bt=1024):
    m, w = pab.shape
    bt = min(bt, m)
    zeros = jnp.zeros((w // 2,), F32)
    alog = jnp.concatenate([a_log.reshape(-1), zeros]).reshape(1, w)
    dtb = jnp.concatenate([dt_bias.reshape(-1), zeros]).reshape(1, w)
    row = pl.BlockSpec((1, w), lambda i: (0, 0))
    blk = pl.BlockSpec((bt, w), lambda i: (i, 0))
    return pl.pallas_call(
        _dnab_body, out_shape=jax.ShapeDtypeStruct((m, w), F32), grid=(m // bt,),
        in_specs=[blk, row, row], out_specs=blk,
        compiler_params=_cparams('parallel'), name='dn_gates')(pab, alog, dtb)


def _gated_head_norm(o, z, gain):
    y = o * lax.rsqrt(jnp.mean(o * o, axis=-1, keepdims=True) + EPS) * gain
    return y * _silu(z)


def _tri_masks(c, rev):
    ii = lax.broadcasted_iota(jnp.int32, (c, c), 0)
    jj = lax.broadcasted_iota(jnp.int32, (c, c), 1)
    if rev:
        return ii == jj, ii <= jj, ii < jj
    return ii == jj, ii >= jj, ii > jj


def _dn_intra(q, k, v, g_row, beta_row, masks, tri):
    eye, incl, strict = masks
    c = q[0].shape[0]
    heads = range(len(q))
    gc_col, beta_col, g_tot, decay = [], [], [], []
    for h in heads:
        g_bc = jnp.broadcast_to(g_row[h], (c, c))
        gcc = jnp.sum(jnp.where(incl, g_bc, 0.0), axis=1, keepdims=True)
        gcr = jnp.sum(jnp.where(eye, jnp.broadcast_to(gcc, (c, c)), 0.0), axis=0, keepdims=True)
        gc_col.append(gcc)
        beta_col.append(jnp.sum(jnp.where(eye, jnp.broadcast_to(beta_row[h], (c, c)), 0.0),
                                axis=1, keepdims=True))
        g_tot.append(jnp.sum(g_row[h], axis=1, keepdims=True))
        decay.append(jnp.where(incl, jnp.exp(jnp.where(incl, gcc - gcr, 0.0)), 0.0))
    kb = [k[h] * beta_col[h] for h in heads]
    kk = [_dot_nt(kb[h], k[h]) for h in heads]
    qk = [_dot_nt(q[h], k[h]) for h in heads]
    n = [-jnp.where(strict, kk[h] * decay[h], 0.0) for h in heads]
    egc = [jnp.exp(gc_col[h]) for h in heads]
    rhs = [jnp.concatenate([kb[h] * egc[h], v[h] * beta_col[h]], axis=1) for h in heads]
    sol = _unit_tri_solve_many(n, rhs, tri)
    wq = [jnp.concatenate([sol[h][:, :DN_DK], q[h] * egc[h]], axis=0) for h in heads]
    u = [sol[h][:, DN_DK:] for h in heads]
    attn = [jnp.where(incl, qk[h] * decay[h], 0.0) for h in heads]
    kd = [k[h] * jnp.exp(g_tot[h] - gc_col[h]) for h in heads]
    gl = [jnp.exp(g_tot[h]) for h in heads]
    return wq, u, attn, kd, gl


def _dn_scan_body(q_ref, k_ref, v_ref, g_ref, b_ref, s0_ref, *rest, rev, nc, hg, fused):
    if fused:
        of_ref, z_ref, gain_ref = rest[:3]
        rest = rest[3:]
    o_ref, sf_ref, s_scr, wq_scr, u_scr, at_scr, kd_scr, gl_scr = rest
    j = pl.program_id(1)

    @pl.when(j == 0)
    def _():
        s_scr[...] = s0_ref[...]

    c = DN_CHUNK
    masks = _tri_masks(c, rev)
    tri = _tri_levels(c, c)
    heads = range(hg)
    lanes = [slice(h * DN_DK, (h + 1) * DN_DK) for h in heads]

    def intra(ci, carry):
        rows = pl.ds(pl.multiple_of(ci * c, c), c)
        wq, u, attn, kd, gl = _dn_intra(
            [q_ref[rows, sl] for sl in lanes], [k_ref[rows, sl] for sl in lanes],
            [v_ref[rows, sl] for sl in lanes], [g_ref[ci, h:h + 1, :] for h in heads],
            [b_ref[ci, h:h + 1, :] for h in heads], masks, tri)
        for h in heads:
            wq_scr[ci, h] = wq[h].astype(BF16)
            u_scr[ci, h] = u[h]
            at_scr[ci, h] = attn[h].astype(BF16)
            kd_scr[ci, h] = kd[h].astype(BF16)
            gl_scr[ci, h] = jnp.broadcast_to(gl[h], (1, DN_DK))
        return carry

    lax.fori_loop(0, nc, intra, 0)

    def inter(ci, carry):
        cc = (nc - 1 - ci) if rev else ci
        rows = pl.ds(pl.multiple_of(cc * c, c), c)
        s = [s_scr[h] for h in heads]
        ws = [_dot(wq_scr[cc, h], s[h]) for h in heads]
        v_new = [u_scr[cc, h] - ws[h][:c] for h in heads]
        av = [_dot(at_scr[cc, h], v_new[h]) for h in heads]
        kv = [_dot_tn(kd_scr[cc, h], v_new[h]) for h in heads]
        for h in heads:
            o = ws[h][c:] + av[h]
            if fused:
                o = _gated_head_norm(o + of_ref[rows, lanes[h]], z_ref[rows, lanes[h]],
                                     gain_ref[:, lanes[h]])
            o_ref[rows, lanes[h]] = o.astype(o_ref.dtype)
            s_scr[h] = s[h] * gl_scr[cc, h] + kv[h]
        return carry

    lax.fori_loop(0, nc, inter, 0)

    @pl.when(j == pl.num_programs(1) - 1)
    def _():
        sf_ref[...] = s_scr[...]


def dn_scan(q, k, v, gbt, s0, direction, fuse=None, hg=16, nc=2):
    m, d = q.shape
    h = d // DN_DK
    hg = min(hg, h)
    ng = h // hg
    nc = min(nc, m // DN_CHUNK)
    bt = nc * DN_CHUNK
    nb = m // bt
    rev = direction == 1
    gb4 = gbt.reshape(4 * ng, hg, m // DN_CHUNK, DN_CHUNK).transpose(0, 2, 1, 3)

    def tb(j):
        return (nb - 1 - j) if rev else j

    tok = pl.BlockSpec((bt, hg * DN_DK), lambda g, j: (tb(j), g))
    st = pl.BlockSpec((hg, DN_DK, DN_DK), lambda g, j: (g, 0, 0))
    g_spec = pl.BlockSpec((None, nc, hg, DN_CHUNK),
                          lambda g, j: (direction * ng + g, tb(j), 0, 0))
    b_spec = pl.BlockSpec((None, nc, hg, DN_CHUNK),
                          lambda g, j: ((2 + direction) * ng + g, tb(j), 0, 0))
    args, specs = [q, k, v, gb4, gb4, s0], [tok, tok, tok, g_spec, b_spec, st]
    if fuse is not None:
        o_other, p, z_part, gain = fuse
        args += [o_other, p, gain.reshape(1, d)]
        specs += [tok, pl.BlockSpec((bt, hg * DN_DK), lambda g, j: (tb(j), z_part * ng + g)),
                  pl.BlockSpec((1, hg * DN_DK), lambda g, j: (0, g))]
    return pl.pallas_call(
        functools.partial(_dn_scan_body, rev=rev, nc=nc, hg=hg, fused=fuse is not None),
        out_shape=(jax.ShapeDtypeStruct((m, d), F32 if fuse is None else BF16),
                   jax.ShapeDtypeStruct(s0.shape, F32)),
        grid=(ng, nb), in_specs=specs, out_specs=(tok, st),
        scratch_shapes=[pltpu.VMEM((hg, DN_DK, DN_DK), F32),
                        pltpu.VMEM((nc, hg, 2 * DN_CHUNK, DN_DK), BF16),
                        pltpu.VMEM((nc, hg, DN_CHUNK, DN_DK), F32),
                        pltpu.VMEM((nc, hg, DN_CHUNK, DN_CHUNK), BF16),
                        pltpu.VMEM((nc, hg, DN_CHUNK, DN_DK), BF16),
                        pltpu.VMEM((nc, hg, 1, DN_DK), F32)],
        compiler_params=_cparams('parallel', 'arbitrary'),
        name='dn_scan_bwd' if rev else 'dn_scan_fwd')(*args)


def _hg_chunk(q_raw, f_raw, v, lb, st, masks, rev):
    _, incl, _ = masks
    c = q_raw[0].shape[0]
    heads = range(len(q_raw))
    mid = (c - 1 - c // 2) if rev else c // 2
    last = 0 if rev else c - 1
    qm, km, qs, kd, gl = [], [], [], [], []
    for h in heads:
        q = _silu(q_raw[h])
        f = lb[h] + (1.0 - lb[h]) * _sigmoid(f_raw[h])
        k = 1.0 - f
        b = _cumsum_rows(jnp.log(f), rev)
        m = b[mid:mid + 1, :]
        b_last = b[last:last + 1, :]
        qm.append(q * jnp.exp(b - m))
        km.append(k * jnp.exp(m - b))
        qs.append(q * jnp.exp(b))
        kd.append(k * jnp.exp(b_last - b))
        gl.append(jnp.exp(b_last))
    a_qk = [jnp.where(incl, _dot_nt(qm[h], km[h]), 0.0) for h in heads]
    inter = [_dot_nt(qs[h], st[h]) for h in heads]
    kv = [_dot_tn(v[h], kd[h]) for h in heads]
    intra = [_dot(a_qk[h], v[h]) for h in heads]
    o = [inter[h] + intra[h] for h in heads]
    st_new = [st[h] * gl[h] + kv[h] for h in heads]
    return o, st_new


def _hg_scan_body(q_ref, f_ref, v_ref, lb_ref, s0_ref, *rest, rev, nc, hg, fused):
    if fused:
        of_ref, z_ref, gain_ref = rest[:3]
        rest = rest[3:]
    o_ref, sf_ref, s_scr = rest
    j = pl.program_id(1)

    @pl.when(j == 0)
    def _():
        s_scr[...] = s0_ref[...]

    c = HG_CHUNK
    masks = _tri_masks(c, rev)
    heads = range(hg)
    lanes = [slice(h * HG_DK, (h + 1) * HG_DK) for h in heads]

    def chunk(ci, carry):
        cc = (nc - 1 - ci) if rev else ci
        rows = pl.ds(pl.multiple_of(cc * c, c), c)
        o, s_new = _hg_chunk([q_ref[rows, sl] for sl in lanes], [f_ref[rows, sl] for sl in lanes],
                             [v_ref[rows, sl] for sl in lanes], [lb_ref[:, sl] for sl in lanes],
                             [s_scr[h] for h in heads], masks, rev)
        for h in heads:
            oh = o[h]
            if fused:
                oh = _gated_head_norm(oh + of_ref[rows, lanes[h]], z_ref[rows, lanes[h]],
                                      gain_ref[:, lanes[h]])
            o_ref[rows, lanes[h]] = oh.astype(o_ref.dtype)
            s_scr[h] = s_new[h]
        return carry

    lax.fori_loop(0, nc, chunk, 0)

    @pl.when(j == pl.num_programs(1) - 1)
    def _():
        sf_ref[...] = s_scr[...]


def hg_scan(p, lower, s0, direction, d, fuse=None, hg=16, nc=8):
    m = p.shape[0]
    h = d // HG_DK
    hg = min(hg, h)
    ng = h // hg
    nc = min(nc, m // HG_CHUNK)
    bt = nc * HG_CHUNK
    nb = m // bt
    rev = direction == 1

    def tb(j):
        return (nb - 1 - j) if rev else j

    def part(pi):
        return pl.BlockSpec((bt, hg * HG_DK), lambda g, j: (tb(j), pi * ng + g))

    st = pl.BlockSpec((hg, HG_DK, HG_DK), lambda g, j: (g, 0, 0))
    row = pl.BlockSpec((1, hg * HG_DK), lambda g, j: (0, g))
    tok = pl.BlockSpec((bt, hg * HG_DK), lambda g, j: (tb(j), g))
    args = [p, p, p, lower.reshape(1, d), s0]
    specs = [part(0), part(1 + direction), part(3), row, st]
    if fuse is not None:
        o_other, gain = fuse
        args += [o_other, p, gain.reshape(1, d)]
        specs += [tok, part(4), row]
    return pl.pallas_call(
        functools.partial(_hg_scan_body, rev=rev, nc=nc, hg=hg, fused=fuse is not None),
        out_shape=(jax.ShapeDtypeStruct((m, d), F32 if fuse is None else BF16),
                   jax.ShapeDtypeStruct(s0.shape, F32)),
        grid=(ng, nb), in_specs=specs, out_specs=(tok, st),
        scratch_shapes=[pltpu.VMEM((hg, HG_DK, HG_DK), F32)],
        compiler_params=_cparams('parallel', 'arbitrary'),
        name='hg_scan_bwd' if rev else 'hg_scan_fwd')(*args)


def _shiftmix_body(x_ref, prev_ref, next_ref, mu_ref, *o_refs, width):
    i = pl.program_id(0)
    nb = pl.num_programs(0)
    bt, d = x_ref.shape
    dq = d // 4
    col = lax.broadcasted_iota(jnp.int32, (bt, dq), 0) % width
    for qi in range(4):
        sl = slice(qi * dq, (qi + 1) * dq)
        x = x_ref[:, sl]
        if qi == 0:
            sh = jnp.where(col == 0, 0.0, pltpu.roll(x, 1, 0))
        elif qi == 1:
            sh = jnp.where(col == width - 1, 0.0, pltpu.roll(x, bt - 1, 0))
        elif qi == 2:
            edge = jnp.where(i > 0, prev_ref[:, sl], 0.0)
            sh = edge if bt == width else jnp.concatenate([edge, x[:bt - width]], axis=0)
        else:
            edge = jnp.where(i < nb - 1, next_ref[:, sl], 0.0)
            sh = edge if bt == width else jnp.concatenate([x[width:], edge], axis=0)
        xx = sh - x
        for n, o_ref in enumerate(o_refs):
            o_ref[:, sl] = (x + xx * mu_ref[n:n + 1, sl]).astype(o_ref.dtype)


def shift_mix(h, mu, width, bt=128):
    m, d = h.shape
    bt = max(min(bt, m), width)
    rw = bt // width
    lastw = m // width - 1
    blk = pl.BlockSpec((bt, d), lambda i: (i, 0))
    out = jax.ShapeDtypeStruct((m, d), BF16)
    return pl.pallas_call(
        functools.partial(_shiftmix_body, width=width),
        out_shape=(out,) * 6, grid=(m // bt,),
        in_specs=[blk,
                  pl.BlockSpec((width, d), lambda i: (jnp.maximum(i * rw - 1, 0), 0)),
                  pl.BlockSpec((width, d), lambda i: (jnp.minimum((i + 1) * rw, lastw), 0)),
                  pl.BlockSpec((6, d), lambda i: (0, 0))],
        out_specs=(blk,) * 6, compiler_params=_cparams('arbitrary'), name='shift_mix')(h, h, h, mu)


def _halves(x, lo_mask):
    return jnp.concatenate([jnp.where(lo_mask, x, 0.0), jnp.where(lo_mask, 0.0, x)], axis=0)


def _head_sum(x, lo_mask):
    s_lo = jnp.sum(jnp.where(lo_mask, x, 0.0), axis=-1, keepdims=True)
    s_hi = jnp.sum(jnp.where(lo_mask, 0.0, x), axis=-1, keepdims=True)
    return jnp.where(lo_mask, s_lo, s_hi)


def _rw_intra(r, k, v, a_sig, logw, k_k, k_a, masks, tri, rev):
    lo, strict2, incl_cat = masks
    c = r[0].shape[0]
    groups = range(len(r))
    mid = (c - 1 - c // 2) if rev else c // 2
    last = 0 if rev else c - 1
    am2, bm2, km2, v2, rm, ae2, rg, bd2, kd2, gl = ([] for _ in range(10))
    for g in groups:
        kq = k[g] * k_k[g]
        kk = kq * lax.rsqrt(_head_sum(kq * kq, lo) + EPS)
        b = kk * a_sig[g]
        kd = k[g] * (1.0 + (a_sig[g] - 1.0) * k_a[g])
        cs = _cumsum_rows(logw[g], rev)
        ce = cs - logw[g]
        m = cs[mid:mid + 1, :]
        c_last = cs[last:last + 1, :]
        e_mc = jnp.exp(m - cs)
        dec = jnp.exp(c_last - cs)
        am2.append(_halves(-kk * jnp.exp(ce - m), lo))
        bm2.append(_halves(b * e_mc, lo))
        km2.append(_halves(kd * e_mc, lo))
        v2.append(_halves(v[g], lo))
        rm.append(r[g] * jnp.exp(cs - m))
        ae2.append(_halves(-kk * jnp.exp(ce), lo))
        rg.append(r[g] * jnp.exp(cs))
        bd2.append(_halves(b * dec, lo))
        kd2.append(_halves(kd * dec, lo))
        gl.append(jnp.exp(c_last))
    n = [jnp.where(strict2, _dot_nt(am2[g], bm2[g]), 0.0) for g in groups]
    a_ak = [jnp.where(strict2, _dot_nt(am2[g], km2[g]), 0.0) for g in groups]
    a_rb = [jnp.where(incl_cat, _dot_nt(rm[g], bm2[g]), 0.0) for g in groups]
    a_rk = [jnp.where(incl_cat, _dot_nt(rm[g], km2[g]), 0.0) for g in groups]
    akv = [_dot(a_ak[g], v2[g]) for g in groups]
    ov = [_dot(a_rk[g], v2[g]) for g in groups]
    kv = [_dot_tn(v2[g], kd2[g]) for g in groups]
    sol = _unit_tri_solve_many(n, [jnp.concatenate([ae2[g], akv[g]], axis=1) for g in groups],
                               tri)
    wr = [jnp.concatenate([sol[g][:, :LANES], rg[g]], axis=0) for g in groups]
    uv2 = [sol[g][:, LANES:] for g in groups]
    return wr, uv2, a_rb, ov, bd2, kv, gl


def _rw_output(o, r, k, v, a_sum, gate, ln_w, ln_b, k_a, r_k, lo):
    inv_n = 1.0 / RW_HEAD
    oc = o - _head_sum(o, lo) * inv_n
    y = oc * lax.rsqrt(_head_sum(oc * oc, lo) * inv_n + RW_LN_EPS) * ln_w + ln_b
    k_sum = k * (2.0 + (a_sum - 2.0) * k_a)
    bonus = _head_sum(r * k_sum * r_k, lo) * v
    return (y + bonus) * gate


def _rw_masks(c, rev):
    lo = lax.broadcasted_iota(jnp.int32, (1, LANES), 1) < RW_HEAD
    ii = lax.broadcasted_iota(jnp.int32, (2 * c, 2 * c), 0)
    jj = lax.broadcasted_iota(jnp.int32, (2 * c, 2 * c), 1)
    same = (ii // c) == (jj // c)
    il, jl = ii % c, jj % c
    strict2 = same & ((il < jl) if rev else (il > jl))
    ic = lax.broadcasted_iota(jnp.int32, (c, 2 * c), 0)
    jc = lax.broadcasted_iota(jnp.int32, (c, 2 * c), 1) % c
    incl_cat = (ic <= jc) if rev else (ic >= jc)
    return lo, strict2, incl_cat


def _rw_scan_body(r_ref, k_ref, v_ref, a_ref, lw_ref, kk_ref, ka_ref, s0_ref, *rest,
                  rev, nc, hg, fused):
    if fused:
        of_ref, ao_ref, gate_ref, lnw_ref, lnb_ref, rk_ref = rest[:6]
        rest = rest[6:]
    o_ref, sf_ref, s_scr, wr_scr, uv_scr, arb_scr, ov_scr, bd_scr, kv_scr, gl_scr = rest
    j = pl.program_id(1)

    @pl.when(j == 0)
    def _():
        s_scr[...] = s0_ref[...]

    c = RW_CHUNK
    masks = _rw_masks(c, rev)
    tri = _tri_levels(2 * c, c)
    groups = range(hg)
    lanes = [slice(g * LANES, (g + 1) * LANES) for g in groups]

    def intra(ci, carry):
        rows = pl.ds(pl.multiple_of(ci * c, c), c)
        wr, uv2, a_rb, ov, bd2, kv, gl = _rw_intra(
            [r_ref[rows, sl] for sl in lanes], [k_ref[rows, sl] for sl in lanes],
            [v_ref[rows, sl] for sl in lanes], [a_ref[rows, sl] for sl in lanes],
            [lw_ref[rows, sl] for sl in lanes], [kk_ref[:, sl] for sl in lanes],
            [ka_ref[:, sl] for sl in lanes], masks, tri, rev)
        for g in groups:
            wr_scr[ci, g] = wr[g].astype(BF16)
            uv_scr[ci, g] = uv2[g]
            arb_scr[ci, g] = a_rb[g].astype(BF16)
            ov_scr[ci, g] = ov[g]
            bd_scr[ci, g] = bd2[g].astype(BF16)
            kv_scr[ci, g] = kv[g]
            gl_scr[ci, g] = gl[g]
        return carry

    lax.fori_loop(0, nc, intra, 0)

    def inter(ci, carry):
        cc = (nc - 1 - ci) if rev else ci
        rows = pl.ds(pl.multiple_of(cc * c, c), c)
        st = [s_scr[g] for g in groups]
        ws = [_dot_nt(wr_scr[cc, g], st[g]) for g in groups]
        u2 = [ws[g][:2 * c] + uv_scr[cc, g] for g in groups]
        au = [_dot(arb_scr[cc, g], u2[g]) for g in groups]
        ub = [_dot_tn(u2[g], bd_scr[cc, g]) for g in groups]
        for g in groups:
            o = ws[g][2 * c:] + au[g] + ov_scr[cc, g]
            if fused:
                sl = lanes[g]
                o = _rw_output(o + of_ref[rows, sl], r_ref[rows, sl], k_ref[rows, sl],
                               v_ref[rows, sl], a_ref[rows, sl] + ao_ref[rows, sl],
                               gate_ref[rows, sl], lnw_ref[:, sl], lnb_ref[:, sl], ka_ref[:, sl],
                               rk_ref[:, sl], masks[0])
            o_ref[rows, lanes[g]] = o.astype(o_ref.dtype)
            s_scr[g] = st[g] * gl_scr[cc, g] + kv_scr[cc, g] + ub[g]
        return carry

    lax.fori_loop(0, nc, inter, 0)

    @pl.when(j == pl.num_programs(1) - 1)
    def _():
        sf_ref[...] = s_scr[...]


def rw_scan(r, k, v, a_sig, logw, k_k, k_a, s0, direction, fuse=None, hg=16, nc=2):
    m, d = r.shape
    npair = d // LANES
    hg = min(hg, npair)
    ng = npair // hg
    nc = min(nc, m // RW_CHUNK)
    bt = nc * RW_CHUNK
    nb = m // bt
    rev = direction == 1

    def tb(j):
        return (nb - 1 - j) if rev else j

    tok = pl.BlockSpec((bt, hg * LANES), lambda g, j: (tb(j), g))
    row = pl.BlockSpec((1, hg * LANES), lambda g, j: (0, g))
    st = pl.BlockSpec((hg, LANES, LANES), lambda g, j: (g, 0, 0))
    args = [r, k, v, a_sig, logw, k_k.reshape(1, d), k_a.reshape(1, d), s0]
    specs = [tok, tok, tok, tok, tok, row, row, st]
    if fuse is not None:
        args += list(fuse[:3]) + [t.reshape(1, d) for t in fuse[3:]]
        specs += [tok, tok, tok, row, row, row]
    return pl.pallas_call(
        functools.partial(_rw_scan_body, rev=rev, nc=nc, hg=hg, fused=fuse is not None),
        out_shape=(jax.ShapeDtypeStruct((m, d), F32 if fuse is None else BF16),
                   jax.ShapeDtypeStruct(s0.shape, F32)),
        grid=(ng, nb), in_specs=specs, out_specs=(tok, st),
        scratch_shapes=[pltpu.VMEM((hg, LANES, LANES), F32),
                        pltpu.VMEM((nc, hg, 3 * RW_CHUNK, LANES), BF16),
                        pltpu.VMEM((nc, hg, 2 * RW_CHUNK, LANES), F32),
                        pltpu.VMEM((nc, hg, RW_CHUNK, 2 * RW_CHUNK), BF16),
                        pltpu.VMEM((nc, hg, RW_CHUNK, LANES), F32),
                        pltpu.VMEM((nc, hg, 2 * RW_CHUNK, LANES), BF16),
                        pltpu.VMEM((nc, hg, LANES, LANES), F32),
                        pltpu.VMEM((nc, hg, 1, LANES), F32)],
        compiler_params=_cparams('parallel', 'arbitrary'),
        name='rw_scan_bwd' if rev else 'rw_scan_fwd')(*args)


def _convglu_body(u_ref, prev_ref, next_ref, v_ref, cw_ref, cb_ref, o_ref, *, width):
    i = pl.program_id(0)
    nb = pl.num_programs(0)
    x = u_ref[...].astype(F32)
    bt = x.shape[0]
    up = jnp.where(i > 0, prev_ref[...].astype(F32), 0.0)
    dn = jnp.where(i < nb - 1, next_ref[...].astype(F32), 0.0)
    if bt > width:
        up = jnp.concatenate([up, x[:bt - width]], axis=0)
        dn = jnp.concatenate([x[width:], dn], axis=0)
    col = lax.broadcasted_iota(jnp.int32, x.shape, 0) % width
    rows = (up, x, dn)

    def column(kw):
        return sum(cw_ref[3 * r + kw:3 * r + kw + 1, :] * rows[r] for r in range(3))

    left = jnp.where(col == 0, 0.0, pltpu.roll(column(0), 1, 0))
    right = jnp.where(col == width - 1, 0.0, pltpu.roll(column(2), bt - 1, 0))
    y = column(1) + left + right + cb_ref[...]
    o_ref[...] = (_gelu_tanh(y) * v_ref[...].astype(F32)).astype(o_ref.dtype)


def conv_glu_gate(uv, conv_w, conv_b, width, f, bt=512, bc=512):
    m = uv.shape[0]
    bt = max(min(bt, m), width)
    bc = _fit(f, bc)
    rw = bt // width
    lastw = m // width - 1
    voff = f // bc
    return pl.pallas_call(
        functools.partial(_convglu_body, width=width),
        out_shape=jax.ShapeDtypeStruct((m, f), BF16), grid=(m // bt, f // bc),
        in_specs=[pl.BlockSpec((bt, bc), lambda i, j: (i, j)),
                  pl.BlockSpec((width, bc), lambda i, j: (jnp.maximum(i * rw - 1, 0), j)),
                  pl.BlockSpec((width, bc), lambda i, j: (jnp.minimum((i + 1) * rw, lastw), j)),
                  pl.BlockSpec((bt, bc), lambda i, j: (i, voff + j)),
                  pl.BlockSpec((9, bc), lambda i, j: (0, j)),
                  pl.BlockSpec((1, bc), lambda i, j: (0, j))],
        out_specs=pl.BlockSpec((bt, bc), lambda i, j: (i, j)),
        compiler_params=_cparams('parallel', 'parallel'), name='conv_glu')(
            uv, uv, uv, uv, conv_w.reshape(9, f), conv_b.reshape(1, f))


def _modulation(conds, down, up, bias):
    low = matmul(conds, down, precise=True, bm=8, bn=512, name='ada_down')
    return matmul(low, up, bias=bias, precise=True, bm=8, bn=2048, name='ada_up')


def _dn_mixer(streams, w, d, last):
    heads = d // DN_DK
    w_main = w['w_in'][:, :4 * d].astype(BF16)
    w_ab = w['w_in'][:, 4 * d:].astype(BF16)
    gain = jnp.tile(w['norm_g'], heads)
    prepped = []
    for h in streams:
        p = matmul(h, w_main, name='dn_in')
        pab = matmul(h, w_ab, bn=4 * heads, name='dn_in_ab')
        q = dn_conv(p, w['conv_w'], 0, d, 'q')
        k = dn_conv(p, w['conv_w'], 1, d, 'k')
        v = dn_conv(p, w['conv_w'], 2, d, 'v')
        gb = dn_gates(pab, w['a_log'], w['dt_bias'])
        prepped.append((p, q, k, v, gb.T))
    states = [jnp.zeros((heads, DN_DK, DN_DK), F32)] * 2
    outs = []
    for si, (p, q, k, v, gbt) in enumerate(prepped):
        o_f, states[0] = dn_scan(q, k, v, gbt, states[0], 0)
        y, states[1] = dn_scan(q, k, v, gbt, states[1], 1, fuse=(o_f, p, 3, gain))
        outs.append(None if si == 0 and last else y)
    return outs


def _hg_mixer(streams, w, d, last):
    heads = d // HG_DK
    w_in = w['w_in'].astype(BF16)
    states = [jnp.zeros((heads, HG_DK, HG_DK), F32)] * 2
    outs = []
    for si, h in enumerate(streams):
        p = matmul(h, w_in, name='hg_in')
        o_f, states[0] = hg_scan(p, w['lower'], states[0], 0, d)
        y, states[1] = hg_scan(p, w['lower'], states[1], 1, d, fuse=(o_f, w['norm_g']))
        outs.append(None if si == 0 and last else y)
    return outs


def _rw_mixer(streams, widths, w, d, last):
    w_rkv = w['w_rkv'].astype(BF16)
    w1 = jnp.concatenate([w['w1'][0], w['w1'][1]], axis=1).astype(BF16)
    a1 = jnp.concatenate([w['a1'][0], w['a1'][1]], axis=1).astype(BF16)
    w2 = w['w2'].astype(BF16)
    a2 = w['a2'].astype(BF16)
    lora = w['w1'].shape[-1]
    gl = w['g1'].shape[-1]
    glp = -(-gl // LANES) * LANES
    g1 = jnp.pad(w['g1'], ((0, 0), (0, glp - gl))).astype(BF16)
    g2 = jnp.pad(w['g2'], ((0, glp - gl), (0, 0))).astype(BF16)
    states = [jnp.zeros((d // LANES, LANES, LANES), F32)] * 2
    outs = []
    for si, (h, width) in enumerate(zip(streams, widths)):
        xr, xw, xk, xv, xa, xg = shift_mix(h, w['mu'], width)
        r = matmul(xr, w_rkv[0], name='rw_r')
        k = matmul(xk, w_rkv[1], name='rw_k')
        v = matmul(xv, w_rkv[2], name='rw_v')
        tw = matmul(xw, w1, epi='tanh', out_dtype=BF16, bn=2 * lora, name='rw_w1')
        ta = matmul(xa, a1, out_dtype=BF16, bn=2 * lora, name='rw_a1')
        tg = matmul(xg, g1, epi='sigmoid', out_dtype=BF16, name='rw_g1')
        gate = matmul(tg, g2, name='rw_g2')
        logw, a_sig = [], []
        for direction in range(2):
            logw.append(matmul(tw, w2[direction], bias=w['w0'][direction], epi='logw',
                               a_koff=direction, name='rw_w2'))
            a_sig.append(matmul(ta, a2[direction], bias=w['a0'][direction], epi='sigmoid',
                                a_koff=direction, name='rw_a2'))
        o_f, states[0] = rw_scan(r, k, v, a_sig[0], logw[0], w['k_k'], w['k_a'], states[0], 0)
        y, states[1] = rw_scan(r, k, v, a_sig[1], logw[1], w['k_k'], w['k_a'], states[1], 1,
                               fuse=(o_f, a_sig[0], gate, w['lnx_w'], w['lnx_b'],
                                     w['r_k'].reshape(-1)))
        outs.append(None if si == 0 and last else y)
    return outs


def kernel(x, c, ctx, c_ctx, ada_down, ada_up, ada_b, norm1_g, norm2_g, ffn_w_up, ffn_conv_w, ffn_conv_b, ffn_w_down, dn_w_in, dn_conv_w, dn_a_log, dn_dt_bias, dn_norm_g, dn_w_out, hg_w_in, hg_lower, hg_norm_g, hg_w_out, rw_mu, rw_w_rkv, rw_w0, rw_w1, rw_w2, rw_a0, rw_a1, rw_a2, rw_g1, rw_g2, rw_k_k, rw_k_a, rw_r_k, rw_lnx_w, rw_lnx_b, rw_w_out, final_g):
    _, seq, d = x.shape
    depth = ada_down.shape[0]
    n_ctx = ctx.shape[1]
    f = ffn_w_down.shape[1]
    xl, xc = x[0], ctx[0]
    sm = jax.nn.softmax(hg_lower.astype(F32), axis=0)
    lower_bounds = jnp.cumsum(sm, axis=0) - sm[0]
    conds = jnp.zeros((8, d), F32).at[0].set(jax.nn.silu(c[0])).at[1].set(jax.nn.silu(c_ctx))
    widths = (n_ctx, GRID_W)
    for i in range(depth):
        kind, j = i % 3, i // 3
        last = i == depth - 1
        mod = _modulation(conds, ada_down[i], ada_up[i], ada_b[i])
        mods = [[mod[row, n * d:(n + 1) * d] for n in range(6)] for row in (1, 0)]
        xs = [xc, xl]
        h_dtype = F32 if kind == 2 else BF16
        hs = [normmod(t, norm1_g[i], mm[1], mm[0], h_dtype) for t, mm in zip(xs, mods)]
        if kind == 0:
            w = dict(w_in=dn_w_in[j], conv_w=dn_conv_w[j], a_log=dn_a_log[j],
                     dt_bias=dn_dt_bias[j], norm_g=dn_norm_g[j])
            ys = _dn_mixer(hs, w, d, last)
            w_out = dn_w_out[j]
        elif kind == 1:
            w = dict(w_in=hg_w_in[j], lower=lower_bounds[i], norm_g=hg_norm_g[j])
            ys = _hg_mixer(hs, w, d, last)
            w_out = hg_w_out[j]
        else:
            w = dict(mu=rw_mu[j], w_rkv=rw_w_rkv[j], w0=rw_w0[j], w1=rw_w1[j], w2=rw_w2[j],
                     a0=rw_a0[j], a1=rw_a1[j], a2=rw_a2[j], g1=rw_g1[j], g2=rw_g2[j],
                     k_k=rw_k_k[j], k_a=rw_k_a[j], r_k=rw_r_k[j], lnx_w=rw_lnx_w[j],
                     lnx_b=rw_lnx_b[j])
            ys = _rw_mixer(hs, widths, w, d, last)
            w_out = rw_w_out[j]
        w_out = w_out.astype(BF16)
        w_up = ffn_w_up[i].astype(BF16)
        w_down = ffn_w_down[i].astype(BF16)
        new = []
        for t, y, mm, width in zip(xs, ys, mods, widths):
            if y is None:
                new.append(t)
                continue
            t = matmul(y, w_out, gate=mm[2], resid=t, name='mix_out')
            h2 = normmod(t, norm2_g[i], mm[4], mm[3], BF16)
            uv = matmul(h2, w_up, out_dtype=BF16, name='ffn_up')
            gated = conv_glu_gate(uv, ffn_conv_w[i], ffn_conv_b[i], width, f)
            t = matmul(gated, w_down, gate=mm[5], resid=t, bk=2048, name='ffn_down')
            new.append(t)
        xc, xl = new
    zeros = jnp.zeros((d,), F32)
    return normmod(xl, final_g, zeros, zeros, F32)[None]
```

```python
import functools
import math

import jax
import jax.numpy as jnp
from jax import lax
from jax.experimental import pallas as pl
from jax.experimental.pallas import tpu as pltpu

F32 = jnp.float32
BF16 = jnp.bfloat16

EPS = 1e-6
RW_LN_EPS = 64e-5
GRID_W = 64
DN_DK = 128
DN_CHUNK = 128
HG_DK = 128
HG_CHUNK = 32
RW_HEAD = 64
RW_CHUNK = 64
LANES = 128
VMEM_LIMIT = 60 * 1024 * 1024


def _fit(n, b, unit=LANES):
    if n <= b:
        return n
    for cand in range(b - b % unit, 0, -unit):
        if n % cand == 0:
            return cand
    raise ValueError((n, b, unit))


def _cparams(*sem):
    return pltpu.CompilerParams(dimension_semantics=sem, vmem_limit_bytes=VMEM_LIMIT)


def _sigmoid(x):
    return 1.0 / (1.0 + jnp.exp(-x))


def _silu(x):
    return x * _sigmoid(x)


def _softplus(x):
    return jnp.maximum(x, 0.0) + jnp.log(1.0 + jnp.exp(-jnp.abs(x)))


def _gelu_tanh(x):
    c = math.sqrt(2.0 / math.pi)
    return 0.5 * x * (1.0 + jnp.tanh(c * (x + 0.044715 * (x * x * x))))


def _dot(a, b, dims=(((1,), (0,)), ((), ()))):
    return lax.dot_general(a.astype(BF16), b.astype(BF16), dims, preferred_element_type=F32)


def _dot_nt(a, b):
    return _dot(a, b, (((1,), (1,)), ((), ())))


def _dot_tn(a, b):
    return _dot(a, b, (((0,), (0,)), ((), ())))


def _split2(x):
    hi = x.astype(BF16)
    lo = (x - hi.astype(F32)).astype(BF16)
    return hi, lo


def _dot3(a, b, dims=(((1,), (0,)), ((), ()))):
    ah, al = _split2(a)
    bh, bl = _split2(b)
    d = functools.partial(lax.dot_general, dimension_numbers=dims, preferred_element_type=F32)
    return d(ah, bh) + (d(ah, bl) + d(al, bh))


def _tri_levels(c, size):
    ii = lax.broadcasted_iota(jnp.int32, (c, c), 0)
    jj = lax.broadcasted_iota(jnp.int32, (c, c), 1)
    levels = []
    s = 1
    while (1 << s) < size:
        lo = (ii >> s) != (jj >> s)
        hi = (ii >> (s + 1)) == (jj >> (s + 1))
        levels.append(lo & hi)
        s += 1
    return (ii >> 1) == (jj >> 1), ii == jj, levels


def _unit_tri_solve_many(ns, rhss, tri):
    pair, eye, levels = tri
    units = range(len(ns))
    ts = [jnp.where(pair, jnp.where(eye, 1.0, n), 0.0) for n in ns]
    for lmask in levels:
        ls = [jnp.where(lmask, n, 0.0) for n in ns]
        lt = [_dot(ls[i], ts[i]) for i in units]
        ts = [ts[i] + _dot(ts[i], lt[i]) for i in units]
    return [_dot(ts[i], rhss[i]) for i in units]


def _cumsum_rows(x, rev):
    c = x.shape[0]
    row = lax.broadcasted_iota(jnp.int32, x.shape, 0)
    s = 1
    while s < c:
        if rev:
            x = x + jnp.where(row < c - s, pltpu.roll(x, c - s, 0), 0.0)
        else:
            x = x + jnp.where(row >= s, pltpu.roll(x, s, 0), 0.0)
        s *= 2
    return x


def _epi_none(y):
    return y


def _epi_logw(y):
    return -jnp.exp(-_softplus(-y) - 0.5)


_EPILOGUES = {'none': _epi_none, 'tanh': jnp.tanh, 'sigmoid': _sigmoid, 'logw': _epi_logw}


def _mm_body(*refs, nk, epi, has_bias, has_gate, has_resid, precise):
    it = iter(refs)
    a_ref, w_ref = next(it), next(it)
    bias_ref = next(it) if has_bias else None
    gate_ref = next(it) if has_gate else None
    resid_ref = next(it) if has_resid else None
    o_ref = next(it)
    acc_ref = next(it) if nk > 1 else None

    def finish(y):
        if has_bias:
            y = y + bias_ref[...]
        y = _EPILOGUES[epi](y)
        if has_gate:
            y = y * gate_ref[...]
        if has_resid:
            y = resid_ref[...] + y
        o_ref[...] = y.astype(o_ref.dtype)

    if precise:
        part = _dot3(a_ref[...], w_ref[...])
    else:
        part = _dot(a_ref[...], w_ref[...])
    if nk == 1:
        finish(part)
    else:
        k = pl.program_id(2)

        @pl.when(k == 0)
        def _():
            acc_ref[...] = part

        @pl.when(k > 0)
        def _():
            acc_ref[...] += part

        @pl.when(k == nk - 1)
        def _():
            finish(acc_ref[...])


def matmul(a, w, *, bias=None, gate=None, resid=None, epi='none', out_dtype=F32,
           bm=1024, bn=1024, bk=None, a_koff=0, precise=False, w_single=False, name='matmul'):
    m = a.shape[0]
    k, n = w.shape
    bm, bn = _fit(m, bm, 8), _fit(n, bn)
    bk = k if bk is None else _fit(k, bk)
    nk = k // bk
    if nk == 1:
        grid = (n // bn, m // bm)
        a_spec = pl.BlockSpec((bm, bk), lambda j, i: (i, a_koff))
        w_mode = dict(pipeline_mode=pl.Buffered(1)) if w_single else {}
        w_spec = pl.BlockSpec((bk, bn), lambda j, i: (0, j), **w_mode)
        row_spec = pl.BlockSpec((1, bn), lambda j, i: (0, j))
        o_spec = pl.BlockSpec((bm, bn), lambda j, i: (i, j))
        sem = ('parallel', 'parallel')
        scratch = []
    else:
        grid = (n // bn, m // bm, nk)
        a_spec = pl.BlockSpec((bm, bk), lambda j, i, kk: (i, kk + a_koff * nk))
        w_spec = pl.BlockSpec((bk, bn), lambda j, i, kk: (kk, j))
        row_spec = pl.BlockSpec((1, bn), lambda j, i, kk: (0, j))
        o_spec = pl.BlockSpec((bm, bn), lambda j, i, kk: (i, j))
        sem = ('parallel', 'parallel', 'arbitrary')
        scratch = [pltpu.VMEM((bm, bn), F32)]
    args, specs = [a, w], [a_spec, w_spec]
    for extra in (bias, gate):
        if extra is not None:
            args.append(extra.reshape(1, n).astype(F32))
            specs.append(row_spec)
    if resid is not None:
        args.append(resid)
        specs.append(o_spec)
    body = functools.partial(_mm_body, nk=nk, epi=epi, has_bias=bias is not None,
                             has_gate=gate is not None, has_resid=resid is not None,
                             precise=precise)
    return pl.pallas_call(
        body, out_shape=jax.ShapeDtypeStruct((m, n), out_dtype), grid=grid,
        in_specs=specs, out_specs=o_spec, scratch_shapes=scratch,
        compiler_params=_cparams(*sem), name=name)(*args)


def _normmod_body(x_ref, g_ref, sc_ref, sh_ref, o_ref):
    x = x_ref[...]
    y = x * lax.rsqrt(jnp.mean(x * x, axis=-1, keepdims=True) + EPS) * g_ref[...]
    o_ref[...] = (y * (1.0 + sc_ref[...]) + sh_ref[...]).astype(o_ref.dtype)


def normmod(x, g, sc, sh, out_dtype, bt=256):
    m, d = x.shape
    bt = min(bt, m)
    row = pl.BlockSpec((1, d), lambda i: (0, 0))
    blk = pl.BlockSpec((bt, d), lambda i: (i, 0))
    return pl.pallas_call(
        _normmod_body, out_shape=jax.ShapeDtypeStruct((m, d), out_dtype), grid=(m // bt,),
        in_specs=[blk, row, row, row], out_specs=blk,
        compiler_params=_cparams('parallel'), name='normmod')(
            x, g.reshape(1, d), sc.reshape(1, d), sh.reshape(1, d))


def _dnconv_body(x_ref, prev_ref, next_ref, w_ref, o_ref, *, mode):
    i = pl.program_id(0)
    nb = pl.num_programs(0)
    x = x_ref[...]
    bt = x.shape[0]
    row = lax.broadcasted_iota(jnp.int32, x.shape, 0)
    prev_row = jnp.where(i > 0, prev_ref[7:8, :], 0.0)
    next_row = jnp.where(i < nb - 1, next_ref[0:1, :], 0.0)
    xm = jnp.where(row == 0, prev_row, pltpu.roll(x, 1, 0))
    xp = jnp.where(row == bt - 1, next_row, pltpu.roll(x, bt - 1, 0))
    y = w_ref[0:1, :] * xm + w_ref[1:2, :] * x + w_ref[2:3, :] * xp
    y = _silu(y)
    if mode == 'v':
        o_ref[...] = y.astype(o_ref.dtype)
        return
    scale = DN_DK ** -0.5 if mode == 'q' else 1.0
    for h in range(x.shape[1] // DN_DK):
        sl = slice(h * DN_DK, (h + 1) * DN_DK)
        yh = y[:, sl]
        inv = lax.rsqrt(jnp.sum(yh * yh, axis=-1, keepdims=True) + EPS)
        o_ref[:, sl] = (yh * (inv * scale)).astype(o_ref.dtype)


def dn_conv(p, conv_w, part, d, mode, bt=512, bc=512):
    m = p.shape[0]
    bt, bc = _fit(m, bt, 8), _fit(d, bc)
    off = part * d // bc
    last8 = m // 8 - 1
    r8 = bt // 8
    return pl.pallas_call(
        functools.partial(_dnconv_body, mode=mode),
        out_shape=jax.ShapeDtypeStruct((m, d), F32), grid=(m // bt, d // bc),
        in_specs=[pl.BlockSpec((bt, bc), lambda i, j: (i, off + j)),
                  pl.BlockSpec((8, bc), lambda i, j: (jnp.maximum(i * r8 - 1, 0), off + j)),
                  pl.BlockSpec((8, bc), lambda i, j: (jnp.minimum((i + 1) * r8, last8), off + j)),
                  pl.BlockSpec((3, bc), lambda i, j: (0, off + j))],
        out_specs=pl.BlockSpec((bt, bc), lambda i, j: (i, j)),
        compiler_params=_cparams('parallel', 'parallel'), name='dn_conv_' + mode)(p, p, p, conv_w)


def _dnab_body(x_ref, alog_ref, dtb_ref, o_ref):
    x = x_ref[...]
    lane = lax.broadcasted_iota(jnp.int32, x.shape, 1)
    g = -jnp.exp(alog_ref[...]) * _softplus(x + dtb_ref[...])
    o_ref[...] = jnp.where(lane < x.shape[1] // 2, g, _sigmoid(x))


def dn_gates(pab, a_log, dt_bias, bt=1024):
    m, w = pab.shape
    bt = min(bt, m)
    zeros = jnp.zeros((w // 2,), F32)
    alog = jnp.concatenate([a_log.reshape(-1), zeros]).reshape(1, w)
    dtb = jnp.concatenate([dt_bias.reshape(-1), zeros]).reshape(1, w)
    row = pl.BlockSpec((1, w), lambda i: (0, 0))
    blk = pl.BlockSpec((bt, w), lambda i: (i, 0))
    return pl.pallas_call(
        _dnab_body, out_shape=jax.ShapeDtypeStruct((m, w), F32), grid=(m // bt,),
        in_specs=[blk, row, row], out_specs=blk,
        compiler_params=_cparams('parallel'), name='dn_gates')(pab, alog, dtb)


def _gated_head_norm(o, z, gain):
    y = o * lax.rsqrt(jnp.mean(o * o, axis=-1, keepdims=True) + EPS) * gain
    return y * _silu(z)


def _tri_masks(c, rev):
    ii = lax.broadcasted_iota(jnp.int32, (c, c), 0)
    jj = lax.broadcasted_iota(jnp.int32, (c, c), 1)
    if rev:
        return ii == jj, ii <= jj, ii < jj
    return ii == jj, ii >= jj, ii > jj


def _dn_intra(q, k, v, g_row, beta_row, masks, tri):
    eye, incl, strict = masks
    c = q[0].shape[0]
    heads = range(len(q))
    gc_col, beta_col, g_tot, decay = [], [], [], []
    for h in heads:
        g_bc = jnp.broadcast_to(g_row[h], (c, c))
        gcc = jnp.sum(jnp.where(incl, g_bc, 0.0), axis=1, keepdims=True)
        gcr = jnp.sum(jnp.where(eye, jnp.broadcast_to(gcc, (c, c)), 0.0), axis=0, keepdims=True)
        gc_col.append(gcc)
        beta_col.append(jnp.sum(jnp.where(eye, jnp.broadcast_to(beta_row[h], (c, c)), 0.0),
                                axis=1, keepdims=True))
        g_tot.append(jnp.sum(g_row[h], axis=1, keepdims=True))
        decay.append(jnp.where(incl, jnp.exp(jnp.where(incl, gcc - gcr, 0.0)), 0.0))
    kb = [k[h] * beta_col[h] for h in heads]
    kq = [_dot_nt(jnp.concatenate([kb[h], q[h]], axis=0), k[h]) for h in heads]
    kk = [kq[h][:c] for h in heads]
    qk = [kq[h][c:] for h in heads]
    n = [-jnp.where(strict, kk[h] * decay[h], 0.0) for h in heads]
    egc = [jnp.exp(gc_col[h]) for h in heads]
    rhs = [jnp.concatenate([kb[h] * egc[h], v[h] * beta_col[h]], axis=1) for h in heads]
    sol = _unit_tri_solve_many(n, rhs, tri)
    wq = [jnp.concatenate([sol[h][:, :DN_DK], q[h] * egc[h]], axis=0) for h in heads]
    u = [sol[h][:, DN_DK:] for h in heads]
    attn = [jnp.where(incl, qk[h] * decay[h], 0.0) for h in heads]
    kd = [k[h] * jnp.exp(g_tot[h] - gc_col[h]) for h in heads]
    gl = [jnp.exp(g_tot[h]) for h in heads]
    return wq, u, attn, kd, gl


def _dn_scan_body(q_ref, k_ref, v_ref, g_ref, b_ref, s0_ref, *rest, rev, nc, hg, fused):
    if fused:
        of_ref, z_ref, gain_ref = rest[:3]
        rest = rest[3:]
    o_ref, sf_ref, s_scr, wq_scr, u_scr, at_scr, kd_scr, gl_scr = rest
    j = pl.program_id(1)

    @pl.when(j == 0)
    def _():
        s_scr[...] = s0_ref[...]

    c = DN_CHUNK
    masks = _tri_masks(c, rev)
    tri = _tri_levels(c, c)
    heads = range(hg)
    lanes = [slice(h * DN_DK, (h + 1) * DN_DK) for h in heads]

    def intra(ci, carry):
        rows = pl.ds(pl.multiple_of(ci * c, c), c)
        wq, u, attn, kd, gl = _dn_intra(
            [q_ref[rows, sl] for sl in lanes], [k_ref[rows, sl] for sl in lanes],
            [v_ref[rows, sl] for sl in lanes], [g_ref[ci, h:h + 1, :] for h in heads],
            [b_ref[ci, h:h + 1, :] for h in heads], masks, tri)
        for h in heads:
            wq_scr[ci, h] = wq[h].astype(BF16)
            u_scr[ci, h] = u[h]
            at_scr[ci, h] = attn[h].astype(BF16)
            kd_scr[ci, h] = kd[h].astype(BF16)
            gl_scr[ci, h] = jnp.broadcast_to(gl[h], (1, DN_DK))
        return carry

    lax.fori_loop(0, nc, intra, 0)

    def inter(ci, carry):
        cc = (nc - 1 - ci) if rev else ci
        rows = pl.ds(pl.multiple_of(cc * c, c), c)
        s = [s_scr[h] for h in heads]
        ws = [_dot(wq_scr[cc, h], s[h]) for h in heads]
        v_new = [u_scr[cc, h] - ws[h][:c] for h in heads]
        av = [_dot(at_scr[cc, h], v_new[h]) for h in heads]
        kv = [_dot_tn(kd_scr[cc, h], v_new[h]) for h in heads]
        for h in heads:
            o = ws[h][c:] + av[h]
            if fused:
                o = _gated_head_norm(o + of_ref[rows, lanes[h]], z_ref[rows, lanes[h]],
                                     gain_ref[:, lanes[h]])
            o_ref[rows, lanes[h]] = o.astype(o_ref.dtype)
            s_scr[h] = s[h] * gl_scr[cc, h] + kv[h]
        return carry

    lax.fori_loop(0, nc, inter, 0)

    @pl.when(j == pl.num_programs(1) - 1)
    def _():
        sf_ref[...] = s_scr[...]


def dn_scan(q, k, v, gbt, s0, direction, fuse=None, hg=16, nc=2):
    m, d = q.shape
    h = d // DN_DK
    hg = min(hg, h)
    ng = h // hg
    nc = min(nc, m // DN_CHUNK)
    bt = nc * DN_CHUNK
    nb = m // bt
    rev = direction == 1
    gb4 = gbt.reshape(4 * ng, hg, m // DN_CHUNK, DN_CHUNK).transpose(0, 2, 1, 3)

    def tb(j):
        return (nb - 1 - j) if rev else j

    tok = pl.BlockSpec((bt, hg * DN_DK), lambda g, j: (tb(j), g))
    st = pl.BlockSpec((hg, DN_DK, DN_DK), lambda g, j: (g, 0, 0))
    g_spec = pl.BlockSpec((None, nc, hg, DN_CHUNK),
                          lambda g, j: (direction * ng + g, tb(j), 0, 0))
    b_spec = pl.BlockSpec((None, nc, hg, DN_CHUNK),
                          lambda g, j: ((2 + direction) * ng + g, tb(j), 0, 0))
    args, specs = [q, k, v, gb4, gb4, s0], [tok, tok, tok, g_spec, b_spec, st]
    if fuse is not None:
        o_other, p, z_part, gain = fuse
        args += [o_other, p, gain.reshape(1, d)]
        specs += [tok, pl.BlockSpec((bt, hg * DN_DK), lambda g, j: (tb(j), z_part * ng + g)),
                  pl.BlockSpec((1, hg * DN_DK), lambda g, j: (0, g))]
    return pl.pallas_call(
        functools.partial(_dn_scan_body, rev=rev, nc=nc, hg=hg, fused=fuse is not None),
        out_shape=(jax.ShapeDtypeStruct((m, d), F32 if fuse is None else BF16),
                   jax.ShapeDtypeStruct(s0.shape, F32)),
        grid=(ng, nb), in_specs=specs, out_specs=(tok, st),
        scratch_shapes=[pltpu.VMEM((hg, DN_DK, DN_DK), F32),
                        pltpu.VMEM((nc, hg, 2 * DN_CHUNK, DN_DK), BF16),
                        pltpu.VMEM((nc, hg, DN_CHUNK, DN_DK), F32),
                        pltpu.VMEM((nc, hg, DN_CHUNK, DN_CHUNK), BF16),
                        pltpu.VMEM((nc, hg, DN_CHUNK, DN_DK), BF16),
                        pltpu.VMEM((nc, hg, 1, DN_DK), F32)],
        compiler_params=_cparams('parallel', 'arbitrary'),
        name='dn_scan_bwd' if rev else 'dn_scan_fwd')(*args)


def _hg_chunk(q_raw, f_raw, v, lb, st, masks, rev):
    _, incl, _ = masks
    c = q_raw[0].shape[0]
    heads = range(len(q_raw))
    mid = (c - 1 - c // 2) if rev else c // 2
    last = 0 if rev else c - 1
    qm, km, qs, kd, gl = [], [], [], [], []
    for h in heads:
        q = _silu(q_raw[h])
        f = lb[h] + (1.0 - lb[h]) * _sigmoid(f_raw[h])
        k = 1.0 - f
        b = _cumsum_rows(jnp.log(f), rev)
        m = b[mid:mid + 1, :]
        b_last = b[last:last + 1, :]
        qm.append(q * jnp.exp(b - m))
        km.append(k * jnp.exp(m - b))
        qs.append(q * jnp.exp(b))
        kd.append(k * jnp.exp(b_last - b))
        gl.append(jnp.exp(b_last))
    a_qk = [jnp.where(incl, _dot_nt(qm[h], km[h]), 0.0) for h in heads]
    inter = [_dot_nt(qs[h], st[h]) for h in heads]
    kv = [_dot_tn(v[h], kd[h]) for h in heads]
    intra = [_dot(a_qk[h], v[h]) for h in heads]
    o = [inter[h] + intra[h] for h in heads]
    st_new = [st[h] * gl[h] + kv[h] for h in heads]
    return o, st_new


def _hg_scan_body(q_ref, f_ref, v_ref, lb_ref, s0_ref, *rest, rev, nc, hg, fused):
    if fused:
        of_ref, z_ref, gain_ref = rest[:3]
        rest = rest[3:]
    o_ref, sf_ref, s_scr = rest
    j = pl.program_id(1)

    @pl.when(j == 0)
    def _():
        s_scr[...] = s0_ref[...]

    c = HG_CHUNK
    masks = _tri_masks(c, rev)
    heads = range(hg)
    lanes = [slice(h * HG_DK, (h + 1) * HG_DK) for h in heads]

    def chunk(ci, carry):
        cc = (nc - 1 - ci) if rev else ci
        rows = pl.ds(pl.multiple_of(cc * c, c), c)
        o, s_new = _hg_chunk([q_ref[rows, sl] for sl in lanes], [f_ref[rows, sl] for sl in lanes],
                             [v_ref[rows, sl] for sl in lanes], [lb_ref[:, sl] for sl in lanes],
                             [s_scr[h] for h in heads], masks, rev)
        for h in heads:
            oh = o[h]
            if fused:
                oh = _gated_head_norm(oh + of_ref[rows, lanes[h]], z_ref[rows, lanes[h]],
                                      gain_ref[:, lanes[h]])
            o_ref[rows, lanes[h]] = oh.astype(o_ref.dtype)
            s_scr[h] = s_new[h]
        return carry

    lax.fori_loop(0, nc, chunk, 0)

    @pl.when(j == pl.num_programs(1) - 1)
    def _():
        sf_ref[...] = s_scr[...]


def hg_scan(p, lower, s0, direction, d, fuse=None, hg=16, nc=8):
    m = p.shape[0]
    h = d // HG_DK
    hg = min(hg, h)
    ng = h // hg
    nc = min(nc, m // HG_CHUNK)
    bt = nc * HG_CHUNK
    nb = m // bt
    rev = direction == 1

    def tb(j):
        return (nb - 1 - j) if rev else j

    def part(pi):
        return pl.BlockSpec((bt, hg * HG_DK), lambda g, j: (tb(j), pi * ng + g))

    st = pl.BlockSpec((hg, HG_DK, HG_DK), lambda g, j: (g, 0, 0))
    row = pl.BlockSpec((1, hg * HG_DK), lambda g, j: (0, g))
    tok = pl.BlockSpec((bt, hg * HG_DK), lambda g, j: (tb(j), g))
    args = [p, p, p, lower.reshape(1, d), s0]
    specs = [part(0), part(1 + direction), part(3), row, st]
    if fuse is not None:
        o_other, gain = fuse
        args += [o_other, p, gain.reshape(1, d)]
        specs += [tok, part(4), row]
    return pl.pallas_call(
        functools.partial(_hg_scan_body, rev=rev, nc=nc, hg=hg, fused=fuse is not None),
        out_shape=(jax.ShapeDtypeStruct((m, d), F32 if fuse is None else BF16),
                   jax.ShapeDtypeStruct(s0.shape, F32)),
        grid=(ng, nb), in_specs=specs, out_specs=(tok, st),
        scratch_shapes=[pltpu.VMEM((hg, HG_DK, HG_DK), F32)],
        compiler_params=_cparams('parallel', 'arbitrary'),
        name='hg_scan_bwd' if rev else 'hg_scan_fwd')(*args)


def _shiftmix_body(x_ref, prev_ref, next_ref, mu_ref, *o_refs, width):
    i = pl.program_id(0)
    nb = pl.num_programs(0)
    bt, d = x_ref.shape
    dq = d // 4
    col = lax.broadcasted_iota(jnp.int32, (bt, dq), 0) % width
    for qi in range(4):
        sl = slice(qi * dq, (qi + 1) * dq)
        x = x_ref[:, sl]
        if qi == 0:
            sh = jnp.where(col == 0, 0.0, pltpu.roll(x, 1, 0))
        elif qi == 1:
            sh = jnp.where(col == width - 1, 0.0, pltpu.roll(x, bt - 1, 0))
        elif qi == 2:
            edge = jnp.where(i > 0, prev_ref[:, sl], 0.0)
            sh = edge if bt == width else jnp.concatenate([edge, x[:bt - width]], axis=0)
        else:
            edge = jnp.where(i < nb - 1, next_ref[:, sl], 0.0)
            sh = edge if bt == width else jnp.concatenate([x[width:], edge], axis=0)
        xx = sh - x
        for n, o_ref in enumerate(o_refs):
            o_ref[:, sl] = (x + xx * mu_ref[n:n + 1, sl]).astype(o_ref.dtype)


def shift_mix(h, mu, width, bt=128):
    m, d = h.shape
    bt = max(min(bt, m), width)
    rw = bt // width
    lastw = m // width - 1
    blk = pl.BlockSpec((bt, d), lambda i: (i, 0))
    out = jax.ShapeDtypeStruct((m, d), BF16)
    return pl.pallas_call(
        functools.partial(_shiftmix_body, width=width),
        out_shape=(out,) * 6, grid=(m // bt,),
        in_specs=[blk,
                  pl.BlockSpec((width, d), lambda i: (jnp.maximum(i * rw - 1, 0), 0)),
                  pl.BlockSpec((width, d), lambda i: (jnp.minimum((i + 1) * rw, lastw), 0)),
                  pl.BlockSpec((6, d), lambda i: (0, 0))],
        out_specs=(blk,) * 6, compiler_params=_cparams('arbitrary'), name='shift_mix')(h, h, h, mu)


def _halves(x, lo_mask):
    return jnp.concatenate([jnp.where(lo_mask, x, 0.0), jnp.where(lo_mask, 0.0, x)], axis=0)


def _head_sum(x, lo_mask):
    s_lo = jnp.sum(jnp.where(lo_mask, x, 0.0), axis=-1, keepdims=True)
    s_hi = jnp.sum(jnp.where(lo_mask, 0.0, x), axis=-1, keepdims=True)
    return jnp.where(lo_mask, s_lo, s_hi)


def _rw_intra(r, k, v, a_sig, logw, k_k, k_a, masks, tri, rev):
    lo, strict2, incl_cat = masks
    c = r[0].shape[0]
    groups = range(len(r))
    mid = (c - 1 - c // 2) if rev else c // 2
    last = 0 if rev else c - 1
    am2, bm2, km2, v2, rm, ae2, rg, bd2, kd2, gl = ([] for _ in range(10))
    for g in groups:
        kq = k[g] * k_k[g]
        kk = kq * lax.rsqrt(_head_sum(kq * kq, lo) + EPS)
        b = kk * a_sig[g]
        kd = k[g] * (1.0 + (a_sig[g] - 1.0) * k_a[g])
        cs = _cumsum_rows(logw[g], rev)
        ce = cs - logw[g]
        m = cs[mid:mid + 1, :]
        c_last = cs[last:last + 1, :]
        e_mc = jnp.exp(m - cs)
        dec = jnp.exp(c_last - cs)
        am2.append(_halves(-kk * jnp.exp(ce - m), lo))
        bm2.append(_halves(b * e_mc, lo))
        km2.append(_halves(kd * e_mc, lo))
        v2.append(_halves(v[g], lo))
        rm.append(r[g] * jnp.exp(cs - m))
        ae2.append(_halves(-kk * jnp.exp(ce), lo))
        rg.append(r[g] * jnp.exp(cs))
        bd2.append(_halves(b * dec, lo))
        kd2.append(_halves(kd * dec, lo))
        gl.append(jnp.exp(c_last))
    pairs = [_dot_nt(jnp.concatenate([am2[g], rm[g]], axis=0),
                     jnp.concatenate([bm2[g], km2[g]], axis=0)) for g in groups]
    n = [jnp.where(strict2, pairs[g][:2 * c, :2 * c], 0.0) for g in groups]
    a_ak = [jnp.where(strict2, pairs[g][:2 * c, 2 * c:], 0.0) for g in groups]
    a_rb = [jnp.where(incl_cat, pairs[g][2 * c:, :2 * c], 0.0) for g in groups]
    a_rk = [jnp.where(incl_cat, pairs[g][2 * c:, 2 * c:], 0.0) for g in groups]
    akv = [_dot(a_ak[g], v2[g]) for g in groups]
    ov = [_dot(a_rk[g], v2[g]) for g in groups]
    kv = [_dot_tn(v2[g], kd2[g]) for g in groups]
    sol = _unit_tri_solve_many(n, [jnp.concatenate([ae2[g], akv[g]], axis=1) for g in groups],
                               tri)
    wr = [jnp.concatenate([sol[g][:, :LANES], rg[g]], axis=0) for g in groups]
    uv2 = [sol[g][:, LANES:] for g in groups]
    return wr, uv2, a_rb, ov, bd2, kv, gl


def _rw_output(o, r, k, v, a_sum, gate, ln_w, ln_b, k_a, r_k, lo):
    inv_n = 1.0 / RW_HEAD
    oc = o - _head_sum(o, lo) * inv_n
    y = oc * lax.rsqrt(_head_sum(oc * oc, lo) * inv_n + RW_LN_EPS) * ln_w + ln_b
    k_sum = k * (2.0 + (a_sum - 2.0) * k_a)
    bonus = _head_sum(r * k_sum * r_k, lo) * v
    return (y + bonus) * gate


def _rw_masks(c, rev):
    lo = lax.broadcasted_iota(jnp.int32, (1, LANES), 1) < RW_HEAD
    ii = lax.broadcasted_iota(jnp.int32, (2 * c, 2 * c), 0)
    jj = lax.broadcasted_iota(jnp.int32, (2 * c, 2 * c), 1)
    same = (ii // c) == (jj // c)
    il, jl = ii % c, jj % c
    strict2 = same & ((il < jl) if rev else (il > jl))
    ic = lax.broadcasted_iota(jnp.int32, (c, 2 * c), 0)
    jc = lax.broadcasted_iota(jnp.int32, (c, 2 * c), 1) % c
    incl_cat = (ic <= jc) if rev else (ic >= jc)
    return lo, strict2, incl_cat


def _rw_scan_body(r_ref, k_ref, v_ref, a_ref, lw_ref, kk_ref, ka_ref, s0_ref, *rest,
                  rev, nc, hg, fused):
    if fused:
        of_ref, ao_ref, gate_ref, lnw_ref, lnb_ref, rk_ref = rest[:6]
        rest = rest[6:]
    o_ref, sf_ref, s_scr, wr_scr, uv_scr, arb_scr, ov_scr, bd_scr, kv_scr, gl_scr = rest
    j = pl.program_id(1)

    @pl.when(j == 0)
    def _():
        s_scr[...] = s0_ref[...]

    c = RW_CHUNK
    masks = _rw_masks(c, rev)
    tri = _tri_levels(2 * c, c)
    groups = range(hg)
    lanes = [slice(g * LANES, (g + 1) * LANES) for g in groups]

    def intra(ci, carry):
        rows = pl.ds(pl.multiple_of(ci * c, c), c)
        wr, uv2, a_rb, ov, bd2, kv, gl = _rw_intra(
            [r_ref[rows, sl] for sl in lanes], [k_ref[rows, sl] for sl in lanes],
            [v_ref[rows, sl] for sl in lanes], [a_ref[rows, sl] for sl in lanes],
            [lw_ref[rows, sl] for sl in lanes], [kk_ref[:, sl] for sl in lanes],
            [ka_ref[:, sl] for sl in lanes], masks, tri, rev)
        for g in groups:
            wr_scr[ci, g] = wr[g].astype(BF16)
            uv_scr[ci, g] = uv2[g]
            arb_scr[ci, g] = a_rb[g].astype(BF16)
            ov_scr[ci, g] = ov[g]
            bd_scr[ci, g] = bd2[g].astype(BF16)
            kv_scr[ci, g] = kv[g]
            gl_scr[ci, g] = gl[g]
        return carry

    lax.fori_loop(0, nc, intra, 0)

    def inter(ci, carry):
        cc = (nc - 1 - ci) if rev else ci
        rows = pl.ds(pl.multiple_of(cc * c, c), c)
        st = [s_scr[g] for g in groups]
        ws = [_dot_nt(wr_scr[cc, g], st[g]) for g in groups]
        u2 = [ws[g][:2 * c] + uv_scr[cc, g] for g in groups]
        au = [_dot(arb_scr[cc, g], u2[g]) for g in groups]
        ub = [_dot_tn(u2[g], bd_scr[cc, g]) for g in groups]
        for g in groups:
            o = ws[g][2 * c:] + au[g] + ov_scr[cc, g]
            if fused:
                sl = lanes[g]
                o = _rw_output(o + of_ref[rows, sl], r_ref[rows, sl], k_ref[rows, sl],
                               v_ref[rows, sl], a_ref[rows, sl] + ao_ref[rows, sl],
                               gate_ref[rows, sl], lnw_ref[:, sl], lnb_ref[:, sl], ka_ref[:, sl],
                               rk_ref[:, sl], masks[0])
            o_ref[rows, lanes[g]] = o.astype(o_ref.dtype)
            s_scr[g] = st[g] * gl_scr[cc, g] + kv_scr[cc, g] + ub[g]
        return carry

    lax.fori_loop(0, nc, inter, 0)

    @pl.when(j == pl.num_programs(1) - 1)
    def _():
        sf_ref[...] = s_scr[...]


def rw_scan(r, k, v, a_sig, logw, k_k, k_a, s0, direction, fuse=None, hg=16, nc=2):
    m, d = r.shape
    npair = d // LANES
    hg = min(hg, npair)
    ng = npair // hg
    nc = min(nc, m // RW_CHUNK)
    bt = nc * RW_CHUNK
    nb = m // bt
    rev = direction == 1

    def tb(j):
        return (nb - 1 - j) if rev else j

    tok = pl.BlockSpec((bt, hg * LANES), lambda g, j: (tb(j), g))
    row = pl.BlockSpec((1, hg * LANES), lambda g, j: (0, g))
    st = pl.BlockSpec((hg, LANES, LANES), lambda g, j: (g, 0, 0))
    args = [r, k, v, a_sig, logw, k_k.reshape(1, d), k_a.reshape(1, d), s0]
    specs = [tok, tok, tok, tok, tok, row, row, st]
    if fuse is not None:
        args += list(fuse[:3]) + [t.reshape(1, d) for t in fuse[3:]]
        specs += [tok, tok, tok, row, row, row]
    return pl.pallas_call(
        functools.partial(_rw_scan_body, rev=rev, nc=nc, hg=hg, fused=fuse is not None),
        out_shape=(jax.ShapeDtypeStruct((m, d), F32 if fuse is None else BF16),
                   jax.ShapeDtypeStruct(s0.shape, F32)),
        grid=(ng, nb), in_specs=specs, out_specs=(tok, st),
        scratch_shapes=[pltpu.VMEM((hg, LANES, LANES), F32),
                        pltpu.VMEM((nc, hg, 3 * RW_CHUNK, LANES), BF16),
                        pltpu.VMEM((nc, hg, 2 * RW_CHUNK, LANES), F32),
                        pltpu.VMEM((nc, hg, RW_CHUNK, 2 * RW_CHUNK), BF16),
                        pltpu.VMEM((nc, hg, RW_CHUNK, LANES), F32),
                        pltpu.VMEM((nc, hg, 2 * RW_CHUNK, LANES), BF16),
                        pltpu.VMEM((nc, hg, LANES, LANES), F32),
                        pltpu.VMEM((nc, hg, 1, LANES), F32)],
        compiler_params=_cparams('parallel', 'arbitrary'),
        name='rw_scan_bwd' if rev else 'rw_scan_fwd')(*args)


def _convglu_body(u_ref, prev_ref, next_ref, v_ref, cw_ref, cb_ref, o_ref, *, width):
    i = pl.program_id(0)
    nb = pl.num_programs(0)
    x = u_ref[...].astype(F32)
    bt = x.shape[0]
    up = jnp.where(i > 0, prev_ref[...].astype(F32), 0.0)
    dn = jnp.where(i < nb - 1, next_ref[...].astype(F32), 0.0)
    if bt > width:
        up = jnp.concatenate([up, x[:bt - width]], axis=0)
        dn = jnp.concatenate([x[width:], dn], axis=0)
    col = lax.broadcasted_iota(jnp.int32, x.shape, 0) % width
    rows = (up, x, dn)

    def column(kw):
        return sum(cw_ref[3 * r + kw:3 * r + kw + 1, :] * rows[r] for r in range(3))

    left = jnp.where(col == 0, 0.0, pltpu.roll(column(0), 1, 0))
    right = jnp.where(col == width - 1, 0.0, pltpu.roll(column(2), bt - 1, 0))
    h = column(1) + left + right + cb_ref[...]
    c = math.sqrt(2.0 / math.pi)
    t = jnp.tanh(h * (2.0 * c + (8.0 * 0.044715 * c) * (h * h)))
    o_ref[...] = ((h + h * t) * v_ref[...].astype(F32)).astype(o_ref.dtype)


def conv_glu_gate(uv, conv_w, conv_b, width, f, bt=512, bc=512):
    m = uv.shape[0]
    bt = max(min(bt, m), width)
    bc = _fit(f, bc)
    rw = bt // width
    lastw = m // width - 1
    voff = f // bc
    return pl.pallas_call(
        functools.partial(_convglu_body, width=width),
        out_shape=jax.ShapeDtypeStruct((m, f), BF16), grid=(m // bt, f // bc),
        in_specs=[pl.BlockSpec((bt, bc), lambda i, j: (i, j)),
                  pl.BlockSpec((width, bc), lambda i, j: (jnp.maximum(i * rw - 1, 0), j)),
                  pl.BlockSpec((width, bc), lambda i, j: (jnp.minimum((i + 1) * rw, lastw), j)),
                  pl.BlockSpec((bt, bc), lambda i, j: (i, voff + j)),
                  pl.BlockSpec((9, bc), lambda i, j: (0, j)),
                  pl.BlockSpec((1, bc), lambda i, j: (0, j))],
        out_specs=pl.BlockSpec((bt, bc), lambda i, j: (i, j)),
        compiler_params=_cparams('parallel', 'parallel'), name='conv_glu')(
            uv, uv, uv, uv, 0.5 * conv_w.reshape(9, f), 0.5 * conv_b.reshape(1, f))


def _modulation(conds, down, up, bias):
    low = matmul(conds, down, precise=True, bm=8, bn=512, name='ada_down')
    return matmul(low, up, bias=bias, precise=True, bm=8, bn=2048, name='ada_up')


def _dn_mixer(streams, w, d, last):
    heads = d // DN_DK
    w_main = w['w_in'][:, :4 * d].astype(BF16)
    w_ab = w['w_in'][:, 4 * d:].astype(BF16)
    gain = jnp.tile(w['norm_g'], heads)
    prepped = []
    for h in streams:
        p = matmul(h, w_main, name='dn_in')
        pab = matmul(h, w_ab, bn=4 * heads, name='dn_in_ab')
        q = dn_conv(p, w['conv_w'], 0, d, 'q')
        k = dn_conv(p, w['conv_w'], 1, d, 'k')
        v = dn_conv(p, w['conv_w'], 2, d, 'v')
        gb = dn_gates(pab, w['a_log'], w['dt_bias'])
        prepped.append((p, q, k, v, gb.T))
    states = [jnp.zeros((heads, DN_DK, DN_DK), F32)] * 2
    outs = []
    for si, (p, q, k, v, gbt) in enumerate(prepped):
        o_f, states[0] = dn_scan(q, k, v, gbt, states[0], 0)
        y, states[1] = dn_scan(q, k, v, gbt, states[1], 1, fuse=(o_f, p, 3, gain))
        outs.append(None if si == 0 and last else y)
    return outs


def _hg_mixer(streams, w, d, last):
    heads = d // HG_DK
    w_in = w['w_in'].astype(BF16)
    states = [jnp.zeros((heads, HG_DK, HG_DK), F32)] * 2
    outs = []
    for si, h in enumerate(streams):
        p = matmul(h, w_in, name='hg_in')
        o_f, states[0] = hg_scan(p, w['lower'], states[0], 0, d)
        y, states[1] = hg_scan(p, w['lower'], states[1], 1, d, fuse=(o_f, w['norm_g']))
        outs.append(None if si == 0 and last else y)
    return outs


def _rw_mixer(streams, widths, w, d, last):
    w_rkv = w['w_rkv'].astype(BF16)
    w1 = jnp.concatenate([w['w1'][0], w['w1'][1]], axis=1).astype(BF16)
    a1 = jnp.concatenate([w['a1'][0], w['a1'][1]], axis=1).astype(BF16)
    w2 = w['w2'].astype(BF16)
    a2 = w['a2'].astype(BF16)
    lora = w['w1'].shape[-1]
    gl = w['g1'].shape[-1]
    glp = -(-gl // LANES) * LANES
    g1 = jnp.pad(w['g1'], ((0, 0), (0, glp - gl))).astype(BF16)
    g2 = jnp.pad(w['g2'], ((0, glp - gl), (0, 0))).astype(BF16)
    states = [jnp.zeros((d // LANES, LANES, LANES), F32)] * 2
    outs = []
    for si, (h, width) in enumerate(zip(streams, widths)):
        xr, xw, xk, xv, xa, xg = shift_mix(h, w['mu'], width)
        r = matmul(xr, w_rkv[0], name='rw_r')
        k = matmul(xk, w_rkv[1], name='rw_k')
        v = matmul(xv, w_rkv[2], name='rw_v')
        tw = matmul(xw, w1, epi='tanh', out_dtype=BF16, bn=2 * lora, name='rw_w1')
        ta = matmul(xa, a1, out_dtype=BF16, bn=2 * lora, name='rw_a1')
        tg = matmul(xg, g1, epi='sigmoid', out_dtype=BF16, name='rw_g1')
        gate = matmul(tg, g2, name='rw_g2')
        logw, a_sig = [], []
        for direction in range(2):
            logw.append(matmul(tw, w2[direction], bias=w['w0'][direction], epi='logw',
                               a_koff=direction, name='rw_w2'))
            a_sig.append(matmul(ta, a2[direction], bias=w['a0'][direction], epi='sigmoid',
                                a_koff=direction, name='rw_a2'))
        o_f, states[0] = rw_scan(r, k, v, a_sig[0], logw[0], w['k_k'], w['k_a'], states[0], 0)
        y, states[1] = rw_scan(r, k, v, a_sig[1], logw[1], w['k_k'], w['k_a'], states[1], 1,
                               fuse=(o_f, a_sig[0], gate, w['lnx_w'], w['lnx_b'],
                                     w['r_k'].reshape(-1)))
        outs.append(None if si == 0 and last else y)
    return outs


def kernel(x, c, ctx, c_ctx, ada_down, ada_up, ada_b, norm1_g, norm2_g, ffn_w_up, ffn_conv_w, ffn_conv_b, ffn_w_down, dn_w_in, dn_conv_w, dn_a_log, dn_dt_bias, dn_norm_g, dn_w_out, hg_w_in, hg_lower, hg_norm_g, hg_w_out, rw_mu, rw_w_rkv, rw_w0, rw_w1, rw_w2, rw_a0, rw_a1, rw_a2, rw_g1, rw_g2, rw_k_k, rw_k_a, rw_r_k, rw_lnx_w, rw_lnx_b, rw_w_out, final_g):
    _, seq, d = x.shape
    depth = ada_down.shape[0]
    n_ctx = ctx.shape[1]
    f = ffn_w_down.shape[1]
    xl, xc = x[0], ctx[0]
    sm = jax.nn.softmax(hg_lower.astype(F32), axis=0)
    lower_bounds = jnp.cumsum(sm, axis=0) - sm[0]
    conds = jnp.zeros((8, d), F32).at[0].set(jax.nn.silu(c[0])).at[1].set(jax.nn.silu(c_ctx))
    widths = (n_ctx, GRID_W)
    for i in range(depth):
        kind, j = i % 3, i // 3
        last = i == depth - 1
        mod = _modulation(conds, ada_down[i], ada_up[i], ada_b[i])
        mods = [[mod[row, n * d:(n + 1) * d] for n in range(6)] for row in (1, 0)]
        xs = [xc, xl]
        h_dtype = F32 if kind == 2 else BF16
        hs = [normmod(t, norm1_g[i], mm[1], mm[0], h_dtype) for t, mm in zip(xs, mods)]
        if kind == 0:
            w = dict(w_in=dn_w_in[j], conv_w=dn_conv_w[j], a_log=dn_a_log[j],
                     dt_bias=dn_dt_bias[j], norm_g=dn_norm_g[j])
            ys = _dn_mixer(hs, w, d, last)
            w_out = dn_w_out[j]
        elif kind == 1:
            w = dict(w_in=hg_w_in[j], lower=lower_bounds[i], norm_g=hg_norm_g[j])
            ys = _hg_mixer(hs, w, d, last)
            w_out = hg_w_out[j]
        else:
            w = dict(mu=rw_mu[j], w_rkv=rw_w_rkv[j], w0=rw_w0[j], w1=rw_w1[j], w2=rw_w2[j],
                     a0=rw_a0[j], a1=rw_a1[j], a2=rw_a2[j], g1=rw_g1[j], g2=rw_g2[j],
                     k_k=rw_k_k[j], k_a=rw_k_a[j], r_k=rw_r_k[j], lnx_w=rw_lnx_w[j],
                     lnx_b=rw_lnx_b[j])
            ys = _rw_mixer(hs, widths, w, d, last)
            w_out = rw_w_out[j]
        w_out = w_out.astype(BF16)
        w_up = ffn_w_up[i].astype(BF16)
        w_down = ffn_w_down[i].astype(BF16)
        new = []
        for t, y, mm, width in zip(xs, ys, mods, widths):
            if y is None:
                new.append(t)
                continue
            t = matmul(y, w_out, gate=mm[2], resid=t, name='mix_out')
            h2 = normmod(t, norm2_g[i], mm[4], mm[3], BF16)
            uv = matmul(h2, w_up, out_dtype=BF16, name='ffn_up')
            gated = conv_glu_gate(uv, ffn_conv_w[i], ffn_conv_b[i], width, f)
            t = matmul(gated, w_down, gate=mm[5], resid=t, bm=512, bn=1024, w_single=True,
                       name='ffn_down')
            new.append(t)
        xc, xl = new
    zeros = jnp.zeros((d,), F32)
    return normmod(xl, final_g, zeros, zeros, F32)[None]
```

```python
import functools
import math

import jax
import jax.numpy as jnp
from jax import lax
from jax.experimental import pallas as pl
from jax.experimental.pallas import tpu as pltpu

F32 = jnp.float32
BF16 = jnp.bfloat16

EPS = 1e-6
RW_LN_EPS = 64e-5
GRID_W = 64
DN_DK = 128
DN_CHUNK = 128
HG_DK = 128
HG_CHUNK = 32
RW_HEAD = 64
RW_CHUNK = 64
LANES = 128
VMEM_LIMIT = 60 * 1024 * 1024


def _fit(n, b, unit=LANES):
    if n <= b:
        return n
    for cand in range(b - b % unit, 0, -unit):
        if n % cand == 0:
            return cand
    raise ValueError((n, b, unit))


def _cparams(*sem):
    return pltpu.CompilerParams(dimension_semantics=sem, vmem_limit_bytes=VMEM_LIMIT)


def _sigmoid(x):
    return 1.0 / (1.0 + jnp.exp(-x))


def _silu(x):
    return x * _sigmoid(x)


def _softplus(x):
    return jnp.maximum(x, 0.0) + jnp.log(1.0 + jnp.exp(-jnp.abs(x)))


def _gelu_tanh(x):
    c = math.sqrt(2.0 / math.pi)
    return 0.5 * x * (1.0 + jnp.tanh(c * (x + 0.044715 * (x * x * x))))


def _dot(a, b, dims=(((1,), (0,)), ((), ()))):
    return lax.dot_general(a.astype(BF16), b.astype(BF16), dims, preferred_element_type=F32)


def _dot_nt(a, b):
    return _dot(a, b, (((1,), (1,)), ((), ())))


def _dot_tn(a, b):
    return _dot(a, b, (((0,), (0,)), ((), ())))


def _split2(x):
    hi = x.astype(BF16)
    lo = (x - hi.astype(F32)).astype(BF16)
    return hi, lo


def _dot3(a, b, dims=(((1,), (0,)), ((), ()))):
    ah, al = _split2(a)
    bh, bl = _split2(b)
    d = functools.partial(lax.dot_general, dimension_numbers=dims, preferred_element_type=F32)
    return d(ah, bh) + (d(ah, bl) + d(al, bh))


def _tri_levels(c, size):
    ii = lax.broadcasted_iota(jnp.int32, (c, c), 0)
    jj = lax.broadcasted_iota(jnp.int32, (c, c), 1)
    levels = []
    s = 1
    while (1 << s) < size:
        lo = (ii >> s) != (jj >> s)
        hi = (ii >> (s + 1)) == (jj >> (s + 1))
        levels.append(lo & hi)
        s += 1
    return (ii >> 1) == (jj >> 1), ii == jj, levels


def _unit_tri_solve_many(ns, rhss, tri):
    pair, eye, levels = tri
    units = range(len(ns))
    ts = [jnp.where(pair, jnp.where(eye, 1.0, n), 0.0) for n in ns]
    for lmask in levels:
        ls = [jnp.where(lmask, n, 0.0) for n in ns]
        lt = [_dot(ls[i], ts[i]) for i in units]
        ts = [ts[i] + _dot(ts[i], lt[i]) for i in units]
    return [_dot(ts[i], rhss[i]) for i in units]


def _cumsum_rows(x, rev):
    c = x.shape[0]
    row = lax.broadcasted_iota(jnp.int32, x.shape, 0)
    s = 1
    while s < c:
        if rev:
            x = x + jnp.where(row < c - s, pltpu.roll(x, c - s, 0), 0.0)
        else:
            x = x + jnp.where(row >= s, pltpu.roll(x, s, 0), 0.0)
        s *= 2
    return x


def _epi_none(y):
    return y


def _epi_logw(y):
    return -jnp.exp(-_softplus(-y) - 0.5)


_EPILOGUES = {'none': _epi_none, 'tanh': jnp.tanh, 'sigmoid': _sigmoid, 'logw': _epi_logw}


def _mm_body(*refs, nk, epi, has_bias, has_gate, has_resid, precise):
    it = iter(refs)
    a_ref, w_ref = next(it), next(it)
    bias_ref = next(it) if has_bias else None
    gate_ref = next(it) if has_gate else None
    resid_ref = next(it) if has_resid else None
    o_ref = next(it)
    acc_ref = next(it) if nk > 1 else None

    def finish(y):
        if has_bias:
            y = y + bias_ref[...]
        y = _EPILOGUES[epi](y)
        if has_gate:
            y = y * gate_ref[...]
        if has_resid:
            y = resid_ref[...] + y
        o_ref[...] = y.astype(o_ref.dtype)

    if precise:
        part = _dot3(a_ref[...], w_ref[...])
    else:
        part = _dot(a_ref[...], w_ref[...])
    if nk == 1:
        finish(part)
    else:
        k = pl.program_id(2)

        @pl.when(k == 0)
        def _():
            acc_ref[...] = part

        @pl.when(k > 0)
        def _():
            acc_ref[...] += part

        @pl.when(k == nk - 1)
        def _():
            finish(acc_ref[...])


def matmul(a, w, *, bias=None, gate=None, resid=None, epi='none', out_dtype=F32,
           bm=1024, bn=1024, bk=None, a_koff=0, precise=False, w_single=False, w_cols=None,
           name='matmul'):
    m = a.shape[0]
    k, n = w.shape
    c0, n = (0, n) if w_cols is None else w_cols
    bm, bn = _fit(m, bm, 8), _fit(n, bn)
    assert c0 % bn == 0
    c0 //= bn
    bk = k if bk is None else _fit(k, bk)
    nk = k // bk
    if nk == 1:
        grid = (n // bn, m // bm)
        a_spec = pl.BlockSpec((bm, bk), lambda j, i: (i, a_koff))
        w_mode = dict(pipeline_mode=pl.Buffered(1)) if w_single else {}
        w_spec = pl.BlockSpec((bk, bn), lambda j, i: (0, j + c0), **w_mode)
        row_spec = pl.BlockSpec((1, bn), lambda j, i: (0, j))
        o_spec = pl.BlockSpec((bm, bn), lambda j, i: (i, j))
        sem = ('parallel', 'parallel')
        scratch = []
    else:
        grid = (n // bn, m // bm, nk)
        a_spec = pl.BlockSpec((bm, bk), lambda j, i, kk: (i, kk + a_koff * nk))
        w_spec = pl.BlockSpec((bk, bn), lambda j, i, kk: (kk, j + c0))
        row_spec = pl.BlockSpec((1, bn), lambda j, i, kk: (0, j))
        o_spec = pl.BlockSpec((bm, bn), lambda j, i, kk: (i, j))
        sem = ('parallel', 'parallel', 'arbitrary')
        scratch = [pltpu.VMEM((bm, bn), F32)]
    args, specs = [a, w], [a_spec, w_spec]
    for extra in (bias, gate):
        if extra is not None:
            args.append(extra.reshape(1, n).astype(F32))
            specs.append(row_spec)
    if resid is not None:
        args.append(resid)
        specs.append(o_spec)
    body = functools.partial(_mm_body, nk=nk, epi=epi, has_bias=bias is not None,
                             has_gate=gate is not None, has_resid=resid is not None,
                             precise=precise)
    return pl.pallas_call(
        body, out_shape=jax.ShapeDtypeStruct((m, n), out_dtype), grid=grid,
        in_specs=specs, out_specs=o_spec, scratch_shapes=scratch,
        compiler_params=_cparams(*sem), name=name)(*args)


def _normmod_body(x_ref, g_ref, sc_ref, sh_ref, o_ref):
    x = x_ref[...]
    y = x * lax.rsqrt(jnp.mean(x * x, axis=-1, keepdims=True) + EPS) * g_ref[...]
    o_ref[...] = (y * (1.0 + sc_ref[...]) + sh_ref[...]).astype(o_ref.dtype)


def normmod(x, g, sc, sh, out_dtype, bt=256):
    m, d = x.shape
    bt = min(bt, m)
    row = pl.BlockSpec((1, d), lambda i: (0, 0))
    blk = pl.BlockSpec((bt, d), lambda i: (i, 0))
    return pl.pallas_call(
        _normmod_body, out_shape=jax.ShapeDtypeStruct((m, d), out_dtype), grid=(m // bt,),
        in_specs=[blk, row, row, row], out_specs=blk,
        compiler_params=_cparams('parallel'), name='normmod')(
            x, g.reshape(1, d), sc.reshape(1, d), sh.reshape(1, d))


def _dnconv_body(x_ref, prev_ref, next_ref, w_ref, o_ref, *, mode):
    i = pl.program_id(0)
    nb = pl.num_programs(0)
    x = x_ref[...]
    bt = x.shape[0]
    row = lax.broadcasted_iota(jnp.int32, x.shape, 0)
    prev_row = jnp.where(i > 0, prev_ref[7:8, :], 0.0)
    next_row = jnp.where(i < nb - 1, next_ref[0:1, :], 0.0)
    xm = jnp.where(row == 0, prev_row, pltpu.roll(x, 1, 0))
    xp = jnp.where(row == bt - 1, next_row, pltpu.roll(x, bt - 1, 0))
    y = w_ref[0:1, :] * xm + w_ref[1:2, :] * x + w_ref[2:3, :] * xp
    y = _silu(y)
    if mode == 'v':
        o_ref[...] = y.astype(o_ref.dtype)
        return
    scale = DN_DK ** -0.5 if mode == 'q' else 1.0
    for h in range(x.shape[1] // DN_DK):
        sl = slice(h * DN_DK, (h + 1) * DN_DK)
        yh = y[:, sl]
        inv = lax.rsqrt(jnp.sum(yh * yh, axis=-1, keepdims=True) + EPS)
        o_ref[:, sl] = (yh * (inv * scale)).astype(o_ref.dtype)


def dn_conv(p, conv_w, part, d, mode, bt=512, bc=512):
    m = p.shape[0]
    bt, bc = _fit(m, bt, 8), _fit(d, bc)
    off = part * d // bc
    last8 = m // 8 - 1
    r8 = bt // 8
    return pl.pallas_call(
        functools.partial(_dnconv_body, mode=mode),
        out_shape=jax.ShapeDtypeStruct((m, d), F32), grid=(m // bt, d // bc),
        in_specs=[pl.BlockSpec((bt, bc), lambda i, j: (i, off + j)),
                  pl.BlockSpec((8, bc), lambda i, j: (jnp.maximum(i * r8 - 1, 0), off + j)),
                  pl.BlockSpec((8, bc), lambda i, j: (jnp.minimum((i + 1) * r8, last8), off + j)),
                  pl.BlockSpec((3, bc), lambda i, j: (0, off + j))],
        out_specs=pl.BlockSpec((bt, bc), lambda i, j: (i, j)),
        compiler_params=_cparams('parallel', 'parallel'), name='dn_conv_' + mode)(p, p, p, conv_w)


def _dnab_body(x_ref, alog_ref, dtb_ref, o_ref):
    x = x_ref[...]
    lane = lax.broadcasted_iota(jnp.int32, x.shape, 1)
    g = -jnp.exp(alog_ref[...]) * _softplus(x + dtb_ref[...])
    o_ref[...] = jnp.where(lane < x.shape[1] // 2, g, _sigmoid(x))


def dn_gates(pab, a_log, dt_bias, bt=1024):
    m, w = pab.shape
    bt = min(bt, m)
    zeros = jnp.zeros((w // 2,), F32)
    alog = jnp.concatenate([a_log.reshape(-1), zeros]).reshape(1, w)
    dtb = jnp.concatenate([dt_bias.reshape(-1), zeros]).reshape(1, w)
    row = pl.BlockSpec((1, w), lambda i: (0, 0))
    blk = pl.BlockSpec((bt, w), lambda i: (i, 0))
    return pl.pallas_call(
        _dnab_body, out_shape=jax.ShapeDtypeStruct((m, w), F32), grid=(m // bt,),
        in_specs=[blk, row, row], out_specs=blk,
        compiler_params=_cparams('parallel'), name='dn_gates')(pab, alog, dtb)


def _cast_jobs(ws, steps, step_index):
    args, in_specs, out_shapes, out_specs = [], [], [], []
    for w in ws:
        rows, cols = w.shape
        ratio = next(r for r in (1, 2, 4, 8, 16) if steps % r == 0
                     and rows % (steps // r) == 0 and (rows // (steps // r)) % 16 == 0)
        blk = rows // (steps // ratio)
        spec = pl.BlockSpec((blk, cols), lambda g, j, ratio=ratio: (step_index(g, j) // ratio, 0))
        args.append(w)
        in_specs.append(spec)
        out_shapes.append(jax.ShapeDtypeStruct(w.shape, BF16))
        out_specs.append(spec)
    return args, in_specs, out_shapes, out_specs


def _split_refs(rest, n_fused, n_cast):
    a, b, c, d = n_fused, n_fused + n_cast, n_fused + n_cast + 2, n_fused + 2 * n_cast + 2
    return rest[:a], rest[a:b], rest[b:c], rest[c:d], rest[d:]


def _run_casts(cast_in, cast_out):
    for wi, wo in zip(cast_in, cast_out):
        wo[...] = wi[...].astype(BF16)


def _gated_head_norm(o, z, gain):
    y = o * lax.rsqrt(jnp.mean(o * o, axis=-1, keepdims=True) + EPS) * gain
    return y * _silu(z)


def _tri_masks(c, rev):
    ii = lax.broadcasted_iota(jnp.int32, (c, c), 0)
    jj = lax.broadcasted_iota(jnp.int32, (c, c), 1)
    if rev:
        return ii == jj, ii <= jj, ii < jj
    return ii == jj, ii >= jj, ii > jj


def _dn_intra(q, k, v, g_row, beta_row, masks, tri):
    eye, incl, strict = masks
    c = q[0].shape[0]
    heads = range(len(q))
    gc_col, beta_col, g_tot, decay = [], [], [], []
    for h in heads:
        g_bc = jnp.broadcast_to(g_row[h], (c, c))
        gcc = jnp.sum(jnp.where(incl, g_bc, 0.0), axis=1, keepdims=True)
        gcr = jnp.sum(jnp.where(eye, jnp.broadcast_to(gcc, (c, c)), 0.0), axis=0, keepdims=True)
        gc_col.append(gcc)
        beta_col.append(jnp.sum(jnp.where(eye, jnp.broadcast_to(beta_row[h], (c, c)), 0.0),
                                axis=1, keepdims=True))
        g_tot.append(jnp.sum(g_row[h], axis=1, keepdims=True))
        decay.append(jnp.where(incl, jnp.exp(jnp.where(incl, gcc - gcr, 0.0)), 0.0))
    kb = [k[h] * beta_col[h] for h in heads]
    kq = [_dot_nt(jnp.concatenate([kb[h], q[h]], axis=0), k[h]) for h in heads]
    kk = [kq[h][:c] for h in heads]
    qk = [kq[h][c:] for h in heads]
    n = [-jnp.where(strict, kk[h] * decay[h], 0.0) for h in heads]
    egc = [jnp.exp(gc_col[h]) for h in heads]
    rhs = [jnp.concatenate([kb[h] * egc[h], v[h] * beta_col[h]], axis=1) for h in heads]
    sol = _unit_tri_solve_many(n, rhs, tri)
    wq = [jnp.concatenate([sol[h][:, :DN_DK], q[h] * egc[h]], axis=0) for h in heads]
    u = [sol[h][:, DN_DK:] for h in heads]
    attn = [jnp.where(incl, qk[h] * decay[h], 0.0) for h in heads]
    kd = [k[h] * jnp.exp(g_tot[h] - gc_col[h]) for h in heads]
    gl = [jnp.exp(g_tot[h]) for h in heads]
    return wq, u, attn, kd, gl


def _dn_scan_body(q_ref, k_ref, v_ref, g_ref, b_ref, s0_ref, *rest, rev, nc, hg, fused, n_cast):
    fused_refs, cast_in, (o_ref, sf_ref), cast_out, scratch = _split_refs(
        rest, 3 if fused else 0, n_cast)
    if fused:
        of_ref, z_ref, gain_ref = fused_refs
    s_scr, wq_scr, u_scr, at_scr, kd_scr, gl_scr = scratch
    _run_casts(cast_in, cast_out)
    j = pl.program_id(1)

    @pl.when(j == 0)
    def _():
        s_scr[...] = s0_ref[...]

    c = DN_CHUNK
    masks = _tri_masks(c, rev)
    tri = _tri_levels(c, c)
    heads = range(hg)
    lanes = [slice(h * DN_DK, (h + 1) * DN_DK) for h in heads]

    def intra(ci, carry):
        rows = pl.ds(pl.multiple_of(ci * c, c), c)
        wq, u, attn, kd, gl = _dn_intra(
            [q_ref[rows, sl] for sl in lanes], [k_ref[rows, sl] for sl in lanes],
            [v_ref[rows, sl] for sl in lanes], [g_ref[ci, h:h + 1, :] for h in heads],
            [b_ref[ci, h:h + 1, :] for h in heads], masks, tri)
        for h in heads:
            wq_scr[ci, h] = wq[h].astype(BF16)
            u_scr[ci, h] = u[h]
            at_scr[ci, h] = attn[h].astype(BF16)
            kd_scr[ci, h] = kd[h].astype(BF16)
            gl_scr[ci, h] = jnp.broadcast_to(gl[h], (1, DN_DK))
        return carry

    lax.fori_loop(0, nc, intra, 0)

    def inter(ci, carry):
        cc = (nc - 1 - ci) if rev else ci
        rows = pl.ds(pl.multiple_of(cc * c, c), c)
        s = [s_scr[h] for h in heads]
        ws = [_dot(wq_scr[cc, h], s[h]) for h in heads]
        v_new = [u_scr[cc, h] - ws[h][:c] for h in heads]
        av = [_dot(at_scr[cc, h], v_new[h]) for h in heads]
        kv = [_dot_tn(kd_scr[cc, h], v_new[h]) for h in heads]
        for h in heads:
            o = ws[h][c:] + av[h]
            if fused:
                o = _gated_head_norm(o + of_ref[rows, lanes[h]], z_ref[rows, lanes[h]],
                                     gain_ref[:, lanes[h]])
            o_ref[rows, lanes[h]] = o.astype(o_ref.dtype)
            s_scr[h] = s[h] * gl_scr[cc, h] + kv[h]
        return carry

    lax.fori_loop(0, nc, inter, 0)

    @pl.when(j == pl.num_programs(1) - 1)
    def _():
        sf_ref[...] = s_scr[...]


def dn_scan(q, k, v, gbt, s0, direction, fuse=None, casts=(), hg=16, nc=2):
    m, d = q.shape
    h = d // DN_DK
    hg = min(hg, h)
    ng = h // hg
    nc = min(nc, m // DN_CHUNK)
    bt = nc * DN_CHUNK
    nb = m // bt
    rev = direction == 1
    gb4 = gbt.reshape(4 * ng, hg, m // DN_CHUNK, DN_CHUNK).transpose(0, 2, 1, 3)

    def tb(j):
        return (nb - 1 - j) if rev else j

    tok = pl.BlockSpec((bt, hg * DN_DK), lambda g, j: (tb(j), g))
    st = pl.BlockSpec((hg, DN_DK, DN_DK), lambda g, j: (g, 0, 0))
    g_spec = pl.BlockSpec((None, nc, hg, DN_CHUNK),
                          lambda g, j: (direction * ng + g, tb(j), 0, 0))
    b_spec = pl.BlockSpec((None, nc, hg, DN_CHUNK),
                          lambda g, j: ((2 + direction) * ng + g, tb(j), 0, 0))
    args, specs = [q, k, v, gb4, gb4, s0], [tok, tok, tok, g_spec, b_spec, st]
    if fuse is not None:
        o_other, p, z_part, gain = fuse
        args += [o_other, p, gain.reshape(1, d)]
        specs += [tok, pl.BlockSpec((bt, hg * DN_DK), lambda g, j: (tb(j), z_part * ng + g)),
                  pl.BlockSpec((1, hg * DN_DK), lambda g, j: (0, g))]
    c_args, c_in, c_shapes, c_out = _cast_jobs(casts, ng * nb, lambda g, j: g * nb + j)
    res = pl.pallas_call(
        functools.partial(_dn_scan_body, rev=rev, nc=nc, hg=hg, fused=fuse is not None,
                          n_cast=len(casts)),
        out_shape=(jax.ShapeDtypeStruct((m, d), F32 if fuse is None else BF16),
                   jax.ShapeDtypeStruct(s0.shape, F32), *c_shapes),
        grid=(ng, nb), in_specs=specs + c_in, out_specs=(tok, st, *c_out),
        scratch_shapes=[pltpu.VMEM((hg, DN_DK, DN_DK), F32),
                        pltpu.VMEM((nc, hg, 2 * DN_CHUNK, DN_DK), BF16),
                        pltpu.VMEM((nc, hg, DN_CHUNK, DN_DK), F32),
                        pltpu.VMEM((nc, hg, DN_CHUNK, DN_CHUNK), BF16),
                        pltpu.VMEM((nc, hg, DN_CHUNK, DN_DK), BF16),
                        pltpu.VMEM((nc, hg, 1, DN_DK), F32)],
        compiler_params=_cparams('parallel', 'arbitrary'),
        name='dn_scan_bwd' if rev else 'dn_scan_fwd')(*args, *c_args)
    return res[0], res[1], list(res[2:])


def _hg_chunk(q_raw, f_raw, v, lb, st, masks, rev):
    _, incl, _ = masks
    c = q_raw[0].shape[0]
    heads = range(len(q_raw))
    mid = (c - 1 - c // 2) if rev else c // 2
    last = 0 if rev else c - 1
    qm, km, qs, kd, gl = [], [], [], [], []
    for h in heads:
        q = _silu(q_raw[h])
        f = lb[h] + (1.0 - lb[h]) * _sigmoid(f_raw[h])
        k = 1.0 - f
        b = _cumsum_rows(jnp.log(f), rev)
        m = b[mid:mid + 1, :]
        b_last = b[last:last + 1, :]
        qm.append(q * jnp.exp(b - m))
        km.append(k * jnp.exp(m - b))
        qs.append(q * jnp.exp(b))
        kd.append(k * jnp.exp(b_last - b))
        gl.append(jnp.exp(b_last))
    a_qk = [jnp.where(incl, _dot_nt(qm[h], km[h]), 0.0) for h in heads]
    inter = [_dot_nt(qs[h], st[h]) for h in heads]
    kv = [_dot_tn(v[h], kd[h]) for h in heads]
    intra = [_dot(a_qk[h], v[h]) for h in heads]
    o = [inter[h] + intra[h] for h in heads]
    st_new = [st[h] * gl[h] + kv[h] for h in heads]
    return o, st_new


def _hg_scan_body(q_ref, f_ref, v_ref, lb_ref, s0_ref, *rest, rev, nc, hg, fused, n_cast):
    fused_refs, cast_in, (o_ref, sf_ref), cast_out, (s_scr,) = _split_refs(
        rest, 3 if fused else 0, n_cast)
    if fused:
        of_ref, z_ref, gain_ref = fused_refs
    _run_casts(cast_in, cast_out)
    j = pl.program_id(1)

    @pl.when(j == 0)
    def _():
        s_scr[...] = s0_ref[...]

    c = HG_CHUNK
    masks = _tri_masks(c, rev)
    heads = range(hg)
    lanes = [slice(h * HG_DK, (h + 1) * HG_DK) for h in heads]

    def chunk(ci, carry):
        cc = (nc - 1 - ci) if rev else ci
        rows = pl.ds(pl.multiple_of(cc * c, c), c)
        o, s_new = _hg_chunk([q_ref[rows, sl] for sl in lanes], [f_ref[rows, sl] for sl in lanes],
                             [v_ref[rows, sl] for sl in lanes], [lb_ref[:, sl] for sl in lanes],
                             [s_scr[h] for h in heads], masks, rev)
        for h in heads:
            oh = o[h]
            if fused:
                oh = _gated_head_norm(oh + of_ref[rows, lanes[h]], z_ref[rows, lanes[h]],
                                      gain_ref[:, lanes[h]])
            o_ref[rows, lanes[h]] = oh.astype(o_ref.dtype)
            s_scr[h] = s_new[h]
        return carry

    lax.fori_loop(0, nc, chunk, 0)

    @pl.when(j == pl.num_programs(1) - 1)
    def _():
        sf_ref[...] = s_scr[...]


def hg_scan(p, lower, s0, direction, d, fuse=None, casts=(), hg=16, nc=8):
    m = p.shape[0]
    h = d // HG_DK
    hg = min(hg, h)
    ng = h // hg
    nc = min(nc, m // HG_CHUNK)
    bt = nc * HG_CHUNK
    nb = m // bt
    rev = direction == 1

    def tb(j):
        return (nb - 1 - j) if rev else j

    def part(pi):
        return pl.BlockSpec((bt, hg * HG_DK), lambda g, j: (tb(j), pi * ng + g))

    st = pl.BlockSpec((hg, HG_DK, HG_DK), lambda g, j: (g, 0, 0))
    row = pl.BlockSpec((1, hg * HG_DK), lambda g, j: (0, g))
    tok = pl.BlockSpec((bt, hg * HG_DK), lambda g, j: (tb(j), g))
    args = [p, p, p, lower.reshape(1, d), s0]
    specs = [part(0), part(1 + direction), part(3), row, st]
    if fuse is not None:
        o_other, gain = fuse
        args += [o_other, p, gain.reshape(1, d)]
        specs += [tok, part(4), row]
    c_args, c_in, c_shapes, c_out = _cast_jobs(casts, ng * nb, lambda g, j: g * nb + j)
    res = pl.pallas_call(
        functools.partial(_hg_scan_body, rev=rev, nc=nc, hg=hg, fused=fuse is not None,
                          n_cast=len(casts)),
        out_shape=(jax.ShapeDtypeStruct((m, d), F32 if fuse is None else BF16),
                   jax.ShapeDtypeStruct(s0.shape, F32), *c_shapes),
        grid=(ng, nb), in_specs=specs + c_in, out_specs=(tok, st, *c_out),
        scratch_shapes=[pltpu.VMEM((hg, HG_DK, HG_DK), F32)],
        compiler_params=_cparams('parallel', 'arbitrary'),
        name='hg_scan_bwd' if rev else 'hg_scan_fwd')(*args, *c_args)
    return res[0], res[1], list(res[2:])


def _shiftmix_body(x_ref, prev_ref, next_ref, mu_ref, *o_refs, width):
    i = pl.program_id(0)
    nb = pl.num_programs(0)
    bt, d = x_ref.shape
    dq = d // 4
    col = lax.broadcasted_iota(jnp.int32, (bt, dq), 0) % width
    for qi in range(4):
        sl = slice(qi * dq, (qi + 1) * dq)
        x = x_ref[:, sl]
        if qi == 0:
            sh = jnp.where(col == 0, 0.0, pltpu.roll(x, 1, 0))
        elif qi == 1:
            sh = jnp.where(col == width - 1, 0.0, pltpu.roll(x, bt - 1, 0))
        elif qi == 2:
            edge = jnp.where(i > 0, prev_ref[:, sl], 0.0)
            sh = edge if bt == width else jnp.concatenate([edge, x[:bt - width]], axis=0)
        else:
            edge = jnp.where(i < nb - 1, next_ref[:, sl], 0.0)
            sh = edge if bt == width else jnp.concatenate([x[width:], edge], axis=0)
        xx = sh - x
        for n, o_ref in enumerate(o_refs):
            o_ref[:, sl] = (x + xx * mu_ref[n:n + 1, sl]).astype(o_ref.dtype)


def shift_mix(h, mu, width, bt=128):
    m, d = h.shape
    bt = max(min(bt, m), width)
    rw = bt // width
    lastw = m // width - 1
    blk = pl.BlockSpec((bt, d), lambda i: (i, 0))
    out = jax.ShapeDtypeStruct((m, d), BF16)
    return pl.pallas_call(
        functools.partial(_shiftmix_body, width=width),
        out_shape=(out,) * 6, grid=(m // bt,),
        in_specs=[blk,
                  pl.BlockSpec((width, d), lambda i: (jnp.maximum(i * rw - 1, 0), 0)),
                  pl.BlockSpec((width, d), lambda i: (jnp.minimum((i + 1) * rw, lastw), 0)),
                  pl.BlockSpec((6, d), lambda i: (0, 0))],
        out_specs=(blk,) * 6, compiler_params=_cparams('arbitrary'), name='shift_mix')(h, h, h, mu)


def _halves(x, lo_mask):
    return jnp.concatenate([jnp.where(lo_mask, x, 0.0), jnp.where(lo_mask, 0.0, x)], axis=0)


def _head_sum(x, lo_mask):
    s_lo = jnp.sum(jnp.where(lo_mask, x, 0.0), axis=-1, keepdims=True)
    s_hi = jnp.sum(jnp.where(lo_mask, 0.0, x), axis=-1, keepdims=True)
    return jnp.where(lo_mask, s_lo, s_hi)


def _rw_intra(r, k, v, a_sig, logw, k_k, k_a, masks, tri, rev):
    lo, strict2, incl_cat = masks
    c = r[0].shape[0]
    groups = range(len(r))
    mid = (c - 1 - c // 2) if rev else c // 2
    last = 0 if rev else c - 1
    am2, bm2, km2, v2, rm, ae2, rg, bd2, kd2, gl = ([] for _ in range(10))
    for g in groups:
        kq = k[g] * k_k[g]
        kk = kq * lax.rsqrt(_head_sum(kq * kq, lo) + EPS)
        b = kk * a_sig[g]
        kd = k[g] * (1.0 + (a_sig[g] - 1.0) * k_a[g])
        cs = _cumsum_rows(logw[g], rev)
        ce = cs - logw[g]
        m = cs[mid:mid + 1, :]
        c_last = cs[last:last + 1, :]
        e_mc = jnp.exp(m - cs)
        dec = jnp.exp(c_last - cs)
        am2.append(_halves(-kk * jnp.exp(ce - m), lo))
        bm2.append(_halves(b * e_mc, lo))
        km2.append(_halves(kd * e_mc, lo))
        v2.append(_halves(v[g], lo))
        rm.append(r[g] * jnp.exp(cs - m))
        ae2.append(_halves(-kk * jnp.exp(ce), lo))
        rg.append(r[g] * jnp.exp(cs))
        bd2.append(_halves(b * dec, lo))
        kd2.append(_halves(kd * dec, lo))
        gl.append(jnp.exp(c_last))
    pairs = [_dot_nt(jnp.concatenate([am2[g], rm[g]], axis=0),
                     jnp.concatenate([bm2[g], km2[g]], axis=0)) for g in groups]
    n = [jnp.where(strict2, pairs[g][:2 * c, :2 * c], 0.0) for g in groups]
    a_ak = [jnp.where(strict2, pairs[g][:2 * c, 2 * c:], 0.0) for g in groups]
    a_rb = [jnp.where(incl_cat, pairs[g][2 * c:, :2 * c], 0.0) for g in groups]
    a_rk = [jnp.where(incl_cat, pairs[g][2 * c:, 2 * c:], 0.0) for g in groups]
    akv = [_dot(a_ak[g], v2[g]) for g in groups]
    ov = [_dot(a_rk[g], v2[g]) for g in groups]
    kv = [_dot_tn(v2[g], kd2[g]) for g in groups]
    sol = _unit_tri_solve_many(n, [jnp.concatenate([ae2[g], akv[g]], axis=1) for g in groups],
                               tri)
    wr = [jnp.concatenate([sol[g][:, :LANES], rg[g]], axis=0) for g in groups]
    uv2 = [sol[g][:, LANES:] for g in groups]
    return wr, uv2, a_rb, ov, bd2, kv, gl


def _rw_output(o, r, k, v, a_sum, gate, ln_w, ln_b, k_a, r_k, lo):
    inv_n = 1.0 / RW_HEAD
    oc = o - _head_sum(o, lo) * inv_n
    y = oc * lax.rsqrt(_head_sum(oc * oc, lo) * inv_n + RW_LN_EPS) * ln_w + ln_b
    k_sum = k * (2.0 + (a_sum - 2.0) * k_a)
    bonus = _head_sum(r * k_sum * r_k, lo) * v
    return (y + bonus) * gate


def _rw_masks(c, rev):
    lo = lax.broadcasted_iota(jnp.int32, (1, LANES), 1) < RW_HEAD
    ii = lax.broadcasted_iota(jnp.int32, (2 * c, 2 * c), 0)
    jj = lax.broadcasted_iota(jnp.int32, (2 * c, 2 * c), 1)
    same = (ii // c) == (jj // c)
    il, jl = ii % c, jj % c
    strict2 = same & ((il < jl) if rev else (il > jl))
    ic = lax.broadcasted_iota(jnp.int32, (c, 2 * c), 0)
    jc = lax.broadcasted_iota(jnp.int32, (c, 2 * c), 1) % c
    incl_cat = (ic <= jc) if rev else (ic >= jc)
    return lo, strict2, incl_cat


def _rw_scan_body(r_ref, k_ref, v_ref, a_ref, lw_ref, kk_ref, ka_ref, s0_ref, *rest,
                  rev, nc, hg, fused, n_cast):
    fused_refs, cast_in, (o_ref, sf_ref), cast_out, scratch = _split_refs(
        rest, 6 if fused else 0, n_cast)
    if fused:
        of_ref, ao_ref, gate_ref, lnw_ref, lnb_ref, rk_ref = fused_refs
    s_scr, wr_scr, uv_scr, arb_scr, ov_scr, bd_scr, kv_scr, gl_scr = scratch
    _run_casts(cast_in, cast_out)
    j = pl.program_id(1)

    @pl.when(j == 0)
    def _():
        s_scr[...] = s0_ref[...]

    c = RW_CHUNK
    masks = _rw_masks(c, rev)
    tri = _tri_levels(2 * c, c)
    groups = range(hg)
    lanes = [slice(g * LANES, (g + 1) * LANES) for g in groups]

    def intra(ci, carry):
        rows = pl.ds(pl.multiple_of(ci * c, c), c)
        wr, uv2, a_rb, ov, bd2, kv, gl = _rw_intra(
            [r_ref[rows, sl] for sl in lanes], [k_ref[rows, sl] for sl in lanes],
            [v_ref[rows, sl] for sl in lanes], [a_ref[rows, sl] for sl in lanes],
            [lw_ref[rows, sl] for sl in lanes], [kk_ref[:, sl] for sl in lanes],
            [ka_ref[:, sl] for sl in lanes], masks, tri, rev)
        for g in groups:
            wr_scr[ci, g] = wr[g].astype(BF16)
            uv_scr[ci, g] = uv2[g]
            arb_scr[ci, g] = a_rb[g].astype(BF16)
            ov_scr[ci, g] = ov[g]
            bd_scr[ci, g] = bd2[g].astype(BF16)
            kv_scr[ci, g] = kv[g]
            gl_scr[ci, g] = gl[g]
        return carry

    lax.fori_loop(0, nc, intra, 0)

    def inter(ci, carry):
        cc = (nc - 1 - ci) if rev else ci
        rows = pl.ds(pl.multiple_of(cc * c, c), c)
        st = [s_scr[g] for g in groups]
        ws = [_dot_nt(wr_scr[cc, g], st[g]) for g in groups]
        u2 = [ws[g][:2 * c] + uv_scr[cc, g] for g in groups]
        au = [_dot(arb_scr[cc, g], u2[g]) for g in groups]
        ub = [_dot_tn(u2[g], bd_scr[cc, g]) for g in groups]
        for g in groups:
            o = ws[g][2 * c:] + au[g] + ov_scr[cc, g]
            if fused:
                sl = lanes[g]
                o = _rw_output(o + of_ref[rows, sl], r_ref[rows, sl], k_ref[rows, sl],
                               v_ref[rows, sl], a_ref[rows, sl] + ao_ref[rows, sl],
                               gate_ref[rows, sl], lnw_ref[:, sl], lnb_ref[:, sl], ka_ref[:, sl],
                               rk_ref[:, sl], masks[0])
            o_ref[rows, lanes[g]] = o.astype(o_ref.dtype)
            s_scr[g] = st[g] * gl_scr[cc, g] + kv_scr[cc, g] + ub[g]
        return carry

    lax.fori_loop(0, nc, inter, 0)

    @pl.when(j == pl.num_programs(1) - 1)
    def _():
        sf_ref[...] = s_scr[...]


def rw_scan(r, k, v, a_sig, logw, k_k, k_a, s0, direction, fuse=None, casts=(), hg=16, nc=2):
    m, d = r.shape
    npair = d // LANES
    hg = min(hg, npair)
    ng = npair // hg
    nc = min(nc, m // RW_CHUNK)
    bt = nc * RW_CHUNK
    nb = m // bt
    rev = direction == 1

    def tb(j):
        return (nb - 1 - j) if rev else j

    tok = pl.BlockSpec((bt, hg * LANES), lambda g, j: (tb(j), g))
    row = pl.BlockSpec((1, hg * LANES), lambda g, j: (0, g))
    st = pl.BlockSpec((hg, LANES, LANES), lambda g, j: (g, 0, 0))
    args = [r, k, v, a_sig, logw, k_k.reshape(1, d), k_a.reshape(1, d), s0]
    specs = [tok, tok, tok, tok, tok, row, row, st]
    if fuse is not None:
        args += list(fuse[:3]) + [t.reshape(1, d) for t in fuse[3:]]
        specs += [tok, tok, tok, row, row, row]
    c_args, c_in, c_shapes, c_out = _cast_jobs(casts, ng * nb, lambda g, j: g * nb + j)
    res = pl.pallas_call(
        functools.partial(_rw_scan_body, rev=rev, nc=nc, hg=hg, fused=fuse is not None,
                          n_cast=len(casts)),
        out_shape=(jax.ShapeDtypeStruct((m, d), F32 if fuse is None else BF16),
                   jax.ShapeDtypeStruct(s0.shape, F32), *c_shapes),
        grid=(ng, nb), in_specs=specs + c_in, out_specs=(tok, st, *c_out),
        scratch_shapes=[pltpu.VMEM((hg, LANES, LANES), F32),
                        pltpu.VMEM((nc, hg, 3 * RW_CHUNK, LANES), BF16),
                        pltpu.VMEM((nc, hg, 2 * RW_CHUNK, LANES), F32),
                        pltpu.VMEM((nc, hg, RW_CHUNK, 2 * RW_CHUNK), BF16),
                        pltpu.VMEM((nc, hg, RW_CHUNK, LANES), F32),
                        pltpu.VMEM((nc, hg, 2 * RW_CHUNK, LANES), BF16),
                        pltpu.VMEM((nc, hg, LANES, LANES), F32),
                        pltpu.VMEM((nc, hg, 1, LANES), F32)],
        compiler_params=_cparams('parallel', 'arbitrary'),
        name='rw_scan_bwd' if rev else 'rw_scan_fwd')(*args, *c_args)
    return res[0], res[1], list(res[2:])


def _convglu_body(u_ref, prev_ref, next_ref, v_ref, cw_ref, cb_ref, o_ref, *, width):
    i = pl.program_id(0)
    nb = pl.num_programs(0)
    x = u_ref[...].astype(F32)
    bt = x.shape[0]
    up = jnp.where(i > 0, prev_ref[...].astype(F32), 0.0)
    dn = jnp.where(i < nb - 1, next_ref[...].astype(F32), 0.0)
    if bt > width:
        up = jnp.concatenate([up, x[:bt - width]], axis=0)
        dn = jnp.concatenate([x[width:], dn], axis=0)
    col = lax.broadcasted_iota(jnp.int32, x.shape, 0) % width
    rows = (up, x, dn)

    def column(kw):
        return sum(cw_ref[3 * r + kw:3 * r + kw + 1, :] * rows[r] for r in range(3))

    left = jnp.where(col == 0, 0.0, pltpu.roll(column(0), 1, 0))
    right = jnp.where(col == width - 1, 0.0, pltpu.roll(column(2), bt - 1, 0))
    h = column(1) + left + right + cb_ref[...]
    c = math.sqrt(2.0 / math.pi)
    t = jnp.tanh(h * (2.0 * c + (8.0 * 0.044715 * c) * (h * h)))
    o_ref[...] = ((h + h * t) * v_ref[...].astype(F32)).astype(o_ref.dtype)


def conv_glu_gate(uv, conv_w, conv_b, width, f, bt=512, bc=512):
    m = uv.shape[0]
    bt = max(min(bt, m), width)
    bc = _fit(f, bc)
    rw = bt // width
    lastw = m // width - 1
    voff = f // bc
    return pl.pallas_call(
        functools.partial(_convglu_body, width=width),
        out_shape=jax.ShapeDtypeStruct((m, f), BF16), grid=(m // bt, f // bc),
        in_specs=[pl.BlockSpec((bt, bc), lambda i, j: (i, j)),
                  pl.BlockSpec((width, bc), lambda i, j: (jnp.maximum(i * rw - 1, 0), j)),
                  pl.BlockSpec((width, bc), lambda i, j: (jnp.minimum((i + 1) * rw, lastw), j)),
                  pl.BlockSpec((bt, bc), lambda i, j: (i, voff + j)),
                  pl.BlockSpec((9, bc), lambda i, j: (0, j)),
                  pl.BlockSpec((1, bc), lambda i, j: (0, j))],
        out_specs=pl.BlockSpec((bt, bc), lambda i, j: (i, j)),
        compiler_params=_cparams('parallel', 'parallel'), name='conv_glu')(
            uv, uv, uv, uv, 0.5 * conv_w.reshape(9, f), 0.5 * conv_b.reshape(1, f))


def _modulation(conds, down, up, bias):
    low = matmul(conds, down, precise=True, bm=8, bn=512, name='ada_down')
    return matmul(low, up, bias=bias, precise=True, bm=8, bn=2048, name='ada_up')


def _dn_mixer(streams, w, d, last, casts):
    heads = d // DN_DK
    w_in = w['w_in']
    if w_in.dtype == BF16:
        main = dict(w=w_in, w_cols=(0, 4 * d))
        ab = dict(w=w_in, w_cols=(4 * d, 4 * heads))
    else:
        main = dict(w=w_in[:, :4 * d].astype(BF16))
        ab = dict(w=w_in[:, 4 * d:].astype(BF16))
    gain = jnp.tile(w['norm_g'], heads)
    prepped = []
    for h in streams:
        p = matmul(h, name='dn_in', **main)
        pab = matmul(h, bn=4 * heads, name='dn_in_ab', **ab)
        q = dn_conv(p, w['conv_w'], 0, d, 'q')
        k = dn_conv(p, w['conv_w'], 1, d, 'k')
        v = dn_conv(p, w['conv_w'], 2, d, 'v')
        gb = dn_gates(pab, w['a_log'], w['dt_bias'])
        prepped.append((p, q, k, v, gb.T))
    states = [jnp.zeros((heads, DN_DK, DN_DK), F32)] * 2
    outs, done = [], [[], []]
    for si, (p, q, k, v, gbt) in enumerate(prepped):
        jobs = casts if si == 1 else ((), ())
        o_f, states[0], done[0] = dn_scan(q, k, v, gbt, states[0], 0, casts=jobs[0])
        y, states[1], done[1] = dn_scan(q, k, v, gbt, states[1], 1, fuse=(o_f, p, 3, gain),
                                        casts=jobs[1])
        outs.append(None if si == 0 and last else y)
    return outs, done[0], done[1]


def _hg_mixer(streams, w, d, last, casts):
    heads = d // HG_DK
    states = [jnp.zeros((heads, HG_DK, HG_DK), F32)] * 2
    outs, done = [], [[], []]
    for si, h in enumerate(streams):
        jobs = casts if si == 1 else ((), ())
        p = matmul(h, w['w_in'], name='hg_in')
        o_f, states[0], done[0] = hg_scan(p, w['lower'], states[0], 0, d, casts=jobs[0])
        y, states[1], done[1] = hg_scan(p, w['lower'], states[1], 1, d,
                                        fuse=(o_f, w['norm_g']), casts=jobs[1])
        outs.append(None if si == 0 and last else y)
    return outs, done[0], done[1]


def _rw_mixer(streams, widths, w, d, last, casts):
    w_rkv = w['w_rkv']
    w1 = jnp.concatenate([w['w1'][0], w['w1'][1]], axis=1).astype(BF16)
    a1 = jnp.concatenate([w['a1'][0], w['a1'][1]], axis=1).astype(BF16)
    w2 = w['w2'].astype(BF16)
    a2 = w['a2'].astype(BF16)
    lora = w['w1'].shape[-1]
    gl = w['g1'].shape[-1]
    glp = -(-gl // LANES) * LANES
    g1 = jnp.pad(w['g1'], ((0, 0), (0, glp - gl))).astype(BF16)
    g2 = jnp.pad(w['g2'], ((0, glp - gl), (0, 0))).astype(BF16)
    states = [jnp.zeros((d // LANES, LANES, LANES), F32)] * 2
    outs, done = [], [[], []]
    for si, (h, width) in enumerate(zip(streams, widths)):
        jobs = casts if si == 1 else ((), ())
        xr, xw, xk, xv, xa, xg = shift_mix(h, w['mu'], width)
        r = matmul(xr, w_rkv[0], name='rw_r')
        k = matmul(xk, w_rkv[1], name='rw_k')
        v = matmul(xv, w_rkv[2], name='rw_v')
        tw = matmul(xw, w1, epi='tanh', out_dtype=BF16, bn=2 * lora, name='rw_w1')
        ta = matmul(xa, a1, out_dtype=BF16, bn=2 * lora, name='rw_a1')
        tg = matmul(xg, g1, epi='sigmoid', out_dtype=BF16, name='rw_g1')
        gate = matmul(tg, g2, name='rw_g2')
        logw, a_sig = [], []
        for direction in range(2):
            logw.append(matmul(tw, w2[direction], bias=w['w0'][direction], epi='logw',
                               a_koff=direction, name='rw_w2'))
            a_sig.append(matmul(ta, a2[direction], bias=w['a0'][direction], epi='sigmoid',
                                a_koff=direction, name='rw_a2'))
        o_f, states[0], done[0] = rw_scan(r, k, v, a_sig[0], logw[0], w['k_k'], w['k_a'],
                                          states[0], 0, casts=jobs[0])
        y, states[1], done[1] = rw_scan(r, k, v, a_sig[1], logw[1], w['k_k'], w['k_a'],
                                        states[1], 1, casts=jobs[1],
                                        fuse=(o_f, a_sig[0], gate, w['lnx_w'], w['lnx_b'],
                                              w['r_k'].reshape(-1)))
        outs.append(None if si == 0 and last else y)
    return outs, done[0], done[1]


def kernel(x, c, ctx, c_ctx, ada_down, ada_up, ada_b, norm1_g, norm2_g, ffn_w_up, ffn_conv_w, ffn_conv_b, ffn_w_down, dn_w_in, dn_conv_w, dn_a_log, dn_dt_bias, dn_norm_g, dn_w_out, hg_w_in, hg_lower, hg_norm_g, hg_w_out, rw_mu, rw_w_rkv, rw_w0, rw_w1, rw_w2, rw_a0, rw_a1, rw_a2, rw_g1, rw_g2, rw_k_k, rw_k_a, rw_r_k, rw_lnx_w, rw_lnx_b, rw_w_out, final_g):
    _, seq, d = x.shape
    depth = ada_down.shape[0]
    n_ctx = ctx.shape[1]
    f = ffn_w_down.shape[1]
    xl, xc = x[0], ctx[0]
    sm = jax.nn.softmax(hg_lower.astype(F32), axis=0)
    lower_bounds = jnp.cumsum(sm, axis=0) - sm[0]
    conds = jnp.zeros((8, d), F32).at[0].set(jax.nn.silu(c[0])).at[1].set(jax.nn.silu(c_ctx))
    widths = (n_ctx, GRID_W)
    w_outs = {0: dn_w_out, 1: hg_w_out, 2: rw_w_out}

    def in_proj(i):
        kind, j = i % 3, i // 3
        if kind == 0:
            return dn_w_in[j]
        if kind == 1:
            return hg_w_in[j]
        return rw_w_rkv[j].reshape(3 * d, d)

    w_in_bf = in_proj(0) if in_proj(0).shape[1] % LANES else in_proj(0).astype(BF16)
    for i in range(depth):
        kind, j = i % 3, i // 3
        last = i == depth - 1
        mod = _modulation(conds, ada_down[i], ada_up[i], ada_b[i])
        mods = [[mod[row, n * d:(n + 1) * d] for n in range(6)] for row in (1, 0)]
        xs = [xc, xl]
        h_dtype = F32 if kind == 2 else BF16
        hs = [normmod(t, norm1_g[i], mm[1], mm[0], h_dtype) for t, mm in zip(xs, mods)]
        nxt = in_proj(i + 1) if not last else None
        hide_next = nxt is not None and nxt.shape[1] % LANES == 0
        casts = ([w_outs[kind][j], ffn_w_up[i]], [ffn_w_down[i]] + ([nxt] if hide_next else []))
        if kind == 0:
            w = dict(w_in=w_in_bf, conv_w=dn_conv_w[j], a_log=dn_a_log[j],
                     dt_bias=dn_dt_bias[j], norm_g=dn_norm_g[j])
            ys, done_f, done_b = _dn_mixer(hs, w, d, last, casts)
        elif kind == 1:
            w = dict(w_in=w_in_bf, lower=lower_bounds[i], norm_g=hg_norm_g[j])
            ys, done_f, done_b = _hg_mixer(hs, w, d, last, casts)
        else:
            w = dict(mu=rw_mu[j], w_rkv=w_in_bf.reshape(3, d, d), w0=rw_w0[j], w1=rw_w1[j],
                     w2=rw_w2[j], a0=rw_a0[j], a1=rw_a1[j], a2=rw_a2[j], g1=rw_g1[j],
                     g2=rw_g2[j], k_k=rw_k_k[j], k_a=rw_k_a[j], r_k=rw_r_k[j],
                     lnx_w=rw_lnx_w[j], lnx_b=rw_lnx_b[j])
            ys, done_f, done_b = _rw_mixer(hs, widths, w, d, last, casts)
        w_out, w_up = done_f
        w_down = done_b[0]
        if hide_next:
            w_in_bf = done_b[1]
        elif nxt is not None:
            w_in_bf = nxt
        new = []
        for t, y, mm, width in zip(xs, ys, mods, widths):
            if y is None:
                new.append(t)
                continue
            t = matmul(y, w_out, gate=mm[2], resid=t, name='mix_out')
            h2 = normmod(t, norm2_g[i], mm[4], mm[3], BF16)
            uv = matmul(h2, w_up, out_dtype=BF16, name='ffn_up')
            gated = conv_glu_gate(uv, ffn_conv_w[i], ffn_conv_b[i], width, f)
            t = matmul(gated, w_down, gate=mm[5], resid=t, bm=512, bn=1024, w_single=True,
                       name='ffn_down')
            new.append(t)
        xc, xl = new
    zeros = jnp.zeros((d,), F32)
    return normmod(xl, final_g, zeros, zeros, F32)[None]
```

```python
import functools
import math

import jax
import jax.numpy as jnp
from jax import lax
from jax.experimental import pallas as pl
from jax.experimental.pallas import tpu as pltpu

F32 = jnp.float32
BF16 = jnp.bfloat16

EPS = 1e-6
RW_LN_EPS = 64e-5
GRID_W = 64
DN_DK = 128
DN_CHUNK = 128
HG_DK = 128
HG_CHUNK = 32
RW_HEAD = 64
RW_CHUNK = 64
LANES = 128
VMEM_LIMIT = 60 * 1024 * 1024


def _fit(n, b, unit=LANES):
    if n <= b:
        return n
    for cand in range(b - b % unit, 0, -unit):
        if n % cand == 0:
            return cand
    raise ValueError((n, b, unit))


def _cparams(*sem):
    return pltpu.CompilerParams(dimension_semantics=sem, vmem_limit_bytes=VMEM_LIMIT)


def _sigmoid(x):
    return 1.0 / (1.0 + jnp.exp(-x))


def _silu(x):
    return x * _sigmoid(x)


def _softplus(x):
    return jnp.maximum(x, 0.0) + jnp.log(1.0 + jnp.exp(-jnp.abs(x)))


def _gelu_tanh(x):
    c = math.sqrt(2.0 / math.pi)
    return 0.5 * x * (1.0 + jnp.tanh(c * (x + 0.044715 * (x * x * x))))


def _dot(a, b, dims=(((1,), (0,)), ((), ()))):
    return lax.dot_general(a.astype(BF16), b.astype(BF16), dims, preferred_element_type=F32)


def _dot_nt(a, b):
    return _dot(a, b, (((1,), (1,)), ((), ())))


def _dot_tn(a, b):
    return _dot(a, b, (((0,), (0,)), ((), ())))


def _split2(x):
    hi = x.astype(BF16)
    lo = (x - hi.astype(F32)).astype(BF16)
    return hi, lo


def _dot3(a, b, dims=(((1,), (0,)), ((), ()))):
    ah, al = _split2(a)
    bh, bl = _split2(b)
    d = functools.partial(lax.dot_general, dimension_numbers=dims, preferred_element_type=F32)
    return d(ah, bh) + (d(ah, bl) + d(al, bh))


def _tri_levels(c, size):
    ii = lax.broadcasted_iota(jnp.int32, (c, c), 0)
    jj = lax.broadcasted_iota(jnp.int32, (c, c), 1)
    levels = []
    s = 1
    while (1 << s) < size:
        lo = (ii >> s) != (jj >> s)
        hi = (ii >> (s + 1)) == (jj >> (s + 1))
        levels.append(lo & hi)
        s += 1
    return (ii >> 1) == (jj >> 1), ii == jj, levels


def _unit_tri_solve_many(ns, rhss, tri):
    pair, eye, levels = tri
    units = range(len(ns))
    ts = [jnp.where(pair, jnp.where(eye, 1.0, n), 0.0) for n in ns]
    for lmask in levels:
        ls = [jnp.where(lmask, n, 0.0) for n in ns]
        lt = [_dot(ls[i], ts[i]) for i in units]
        ts = [ts[i] + _dot(ts[i], lt[i]) for i in units]
    return [_dot(ts[i], rhss[i]) for i in units]


def _cumsum_rows(x, rev):
    c = x.shape[0]
    row = lax.broadcasted_iota(jnp.int32, x.shape, 0)
    s = 1
    while s < c:
        if rev:
            x = x + jnp.where(row < c - s, pltpu.roll(x, c - s, 0), 0.0)
        else:
            x = x + jnp.where(row >= s, pltpu.roll(x, s, 0), 0.0)
        s *= 2
    return x


def _epi_none(y):
    return y


def _epi_logw(y):
    return -jnp.exp(-_softplus(-y) - 0.5)


_EPILOGUES = {'none': _epi_none, 'tanh': jnp.tanh, 'sigmoid': _sigmoid, 'logw': _epi_logw}


def _mm_body(*refs, nk, epi, has_bias, has_gate, has_resid, precise):
    it = iter(refs)
    a_ref, w_ref = next(it), next(it)
    bias_ref = next(it) if has_bias else None
    gate_ref = next(it) if has_gate else None
    resid_ref = next(it) if has_resid else None
    o_ref = next(it)
    acc_ref = next(it) if nk > 1 else None

    def finish(y):
        if has_bias:
            y = y + bias_ref[...]
        y = _EPILOGUES[epi](y)
        if has_gate:
            y = y * gate_ref[...]
        if has_resid:
            y = resid_ref[...] + y
        o_ref[...] = y.astype(o_ref.dtype)

    if precise:
        part = _dot3(a_ref[...], w_ref[...])
    else:
        part = _dot(a_ref[...], w_ref[...])
    if nk == 1:
        finish(part)
    else:
        k = pl.program_id(2)

        @pl.when(k == 0)
        def _():
            acc_ref[...] = part

        @pl.when(k > 0)
        def _():
            acc_ref[...] += part

        @pl.when(k == nk - 1)
        def _():
            finish(acc_ref[...])


def matmul(a, w, *, bias=None, gate=None, resid=None, epi='none', out_dtype=F32,
           bm=1024, bn=1024, bk=None, a_koff=0, precise=False, w_single=False, w_cols=None,
           name='matmul'):
    m = a.shape[0]
    k, n = w.shape
    c0, n = (0, n) if w_cols is None else w_cols
    bm, bn = _fit(m, bm, 8), _fit(n, bn)
    assert c0 % bn == 0
    c0 //= bn
    bk = k if bk is None else _fit(k, bk)
    nk = k // bk
    if nk == 1:
        grid = (n // bn, m // bm)
        a_spec = pl.BlockSpec((bm, bk), lambda j, i: (i, a_koff))
        w_mode = dict(pipeline_mode=pl.Buffered(1)) if w_single else {}
        w_spec = pl.BlockSpec((bk, bn), lambda j, i: (0, j + c0), **w_mode)
        row_spec = pl.BlockSpec((1, bn), lambda j, i: (0, j))
        o_spec = pl.BlockSpec((bm, bn), lambda j, i: (i, j))
        sem = ('parallel', 'parallel')
        scratch = []
    else:
        grid = (n // bn, m // bm, nk)
        a_spec = pl.BlockSpec((bm, bk), lambda j, i, kk: (i, kk + a_koff * nk))
        w_spec = pl.BlockSpec((bk, bn), lambda j, i, kk: (kk, j + c0))
        row_spec = pl.BlockSpec((1, bn), lambda j, i, kk: (0, j))
        o_spec = pl.BlockSpec((bm, bn), lambda j, i, kk: (i, j))
        sem = ('parallel', 'parallel', 'arbitrary')
        scratch = [pltpu.VMEM((bm, bn), F32)]
    args, specs = [a, w], [a_spec, w_spec]
    for extra in (bias, gate):
        if extra is not None:
            args.append(extra.reshape(1, n).astype(F32))
            specs.append(row_spec)
    if resid is not None:
        args.append(resid)
        specs.append(o_spec)
    body = functools.partial(_mm_body, nk=nk, epi=epi, has_bias=bias is not None,
                             has_gate=gate is not None, has_resid=resid is not None,
                             precise=precise)
    return pl.pallas_call(
        body, out_shape=jax.ShapeDtypeStruct((m, n), out_dtype), grid=grid,
        in_specs=specs, out_specs=o_spec, scratch_shapes=scratch,
        compiler_params=_cparams(*sem), name=name)(*args)


def _normmod_body(x_ref, g_ref, sc_ref, sh_ref, o_ref):
    x = x_ref[...]
    y = x * lax.rsqrt(jnp.mean(x * x, axis=-1, keepdims=True) + EPS) * g_ref[...]
    o_ref[...] = (y * (1.0 + sc_ref[...]) + sh_ref[...]).astype(o_ref.dtype)


def normmod(x, g, sc, sh, out_dtype, bt=256):
    m, d = x.shape
    bt = min(bt, m)
    row = pl.BlockSpec((1, d), lambda i: (0, 0))
    blk = pl.BlockSpec((bt, d), lambda i: (i, 0))
    return pl.pallas_call(
        _normmod_body, out_shape=jax.ShapeDtypeStruct((m, d), out_dtype), grid=(m // bt,),
        in_specs=[blk, row, row, row], out_specs=blk,
        compiler_params=_cparams('parallel'), name='normmod')(
            x, g.reshape(1, d), sc.reshape(1, d), sh.reshape(1, d))


def _dnconv_body(x_ref, prev_ref, next_ref, w_ref, o_ref, *, mode):
    i = pl.program_id(0)
    nb = pl.num_programs(0)
    x = x_ref[...]
    bt = x.shape[0]
    row = lax.broadcasted_iota(jnp.int32, x.shape, 0)
    prev_row = jnp.where(i > 0, prev_ref[7:8, :], 0.0)
    next_row = jnp.where(i < nb - 1, next_ref[0:1, :], 0.0)
    xm = jnp.where(row == 0, prev_row, pltpu.roll(x, 1, 0))
    xp = jnp.where(row == bt - 1, next_row, pltpu.roll(x, bt - 1, 0))
    y = w_ref[0:1, :] * xm + w_ref[1:2, :] * x + w_ref[2:3, :] * xp
    y = _silu(y)
    if mode == 'v':
        o_ref[...] = y.astype(o_ref.dtype)
        return
    scale = DN_DK ** -0.5 if mode == 'q' else 1.0
    for h in range(x.shape[1] // DN_DK):
        sl = slice(h * DN_DK, (h + 1) * DN_DK)
        yh = y[:, sl]
        inv = lax.rsqrt(jnp.sum(yh * yh, axis=-1, keepdims=True) + EPS)
        o_ref[:, sl] = (yh * (inv * scale)).astype(o_ref.dtype)


def dn_conv(p, conv_w, part, d, mode, bt=512, bc=512):
    m = p.shape[0]
    bt, bc = _fit(m, bt, 8), _fit(d, bc)
    off = part * d // bc
    last8 = m // 8 - 1
    r8 = bt // 8
    return pl.pallas_call(
        functools.partial(_dnconv_body, mode=mode),
        out_shape=jax.ShapeDtypeStruct((m, d), F32), grid=(m // bt, d // bc),
        in_specs=[pl.BlockSpec((bt, bc), lambda i, j: (i, off + j)),
                  pl.BlockSpec((8, bc), lambda i, j: (jnp.maximum(i * r8 - 1, 0), off + j)),
                  pl.BlockSpec((8, bc), lambda i, j: (jnp.minimum((i + 1) * r8, last8), off + j)),
                  pl.BlockSpec((3, bc), lambda i, j: (0, off + j))],
        out_specs=pl.BlockSpec((bt, bc), lambda i, j: (i, j)),
        compiler_params=_cparams('parallel', 'parallel'), name='dn_conv_' + mode)(p, p, p, conv_w)


def _dnab_body(x_ref, alog_ref, dtb_ref, o_ref):
    x = x_ref[...]
    lane = lax.broadcasted_iota(jnp.int32, x.shape, 1)
    g = -jnp.exp(alog_ref[...]) * _softplus(x + dtb_ref[...])
    o_ref[...] = jnp.where(lane < x.shape[1] // 2, g, _sigmoid(x))


def dn_gates(pab, a_log, dt_bias, bt=1024):
    m, w = pab.shape
    bt = min(bt, m)
    zeros = jnp.zeros((w // 2,), F32)
    alog = jnp.concatenate([a_log.reshape(-1), zeros]).reshape(1, w)
    dtb = jnp.concatenate([dt_bias.reshape(-1), zeros]).reshape(1, w)
    row = pl.BlockSpec((1, w), lambda i: (0, 0))
    blk = pl.BlockSpec((bt, w), lambda i: (i, 0))
    return pl.pallas_call(
        _dnab_body, out_shape=jax.ShapeDtypeStruct((m, w), F32), grid=(m // bt,),
        in_specs=[blk, row, row], out_specs=blk,
        compiler_params=_cparams('parallel'), name='dn_gates')(pab, alog, dtb)


def _cast_jobs(ws, steps, step_index):
    args, in_specs, out_shapes, out_specs = [], [], [], []
    for stack, layer in ws:
        _, rows, cols = stack.shape
        ratio = next(r for r in (1, 2, 4, 8, 16) if steps % r == 0
                     and rows % (steps // r) == 0 and (rows // (steps // r)) % 16 == 0)
        blk = rows // (steps // ratio)
        args.append(stack)
        in_specs.append(pl.BlockSpec(
            (None, blk, cols),
            lambda g, j, ratio=ratio, layer=layer: (layer, step_index(g, j) // ratio, 0)))
        out_shapes.append(jax.ShapeDtypeStruct((rows, cols), BF16))
        out_specs.append(pl.BlockSpec(
            (blk, cols), lambda g, j, ratio=ratio: (step_index(g, j) // ratio, 0)))
    return args, in_specs, out_shapes, out_specs


def _split_refs(rest, n_fused, n_cast):
    a, b, c, d = n_fused, n_fused + n_cast, n_fused + n_cast + 2, n_fused + 2 * n_cast + 2
    return rest[:a], rest[a:b], rest[b:c], rest[c:d], rest[d:]


def _run_casts(cast_in, cast_out):
    for wi, wo in zip(cast_in, cast_out):
        wo[...] = wi[...].astype(BF16)


def _gated_head_norm(o, z, gain):
    y = o * lax.rsqrt(jnp.mean(o * o, axis=-1, keepdims=True) + EPS) * gain
    return y * _silu(z)


def _tri_masks(c, rev):
    ii = lax.broadcasted_iota(jnp.int32, (c, c), 0)
    jj = lax.broadcasted_iota(jnp.int32, (c, c), 1)
    if rev:
        return ii == jj, ii <= jj, ii < jj
    return ii == jj, ii >= jj, ii > jj


def _dn_intra(q, k, v, g_row, beta_row, masks, tri):
    eye, incl, strict = masks
    c = q[0].shape[0]
    heads = range(len(q))
    gc_col, beta_col, g_tot, decay = [], [], [], []
    for h in heads:
        g_bc = jnp.broadcast_to(g_row[h], (c, c))
        gcc = jnp.sum(jnp.where(incl, g_bc, 0.0), axis=1, keepdims=True)
        gcr = jnp.sum(jnp.where(eye, jnp.broadcast_to(gcc, (c, c)), 0.0), axis=0, keepdims=True)
        gc_col.append(gcc)
        beta_col.append(jnp.sum(jnp.where(eye, jnp.broadcast_to(beta_row[h], (c, c)), 0.0),
                                axis=1, keepdims=True))
        g_tot.append(jnp.sum(g_row[h], axis=1, keepdims=True))
        decay.append(jnp.where(incl, jnp.exp(jnp.where(incl, gcc - gcr, 0.0)), 0.0))
    kb = [k[h] * beta_col[h] for h in heads]
    kq = [_dot_nt(jnp.concatenate([kb[h], q[h]], axis=0), k[h]) for h in heads]
    kk = [kq[h][:c] for h in heads]
    qk = [kq[h][c:] for h in heads]
    n = [-jnp.where(strict, kk[h] * decay[h], 0.0) for h in heads]
    egc = [jnp.exp(gc_col[h]) for h in heads]
    rhs = [jnp.concatenate([kb[h] * egc[h], v[h] * beta_col[h]], axis=1) for h in heads]
    sol = _unit_tri_solve_many(n, rhs, tri)
    wq = [jnp.concatenate([sol[h][:, :DN_DK], q[h] * egc[h]], axis=0) for h in heads]
    u = [sol[h][:, DN_DK:] for h in heads]
    attn = [jnp.where(incl, qk[h] * decay[h], 0.0) for h in heads]
    kd = [k[h] * jnp.exp(g_tot[h] - gc_col[h]) for h in heads]
    gl = [jnp.exp(g_tot[h]) for h in heads]
    return wq, u, attn, kd, gl


def _dn_scan_body(q_ref, k_ref, v_ref, g_ref, b_ref, s0_ref, *rest, rev, nc, hg, fused, n_cast):
    fused_refs, cast_in, (o_ref, sf_ref), cast_out, scratch = _split_refs(
        rest, 3 if fused else 0, n_cast)
    if fused:
        of_ref, z_ref, gain_ref = fused_refs
    s_scr, wq_scr, u_scr, at_scr, kd_scr, gl_scr = scratch
    _run_casts(cast_in, cast_out)
    j = pl.program_id(1)

    @pl.when(j == 0)
    def _():
        s_scr[...] = s0_ref[...]

    c = DN_CHUNK
    masks = _tri_masks(c, rev)
    tri = _tri_levels(c, c)
    heads = range(hg)
    lanes = [slice(h * DN_DK, (h + 1) * DN_DK) for h in heads]

    def intra(ci, carry):
        rows = pl.ds(pl.multiple_of(ci * c, c), c)
        wq, u, attn, kd, gl = _dn_intra(
            [q_ref[rows, sl] for sl in lanes], [k_ref[rows, sl] for sl in lanes],
            [v_ref[rows, sl] for sl in lanes], [g_ref[ci, h:h + 1, :] for h in heads],
            [b_ref[ci, h:h + 1, :] for h in heads], masks, tri)
        for h in heads:
            wq_scr[ci, h] = wq[h].astype(BF16)
            u_scr[ci, h] = u[h]
            at_scr[ci, h] = attn[h].astype(BF16)
            kd_scr[ci, h] = kd[h].astype(BF16)
            gl_scr[ci, h] = jnp.broadcast_to(gl[h], (1, DN_DK))
        return carry

    lax.fori_loop(0, nc, intra, 0)

    def inter(ci, carry):
        cc = (nc - 1 - ci) if rev else ci
        rows = pl.ds(pl.multiple_of(cc * c, c), c)
        s = [s_scr[h] for h in heads]
        ws = [_dot(wq_scr[cc, h], s[h]) for h in heads]
        v_new = [u_scr[cc, h] - ws[h][:c] for h in heads]
        av = [_dot(at_scr[cc, h], v_new[h]) for h in heads]
        kv = [_dot_tn(kd_scr[cc, h], v_new[h]) for h in heads]
        for h in heads:
            o = ws[h][c:] + av[h]
            if fused:
                o = _gated_head_norm(o + of_ref[rows, lanes[h]], z_ref[rows, lanes[h]],
                                     gain_ref[:, lanes[h]])
            o_ref[rows, lanes[h]] = o.astype(o_ref.dtype)
            s_scr[h] = s[h] * gl_scr[cc, h] + kv[h]
        return carry

    lax.fori_loop(0, nc, inter, 0)

    @pl.when(j == pl.num_programs(1) - 1)
    def _():
        sf_ref[...] = s_scr[...]


def dn_scan(q, k, v, gbt, s0, direction, fuse=None, casts=(), hg=16, nc=2):
    m, d = q.shape
    h = d // DN_DK
    hg = min(hg, h)
    ng = h // hg
    nc = min(nc, m // DN_CHUNK)
    bt = nc * DN_CHUNK
    nb = m // bt
    rev = direction == 1
    gb4 = gbt.reshape(4 * ng, hg, m // DN_CHUNK, DN_CHUNK).transpose(0, 2, 1, 3)

    def tb(j):
        return (nb - 1 - j) if rev else j

    tok = pl.BlockSpec((bt, hg * DN_DK), lambda g, j: (tb(j), g))
    st = pl.BlockSpec((hg, DN_DK, DN_DK), lambda g, j: (g, 0, 0))
    g_spec = pl.BlockSpec((None, nc, hg, DN_CHUNK),
                          lambda g, j: (direction * ng + g, tb(j), 0, 0))
    b_spec = pl.BlockSpec((None, nc, hg, DN_CHUNK),
                          lambda g, j: ((2 + direction) * ng + g, tb(j), 0, 0))
    args, specs = [q, k, v, gb4, gb4, s0], [tok, tok, tok, g_spec, b_spec, st]
    if fuse is not None:
        o_other, p, z_part, gain = fuse
        args += [o_other, p, gain.reshape(1, d)]
        specs += [tok, pl.BlockSpec((bt, hg * DN_DK), lambda g, j: (tb(j), z_part * ng + g)),
                  pl.BlockSpec((1, hg * DN_DK), lambda g, j: (0, g))]
    c_args, c_in, c_shapes, c_out = _cast_jobs(casts, ng * nb, lambda g, j: g * nb + j)
    res = pl.pallas_call(
        functools.partial(_dn_scan_body, rev=rev, nc=nc, hg=hg, fused=fuse is not None,
                          n_cast=len(casts)),
        out_shape=(jax.ShapeDtypeStruct((m, d), F32 if fuse is None else BF16),
                   jax.ShapeDtypeStruct(s0.shape, F32), *c_shapes),
        grid=(ng, nb), in_specs=specs + c_in, out_specs=(tok, st, *c_out),
        scratch_shapes=[pltpu.VMEM((hg, DN_DK, DN_DK), F32),
                        pltpu.VMEM((nc, hg, 2 * DN_CHUNK, DN_DK), BF16),
                        pltpu.VMEM((nc, hg, DN_CHUNK, DN_DK), F32),
                        pltpu.VMEM((nc, hg, DN_CHUNK, DN_CHUNK), BF16),
                        pltpu.VMEM((nc, hg, DN_CHUNK, DN_DK), BF16),
                        pltpu.VMEM((nc, hg, 1, DN_DK), F32)],
        compiler_params=_cparams('parallel', 'arbitrary'),
        name='dn_scan_bwd' if rev else 'dn_scan_fwd')(*args, *c_args)
    return res[0], res[1], list(res[2:])


def _hg_chunk(q_raw, f_raw, v, lb, st, masks, rev):
    _, incl, _ = masks
    c = q_raw[0].shape[0]
    heads = range(len(q_raw))
    mid = (c - 1 - c // 2) if rev else c // 2
    last = 0 if rev else c - 1
    qm, km, qs, kd, gl = [], [], [], [], []
    for h in heads:
        q = _silu(q_raw[h])
        f = lb[h] + (1.0 - lb[h]) * _sigmoid(f_raw[h])
        k = 1.0 - f
        b = _cumsum_rows(jnp.log(f), rev)
        m = b[mid:mid + 1, :]
        b_last = b[last:last + 1, :]
        qm.append(q * jnp.exp(b - m))
        km.append(k * jnp.exp(m - b))
        qs.append(q * jnp.exp(b))
        kd.append(k * jnp.exp(b_last - b))
        gl.append(jnp.exp(b_last))
    a_qk = [jnp.where(incl, _dot_nt(qm[h], km[h]), 0.0) for h in heads]
    inter = [_dot_nt(qs[h], st[h]) for h in heads]
    kv = [_dot_tn(v[h], kd[h]) for h in heads]
    intra = [_dot(a_qk[h], v[h]) for h in heads]
    o = [inter[h] + intra[h] for h in heads]
    st_new = [st[h] * gl[h] + kv[h] for h in heads]
    return o, st_new


def _hg_scan_body(q_ref, f_ref, v_ref, lb_ref, s0_ref, *rest, rev, nc, hg, fused, n_cast):
    fused_refs, cast_in, (o_ref, sf_ref), cast_out, (s_scr,) = _split_refs(
        rest, 3 if fused else 0, n_cast)
    if fused:
        of_ref, z_ref, gain_ref = fused_refs
    _run_casts(cast_in, cast_out)
    j = pl.program_id(1)

    @pl.when(j == 0)
    def _():
        s_scr[...] = s0_ref[...]

    c = HG_CHUNK
    masks = _tri_masks(c, rev)
    heads = range(hg)
    lanes = [slice(h * HG_DK, (h + 1) * HG_DK) for h in heads]

    def chunk(ci, carry):
        cc = (nc - 1 - ci) if rev else ci
        rows = pl.ds(pl.multiple_of(cc * c, c), c)
        o, s_new = _hg_chunk([q_ref[rows, sl] for sl in lanes], [f_ref[rows, sl] for sl in lanes],
                             [v_ref[rows, sl] for sl in lanes], [lb_ref[:, sl] for sl in lanes],
                             [s_scr[h] for h in heads], masks, rev)
        for h in heads:
            oh = o[h]
            if fused:
                oh = _gated_head_norm(oh + of_ref[rows, lanes[h]], z_ref[rows, lanes[h]],
                                      gain_ref[:, lanes[h]])
            o_ref[rows, lanes[h]] = oh.astype(o_ref.dtype)
            s_scr[h] = s_new[h]
        return carry

    lax.fori_loop(0, nc, chunk, 0)

    @pl.when(j == pl.num_programs(1) - 1)
    def _():
        sf_ref[...] = s_scr[...]


def hg_scan(p, lower, s0, direction, d, fuse=None, casts=(), hg=16, nc=8):
    m = p.shape[0]
    h = d // HG_DK
    hg = min(hg, h)
    ng = h // hg
    nc = min(nc, m // HG_CHUNK)
    bt = nc * HG_CHUNK
    nb = m // bt
    rev = direction == 1

    def tb(j):
        return (nb - 1 - j) if rev else j

    def part(pi):
        return pl.BlockSpec((bt, hg * HG_DK), lambda g, j: (tb(j), pi * ng + g))

    st = pl.BlockSpec((hg, HG_DK, HG_DK), lambda g, j: (g, 0, 0))
    row = pl.BlockSpec((1, hg * HG_DK), lambda g, j: (0, g))
    tok = pl.BlockSpec((bt, hg * HG_DK), lambda g, j: (tb(j), g))
    args = [p, p, p, lower.reshape(1, d), s0]
    specs = [part(0), part(1 + direction), part(3), row, st]
    if fuse is not None:
        o_other, gain = fuse
        args += [o_other, p, gain.reshape(1, d)]
        specs += [tok, part(4), row]
    c_args, c_in, c_shapes, c_out = _cast_jobs(casts, ng * nb, lambda g, j: g * nb + j)
    res = pl.pallas_call(
        functools.partial(_hg_scan_body, rev=rev, nc=nc, hg=hg, fused=fuse is not None,
                          n_cast=len(casts)),
        out_shape=(jax.ShapeDtypeStruct((m, d), F32 if fuse is None else BF16),
                   jax.ShapeDtypeStruct(s0.shape, F32), *c_shapes),
        grid=(ng, nb), in_specs=specs + c_in, out_specs=(tok, st, *c_out),
        scratch_shapes=[pltpu.VMEM((hg, HG_DK, HG_DK), F32)],
        compiler_params=_cparams('parallel', 'arbitrary'),
        name='hg_scan_bwd' if rev else 'hg_scan_fwd')(*args, *c_args)
    return res[0], res[1], list(res[2:])


def _shiftmix_body(x_ref, prev_ref, next_ref, mu_ref, *o_refs, width):
    i = pl.program_id(0)
    nb = pl.num_programs(0)
    bt, d = x_ref.shape
    dq = d // 4
    col = lax.broadcasted_iota(jnp.int32, (bt, dq), 0) % width
    for qi in range(4):
        sl = slice(qi * dq, (qi + 1) * dq)
        x = x_ref[:, sl]
        if qi == 0:
            sh = jnp.where(col == 0, 0.0, pltpu.roll(x, 1, 0))
        elif qi == 1:
            sh = jnp.where(col == width - 1, 0.0, pltpu.roll(x, bt - 1, 0))
        elif qi == 2:
            edge = jnp.where(i > 0, prev_ref[:, sl], 0.0)
            sh = edge if bt == width else jnp.concatenate([edge, x[:bt - width]], axis=0)
        else:
            edge = jnp.where(i < nb - 1, next_ref[:, sl], 0.0)
            sh = edge if bt == width else jnp.concatenate([x[width:], edge], axis=0)
        xx = sh - x
        for n, o_ref in enumerate(o_refs):
            o_ref[:, sl] = (x + xx * mu_ref[n:n + 1, sl]).astype(o_ref.dtype)


def shift_mix(h, mu, width, bt=128):
    m, d = h.shape
    bt = max(min(bt, m), width)
    rw = bt // width
    lastw = m // width - 1
    blk = pl.BlockSpec((bt, d), lambda i: (i, 0))
    out = jax.ShapeDtypeStruct((m, d), BF16)
    return pl.pallas_call(
        functools.partial(_shiftmix_body, width=width),
        out_shape=(out,) * 6, grid=(m // bt,),
        in_specs=[blk,
                  pl.BlockSpec((width, d), lambda i: (jnp.maximum(i * rw - 1, 0), 0)),
                  pl.BlockSpec((width, d), lambda i: (jnp.minimum((i + 1) * rw, lastw), 0)),
                  pl.BlockSpec((6, d), lambda i: (0, 0))],
        out_specs=(blk,) * 6, compiler_params=_cparams('arbitrary'), name='shift_mix')(h, h, h, mu)


def _halves(x, lo_mask):
    return jnp.concatenate([jnp.where(lo_mask, x, 0.0), jnp.where(lo_mask, 0.0, x)], axis=0)


def _head_sum(x, lo_mask):
    s_lo = jnp.sum(jnp.where(lo_mask, x, 0.0), axis=-1, keepdims=True)
    s_hi = jnp.sum(jnp.where(lo_mask, 0.0, x), axis=-1, keepdims=True)
    return jnp.where(lo_mask, s_lo, s_hi)


def _rw_intra(r, k, v, a_sig, logw, k_k, k_a, masks, tri, rev):
    lo, strict2, incl_cat = masks
    c = r[0].shape[0]
    groups = range(len(r))
    mid = (c - 1 - c // 2) if rev else c // 2
    last = 0 if rev else c - 1
    am2, bm2, km2, v2, rm, ae2, rg, bd2, kd2, gl = ([] for _ in range(10))
    for g in groups:
        kq = k[g] * k_k[g]
        kk = kq * lax.rsqrt(_head_sum(kq * kq, lo) + EPS)
        b = kk * a_sig[g]
        kd = k[g] * (1.0 + (a_sig[g] - 1.0) * k_a[g])
        cs = _cumsum_rows(logw[g], rev)
        ce = cs - logw[g]
        m = cs[mid:mid + 1, :]
        c_last = cs[last:last + 1, :]
        e_mc = jnp.exp(m - cs)
        dec = jnp.exp(c_last - cs)
        am2.append(_halves(-kk * jnp.exp(ce - m), lo))
        bm2.append(_halves(b * e_mc, lo))
        km2.append(_halves(kd * e_mc, lo))
        v2.append(_halves(v[g], lo))
        rm.append(r[g] * jnp.exp(cs - m))
        ae2.append(_halves(-kk * jnp.exp(ce), lo))
        rg.append(r[g] * jnp.exp(cs))
        bd2.append(_halves(b * dec, lo))
        kd2.append(_halves(kd * dec, lo))
        gl.append(jnp.exp(c_last))
    pairs = [_dot_nt(jnp.concatenate([am2[g], rm[g]], axis=0),
                     jnp.concatenate([bm2[g], km2[g]], axis=0)) for g in groups]
    n = [jnp.where(strict2, pairs[g][:2 * c, :2 * c], 0.0) for g in groups]
    a_ak = [jnp.where(strict2, pairs[g][:2 * c, 2 * c:], 0.0) for g in groups]
    a_rb = [jnp.where(incl_cat, pairs[g][2 * c:, :2 * c], 0.0) for g in groups]
    a_rk = [jnp.where(incl_cat, pairs[g][2 * c:, 2 * c:], 0.0) for g in groups]
    akv = [_dot(a_ak[g], v2[g]) for g in groups]
    ov = [_dot(a_rk[g], v2[g]) for g in groups]
    kv = [_dot_tn(v2[g], kd2[g]) for g in groups]
    sol = _unit_tri_solve_many(n, [jnp.concatenate([ae2[g], akv[g]], axis=1) for g in groups],
                               tri)
    wr = [jnp.concatenate([sol[g][:, :LANES], rg[g]], axis=0) for g in groups]
    uv2 = [sol[g][:, LANES:] for g in groups]
    return wr, uv2, a_rb, ov, bd2, kv, gl


def _rw_output(o, r, k, v, a_sum, gate, ln_w, ln_b, k_a, r_k, lo):
    inv_n = 1.0 / RW_HEAD
    oc = o - _head_sum(o, lo) * inv_n
    y = oc * lax.rsqrt(_head_sum(oc * oc, lo) * inv_n + RW_LN_EPS) * ln_w + ln_b
    k_sum = k * (2.0 + (a_sum - 2.0) * k_a)
    bonus = _head_sum(r * k_sum * r_k, lo) * v
    return (y + bonus) * gate


def _rw_masks(c, rev):
    lo = lax.broadcasted_iota(jnp.int32, (1, LANES), 1) < RW_HEAD
    ii = lax.broadcasted_iota(jnp.int32, (2 * c, 2 * c), 0)
    jj = lax.broadcasted_iota(jnp.int32, (2 * c, 2 * c), 1)
    same = (ii // c) == (jj // c)
    il, jl = ii % c, jj % c
    strict2 = same & ((il < jl) if rev else (il > jl))
    ic = lax.broadcasted_iota(jnp.int32, (c, 2 * c), 0)
    jc = lax.broadcasted_iota(jnp.int32, (c, 2 * c), 1) % c
    incl_cat = (ic <= jc) if rev else (ic >= jc)
    return lo, strict2, incl_cat


def _rw_scan_body(r_ref, k_ref, v_ref, a_ref, lw_ref, kk_ref, ka_ref, s0_ref, *rest,
                  rev, nc, hg, fused, n_cast):
    fused_refs, cast_in, (o_ref, sf_ref), cast_out, scratch = _split_refs(
        rest, 6 if fused else 0, n_cast)
    if fused:
        of_ref, ao_ref, gate_ref, lnw_ref, lnb_ref, rk_ref = fused_refs
    s_scr, wr_scr, uv_scr, arb_scr, ov_scr, bd_scr, kv_scr, gl_scr = scratch
    _run_casts(cast_in, cast_out)
    j = pl.program_id(1)

    @pl.when(j == 0)
    def _():
        s_scr[...] = s0_ref[...]

    c = RW_CHUNK
    masks = _rw_masks(c, rev)
    tri = _tri_levels(2 * c, c)
    groups = range(hg)
    lanes = [slice(g * LANES, (g + 1) * LANES) for g in groups]

    def intra(ci, carry):
        rows = pl.ds(pl.multiple_of(ci * c, c), c)
        wr, uv2, a_rb, ov, bd2, kv, gl = _rw_intra(
            [r_ref[rows, sl] for sl in lanes], [k_ref[rows, sl] for sl in lanes],
            [v_ref[rows, sl] for sl in lanes], [a_ref[rows, sl] for sl in lanes],
            [lw_ref[rows, sl] for sl in lanes], [kk_ref[:, sl] for sl in lanes],
            [ka_ref[:, sl] for sl in lanes], masks, tri, rev)
        for g in groups:
            wr_scr[ci, g] = wr[g].astype(BF16)
            uv_scr[ci, g] = uv2[g]
            arb_scr[ci, g] = a_rb[g].astype(BF16)
            ov_scr[ci, g] = ov[g]
            bd_scr[ci, g] = bd2[g].astype(BF16)
            kv_scr[ci, g] = kv[g]
            gl_scr[ci, g] = gl[g]
        return carry

    lax.fori_loop(0, nc, intra, 0)

    def inter(ci, carry):
        cc = (nc - 1 - ci) if rev else ci
        rows = pl.ds(pl.multiple_of(cc * c, c), c)
        st = [s_scr[g] for g in groups]
        ws = [_dot_nt(wr_scr[cc, g], st[g]) for g in groups]
        u2 = [ws[g][:2 * c] + uv_scr[cc, g] for g in groups]
        au = [_dot(arb_scr[cc, g], u2[g]) for g in groups]
        ub = [_dot_tn(u2[g], bd_scr[cc, g]) for g in groups]
        for g in groups:
            o = ws[g][2 * c:] + au[g] + ov_scr[cc, g]
            if fused:
                sl = lanes[g]
                o = _rw_output(o + of_ref[rows, sl], r_ref[rows, sl], k_ref[rows, sl],
                               v_ref[rows, sl], a_ref[rows, sl] + ao_ref[rows, sl],
                               gate_ref[rows, sl], lnw_ref[:, sl], lnb_ref[:, sl], ka_ref[:, sl],
                               rk_ref[:, sl], masks[0])
            o_ref[rows, lanes[g]] = o.astype(o_ref.dtype)
            s_scr[g] = st[g] * gl_scr[cc, g] + kv_scr[cc, g] + ub[g]
        return carry

    lax.fori_loop(0, nc, inter, 0)

    @pl.when(j == pl.num_programs(1) - 1)
    def _():
        sf_ref[...] = s_scr[...]


def rw_scan(r, k, v, a_sig, logw, k_k, k_a, s0, direction, fuse=None, casts=(), hg=16, nc=2):
    m, d = r.shape
    npair = d // LANES
    hg = min(hg, npair)
    ng = npair // hg
    nc = min(nc, m // RW_CHUNK)
    bt = nc * RW_CHUNK
    nb = m // bt
    rev = direction == 1

    def tb(j):
        return (nb - 1 - j) if rev else j

    tok = pl.BlockSpec((bt, hg * LANES), lambda g, j: (tb(j), g))
    row = pl.BlockSpec((1, hg * LANES), lambda g, j: (0, g))
    st = pl.BlockSpec((hg, LANES, LANES), lambda g, j: (g, 0, 0))
    args = [r, k, v, a_sig, logw, k_k.reshape(1, d), k_a.reshape(1, d), s0]
    specs = [tok, tok, tok, tok, tok, row, row, st]
    if fuse is not None:
        args += list(fuse[:3]) + [t.reshape(1, d) for t in fuse[3:]]
        specs += [tok, tok, tok, row, row, row]
    c_args, c_in, c_shapes, c_out = _cast_jobs(casts, ng * nb, lambda g, j: g * nb + j)
    res = pl.pallas_call(
        functools.partial(_rw_scan_body, rev=rev, nc=nc, hg=hg, fused=fuse is not None,
                          n_cast=len(casts)),
        out_shape=(jax.ShapeDtypeStruct((m, d), F32 if fuse is None else BF16),
                   jax.ShapeDtypeStruct(s0.shape, F32), *c_shapes),
        grid=(ng, nb), in_specs=specs + c_in, out_specs=(tok, st, *c_out),
        scratch_shapes=[pltpu.VMEM((hg, LANES, LANES), F32),
                        pltpu.VMEM((nc, hg, 3 * RW_CHUNK, LANES), BF16),
                        pltpu.VMEM((nc, hg, 2 * RW_CHUNK, LANES), F32),
                        pltpu.VMEM((nc, hg, RW_CHUNK, 2 * RW_CHUNK), BF16),
                        pltpu.VMEM((nc, hg, RW_CHUNK, LANES), F32),
                        pltpu.VMEM((nc, hg, 2 * RW_CHUNK, LANES), BF16),
                        pltpu.VMEM((nc, hg, LANES, LANES), F32),
                        pltpu.VMEM((nc, hg, 1, LANES), F32)],
        compiler_params=_cparams('parallel', 'arbitrary'),
        name='rw_scan_bwd' if rev else 'rw_scan_fwd')(*args, *c_args)
    return res[0], res[1], list(res[2:])


def _convglu_body(u_ref, prev_ref, next_ref, v_ref, cw_ref, cb_ref, o_ref, *, width):
    i = pl.program_id(0)
    nb = pl.num_programs(0)
    x = u_ref[...].astype(F32)
    bt = x.shape[0]
    up = jnp.where(i > 0, prev_ref[...].astype(F32), 0.0)
    dn = jnp.where(i < nb - 1, next_ref[...].astype(F32), 0.0)
    if bt > width:
        up = jnp.concatenate([up, x[:bt - width]], axis=0)
        dn = jnp.concatenate([x[width:], dn], axis=0)
    col = lax.broadcasted_iota(jnp.int32, x.shape, 0) % width
    rows = (up, x, dn)

    def column(kw):
        return sum(cw_ref[3 * r + kw:3 * r + kw + 1, :] * rows[r] for r in range(3))

    left = jnp.where(col == 0, 0.0, pltpu.roll(column(0), 1, 0))
    right = jnp.where(col == width - 1, 0.0, pltpu.roll(column(2), bt - 1, 0))
    h = column(1) + left + right + cb_ref[...]
    c = math.sqrt(2.0 / math.pi)
    t = jnp.tanh(h * (2.0 * c + (8.0 * 0.044715 * c) * (h * h)))
    o_ref[...] = ((h + h * t) * v_ref[...].astype(F32)).astype(o_ref.dtype)


def conv_glu_gate(uv, conv_w, conv_b, width, f, bt=512, bc=512):
    m = uv.shape[0]
    bt = max(min(bt, m), width)
    bc = _fit(f, bc)
    rw = bt // width
    lastw = m // width - 1
    voff = f // bc
    return pl.pallas_call(
        functools.partial(_convglu_body, width=width),
        out_shape=jax.ShapeDtypeStruct((m, f), BF16), grid=(m // bt, f // bc),
        in_specs=[pl.BlockSpec((bt, bc), lambda i, j: (i, j)),
                  pl.BlockSpec((width, bc), lambda i, j: (jnp.maximum(i * rw - 1, 0), j)),
                  pl.BlockSpec((width, bc), lambda i, j: (jnp.minimum((i + 1) * rw, lastw), j)),
                  pl.BlockSpec((bt, bc), lambda i, j: (i, voff + j)),
                  pl.BlockSpec((9, bc), lambda i, j: (0, j)),
                  pl.BlockSpec((1, bc), lambda i, j: (0, j))],
        out_specs=pl.BlockSpec((bt, bc), lambda i, j: (i, j)),
        compiler_params=_cparams('parallel', 'parallel'), name='conv_glu')(
            uv, uv, uv, uv, 0.5 * conv_w.reshape(9, f), 0.5 * conv_b.reshape(1, f))


def _modulation(conds, down, up, bias):
    low = matmul(conds, down, precise=True, bm=8, bn=512, name='ada_down')
    return matmul(low, up, bias=bias, precise=True, bm=8, bn=2048, name='ada_up')


def _dn_mixer(streams, w, d, last, casts):
    heads = d // DN_DK
    w_in = w['w_in']
    if w_in.dtype == BF16:
        main = dict(w=w_in, w_cols=(0, 4 * d))
        ab = dict(w=w_in, w_cols=(4 * d, 4 * heads))
    else:
        main = dict(w=w_in[:, :4 * d].astype(BF16))
        ab = dict(w=w_in[:, 4 * d:].astype(BF16))
    gain = jnp.tile(w['norm_g'], heads)
    prepped = []
    for h in streams:
        p = matmul(h, name='dn_in', **main)
        pab = matmul(h, bn=4 * heads, name='dn_in_ab', **ab)
        q = dn_conv(p, w['conv_w'], 0, d, 'q')
        k = dn_conv(p, w['conv_w'], 1, d, 'k')
        v = dn_conv(p, w['conv_w'], 2, d, 'v')
        gb = dn_gates(pab, w['a_log'], w['dt_bias'])
        prepped.append((p, q, k, v, gb.T))
    states = [jnp.zeros((heads, DN_DK, DN_DK), F32)] * 2
    outs, done = [], [[], []]
    for si, (p, q, k, v, gbt) in enumerate(prepped):
        jobs = casts if si == 1 else ((), ())
        o_f, states[0], done[0] = dn_scan(q, k, v, gbt, states[0], 0, casts=jobs[0])
        y, states[1], done[1] = dn_scan(q, k, v, gbt, states[1], 1, fuse=(o_f, p, 3, gain),
                                        casts=jobs[1])
        outs.append(None if si == 0 and last else y)
    return outs, done[0], done[1]


def _hg_mixer(streams, w, d, last, casts):
    heads = d // HG_DK
    states = [jnp.zeros((heads, HG_DK, HG_DK), F32)] * 2
    outs, done = [], [[], []]
    for si, h in enumerate(streams):
        jobs = casts if si == 1 else ((), ())
        p = matmul(h, w['w_in'], name='hg_in')
        o_f, states[0], done[0] = hg_scan(p, w['lower'], states[0], 0, d, casts=jobs[0])
        y, states[1], done[1] = hg_scan(p, w['lower'], states[1], 1, d,
                                        fuse=(o_f, w['norm_g']), casts=jobs[1])
        outs.append(None if si == 0 and last else y)
    return outs, done[0], done[1]


def _rw_mixer(streams, widths, w, d, last, casts):
    w_rkv = w['w_rkv']
    w1 = jnp.concatenate([w['w1'][0], w['w1'][1]], axis=1).astype(BF16)
    a1 = jnp.concatenate([w['a1'][0], w['a1'][1]], axis=1).astype(BF16)
    w2 = w['w2'].astype(BF16)
    a2 = w['a2'].astype(BF16)
    lora = w['w1'].shape[-1]
    gl = w['g1'].shape[-1]
    glp = -(-gl // LANES) * LANES
    g1 = jnp.pad(w['g1'], ((0, 0), (0, glp - gl))).astype(BF16)
    g2 = jnp.pad(w['g2'], ((0, glp - gl), (0, 0))).astype(BF16)
    states = [jnp.zeros((d // LANES, LANES, LANES), F32)] * 2
    outs, done = [], [[], []]
    for si, (h, width) in enumerate(zip(streams, widths)):
        jobs = casts if si == 1 else ((), ())
        xr, xw, xk, xv, xa, xg = shift_mix(h, w['mu'], width)
        r = matmul(xr, w_rkv[0], name='rw_r')
        k = matmul(xk, w_rkv[1], name='rw_k')
        v = matmul(xv, w_rkv[2], name='rw_v')
        tw = matmul(xw, w1, epi='tanh', out_dtype=BF16, bn=2 * lora, name='rw_w1')
        ta = matmul(xa, a1, out_dtype=BF16, bn=2 * lora, name='rw_a1')
        tg = matmul(xg, g1, epi='sigmoid', out_dtype=BF16, name='rw_g1')
        gate = matmul(tg, g2, name='rw_g2')
        logw, a_sig = [], []
        for direction in range(2):
            logw.append(matmul(tw, w2[direction], bias=w['w0'][direction], epi='logw',
                               a_koff=direction, name='rw_w2'))
            a_sig.append(matmul(ta, a2[direction], bias=w['a0'][direction], epi='sigmoid',
                                a_koff=direction, name='rw_a2'))
        o_f, states[0], done[0] = rw_scan(r, k, v, a_sig[0], logw[0], w['k_k'], w['k_a'],
                                          states[0], 0, casts=jobs[0])
        y, states[1], done[1] = rw_scan(r, k, v, a_sig[1], logw[1], w['k_k'], w['k_a'],
                                        states[1], 1, casts=jobs[1],
                                        fuse=(o_f, a_sig[0], gate, w['lnx_w'], w['lnx_b'],
                                              w['r_k'].reshape(-1)))
        outs.append(None if si == 0 and last else y)
    return outs, done[0], done[1]


def kernel(x, c, ctx, c_ctx, ada_down, ada_up, ada_b, norm1_g, norm2_g, ffn_w_up, ffn_conv_w, ffn_conv_b, ffn_w_down, dn_w_in, dn_conv_w, dn_a_log, dn_dt_bias, dn_norm_g, dn_w_out, hg_w_in, hg_lower, hg_norm_g, hg_w_out, rw_mu, rw_w_rkv, rw_w0, rw_w1, rw_w2, rw_a0, rw_a1, rw_a2, rw_g1, rw_g2, rw_k_k, rw_k_a, rw_r_k, rw_lnx_w, rw_lnx_b, rw_w_out, final_g):
    _, seq, d = x.shape
    depth = ada_down.shape[0]
    n_ctx = ctx.shape[1]
    f = ffn_w_down.shape[1]
    xl, xc = x[0], ctx[0]
    sm = jax.nn.softmax(hg_lower.astype(F32), axis=0)
    lower_bounds = jnp.cumsum(sm, axis=0) - sm[0]
    conds = jnp.zeros((8, d), F32).at[0].set(jax.nn.silu(c[0])).at[1].set(jax.nn.silu(c_ctx))
    widths = (n_ctx, GRID_W)
    w_outs = {0: dn_w_out, 1: hg_w_out, 2: rw_w_out}
    w_ins = {0: dn_w_in, 1: hg_w_in, 2: rw_w_rkv.reshape(rw_w_rkv.shape[0], 3 * d, d)}

    def in_proj(i):
        return w_ins[i % 3], i // 3

    first, _ = in_proj(0)
    w_in_bf = first[0] if first.shape[2] % LANES else first[0].astype(BF16)
    for i in range(depth):
        kind, j = i % 3, i // 3
        last = i == depth - 1
        mod = _modulation(conds, ada_down[i], ada_up[i], ada_b[i])
        mods = [[mod[row, n * d:(n + 1) * d] for n in range(6)] for row in (1, 0)]
        xs = [xc, xl]
        h_dtype = F32 if kind == 2 else BF16
        hs = [normmod(t, norm1_g[i], mm[1], mm[0], h_dtype) for t, mm in zip(xs, mods)]
        nxt = in_proj(i + 1) if not last else None
        hide_next = nxt is not None and nxt[0].shape[2] % LANES == 0
        casts = ([(w_outs[kind], j), (ffn_w_up, i)],
                 [(ffn_w_down, i)] + ([nxt] if hide_next else []))
        if kind == 0:
            w = dict(w_in=w_in_bf, conv_w=dn_conv_w[j], a_log=dn_a_log[j],
                     dt_bias=dn_dt_bias[j], norm_g=dn_norm_g[j])
            ys, done_f, done_b = _dn_mixer(hs, w, d, last, casts)
        elif kind == 1:
            w = dict(w_in=w_in_bf, lower=lower_bounds[i], norm_g=hg_norm_g[j])
            ys, done_f, done_b = _hg_mixer(hs, w, d, last, casts)
        else:
            w = dict(mu=rw_mu[j], w_rkv=w_in_bf.reshape(3, d, d), w0=rw_w0[j], w1=rw_w1[j],
                     w2=rw_w2[j], a0=rw_a0[j], a1=rw_a1[j], a2=rw_a2[j], g1=rw_g1[j],
                     g2=rw_g2[j], k_k=rw_k_k[j], k_a=rw_k_a[j], r_k=rw_r_k[j],
                     lnx_w=rw_lnx_w[j], lnx_b=rw_lnx_b[j])
            ys, done_f, done_b = _rw_mixer(hs, widths, w, d, last, casts)
        w_out, w_up = done_f
        w_down = done_b[0]
        if hide_next:
            w_in_bf = done_b[1]
        elif nxt is not None:
            w_in_bf = nxt[0][nxt[1]]
        new = []
        for t, y, mm, width in zip(xs, ys, mods, widths):
            if y is None:
                new.append(t)
                continue
            t = matmul(y, w_out, gate=mm[2], resid=t, name='mix_out')
            h2 = normmod(t, norm2_g[i], mm[4], mm[3], BF16)
            uv = matmul(h2, w_up, out_dtype=BF16, name='ffn_up')
            gated = conv_glu_gate(uv, ffn_conv_w[i], ffn_conv_b[i], width, f)
            t = matmul(gated, w_down, gate=mm[5], resid=t, bm=512, bn=1024, w_single=True,
                       name='ffn_down')
            new.append(t)
        xc, xl = new
    zeros = jnp.zeros((d,), F32)
    return normmod(xl, final_g, zeros, zeros, F32)[None]
```

```python
import functools
import math

import jax
import jax.numpy as jnp
from jax import lax
from jax.experimental import pallas as pl
from jax.experimental.pallas import tpu as pltpu

F32 = jnp.float32
BF16 = jnp.bfloat16

EPS = 1e-6
RW_LN_EPS = 64e-5
GRID_W = 64
DN_DK = 128
DN_CHUNK = 128
HG_DK = 128
HG_CHUNK = 32
RW_HEAD = 64
RW_CHUNK = 64
LANES = 128
VMEM_LIMIT = 60 * 1024 * 1024


def _fit(n, b, unit=LANES):
    if n <= b:
        return n
    for cand in range(b - b % unit, 0, -unit):
        if n % cand == 0:
            return cand
    raise ValueError((n, b, unit))


def _cparams(*sem):
    return pltpu.CompilerParams(dimension_semantics=sem, vmem_limit_bytes=VMEM_LIMIT)


def _sigmoid(x):
    return 1.0 / (1.0 + jnp.exp(-x))


def _silu(x):
    return x * _sigmoid(x)


def _softplus(x):
    return jnp.maximum(x, 0.0) + jnp.log(1.0 + jnp.exp(-jnp.abs(x)))


def _gelu_tanh(x):
    c = math.sqrt(2.0 / math.pi)
    return 0.5 * x * (1.0 + jnp.tanh(c * (x + 0.044715 * (x * x * x))))


def _dot(a, b, dims=(((1,), (0,)), ((), ()))):
    return lax.dot_general(a.astype(BF16), b.astype(BF16), dims, preferred_element_type=F32)


def _dot_nt(a, b):
    return _dot(a, b, (((1,), (1,)), ((), ())))


def _dot_tn(a, b):
    return _dot(a, b, (((0,), (0,)), ((), ())))


def _split2(x):
    hi = x.astype(BF16)
    lo = (x - hi.astype(F32)).astype(BF16)
    return hi, lo


def _dot3(a, b, dims=(((1,), (0,)), ((), ()))):
    ah, al = _split2(a)
    bh, bl = _split2(b)
    d = functools.partial(lax.dot_general, dimension_numbers=dims, preferred_element_type=F32)
    return d(ah, bh) + (d(ah, bl) + d(al, bh))


def _tri_levels(c, size):
    ii = lax.broadcasted_iota(jnp.int32, (c, c), 0)
    jj = lax.broadcasted_iota(jnp.int32, (c, c), 1)
    levels = []
    s = 1
    while (1 << s) < size:
        lo = (ii >> s) != (jj >> s)
        hi = (ii >> (s + 1)) == (jj >> (s + 1))
        levels.append(lo & hi)
        s += 1
    return (ii >> 1) == (jj >> 1), ii == jj, levels


def _unit_tri_solve_many(ns, rhss, tri):
    pair, eye, levels = tri
    units = range(len(ns))
    ts = [jnp.where(pair, jnp.where(eye, 1.0, n), 0.0) for n in ns]
    for lmask in levels:
        ls = [jnp.where(lmask, n, 0.0) for n in ns]
        lt = [_dot(ls[i], ts[i]) for i in units]
        ts = [ts[i] + _dot(ts[i], lt[i]) for i in units]
    return [_dot(ts[i], rhss[i]) for i in units]


def _cumsum_rows(x, rev):
    c = x.shape[0]
    row = lax.broadcasted_iota(jnp.int32, x.shape, 0)
    s = 1
    while s < c:
        if rev:
            x = x + jnp.where(row < c - s, pltpu.roll(x, c - s, 0), 0.0)
        else:
            x = x + jnp.where(row >= s, pltpu.roll(x, s, 0), 0.0)
        s *= 2
    return x


def _epi_none(y):
    return y


def _epi_logw(y):
    return -jnp.exp(-_softplus(-y) - 0.5)


_EPILOGUES = {'none': _epi_none, 'tanh': jnp.tanh, 'sigmoid': _sigmoid, 'logw': _epi_logw}


def _mm_body(*refs, nk, epi, has_bias, has_gate, has_resid, precise):
    it = iter(refs)
    a_ref, w_ref = next(it), next(it)
    bias_ref = next(it) if has_bias else None
    gate_ref = next(it) if has_gate else None
    resid_ref = next(it) if has_resid else None
    o_ref = next(it)
    acc_ref = next(it) if nk > 1 else None

    def finish(y):
        if has_bias:
            y = y + bias_ref[...]
        y = _EPILOGUES[epi](y)
        if has_gate:
            y = y * gate_ref[...]
        if has_resid:
            y = resid_ref[...] + y
        o_ref[...] = y.astype(o_ref.dtype)

    if precise:
        part = _dot3(a_ref[...], w_ref[...])
    else:
        part = _dot(a_ref[...], w_ref[...])
    if nk == 1:
        finish(part)
    else:
        k = pl.program_id(2)

        @pl.when(k == 0)
        def _():
            acc_ref[...] = part

        @pl.when(k > 0)
        def _():
            acc_ref[...] += part

        @pl.when(k == nk - 1)
        def _():
            finish(acc_ref[...])


def matmul(a, w, *, bias=None, gate=None, resid=None, epi='none', out_dtype=F32,
           bm=1024, bn=1024, bk=None, a_koff=0, precise=False, w_single=False, w_cols=None,
           name='matmul'):
    m = a.shape[0]
    k, n = w.shape
    c0, n = (0, n) if w_cols is None else w_cols
    bm, bn = _fit(m, bm, 8), _fit(n, bn)
    assert c0 % bn == 0
    c0 //= bn
    bk = k if bk is None else _fit(k, bk)
    nk = k // bk
    if nk == 1:
        grid = (n // bn, m // bm)
        a_spec = pl.BlockSpec((bm, bk), lambda j, i: (i, a_koff))
        w_mode = dict(pipeline_mode=pl.Buffered(1)) if w_single else {}
        w_spec = pl.BlockSpec((bk, bn), lambda j, i: (0, j + c0), **w_mode)
        row_spec = pl.BlockSpec((1, bn), lambda j, i: (0, j))
        o_spec = pl.BlockSpec((bm, bn), lambda j, i: (i, j))
        sem = ('parallel', 'parallel')
        scratch = []
    else:
        grid = (n // bn, m // bm, nk)
        a_spec = pl.BlockSpec((bm, bk), lambda j, i, kk: (i, kk + a_koff * nk))
        w_spec = pl.BlockSpec((bk, bn), lambda j, i, kk: (kk, j + c0))
        row_spec = pl.BlockSpec((1, bn), lambda j, i, kk: (0, j))
        o_spec = pl.BlockSpec((bm, bn), lambda j, i, kk: (i, j))
        sem = ('parallel', 'parallel', 'arbitrary')
        scratch = [pltpu.VMEM((bm, bn), F32)]
    args, specs = [a, w], [a_spec, w_spec]
    for extra in (bias, gate):
        if extra is not None:
            args.append(extra.reshape(1, n).astype(F32))
            specs.append(row_spec)
    if resid is not None:
        args.append(resid)
        specs.append(o_spec)
    body = functools.partial(_mm_body, nk=nk, epi=epi, has_bias=bias is not None,
                             has_gate=gate is not None, has_resid=resid is not None,
                             precise=precise)
    return pl.pallas_call(
        body, out_shape=jax.ShapeDtypeStruct((m, n), out_dtype), grid=grid,
        in_specs=specs, out_specs=o_spec, scratch_shapes=scratch,
        compiler_params=_cparams(*sem), name=name)(*args)


def _normmod_body(x_ref, g_ref, sc_ref, sh_ref, o_ref):
    x = x_ref[...]
    y = x * lax.rsqrt(jnp.mean(x * x, axis=-1, keepdims=True) + EPS) * g_ref[...]
    o_ref[...] = (y * (1.0 + sc_ref[...]) + sh_ref[...]).astype(o_ref.dtype)


def normmod(x, g, sc, sh, out_dtype, bt=256):
    m, d = x.shape
    bt = min(bt, m)
    row = pl.BlockSpec((1, d), lambda i: (0, 0))
    blk = pl.BlockSpec((bt, d), lambda i: (i, 0))
    return pl.pallas_call(
        _normmod_body, out_shape=jax.ShapeDtypeStruct((m, d), out_dtype), grid=(m // bt,),
        in_specs=[blk, row, row, row], out_specs=blk,
        compiler_params=_cparams('parallel'), name='normmod')(
            x, g.reshape(1, d), sc.reshape(1, d), sh.reshape(1, d))


HALO = 16


def _dnproj_body(a_ref, ap_ref, an_ref, w_ref, cw_ref, o_ref, *, mode):
    i = pl.program_id(1)
    nb = pl.num_programs(1)
    bm = a_ref.shape[0]
    lhs = jnp.concatenate([ap_ref[...], a_ref[...], an_ref[...]], axis=0)
    y = jnp.dot(lhs, w_ref[...], preferred_element_type=F32)
    x = y[HALO:HALO + bm]
    row = lax.broadcasted_iota(jnp.int32, x.shape, 0)
    prev_row = jnp.where(i > 0, y[HALO - 1:HALO], 0.0)
    next_row = jnp.where(i < nb - 1, y[HALO + bm:HALO + bm + 1], 0.0)
    xm = jnp.where(row == 0, prev_row, pltpu.roll(x, 1, 0))
    xp = jnp.where(row == bm - 1, next_row, pltpu.roll(x, bm - 1, 0))
    y = _silu(cw_ref[0:1, :] * xm + cw_ref[1:2, :] * x + cw_ref[2:3, :] * xp)
    if mode == 'v':
        o_ref[...] = y.astype(o_ref.dtype)
        return
    scale = DN_DK ** -0.5 if mode == 'q' else 1.0
    for h in range(x.shape[1] // DN_DK):
        sl = slice(h * DN_DK, (h + 1) * DN_DK)
        yh = y[:, sl]
        inv = lax.rsqrt(jnp.sum(yh * yh, axis=-1, keepdims=True) + EPS)
        o_ref[:, sl] = (yh * (inv * scale)).astype(o_ref.dtype)


def dn_proj_conv(a, w, w_col0, conv_w, part, d, mode, bm=1024, bn=1024):
    m, k = a.shape
    bm, bn = _fit(m, bm, HALO), _fit(d, bn)
    assert w_col0 % bn == 0 and bm % HALO == 0
    c0 = (w_col0 + part * d) // bn
    cw0 = part * d // bn
    rh = bm // HALO
    lasth = m // HALO - 1
    return pl.pallas_call(
        functools.partial(_dnproj_body, mode=mode),
        out_shape=jax.ShapeDtypeStruct((m, d), F32), grid=(d // bn, m // bm),
        in_specs=[pl.BlockSpec((bm, k), lambda j, i: (i, 0)),
                  pl.BlockSpec((HALO, k), lambda j, i: (jnp.maximum(i * rh - 1, 0), 0)),
                  pl.BlockSpec((HALO, k), lambda j, i: (jnp.minimum((i + 1) * rh, lasth), 0)),
                  pl.BlockSpec((k, bn), lambda j, i: (0, c0 + j)),
                  pl.BlockSpec((3, bn), lambda j, i: (0, cw0 + j))],
        out_specs=pl.BlockSpec((bm, bn), lambda j, i: (i, j)),
        compiler_params=_cparams('parallel', 'parallel'), name='dn_proj_' + mode)(
            a, a, a, w, conv_w)


def _dnab_body(x_ref, alog_ref, dtb_ref, o_ref):
    x = x_ref[...]
    lane = lax.broadcasted_iota(jnp.int32, x.shape, 1)
    g = -jnp.exp(alog_ref[...]) * _softplus(x + dtb_ref[...])
    o_ref[...] = jnp.where(lane < x.shape[1] // 2, g, _sigmoid(x))


def dn_gates(pab, a_log, dt_bias, bt=1024):
    m, w = pab.shape
    bt = min(bt, m)
    zeros = jnp.zeros((w // 2,), F32)
    alog = jnp.concatenate([a_log.reshape(-1), zeros]).reshape(1, w)
    dtb = jnp.concatenate([dt_bias.reshape(-1), zeros]).reshape(1, w)
    row = pl.BlockSpec((1, w), lambda i: (0, 0))
    blk = pl.BlockSpec((bt, w), lambda i: (i, 0))
    return pl.pallas_call(
        _dnab_body, out_shape=jax.ShapeDtypeStruct((m, w), F32), grid=(m // bt,),
        in_specs=[blk, row, row], out_specs=blk,
        compiler_params=_cparams('parallel'), name='dn_gates')(pab, alog, dtb)


def _cast_jobs(ws, steps, step_index):
    args, in_specs, out_shapes, out_specs = [], [], [], []
    for stack, layer in ws:
        _, rows, cols = stack.shape
        ratio = next(r for r in (1, 2, 4, 8, 16) if steps % r == 0
                     and rows % (steps // r) == 0 and (rows // (steps // r)) % 16 == 0)
        blk = rows // (steps // ratio)
        args.append(stack)
        in_specs.append(pl.BlockSpec(
            (None, blk, cols),
            lambda g, j, ratio=ratio, layer=layer: (layer, step_index(g, j) // ratio, 0)))
        out_shapes.append(jax.ShapeDtypeStruct((rows, cols), BF16))
        out_specs.append(pl.BlockSpec(
            (blk, cols), lambda g, j, ratio=ratio: (step_index(g, j) // ratio, 0)))
    return args, in_specs, out_shapes, out_specs


def _split_refs(rest, n_fused, n_cast):
    a, b, c, d = n_fused, n_fused + n_cast, n_fused + n_cast + 2, n_fused + 2 * n_cast + 2
    return rest[:a], rest[a:b], rest[b:c], rest[c:d], rest[d:]


def _run_casts(cast_in, cast_out):
    for wi, wo in zip(cast_in, cast_out):
        wo[...] = wi[...].astype(BF16)


def _gated_head_norm(o, z, gain):
    y = o * lax.rsqrt(jnp.mean(o * o, axis=-1, keepdims=True) + EPS) * gain
    return y * _silu(z)


def _tri_masks(c, rev):
    ii = lax.broadcasted_iota(jnp.int32, (c, c), 0)
    jj = lax.broadcasted_iota(jnp.int32, (c, c), 1)
    if rev:
        return ii == jj, ii <= jj, ii < jj
    return ii == jj, ii >= jj, ii > jj


def _dn_intra(q, k, v, g_row, beta_row, masks, tri):
    eye, incl, strict = masks
    c = q[0].shape[0]
    heads = range(len(q))
    gc_col, beta_col, g_tot, decay = [], [], [], []
    for h in heads:
        g_bc = jnp.broadcast_to(g_row[h], (c, c))
        gcc = jnp.sum(jnp.where(incl, g_bc, 0.0), axis=1, keepdims=True)
        gcr = jnp.sum(jnp.where(eye, jnp.broadcast_to(gcc, (c, c)), 0.0), axis=0, keepdims=True)
        gc_col.append(gcc)
        beta_col.append(jnp.sum(jnp.where(eye, jnp.broadcast_to(beta_row[h], (c, c)), 0.0),
                                axis=1, keepdims=True))
        g_tot.append(jnp.sum(g_row[h], axis=1, keepdims=True))
        decay.append(jnp.where(incl, jnp.exp(jnp.where(incl, gcc - gcr, 0.0)), 0.0))
    kb = [k[h] * beta_col[h] for h in heads]
    kq = [_dot_nt(jnp.concatenate([kb[h], q[h]], axis=0), k[h]) for h in heads]
    kk = [kq[h][:c] for h in heads]
    qk = [kq[h][c:] for h in heads]
    n = [-jnp.where(strict, kk[h] * decay[h], 0.0) for h in heads]
    egc = [jnp.exp(gc_col[h]) for h in heads]
    rhs = [jnp.concatenate([kb[h] * egc[h], v[h] * beta_col[h]], axis=1) for h in heads]
    sol = _unit_tri_solve_many(n, rhs, tri)
    wq = [jnp.concatenate([sol[h][:, :DN_DK], q[h] * egc[h]], axis=0) for h in heads]
    u = [sol[h][:, DN_DK:] for h in heads]
    attn = [jnp.where(incl, qk[h] * decay[h], 0.0) for h in heads]
    kd = [k[h] * jnp.exp(g_tot[h] - gc_col[h]) for h in heads]
    gl = [jnp.exp(g_tot[h]) for h in heads]
    return wq, u, attn, kd, gl


def _dn_scan_body(q_ref, k_ref, v_ref, g_ref, b_ref, s0_ref, *rest, rev, nc, hg, fused, n_cast):
    fused_refs, cast_in, (o_ref, sf_ref), cast_out, scratch = _split_refs(
        rest, 3 if fused else 0, n_cast)
    if fused:
        of_ref, z_ref, gain_ref = fused_refs
    s_scr, wq_scr, u_scr, at_scr, kd_scr, gl_scr = scratch
    _run_casts(cast_in, cast_out)
    j = pl.program_id(1)

    @pl.when(j == 0)
    def _():
        s_scr[...] = s0_ref[...]

    c = DN_CHUNK
    masks = _tri_masks(c, rev)
    tri = _tri_levels(c, c)
    heads = range(hg)
    lanes = [slice(h * DN_DK, (h + 1) * DN_DK) for h in heads]

    def intra(ci, carry):
        rows = pl.ds(pl.multiple_of(ci * c, c), c)
        wq, u, attn, kd, gl = _dn_intra(
            [q_ref[rows, sl] for sl in lanes], [k_ref[rows, sl] for sl in lanes],
            [v_ref[rows, sl] for sl in lanes], [g_ref[ci, h:h + 1, :] for h in heads],
            [b_ref[ci, h:h + 1, :] for h in heads], masks, tri)
        for h in heads:
            wq_scr[ci, h] = wq[h].astype(BF16)
            u_scr[ci, h] = u[h]
            at_scr[ci, h] = attn[h].astype(BF16)
            kd_scr[ci, h] = kd[h].astype(BF16)
            gl_scr[ci, h] = jnp.broadcast_to(gl[h], (1, DN_DK))
        return carry

    lax.fori_loop(0, nc, intra, 0)

    def inter(ci, carry):
        cc = (nc - 1 - ci) if rev else ci
        rows = pl.ds(pl.multiple_of(cc * c, c), c)
        s = [s_scr[h] for h in heads]
        ws = [_dot(wq_scr[cc, h], s[h]) for h in heads]
        v_new = [u_scr[cc, h] - ws[h][:c] for h in heads]
        av = [_dot(at_scr[cc, h], v_new[h]) for h in heads]
        kv = [_dot_tn(kd_scr[cc, h], v_new[h]) for h in heads]
        for h in heads:
            o = ws[h][c:] + av[h]
            if fused:
                o = _gated_head_norm(o + of_ref[rows, lanes[h]], z_ref[rows, lanes[h]],
                                     gain_ref[:, lanes[h]])
            o_ref[rows, lanes[h]] = o.astype(o_ref.dtype)
            s_scr[h] = s[h] * gl_scr[cc, h] + kv[h]
        return carry

    lax.fori_loop(0, nc, inter, 0)

    @pl.when(j == pl.num_programs(1) - 1)
    def _():
        sf_ref[...] = s_scr[...]


def dn_scan(q, k, v, gbt, s0, direction, fuse=None, casts=(), hg=16, nc=2):
    m, d = q.shape
    h = d // DN_DK
    hg = min(hg, h)
    ng = h // hg
    nc = min(nc, m // DN_CHUNK)
    bt = nc * DN_CHUNK
    nb = m // bt
    rev = direction == 1
    gb4 = gbt.reshape(4 * ng, hg, m // DN_CHUNK, DN_CHUNK).transpose(0, 2, 1, 3)

    def tb(j):
        return (nb - 1 - j) if rev else j

    tok = pl.BlockSpec((bt, hg * DN_DK), lambda g, j: (tb(j), g))
    st = pl.BlockSpec((hg, DN_DK, DN_DK), lambda g, j: (g, 0, 0))
    g_spec = pl.BlockSpec((None, nc, hg, DN_CHUNK),
                          lambda g, j: (direction * ng + g, tb(j), 0, 0))
    b_spec = pl.BlockSpec((None, nc, hg, DN_CHUNK),
                          lambda g, j: ((2 + direction) * ng + g, tb(j), 0, 0))
    args, specs = [q, k, v, gb4, gb4, s0], [tok, tok, tok, g_spec, b_spec, st]
    if fuse is not None:
        o_other, p, z_part, gain = fuse
        args += [o_other, p, gain.reshape(1, d)]
        specs += [tok, pl.BlockSpec((bt, hg * DN_DK), lambda g, j: (tb(j), z_part * ng + g)),
                  pl.BlockSpec((1, hg * DN_DK), lambda g, j: (0, g))]
    c_args, c_in, c_shapes, c_out = _cast_jobs(casts, ng * nb, lambda g, j: g * nb + j)
    res = pl.pallas_call(
        functools.partial(_dn_scan_body, rev=rev, nc=nc, hg=hg, fused=fuse is not None,
                          n_cast=len(casts)),
        out_shape=(jax.ShapeDtypeStruct((m, d), F32 if fuse is None else BF16),
                   jax.ShapeDtypeStruct(s0.shape, F32), *c_shapes),
        grid=(ng, nb), in_specs=specs + c_in, out_specs=(tok, st, *c_out),
        scratch_shapes=[pltpu.VMEM((hg, DN_DK, DN_DK), F32),
                        pltpu.VMEM((nc, hg, 2 * DN_CHUNK, DN_DK), BF16),
                        pltpu.VMEM((nc, hg, DN_CHUNK, DN_DK), F32),
                        pltpu.VMEM((nc, hg, DN_CHUNK, DN_CHUNK), BF16),
                        pltpu.VMEM((nc, hg, DN_CHUNK, DN_DK), BF16),
                        pltpu.VMEM((nc, hg, 1, DN_DK), F32)],
        compiler_params=_cparams('parallel', 'arbitrary'),
        name='dn_scan_bwd' if rev else 'dn_scan_fwd')(*args, *c_args)
    return res[0], res[1], list(res[2:])


def _hg_chunk(q_raw, f_raw, v, lb, st, masks, rev):
    _, incl, _ = masks
    c = q_raw[0].shape[0]
    heads = range(len(q_raw))
    mid = (c - 1 - c // 2) if rev else c // 2
    last = 0 if rev else c - 1
    qm, km, qs, kd, gl = [], [], [], [], []
    for h in heads:
        q = _silu(q_raw[h])
        f = lb[h] + (1.0 - lb[h]) * _sigmoid(f_raw[h])
        k = 1.0 - f
        b = _cumsum_rows(jnp.log(f), rev)
        m = b[mid:mid + 1, :]
        b_last = b[last:last + 1, :]
        qm.append(q * jnp.exp(b - m))
        km.append(k * jnp.exp(m - b))
        qs.append(q * jnp.exp(b))
        kd.append(k * jnp.exp(b_last - b))
        gl.append(jnp.exp(b_last))
    a_qk = [jnp.where(incl, _dot_nt(qm[h], km[h]), 0.0) for h in heads]
    inter = [_dot_nt(qs[h], st[h]) for h in heads]
    kv = [_dot_tn(v[h], kd[h]) for h in heads]
    intra = [_dot(a_qk[h], v[h]) for h in heads]
    o = [inter[h] + intra[h] for h in heads]
    st_new = [st[h] * gl[h] + kv[h] for h in heads]
    return o, st_new


def _hg_scan_body(q_ref, f_ref, v_ref, lb_ref, s0_ref, *rest, rev, nc, hg, fused, n_cast):
    fused_refs, cast_in, (o_ref, sf_ref), cast_out, (s_scr,) = _split_refs(
        rest, 3 if fused else 0, n_cast)
    if fused:
        of_ref, z_ref, gain_ref = fused_refs
    _run_casts(cast_in, cast_out)
    j = pl.program_id(1)

    @pl.when(j == 0)
    def _():
        s_scr[...] = s0_ref[...]

    c = HG_CHUNK
    masks = _tri_masks(c, rev)
    heads = range(hg)
    lanes = [slice(h * HG_DK, (h + 1) * HG_DK) for h in heads]

    def chunk(ci, carry):
        cc = (nc - 1 - ci) if rev else ci
        rows = pl.ds(pl.multiple_of(cc * c, c), c)
        o, s_new = _hg_chunk([q_ref[rows, sl] for sl in lanes], [f_ref[rows, sl] for sl in lanes],
                             [v_ref[rows, sl] for sl in lanes], [lb_ref[:, sl] for sl in lanes],
                             [s_scr[h] for h in heads], masks, rev)
        for h in heads:
            oh = o[h]
            if fused:
                oh = _gated_head_norm(oh + of_ref[rows, lanes[h]], z_ref[rows, lanes[h]],
                                      gain_ref[:, lanes[h]])
            o_ref[rows, lanes[h]] = oh.astype(o_ref.dtype)
            s_scr[h] = s_new[h]
        return carry

    lax.fori_loop(0, nc, chunk, 0)

    @pl.when(j == pl.num_programs(1) - 1)
    def _():
        sf_ref[...] = s_scr[...]


def hg_scan(p, lower, s0, direction, d, fuse=None, casts=(), hg=16, nc=8):
    m = p.shape[0]
    h = d // HG_DK
    hg = min(hg, h)
    ng = h // hg
    nc = min(nc, m // HG_CHUNK)
    bt = nc * HG_CHUNK
    nb = m // bt
    rev = direction == 1

    def tb(j):
        return (nb - 1 - j) if rev else j

    def part(pi):
        return pl.BlockSpec((bt, hg * HG_DK), lambda g, j: (tb(j), pi * ng + g))

    st = pl.BlockSpec((hg, HG_DK, HG_DK), lambda g, j: (g, 0, 0))
    row = pl.BlockSpec((1, hg * HG_DK), lambda g, j: (0, g))
    tok = pl.BlockSpec((bt, hg * HG_DK), lambda g, j: (tb(j), g))
    args = [p, p, p, lower.reshape(1, d), s0]
    specs = [part(0), part(1 + direction), part(3), row, st]
    if fuse is not None:
        o_other, gain = fuse
        args += [o_other, p, gain.reshape(1, d)]
        specs += [tok, part(4), row]
    c_args, c_in, c_shapes, c_out = _cast_jobs(casts, ng * nb, lambda g, j: g * nb + j)
    res = pl.pallas_call(
        functools.partial(_hg_scan_body, rev=rev, nc=nc, hg=hg, fused=fuse is not None,
                          n_cast=len(casts)),
        out_shape=(jax.ShapeDtypeStruct((m, d), F32 if fuse is None else BF16),
                   jax.ShapeDtypeStruct(s0.shape, F32), *c_shapes),
        grid=(ng, nb), in_specs=specs + c_in, out_specs=(tok, st, *c_out),
        scratch_shapes=[pltpu.VMEM((hg, HG_DK, HG_DK), F32)],
        compiler_params=_cparams('parallel', 'arbitrary'),
        name='hg_scan_bwd' if rev else 'hg_scan_fwd')(*args, *c_args)
    return res[0], res[1], list(res[2:])


def _shiftmix_body(x_ref, prev_ref, next_ref, mu_ref, *o_refs, width):
    i = pl.program_id(0)
    nb = pl.num_programs(0)
    bt, d = x_ref.shape
    dq = d // 4
    col = lax.broadcasted_iota(jnp.int32, (bt, dq), 0) % width
    for qi in range(4):
        sl = slice(qi * dq, (qi + 1) * dq)
        x = x_ref[:, sl]
        if qi == 0:
            sh = jnp.where(col == 0, 0.0, pltpu.roll(x, 1, 0))
        elif qi == 1:
            sh = jnp.where(col == width - 1, 0.0, pltpu.roll(x, bt - 1, 0))
        elif qi == 2:
            edge = jnp.where(i > 0, prev_ref[:, sl], 0.0)
            sh = edge if bt == width else jnp.concatenate([edge, x[:bt - width]], axis=0)
        else:
            edge = jnp.where(i < nb - 1, next_ref[:, sl], 0.0)
            sh = edge if bt == width else jnp.concatenate([x[width:], edge], axis=0)
        xx = sh - x
        for n, o_ref in enumerate(o_refs):
            o_ref[:, sl] = (x + xx * mu_ref[n:n + 1, sl]).astype(o_ref.dtype)


def shift_mix(h, mu, width, bt=128):
    m, d = h.shape
    bt = max(min(bt, m), width)
    rw = bt // width
    lastw = m // width - 1
    blk = pl.BlockSpec((bt, d), lambda i: (i, 0))
    out = jax.ShapeDtypeStruct((m, d), BF16)
    return pl.pallas_call(
        functools.partial(_shiftmix_body, width=width),
        out_shape=(out,) * 6, grid=(m // bt,),
        in_specs=[blk,
                  pl.BlockSpec((width, d), lambda i: (jnp.maximum(i * rw - 1, 0), 0)),
                  pl.BlockSpec((width, d), lambda i: (jnp.minimum((i + 1) * rw, lastw), 0)),
                  pl.BlockSpec((6, d), lambda i: (0, 0))],
        out_specs=(blk,) * 6, compiler_params=_cparams('arbitrary'), name='shift_mix')(h, h, h, mu)


def _halves(x, lo_mask):
    return jnp.concatenate([jnp.where(lo_mask, x, 0.0), jnp.where(lo_mask, 0.0, x)], axis=0)


def _head_sum(x, lo_mask):
    s_lo = jnp.sum(jnp.where(lo_mask, x, 0.0), axis=-1, keepdims=True)
    s_hi = jnp.sum(jnp.where(lo_mask, 0.0, x), axis=-1, keepdims=True)
    return jnp.where(lo_mask, s_lo, s_hi)


def _rw_intra(r, k, v, a_sig, logw, k_k, k_a, masks, tri, rev):
    lo, strict2, incl_cat = masks
    c = r[0].shape[0]
    groups = range(len(r))
    mid = (c - 1 - c // 2) if rev else c // 2
    last = 0 if rev else c - 1
    am2, bm2, km2, v2, rm, ae2, rg, bd2, kd2, gl = ([] for _ in range(10))
    for g in groups:
        kq = k[g] * k_k[g]
        kk = kq * lax.rsqrt(_head_sum(kq * kq, lo) + EPS)
        b = kk * a_sig[g]
        kd = k[g] * (1.0 + (a_sig[g] - 1.0) * k_a[g])
        cs = _cumsum_rows(logw[g], rev)
        ce = cs - logw[g]
        m = cs[mid:mid + 1, :]
        c_last = cs[last:last + 1, :]
        e_mc = jnp.exp(m - cs)
        dec = jnp.exp(c_last - cs)
        am2.append(_halves(-kk * jnp.exp(ce - m), lo))
        bm2.append(_halves(b * e_mc, lo))
        km2.append(_halves(kd * e_mc, lo))
        v2.append(_halves(v[g], lo))
        rm.append(r[g] * jnp.exp(cs - m))
        ae2.append(_halves(-kk * jnp.exp(ce), lo))
        rg.append(r[g] * jnp.exp(cs))
        bd2.append(_halves(b * dec, lo))
        kd2.append(_halves(kd * dec, lo))
        gl.append(jnp.exp(c_last))
    pairs = [_dot_nt(jnp.concatenate([am2[g], rm[g]], axis=0),
                     jnp.concatenate([bm2[g], km2[g]], axis=0)) for g in groups]
    n = [jnp.where(strict2, pairs[g][:2 * c, :2 * c], 0.0) for g in groups]
    a_ak = [jnp.where(strict2, pairs[g][:2 * c, 2 * c:], 0.0) for g in groups]
    a_rb = [jnp.where(incl_cat, pairs[g][2 * c:, :2 * c], 0.0) for g in groups]
    a_rk = [jnp.where(incl_cat, pairs[g][2 * c:, 2 * c:], 0.0) for g in groups]
    akv = [_dot(a_ak[g], v2[g]) for g in groups]
    ov = [_dot(a_rk[g], v2[g]) for g in groups]
    kv = [_dot_tn(v2[g], kd2[g]) for g in groups]
    sol = _unit_tri_solve_many(n, [jnp.concatenate([ae2[g], akv[g]], axis=1) for g in groups],
                               tri)
    wr = [jnp.concatenate([sol[g][:, :LANES], rg[g]], axis=0) for g in groups]
    uv2 = [sol[g][:, LANES:] for g in groups]
    return wr, uv2, a_rb, ov, bd2, kv, gl


def _rw_output(o, r, k, v, a_sum, gate, ln_w, ln_b, k_a, r_k, lo):
    inv_n = 1.0 / RW_HEAD
    oc = o - _head_sum(o, lo) * inv_n
    y = oc * lax.rsqrt(_head_sum(oc * oc, lo) * inv_n + RW_LN_EPS) * ln_w + ln_b
    k_sum = k * (2.0 + (a_sum - 2.0) * k_a)
    bonus = _head_sum(r * k_sum * r_k, lo) * v
    return (y + bonus) * gate


def _rw_masks(c, rev):
    lo = lax.broadcasted_iota(jnp.int32, (1, LANES), 1) < RW_HEAD
    ii = lax.broadcasted_iota(jnp.int32, (2 * c, 2 * c), 0)
    jj = lax.broadcasted_iota(jnp.int32, (2 * c, 2 * c), 1)
    same = (ii // c) == (jj // c)
    il, jl = ii % c, jj % c
    strict2 = same & ((il < jl) if rev else (il > jl))
    ic = lax.broadcasted_iota(jnp.int32, (c, 2 * c), 0)
    jc = lax.broadcasted_iota(jnp.int32, (c, 2 * c), 1) % c
    incl_cat = (ic <= jc) if rev else (ic >= jc)
    return lo, strict2, incl_cat


def _rw_scan_body(r_ref, k_ref, v_ref, a_ref, lw_ref, kk_ref, ka_ref, s0_ref, *rest,
                  rev, nc, hg, fused, n_cast):
    fused_refs, cast_in, (o_ref, sf_ref), cast_out, scratch = _split_refs(
        rest, 6 if fused else 0, n_cast)
    if fused:
        of_ref, ao_ref, gate_ref, lnw_ref, lnb_ref, rk_ref = fused_refs
    s_scr, wr_scr, uv_scr, arb_scr, ov_scr, bd_scr, kv_scr, gl_scr = scratch
    _run_casts(cast_in, cast_out)
    j = pl.program_id(1)

    @pl.when(j == 0)
    def _():
        s_scr[...] = s0_ref[...]

    c = RW_CHUNK
    masks = _rw_masks(c, rev)
    tri = _tri_levels(2 * c, c)
    groups = range(hg)
    lanes = [slice(g * LANES, (g + 1) * LANES) for g in groups]

    def intra(ci, carry):
        rows = pl.ds(pl.multiple_of(ci * c, c), c)
        wr, uv2, a_rb, ov, bd2, kv, gl = _rw_intra(
            [r_ref[rows, sl] for sl in lanes], [k_ref[rows, sl] for sl in lanes],
            [v_ref[rows, sl] for sl in lanes], [a_ref[rows, sl] for sl in lanes],
            [lw_ref[rows, sl] for sl in lanes], [kk_ref[:, sl] for sl in lanes],
            [ka_ref[:, sl] for sl in lanes], masks, tri, rev)
        for g in groups:
            wr_scr[ci, g] = wr[g].astype(BF16)
            uv_scr[ci, g] = uv2[g]
            arb_scr[ci, g] = a_rb[g].astype(BF16)
            ov_scr[ci, g] = ov[g]
            bd_scr[ci, g] = bd2[g].astype(BF16)
            kv_scr[ci, g] = kv[g]
            gl_scr[ci, g] = gl[g]
        return carry

    lax.fori_loop(0, nc, intra, 0)

    def inter(ci, carry):
        cc = (nc - 1 - ci) if rev else ci
        rows = pl.ds(pl.multiple_of(cc * c, c), c)
        st = [s_scr[g] for g in groups]
        ws = [_dot_nt(wr_scr[cc, g], st[g]) for g in groups]
        u2 = [ws[g][:2 * c] + uv_scr[cc, g] for g in groups]
        au = [_dot(arb_scr[cc, g], u2[g]) for g in groups]
        ub = [_dot_tn(u2[g], bd_scr[cc, g]) for g in groups]
        for g in groups:
            o = ws[g][2 * c:] + au[g] + ov_scr[cc, g]
            if fused:
                sl = lanes[g]
                o = _rw_output(o + of_ref[rows, sl], r_ref[rows, sl], k_ref[rows, sl],
                               v_ref[rows, sl], a_ref[rows, sl] + ao_ref[rows, sl],
                               gate_ref[rows, sl], lnw_ref[:, sl], lnb_ref[:, sl], ka_ref[:, sl],
                               rk_ref[:, sl], masks[0])
            o_ref[rows, lanes[g]] = o.astype(o_ref.dtype)
            s_scr[g] = st[g] * gl_scr[cc, g] + kv_scr[cc, g] + ub[g]
        return carry

    lax.fori_loop(0, nc, inter, 0)

    @pl.when(j == pl.num_programs(1) - 1)
    def _():
        sf_ref[...] = s_scr[...]


def rw_scan(r, k, v, a_sig, logw, k_k, k_a, s0, direction, fuse=None, casts=(), hg=16, nc=2):
    m, d = r.shape
    npair = d // LANES
    hg = min(hg, npair)
    ng = npair // hg
    nc = min(nc, m // RW_CHUNK)
    bt = nc * RW_CHUNK
    nb = m // bt
    rev = direction == 1

    def tb(j):
        return (nb - 1 - j) if rev else j

    tok = pl.BlockSpec((bt, hg * LANES), lambda g, j: (tb(j), g))
    row = pl.BlockSpec((1, hg * LANES), lambda g, j: (0, g))
    st = pl.BlockSpec((hg, LANES, LANES), lambda g, j: (g, 0, 0))
    args = [r, k, v, a_sig, logw, k_k.reshape(1, d), k_a.reshape(1, d), s0]
    specs = [tok, tok, tok, tok, tok, row, row, st]
    if fuse is not None:
        args += list(fuse[:3]) + [t.reshape(1, d) for t in fuse[3:]]
        specs += [tok, tok, tok, row, row, row]
    c_args, c_in, c_shapes, c_out = _cast_jobs(casts, ng * nb, lambda g, j: g * nb + j)
    res = pl.pallas_call(
        functools.partial(_rw_scan_body, rev=rev, nc=nc, hg=hg, fused=fuse is not None,
                          n_cast=len(casts)),
        out_shape=(jax.ShapeDtypeStruct((m, d), F32 if fuse is None else BF16),
                   jax.ShapeDtypeStruct(s0.shape, F32), *c_shapes),
        grid=(ng, nb), in_specs=specs + c_in, out_specs=(tok, st, *c_out),
        scratch_shapes=[pltpu.VMEM((hg, LANES, LANES), F32),
                        pltpu.VMEM((nc, hg, 3 * RW_CHUNK, LANES), BF16),
                        pltpu.VMEM((nc, hg, 2 * RW_CHUNK, LANES), F32),
                        pltpu.VMEM((nc, hg, RW_CHUNK, 2 * RW_CHUNK), BF16),
                        pltpu.VMEM((nc, hg, RW_CHUNK, LANES), F32),
                        pltpu.VMEM((nc, hg, 2 * RW_CHUNK, LANES), BF16),
                        pltpu.VMEM((nc, hg, LANES, LANES), F32),
                        pltpu.VMEM((nc, hg, 1, LANES), F32)],
        compiler_params=_cparams('parallel', 'arbitrary'),
        name='rw_scan_bwd' if rev else 'rw_scan_fwd')(*args, *c_args)
    return res[0], res[1], list(res[2:])


def _convglu_body(u_ref, prev_ref, next_ref, v_ref, cw_ref, cb_ref, o_ref, *, width):
    i = pl.program_id(0)
    nb = pl.num_programs(0)
    x = u_ref[...].astype(F32)
    bt = x.shape[0]
    up = jnp.where(i > 0, prev_ref[...].astype(F32), 0.0)
    dn = jnp.where(i < nb - 1, next_ref[...].astype(F32), 0.0)
    if bt > width:
        up = jnp.concatenate([up, x[:bt - width]], axis=0)
        dn = jnp.concatenate([x[width:], dn], axis=0)
    col = lax.broadcasted_iota(jnp.int32, x.shape, 0) % width
    rows = (up, x, dn)

    def column(kw):
        return sum(cw_ref[3 * r + kw:3 * r + kw + 1, :] * rows[r] for r in range(3))

    left = jnp.where(col == 0, 0.0, pltpu.roll(column(0), 1, 0))
    right = jnp.where(col == width - 1, 0.0, pltpu.roll(column(2), bt - 1, 0))
    h = column(1) + left + right + cb_ref[...]
    c = math.sqrt(2.0 / math.pi)
    t = jnp.tanh(h * (2.0 * c + (8.0 * 0.044715 * c) * (h * h)))
    o_ref[...] = ((h + h * t) * v_ref[...].astype(F32)).astype(o_ref.dtype)


def conv_glu_gate(uv, conv_w, conv_b, width, f, bt=512, bc=512):
    m = uv.shape[0]
    bt = max(min(bt, m), width)
    bc = _fit(f, bc)
    rw = bt // width
    lastw = m // width - 1
    voff = f // bc
    return pl.pallas_call(
        functools.partial(_convglu_body, width=width),
        out_shape=jax.ShapeDtypeStruct((m, f), BF16), grid=(m // bt, f // bc),
        in_specs=[pl.BlockSpec((bt, bc), lambda i, j: (i, j)),
                  pl.BlockSpec((width, bc), lambda i, j: (jnp.maximum(i * rw - 1, 0), j)),
                  pl.BlockSpec((width, bc), lambda i, j: (jnp.minimum((i + 1) * rw, lastw), j)),
                  pl.BlockSpec((bt, bc), lambda i, j: (i, voff + j)),
                  pl.BlockSpec((9, bc), lambda i, j: (0, j)),
                  pl.BlockSpec((1, bc), lambda i, j: (0, j))],
        out_specs=pl.BlockSpec((bt, bc), lambda i, j: (i, j)),
        compiler_params=_cparams('parallel', 'parallel'), name='conv_glu')(
            uv, uv, uv, uv, 0.5 * conv_w.reshape(9, f), 0.5 * conv_b.reshape(1, f))


def _modulation(conds, down, up, bias):
    low = matmul(conds, down, precise=True, bm=8, bn=512, name='ada_down')
    return matmul(low, up, bias=bias, precise=True, bm=8, bn=2048, name='ada_up')


def _dn_mixer(streams, w, d, last, casts):
    heads = d // DN_DK
    w_in = w['w_in']
    if w_in.dtype == BF16:
        w_main = w_in
        ab = dict(w=w_in, w_cols=(4 * d, 4 * heads))
    else:
        w_main = w_in[:, :4 * d].astype(BF16)
        ab = dict(w=w_in[:, 4 * d:].astype(BF16))
    gain = jnp.tile(w['norm_g'], heads)
    prepped = []
    for h in streams:
        q, k, v = (dn_proj_conv(h, w_main, 0, w['conv_w'], part, d, mode)
                   for part, mode in enumerate('qkv'))
        z = matmul(h, w_main, w_cols=(3 * d, d), name='dn_in_z')
        pab = matmul(h, bn=4 * heads, name='dn_in_ab', **ab)
        gb = dn_gates(pab, w['a_log'], w['dt_bias'])
        prepped.append((z, q, k, v, gb.T))
    states = [jnp.zeros((heads, DN_DK, DN_DK), F32)] * 2
    outs, done = [], [[], []]
    for si, (z, q, k, v, gbt) in enumerate(prepped):
        jobs = casts if si == 1 else ((), ())
        o_f, states[0], done[0] = dn_scan(q, k, v, gbt, states[0], 0, casts=jobs[0])
        y, states[1], done[1] = dn_scan(q, k, v, gbt, states[1], 1, fuse=(o_f, z, 0, gain),
                                        casts=jobs[1])
        outs.append(None if si == 0 and last else y)
    return outs, done[0], done[1]


def _hg_mixer(streams, w, d, last, casts):
    heads = d // HG_DK
    states = [jnp.zeros((heads, HG_DK, HG_DK), F32)] * 2
    outs, done = [], [[], []]
    for si, h in enumerate(streams):
        jobs = casts if si == 1 else ((), ())
        p = matmul(h, w['w_in'], name='hg_in')
        o_f, states[0], done[0] = hg_scan(p, w['lower'], states[0], 0, d, casts=jobs[0])
        y, states[1], done[1] = hg_scan(p, w['lower'], states[1], 1, d,
                                        fuse=(o_f, w['norm_g']), casts=jobs[1])
        outs.append(None if si == 0 and last else y)
    return outs, done[0], done[1]


def _rw_mixer(streams, widths, w, d, last, casts):
    w_rkv = w['w_rkv']
    w1 = jnp.concatenate([w['w1'][0], w['w1'][1]], axis=1).astype(BF16)
    a1 = jnp.concatenate([w['a1'][0], w['a1'][1]], axis=1).astype(BF16)
    w2 = w['w2'].astype(BF16)
    a2 = w['a2'].astype(BF16)
    lora = w['w1'].shape[-1]
    gl = w['g1'].shape[-1]
    glp = -(-gl // LANES) * LANES
    g1 = jnp.pad(w['g1'], ((0, 0), (0, glp - gl))).astype(BF16)
    g2 = jnp.pad(w['g2'], ((0, glp - gl), (0, 0))).astype(BF16)
    states = [jnp.zeros((d // LANES, LANES, LANES), F32)] * 2
    outs, done = [], [[], []]
    for si, (h, width) in enumerate(zip(streams, widths)):
        jobs = casts if si == 1 else ((), ())
        xr, xw, xk, xv, xa, xg = shift_mix(h, w['mu'], width)
        r = matmul(xr, w_rkv[0], name='rw_r')
        k = matmul(xk, w_rkv[1], name='rw_k')
        v = matmul(xv, w_rkv[2], name='rw_v')
        tw = matmul(xw, w1, epi='tanh', out_dtype=BF16, bn=2 * lora, name='rw_w1')
        ta = matmul(xa, a1, out_dtype=BF16, bn=2 * lora, name='rw_a1')
        tg = matmul(xg, g1, epi='sigmoid', out_dtype=BF16, name='rw_g1')
        gate = matmul(tg, g2, name='rw_g2')
        logw, a_sig = [], []
        for direction in range(2):
            logw.append(matmul(tw, w2[direction], bias=w['w0'][direction], epi='logw',
                               a_koff=direction, name='rw_w2'))
            a_sig.append(matmul(ta, a2[direction], bias=w['a0'][direction], epi='sigmoid',
                                a_koff=direction, name='rw_a2'))
        o_f, states[0], done[0] = rw_scan(r, k, v, a_sig[0], logw[0], w['k_k'], w['k_a'],
                                          states[0], 0, casts=jobs[0])
        y, states[1], done[1] = rw_scan(r, k, v, a_sig[1], logw[1], w['k_k'], w['k_a'],
                                        states[1], 1, casts=jobs[1],
                                        fuse=(o_f, a_sig[0], gate, w['lnx_w'], w['lnx_b'],
                                              w['r_k'].reshape(-1)))
        outs.append(None if si == 0 and last else y)
    return outs, done[0], done[1]


def kernel(x, c, ctx, c_ctx, ada_down, ada_up, ada_b, norm1_g, norm2_g, ffn_w_up, ffn_conv_w, ffn_conv_b, ffn_w_down, dn_w_in, dn_conv_w, dn_a_log, dn_dt_bias, dn_norm_g, dn_w_out, hg_w_in, hg_lower, hg_norm_g, hg_w_out, rw_mu, rw_w_rkv, rw_w0, rw_w1, rw_w2, rw_a0, rw_a1, rw_a2, rw_g1, rw_g2, rw_k_k, rw_k_a, rw_r_k, rw_lnx_w, rw_lnx_b, rw_w_out, final_g):
    _, seq, d = x.shape
    depth = ada_down.shape[0]
    n_ctx = ctx.shape[1]
    f = ffn_w_down.shape[1]
    xl, xc = x[0], ctx[0]
    sm = jax.nn.softmax(hg_lower.astype(F32), axis=0)
    lower_bounds = jnp.cumsum(sm, axis=0) - sm[0]
    conds = jnp.zeros((8, d), F32).at[0].set(jax.nn.silu(c[0])).at[1].set(jax.nn.silu(c_ctx))
    widths = (n_ctx, GRID_W)
    w_outs = {0: dn_w_out, 1: hg_w_out, 2: rw_w_out}
    w_ins = {0: dn_w_in, 1: hg_w_in, 2: rw_w_rkv.reshape(rw_w_rkv.shape[0], 3 * d, d)}

    def in_proj(i):
        return w_ins[i % 3], i // 3

    first, _ = in_proj(0)
    w_in_bf = first[0] if first.shape[2] % LANES else first[0].astype(BF16)
    for i in range(depth):
        kind, j = i % 3, i // 3
        last = i == depth - 1
        mod = _modulation(conds, ada_down[i], ada_up[i], ada_b[i])
        mods = [[mod[row, n * d:(n + 1) * d] for n in range(6)] for row in (1, 0)]
        xs = [xc, xl]
        h_dtype = F32 if kind == 2 else BF16
        hs = [normmod(t, norm1_g[i], mm[1], mm[0], h_dtype) for t, mm in zip(xs, mods)]
        nxt = in_proj(i + 1) if not last else None
        hide_next = nxt is not None and nxt[0].shape[2] % LANES == 0
        casts = ([(w_outs[kind], j), (ffn_w_up, i)],
                 [(ffn_w_down, i)] + ([nxt] if hide_next else []))
        if kind == 0:
            w = dict(w_in=w_in_bf, conv_w=dn_conv_w[j], a_log=dn_a_log[j],
                     dt_bias=dn_dt_bias[j], norm_g=dn_norm_g[j])
            ys, done_f, done_b = _dn_mixer(hs, w, d, last, casts)
        elif kind == 1:
            w = dict(w_in=w_in_bf, lower=lower_bounds[i], norm_g=hg_norm_g[j])
            ys, done_f, done_b = _hg_mixer(hs, w, d, last, casts)
        else:
            w = dict(mu=rw_mu[j], w_rkv=w_in_bf.reshape(3, d, d), w0=rw_w0[j], w1=rw_w1[j],
                     w2=rw_w2[j], a0=rw_a0[j], a1=rw_a1[j], a2=rw_a2[j], g1=rw_g1[j],
                     g2=rw_g2[j], k_k=rw_k_k[j], k_a=rw_k_a[j], r_k=rw_r_k[j],
                     lnx_w=rw_lnx_w[j], lnx_b=rw_lnx_b[j])
            ys, done_f, done_b = _rw_mixer(hs, widths, w, d, last, casts)
        w_out, w_up = done_f
        w_down = done_b[0]
        if hide_next:
            w_in_bf = done_b[1]
        elif nxt is not None:
            w_in_bf = nxt[0][nxt[1]]
        new = []
        for t, y, mm, width in zip(xs, ys, mods, widths):
            if y is None:
                new.append(t)
                continue
            t = matmul(y, w_out, gate=mm[2], resid=t, name='mix_out')
            h2 = normmod(t, norm2_g[i], mm[4], mm[3], BF16)
            uv = matmul(h2, w_up, out_dtype=BF16, name='ffn_up')
            gated = conv_glu_gate(uv, ffn_conv_w[i], ffn_conv_b[i], width, f)
            t = matmul(gated, w_down, gate=mm[5], resid=t, bm=512, bn=1024, w_single=True,
                       name='ffn_down')
            new.append(t)
        xc, xl = new
    zeros = jnp.zeros((d,), F32)
    return normmod(xl, final_g, zeros, zeros, F32)[None]
```

```python
import functools
import math

import jax
import jax.numpy as jnp
from jax import lax
from jax.experimental import pallas as pl
from jax.experimental.pallas import tpu as pltpu

F32 = jnp.float32
BF16 = jnp.bfloat16

EPS = 1e-6
RW_LN_EPS = 64e-5
GRID_W = 64
DN_DK = 128
DN_CHUNK = 128
HG_DK = 128
HG_CHUNK = 32
RW_HEAD = 64
RW_CHUNK = 64
LANES = 128
VMEM_LIMIT = 60 * 1024 * 1024


def _fit(n, b, unit=LANES):
    if n <= b:
        return n
    for cand in range(b - b % unit, 0, -unit):
        if n % cand == 0:
            return cand
    raise ValueError((n, b, unit))


def _cparams(*sem):
    return pltpu.CompilerParams(dimension_semantics=sem, vmem_limit_bytes=VMEM_LIMIT)


def _sigmoid(x):
    return 1.0 / (1.0 + jnp.exp(-x))


def _silu(x):
    return x * _sigmoid(x)


def _softplus(x):
    return jnp.maximum(x, 0.0) + jnp.log(1.0 + jnp.exp(-jnp.abs(x)))


def _dot(a, b, dims=(((1,), (0,)), ((), ()))):
    return lax.dot_general(a.astype(BF16), b.astype(BF16), dims, preferred_element_type=F32)


def _dot_nt(a, b):
    return _dot(a, b, (((1,), (1,)), ((), ())))


def _dot_tn(a, b):
    return _dot(a, b, (((0,), (0,)), ((), ())))


def _split2(x):
    hi = x.astype(BF16)
    lo = (x - hi.astype(F32)).astype(BF16)
    return hi, lo


def _dot3(a, b, dims=(((1,), (0,)), ((), ()))):
    ah, al = _split2(a)
    bh, bl = _split2(b)
    d = functools.partial(lax.dot_general, dimension_numbers=dims, preferred_element_type=F32)
    return d(ah, bh) + (d(ah, bl) + d(al, bh))


def _tri_levels(c, size):
    ii = lax.broadcasted_iota(jnp.int32, (c, c), 0)
    jj = lax.broadcasted_iota(jnp.int32, (c, c), 1)
    levels = []
    s = 1
    while (1 << s) < size:
        lo = (ii >> s) != (jj >> s)
        hi = (ii >> (s + 1)) == (jj >> (s + 1))
        levels.append(lo & hi)
        s += 1
    return (ii >> 1) == (jj >> 1), ii == jj, levels


def _unit_tri_solve_many(ns, rhss, tri):
    pair, eye, levels = tri
    units = range(len(ns))
    ts = [jnp.where(pair, jnp.where(eye, 1.0, n), 0.0) for n in ns]
    for lmask in levels:
        ls = [jnp.where(lmask, n, 0.0) for n in ns]
        lt = [_dot(ls[i], ts[i]) for i in units]
        ts = [ts[i] + _dot(ts[i], lt[i]) for i in units]
    return [_dot(ts[i], rhss[i]) for i in units]


def _cumsum_rows_mxu(x, incl):
    tri = jnp.where(incl, 1.0, 0.0).astype(BF16)
    hi = x.astype(BF16)
    r1 = x - hi.astype(F32)
    mid = r1.astype(BF16)
    lo = (r1 - mid.astype(F32)).astype(BF16)
    d = functools.partial(jnp.dot, preferred_element_type=F32)
    return d(tri, hi) + (d(tri, mid) + d(tri, lo))


def _cumsum_rows(x, rev):
    c = x.shape[0]
    row = lax.broadcasted_iota(jnp.int32, x.shape, 0)
    s = 1
    while s < c:
        if rev:
            x = x + jnp.where(row < c - s, pltpu.roll(x, c - s, 0), 0.0)
        else:
            x = x + jnp.where(row >= s, pltpu.roll(x, s, 0), 0.0)
        s *= 2
    return x


def _epi_none(y):
    return y


def _epi_logw(y):
    return -jnp.exp(-_softplus(-y) - 0.5)


_EPILOGUES = {'none': _epi_none, 'tanh': jnp.tanh, 'sigmoid': _sigmoid, 'logw': _epi_logw}


def _mm_body(*refs, nk, epi, has_bias, has_gate, has_resid, precise):
    it = iter(refs)
    a_ref, w_ref = next(it), next(it)
    bias_ref = next(it) if has_bias else None
    gate_ref = next(it) if has_gate else None
    resid_ref = next(it) if has_resid else None
    o_ref = next(it)
    acc_ref = next(it) if nk > 1 else None

    def finish(y):
        if has_bias:
            y = y + bias_ref[...]
        y = _EPILOGUES[epi](y)
        if has_gate:
            y = y * gate_ref[...]
        if has_resid:
            y = resid_ref[...] + y
        o_ref[...] = y.astype(o_ref.dtype)

    if precise:
        part = _dot3(a_ref[...], w_ref[...])
    else:
        part = _dot(a_ref[...], w_ref[...])
    if nk == 1:
        finish(part)
    else:
        k = pl.program_id(2)

        @pl.when(k == 0)
        def _():
            acc_ref[...] = part

        @pl.when(k > 0)
        def _():
            acc_ref[...] += part

        @pl.when(k == nk - 1)
        def _():
            finish(acc_ref[...])


def matmul(a, w, *, bias=None, gate=None, resid=None, epi='none', out_dtype=F32,
           bm=1024, bn=1024, bk=None, a_koff=0, precise=False, w_single=False, w_cols=None,
           name='matmul'):
    m = a.shape[0]
    k, n = w.shape
    c0, n = (0, n) if w_cols is None else w_cols
    bm, bn = _fit(m, bm, 8), _fit(n, bn)
    assert c0 % bn == 0
    c0 //= bn
    bk = k if bk is None else _fit(k, bk)
    nk = k // bk
    if nk == 1:
        grid = (n // bn, m // bm)
        a_spec = pl.BlockSpec((bm, bk), lambda j, i: (i, a_koff))
        w_mode = dict(pipeline_mode=pl.Buffered(1)) if w_single else {}
        w_spec = pl.BlockSpec((bk, bn), lambda j, i: (0, j + c0), **w_mode)
        row_spec = pl.BlockSpec((1, bn), lambda j, i: (0, j))
        o_spec = pl.BlockSpec((bm, bn), lambda j, i: (i, j))
        sem = ('parallel', 'parallel')
        scratch = []
    else:
        grid = (n // bn, m // bm, nk)
        a_spec = pl.BlockSpec((bm, bk), lambda j, i, kk: (i, kk + a_koff * nk))
        w_spec = pl.BlockSpec((bk, bn), lambda j, i, kk: (kk, j + c0))
        row_spec = pl.BlockSpec((1, bn), lambda j, i, kk: (0, j))
        o_spec = pl.BlockSpec((bm, bn), lambda j, i, kk: (i, j))
        sem = ('parallel', 'parallel', 'arbitrary')
        scratch = [pltpu.VMEM((bm, bn), F32)]
    args, specs = [a, w], [a_spec, w_spec]
    for extra in (bias, gate):
        if extra is not None:
            args.append(extra.reshape(1, n).astype(F32))
            specs.append(row_spec)
    if resid is not None:
        args.append(resid)
        specs.append(o_spec)
    body = functools.partial(_mm_body, nk=nk, epi=epi, has_bias=bias is not None,
                             has_gate=gate is not None, has_resid=resid is not None,
                             precise=precise)
    return pl.pallas_call(
        body, out_shape=jax.ShapeDtypeStruct((m, n), out_dtype), grid=grid,
        in_specs=specs, out_specs=o_spec, scratch_shapes=scratch,
        compiler_params=_cparams(*sem), name=name)(*args)


def _normmod_body(x_ref, g_ref, sc_ref, sh_ref, o_ref):
    x = x_ref[...]
    y = x * lax.rsqrt(jnp.mean(x * x, axis=-1, keepdims=True) + EPS) * g_ref[...]
    o_ref[...] = (y * (1.0 + sc_ref[...]) + sh_ref[...]).astype(o_ref.dtype)


def normmod(x, g, sc, sh, out_dtype, bt=256):
    m, d = x.shape
    bt = min(bt, m)
    row = pl.BlockSpec((1, d), lambda i: (0, 0))
    blk = pl.BlockSpec((bt, d), lambda i: (i, 0))
    return pl.pallas_call(
        _normmod_body, out_shape=jax.ShapeDtypeStruct((m, d), out_dtype), grid=(m // bt,),
        in_specs=[blk, row, row, row], out_specs=blk,
        compiler_params=_cparams('parallel'), name='normmod')(
            x, g.reshape(1, d), sc.reshape(1, d), sh.reshape(1, d))


HALO = 16


def _dnproj_body(a_ref, ap_ref, an_ref, w_ref, cw_ref, o_ref, *, mode):
    i = pl.program_id(1)
    nb = pl.num_programs(1)
    bm = a_ref.shape[0]
    lhs = jnp.concatenate([ap_ref[...], a_ref[...], an_ref[...]], axis=0)
    y = jnp.dot(lhs, w_ref[...], preferred_element_type=F32)
    x = y[HALO:HALO + bm]
    row = lax.broadcasted_iota(jnp.int32, x.shape, 0)
    prev_row = jnp.where(i > 0, y[HALO - 1:HALO], 0.0)
    next_row = jnp.where(i < nb - 1, y[HALO + bm:HALO + bm + 1], 0.0)
    xm = jnp.where(row == 0, prev_row, pltpu.roll(x, 1, 0))
    xp = jnp.where(row == bm - 1, next_row, pltpu.roll(x, bm - 1, 0))
    y = _silu(cw_ref[0:1, :] * xm + cw_ref[1:2, :] * x + cw_ref[2:3, :] * xp)
    if mode == 'v':
        o_ref[...] = y.astype(o_ref.dtype)
        return
    scale = DN_DK ** -0.5 if mode == 'q' else 1.0
    for h in range(x.shape[1] // DN_DK):
        sl = slice(h * DN_DK, (h + 1) * DN_DK)
        yh = y[:, sl]
        inv = lax.rsqrt(jnp.sum(yh * yh, axis=-1, keepdims=True) + EPS)
        o_ref[:, sl] = (yh * (inv * scale)).astype(o_ref.dtype)


def dn_proj_conv(a, w, w_col0, conv_w, part, d, mode, bm=1024, bn=1024):
    m, k = a.shape
    bm, bn = _fit(m, bm, HALO), _fit(d, bn)
    assert w_col0 % bn == 0 and bm % HALO == 0
    c0 = (w_col0 + part * d) // bn
    cw0 = part * d // bn
    rh = bm // HALO
    lasth = m // HALO - 1
    return pl.pallas_call(
        functools.partial(_dnproj_body, mode=mode),
        out_shape=jax.ShapeDtypeStruct((m, d), F32), grid=(d // bn, m // bm),
        in_specs=[pl.BlockSpec((bm, k), lambda j, i: (i, 0)),
                  pl.BlockSpec((HALO, k), lambda j, i: (jnp.maximum(i * rh - 1, 0), 0)),
                  pl.BlockSpec((HALO, k), lambda j, i: (jnp.minimum((i + 1) * rh, lasth), 0)),
                  pl.BlockSpec((k, bn), lambda j, i: (0, c0 + j)),
                  pl.BlockSpec((3, bn), lambda j, i: (0, cw0 + j))],
        out_specs=pl.BlockSpec((bm, bn), lambda j, i: (i, j)),
        compiler_params=_cparams('parallel', 'parallel'), name='dn_proj_' + mode)(
            a, a, a, w, conv_w)


def _dnab_body(x_ref, alog_ref, dtb_ref, o_ref):
    x = x_ref[...]
    lane = lax.broadcasted_iota(jnp.int32, x.shape, 1)
    g = -jnp.exp(alog_ref[...]) * _softplus(x + dtb_ref[...])
    o_ref[...] = jnp.where(lane < x.shape[1] // 2, g, _sigmoid(x))


def dn_gates(pab, a_log, dt_bias, bt=1024):
    m, w = pab.shape
    bt = min(bt, m)
    zeros = jnp.zeros((w // 2,), F32)
    alog = jnp.concatenate([a_log.reshape(-1), zeros]).reshape(1, w)
    dtb = jnp.concatenate([dt_bias.reshape(-1), zeros]).reshape(1, w)
    row = pl.BlockSpec((1, w), lambda i: (0, 0))
    blk = pl.BlockSpec((bt, w), lambda i: (i, 0))
    return pl.pallas_call(
        _dnab_body, out_shape=jax.ShapeDtypeStruct((m, w), F32), grid=(m // bt,),
        in_specs=[blk, row, row], out_specs=blk,
        compiler_params=_cparams('parallel'), name='dn_gates')(pab, alog, dtb)


def _cast_jobs(ws, steps, step_index):
    args, in_specs, out_shapes, out_specs = [], [], [], []
    for stack, layer in ws:
        _, rows, cols = stack.shape
        ratio = next(r for r in (1, 2, 4, 8, 16) if steps % r == 0
                     and rows % (steps // r) == 0 and (rows // (steps // r)) % 16 == 0)
        blk = rows // (steps // ratio)
        args.append(stack)
        in_specs.append(pl.BlockSpec(
            (None, blk, cols),
            lambda g, j, ratio=ratio, layer=layer: (layer, step_index(g, j) // ratio, 0)))
        out_shapes.append(jax.ShapeDtypeStruct((rows, cols), BF16))
        out_specs.append(pl.BlockSpec(
            (blk, cols), lambda g, j, ratio=ratio: (step_index(g, j) // ratio, 0)))
    return args, in_specs, out_shapes, out_specs


def _split_refs(rest, n_fused, n_cast):
    a, b, c, d = n_fused, n_fused + n_cast, n_fused + n_cast + 2, n_fused + 2 * n_cast + 2
    return rest[:a], rest[a:b], rest[b:c], rest[c:d], rest[d:]


def _run_casts(cast_in, cast_out):
    for wi, wo in zip(cast_in, cast_out):
        wo[...] = wi[...].astype(BF16)


def _gated_head_norm(o, z, gain):
    y = o * lax.rsqrt(jnp.mean(o * o, axis=-1, keepdims=True) + EPS) * gain
    return y * _silu(z)


def _tri_masks(c, rev):
    ii = lax.broadcasted_iota(jnp.int32, (c, c), 0)
    jj = lax.broadcasted_iota(jnp.int32, (c, c), 1)
    if rev:
        return ii == jj, ii <= jj, ii < jj
    return ii == jj, ii >= jj, ii > jj


def _dn_intra(q, k, v, g_row, beta_row, masks, tri):
    eye, incl, strict = masks
    c = q[0].shape[0]
    heads = range(len(q))
    gc_col, beta_col, g_tot, decay = [], [], [], []
    for h in heads:
        g_bc = jnp.broadcast_to(g_row[h], (c, c))
        gcc = jnp.sum(jnp.where(incl, g_bc, 0.0), axis=1, keepdims=True)
        gcr = jnp.sum(jnp.where(eye, jnp.broadcast_to(gcc, (c, c)), 0.0), axis=0, keepdims=True)
        gc_col.append(gcc)
        beta_col.append(jnp.sum(jnp.where(eye, jnp.broadcast_to(beta_row[h], (c, c)), 0.0),
                                axis=1, keepdims=True))
        g_tot.append(jnp.sum(g_row[h], axis=1, keepdims=True))
        decay.append(jnp.where(incl, jnp.exp(jnp.where(incl, gcc - gcr, 0.0)), 0.0))
    kb = [k[h] * beta_col[h] for h in heads]
    kq = [_dot_nt(jnp.concatenate([kb[h], q[h]], axis=0), k[h]) for h in heads]
    kk = [kq[h][:c] for h in heads]
    qk = [kq[h][c:] for h in heads]
    n = [-jnp.where(strict, kk[h] * decay[h], 0.0) for h in heads]
    egc = [jnp.exp(gc_col[h]) for h in heads]
    rhs = [jnp.concatenate([kb[h] * egc[h], v[h] * beta_col[h]], axis=1) for h in heads]
    sol = _unit_tri_solve_many(n, rhs, tri)
    wq = [jnp.concatenate([sol[h][:, :DN_DK], q[h] * egc[h]], axis=0) for h in heads]
    u = [sol[h][:, DN_DK:] for h in heads]
    attn = [jnp.where(incl, qk[h] * decay[h], 0.0) for h in heads]
    kd = [k[h] * jnp.exp(g_tot[h] - gc_col[h]) for h in heads]
    gl = [jnp.exp(g_tot[h]) for h in heads]
    return wq, u, attn, kd, gl


def _dn_scan_body(q_ref, k_ref, v_ref, g_ref, b_ref, s0_ref, *rest, rev, nc, hg, fused, n_cast):
    fused_refs, cast_in, (o_ref, sf_ref), cast_out, scratch = _split_refs(
        rest, 3 if fused else 0, n_cast)
    if fused:
        of_ref, z_ref, gain_ref = fused_refs
    s_scr, wq_scr, u_scr, at_scr, kd_scr, gl_scr = scratch
    _run_casts(cast_in, cast_out)
    j = pl.program_id(1)

    @pl.when(j == 0)
    def _():
        s_scr[...] = s0_ref[...]

    c = DN_CHUNK
    masks = _tri_masks(c, rev)
    tri = _tri_levels(c, c)
    heads = range(hg)
    lanes = [slice(h * DN_DK, (h + 1) * DN_DK) for h in heads]

    def intra(ci, carry):
        rows = pl.ds(pl.multiple_of(ci * c, c), c)
        wq, u, attn, kd, gl = _dn_intra(
            [q_ref[rows, sl] for sl in lanes], [k_ref[rows, sl] for sl in lanes],
            [v_ref[rows, sl] for sl in lanes], [g_ref[ci, h:h + 1, :] for h in heads],
            [b_ref[ci, h:h + 1, :] for h in heads], masks, tri)
        for h in heads:
            wq_scr[ci, h] = wq[h].astype(BF16)
            u_scr[ci, h] = u[h]
            at_scr[ci, h] = attn[h].astype(BF16)
            kd_scr[ci, h] = kd[h].astype(BF16)
            gl_scr[ci, h] = jnp.broadcast_to(gl[h], (1, DN_DK))
        return carry

    lax.fori_loop(0, nc, intra, 0)

    def inter(ci, carry):
        cc = (nc - 1 - ci) if rev else ci
        rows = pl.ds(pl.multiple_of(cc * c, c), c)
        s = [s_scr[h] for h in heads]
        ws = [_dot(wq_scr[cc, h], s[h]) for h in heads]
        v_new = [u_scr[cc, h] - ws[h][:c] for h in heads]
        av = [_dot(at_scr[cc, h], v_new[h]) for h in heads]
        kv = [_dot_tn(kd_scr[cc, h], v_new[h]) for h in heads]
        for h in heads:
            o = ws[h][c:] + av[h]
            if fused:
                o = _gated_head_norm(o + of_ref[rows, lanes[h]], z_ref[rows, lanes[h]],
                                     gain_ref[:, lanes[h]])
            o_ref[rows, lanes[h]] = o.astype(o_ref.dtype)
            s_scr[h] = s[h] * gl_scr[cc, h] + kv[h]
        return carry

    lax.fori_loop(0, nc, inter, 0)

    @pl.when(j == pl.num_programs(1) - 1)
    def _():
        sf_ref[...] = s_scr[...]


def dn_scan(q, k, v, gbt, s0, direction, fuse=None, casts=(), hg=16, nc=2):
    m, d = q.shape
    h = d // DN_DK
    hg = min(hg, h)
    ng = h // hg
    nc = min(nc, m // DN_CHUNK)
    bt = nc * DN_CHUNK
    nb = m // bt
    rev = direction == 1
    gb4 = gbt.reshape(4 * ng, hg, m // DN_CHUNK, DN_CHUNK).transpose(0, 2, 1, 3)

    def tb(j):
        return (nb - 1 - j) if rev else j

    tok = pl.BlockSpec((bt, hg * DN_DK), lambda g, j: (tb(j), g))
    st = pl.BlockSpec((hg, DN_DK, DN_DK), lambda g, j: (g, 0, 0))
    g_spec = pl.BlockSpec((None, nc, hg, DN_CHUNK),
                          lambda g, j: (direction * ng + g, tb(j), 0, 0))
    b_spec = pl.BlockSpec((None, nc, hg, DN_CHUNK),
                          lambda g, j: ((2 + direction) * ng + g, tb(j), 0, 0))
    args, specs = [q, k, v, gb4, gb4, s0], [tok, tok, tok, g_spec, b_spec, st]
    if fuse is not None:
        o_other, p, z_part, gain = fuse
        args += [o_other, p, gain.reshape(1, d)]
        specs += [tok, pl.BlockSpec((bt, hg * DN_DK), lambda g, j: (tb(j), z_part * ng + g)),
                  pl.BlockSpec((1, hg * DN_DK), lambda g, j: (0, g))]
    c_args, c_in, c_shapes, c_out = _cast_jobs(casts, ng * nb, lambda g, j: g * nb + j)
    res = pl.pallas_call(
        functools.partial(_dn_scan_body, rev=rev, nc=nc, hg=hg, fused=fuse is not None,
                          n_cast=len(casts)),
        out_shape=(jax.ShapeDtypeStruct((m, d), F32 if fuse is None else BF16),
                   jax.ShapeDtypeStruct(s0.shape, F32), *c_shapes),
        grid=(ng, nb), in_specs=specs + c_in, out_specs=(tok, st, *c_out),
        scratch_shapes=[pltpu.VMEM((hg, DN_DK, DN_DK), F32),
                        pltpu.VMEM((nc, hg, 2 * DN_CHUNK, DN_DK), BF16),
                        pltpu.VMEM((nc, hg, DN_CHUNK, DN_DK), F32),
                        pltpu.VMEM((nc, hg, DN_CHUNK, DN_CHUNK), BF16),
                        pltpu.VMEM((nc, hg, DN_CHUNK, DN_DK), BF16),
                        pltpu.VMEM((nc, hg, 1, DN_DK), F32)],
        compiler_params=_cparams('parallel', 'arbitrary'),
        name='dn_scan_bwd' if rev else 'dn_scan_fwd')(*args, *c_args)
    return res[0], res[1], list(res[2:])


def _hg_chunk(q_raw, f_raw, v, lb, st, masks, rev):
    _, incl, _ = masks
    c = q_raw[0].shape[0]
    heads = range(len(q_raw))
    mid = (c - 1 - c // 2) if rev else c // 2
    last = 0 if rev else c - 1
    qm, km, qs, kd, gl = [], [], [], [], []
    for h in heads:
        q = _silu(q_raw[h])
        f = lb[h] + (1.0 - lb[h]) * _sigmoid(f_raw[h])
        k = 1.0 - f
        b = _cumsum_rows_mxu(jnp.log(f), incl)
        m = b[mid:mid + 1, :]
        b_last = b[last:last + 1, :]
        qm.append(q * jnp.exp(b - m))
        km.append(k * jnp.exp(m - b))
        qs.append(q * jnp.exp(b))
        kd.append(k * jnp.exp(b_last - b))
        gl.append(jnp.exp(b_last))
    a_qk = [jnp.where(incl, _dot_nt(qm[h], km[h]), 0.0) for h in heads]
    inter = [_dot_nt(qs[h], st[h]) for h in heads]
    kv = [_dot_tn(v[h], kd[h]) for h in heads]
    intra = [_dot(a_qk[h], v[h]) for h in heads]
    o = [inter[h] + intra[h] for h in heads]
    st_new = [st[h] * gl[h] + kv[h] for h in heads]
    return o, st_new


def _hg_scan_body(q_ref, f_ref, v_ref, lb_ref, s0_ref, *rest, rev, nc, hg, fused, n_cast):
    fused_refs, cast_in, (o_ref, sf_ref), cast_out, (s_scr,) = _split_refs(
        rest, 3 if fused else 0, n_cast)
    if fused:
        of_ref, z_ref, gain_ref = fused_refs
    _run_casts(cast_in, cast_out)
    j = pl.program_id(1)

    @pl.when(j == 0)
    def _():
        s_scr[...] = s0_ref[...]

    c = HG_CHUNK
    masks = _tri_masks(c, rev)
    heads = range(hg)
    lanes = [slice(h * HG_DK, (h + 1) * HG_DK) for h in heads]

    def chunk(ci, carry):
        cc = (nc - 1 - ci) if rev else ci
        rows = pl.ds(pl.multiple_of(cc * c, c), c)
        o, s_new = _hg_chunk([q_ref[rows, sl] for sl in lanes], [f_ref[rows, sl] for sl in lanes],
                             [v_ref[rows, sl] for sl in lanes], [lb_ref[:, sl] for sl in lanes],
                             [s_scr[h] for h in heads], masks, rev)
        for h in heads:
            oh = o[h]
            if fused:
                oh = _gated_head_norm(oh + of_ref[rows, lanes[h]], z_ref[rows, lanes[h]],
                                      gain_ref[:, lanes[h]])
            o_ref[rows, lanes[h]] = oh.astype(o_ref.dtype)
            s_scr[h] = s_new[h]
        return carry

    lax.fori_loop(0, nc, chunk, 0)

    @pl.when(j == pl.num_programs(1) - 1)
    def _():
        sf_ref[...] = s_scr[...]


def hg_scan(p, lower, s0, direction, d, fuse=None, casts=(), hg=16, nc=8):
    m = p.shape[0]
    h = d // HG_DK
    hg = min(hg, h)
    ng = h // hg
    nc = min(nc, m // HG_CHUNK)
    bt = nc * HG_CHUNK
    nb = m // bt
    rev = direction == 1

    def tb(j):
        return (nb - 1 - j) if rev else j

    def part(pi):
        return pl.BlockSpec((bt, hg * HG_DK), lambda g, j: (tb(j), pi * ng + g))

    st = pl.BlockSpec((hg, HG_DK, HG_DK), lambda g, j: (g, 0, 0))
    row = pl.BlockSpec((1, hg * HG_DK), lambda g, j: (0, g))
    tok = pl.BlockSpec((bt, hg * HG_DK), lambda g, j: (tb(j), g))
    args = [p, p, p, lower.reshape(1, d), s0]
    specs = [part(0), part(1 + direction), part(3), row, st]
    if fuse is not None:
        o_other, gain = fuse
        args += [o_other, p, gain.reshape(1, d)]
        specs += [tok, part(4), row]
    c_args, c_in, c_shapes, c_out = _cast_jobs(casts, ng * nb, lambda g, j: g * nb + j)
    res = pl.pallas_call(
        functools.partial(_hg_scan_body, rev=rev, nc=nc, hg=hg, fused=fuse is not None,
                          n_cast=len(casts)),
        out_shape=(jax.ShapeDtypeStruct((m, d), F32 if fuse is None else BF16),
                   jax.ShapeDtypeStruct(s0.shape, F32), *c_shapes),
        grid=(ng, nb), in_specs=specs + c_in, out_specs=(tok, st, *c_out),
        scratch_shapes=[pltpu.VMEM((hg, HG_DK, HG_DK), F32)],
        compiler_params=_cparams('parallel', 'arbitrary'),
        name='hg_scan_bwd' if rev else 'hg_scan_fwd')(*args, *c_args)
    return res[0], res[1], list(res[2:])


def _shiftmix_body(x_ref, prev_ref, next_ref, mu_ref, *o_refs, width):
    i = pl.program_id(0)
    nb = pl.num_programs(0)
    bt, d = x_ref.shape
    dq = d // 4
    col = lax.broadcasted_iota(jnp.int32, (bt, dq), 0) % width
    for qi in range(4):
        sl = slice(qi * dq, (qi + 1) * dq)
        x = x_ref[:, sl]
        if qi == 0:
            sh = jnp.where(col == 0, 0.0, pltpu.roll(x, 1, 0))
        elif qi == 1:
            sh = jnp.where(col == width - 1, 0.0, pltpu.roll(x, bt - 1, 0))
        elif qi == 2:
            edge = jnp.where(i > 0, prev_ref[:, sl], 0.0)
            sh = edge if bt == width else jnp.concatenate([edge, x[:bt - width]], axis=0)
        else:
            edge = jnp.where(i < nb - 1, next_ref[:, sl], 0.0)
            sh = edge if bt == width else jnp.concatenate([x[width:], edge], axis=0)
        xx = sh - x
        for n, o_ref in enumerate(o_refs):
            o_ref[:, sl] = (x + xx * mu_ref[n:n + 1, sl]).astype(o_ref.dtype)


def shift_mix(h, mu, width, bt=128):
    m, d = h.shape
    bt = max(min(bt, m), width)
    rw = bt // width
    lastw = m // width - 1
    blk = pl.BlockSpec((bt, d), lambda i: (i, 0))
    out = jax.ShapeDtypeStruct((m, d), BF16)
    return pl.pallas_call(
        functools.partial(_shiftmix_body, width=width),
        out_shape=(out,) * 6, grid=(m // bt,),
        in_specs=[blk,
                  pl.BlockSpec((width, d), lambda i: (jnp.maximum(i * rw - 1, 0), 0)),
                  pl.BlockSpec((width, d), lambda i: (jnp.minimum((i + 1) * rw, lastw), 0)),
                  pl.BlockSpec((6, d), lambda i: (0, 0))],
        out_specs=(blk,) * 6, compiler_params=_cparams('arbitrary'), name='shift_mix')(h, h, h, mu)


def _halves(x, lo_mask):
    return jnp.concatenate([jnp.where(lo_mask, x, 0.0), jnp.where(lo_mask, 0.0, x)], axis=0)


def _head_sum(x, lo_mask):
    s_lo = jnp.sum(jnp.where(lo_mask, x, 0.0), axis=-1, keepdims=True)
    s_hi = jnp.sum(jnp.where(lo_mask, 0.0, x), axis=-1, keepdims=True)
    return jnp.where(lo_mask, s_lo, s_hi)


def _rw_intra(r, k, v, a_sig, logw, k_k, k_a, masks, tri, rev):
    lo, strict2, incl_cat = masks
    c = r[0].shape[0]
    groups = range(len(r))
    mid = (c - 1 - c // 2) if rev else c // 2
    last = 0 if rev else c - 1
    am2, bm2, km2, v2, rm, ae2, rg, bd2, kd2, gl = ([] for _ in range(10))
    for g in groups:
        kq = k[g] * k_k[g]
        kk = kq * lax.rsqrt(_head_sum(kq * kq, lo) + EPS)
        b = kk * a_sig[g]
        kd = k[g] * (1.0 + (a_sig[g] - 1.0) * k_a[g])
        cs = _cumsum_rows(logw[g], rev)
        ce = cs - logw[g]
        m = cs[mid:mid + 1, :]
        c_last = cs[last:last + 1, :]
        e_mc = jnp.exp(m - cs)
        dec = jnp.exp(c_last - cs)
        am2.append(_halves(-kk * jnp.exp(ce - m), lo))
        bm2.append(_halves(b * e_mc, lo))
        km2.append(_halves(kd * e_mc, lo))
        v2.append(_halves(v[g], lo))
        rm.append(r[g] * jnp.exp(cs - m))
        ae2.append(_halves(-kk * jnp.exp(ce), lo))
        rg.append(r[g] * jnp.exp(cs))
        bd2.append(_halves(b * dec, lo))
        kd2.append(_halves(kd * dec, lo))
        gl.append(jnp.exp(c_last))
    pairs = [_dot_nt(jnp.concatenate([am2[g], rm[g]], axis=0),
                     jnp.concatenate([bm2[g], km2[g]], axis=0)) for g in groups]
    n = [jnp.where(strict2, pairs[g][:2 * c, :2 * c], 0.0) for g in groups]
    a_ak = [jnp.where(strict2, pairs[g][:2 * c, 2 * c:], 0.0) for g in groups]
    a_rb = [jnp.where(incl_cat, pairs[g][2 * c:, :2 * c], 0.0) for g in groups]
    a_rk = [jnp.where(incl_cat, pairs[g][2 * c:, 2 * c:], 0.0) for g in groups]
    both = [_dot(jnp.concatenate([a_ak[g], a_rk[g]], axis=0), v2[g]) for g in groups]
    akv = [both[g][:2 * c] for g in groups]
    ov = [both[g][2 * c:] for g in groups]
    kv = [_dot_tn(v2[g], kd2[g]) for g in groups]
    sol = _unit_tri_solve_many(n, [jnp.concatenate([ae2[g], akv[g]], axis=1) for g in groups],
                               tri)
    wr = [jnp.concatenate([sol[g][:, :LANES], rg[g]], axis=0) for g in groups]
    uv2 = [sol[g][:, LANES:] for g in groups]
    return wr, uv2, a_rb, ov, bd2, kv, gl


def _rw_output(o, r, k, v, a_sum, gate, ln_w, ln_b, k_a, r_k, lo):
    inv_n = 1.0 / RW_HEAD
    oc = o - _head_sum(o, lo) * inv_n
    y = oc * lax.rsqrt(_head_sum(oc * oc, lo) * inv_n + RW_LN_EPS) * ln_w + ln_b
    k_sum = k * (2.0 + (a_sum - 2.0) * k_a)
    bonus = _head_sum(r * k_sum * r_k, lo) * v
    return (y + bonus) * gate


def _rw_masks(c, rev):
    lo = lax.broadcasted_iota(jnp.int32, (1, LANES), 1) < RW_HEAD
    ii = lax.broadcasted_iota(jnp.int32, (2 * c, 2 * c), 0)
    jj = lax.broadcasted_iota(jnp.int32, (2 * c, 2 * c), 1)
    same = (ii // c) == (jj // c)
    il, jl = ii % c, jj % c
    strict2 = same & ((il < jl) if rev else (il > jl))
    ic = lax.broadcasted_iota(jnp.int32, (c, 2 * c), 0)
    jc = lax.broadcasted_iota(jnp.int32, (c, 2 * c), 1) % c
    incl_cat = (ic <= jc) if rev else (ic >= jc)
    return lo, strict2, incl_cat


def _rw_scan_body(r_ref, k_ref, v_ref, a_ref, lw_ref, kk_ref, ka_ref, s0_ref, *rest,
                  rev, nc, hg, fused, n_cast):
    fused_refs, cast_in, (o_ref, sf_ref), cast_out, scratch = _split_refs(
        rest, 6 if fused else 0, n_cast)
    if fused:
        of_ref, ao_ref, gate_ref, lnw_ref, lnb_ref, rk_ref = fused_refs
    s_scr, wr_scr, uv_scr, arb_scr, ov_scr, bd_scr, kv_scr, gl_scr = scratch
    _run_casts(cast_in, cast_out)
    j = pl.program_id(1)

    @pl.when(j == 0)
    def _():
        s_scr[...] = s0_ref[...]

    c = RW_CHUNK
    masks = _rw_masks(c, rev)
    tri = _tri_levels(2 * c, c)
    groups = range(hg)
    lanes = [slice(g * LANES, (g + 1) * LANES) for g in groups]

    def intra(ci, carry):
        rows = pl.ds(pl.multiple_of(ci * c, c), c)
        wr, uv2, a_rb, ov, bd2, kv, gl = _rw_intra(
            [r_ref[rows, sl] for sl in lanes], [k_ref[rows, sl] for sl in lanes],
            [v_ref[rows, sl] for sl in lanes], [a_ref[rows, sl] for sl in lanes],
            [lw_ref[rows, sl] for sl in lanes], [kk_ref[:, sl] for sl in lanes],
            [ka_ref[:, sl] for sl in lanes], masks, tri, rev)
        for g in groups:
            wr_scr[ci, g] = wr[g].astype(BF16)
            uv_scr[ci, g] = uv2[g]
            arb_scr[ci, g] = a_rb[g].astype(BF16)
            ov_scr[ci, g] = ov[g]
            bd_scr[ci, g] = bd2[g].astype(BF16)
            kv_scr[ci, g] = kv[g]
            gl_scr[ci, g] = gl[g]
        return carry

    lax.fori_loop(0, nc, intra, 0)

    def inter(ci, carry):
        cc = (nc - 1 - ci) if rev else ci
        rows = pl.ds(pl.multiple_of(cc * c, c), c)
        st = [s_scr[g] for g in groups]
        ws = [_dot_nt(wr_scr[cc, g], st[g]) for g in groups]
        u2 = [ws[g][:2 * c] + uv_scr[cc, g] for g in groups]
        au = [_dot(arb_scr[cc, g], u2[g]) for g in groups]
        ub = [_dot_tn(u2[g], bd_scr[cc, g]) for g in groups]
        for g in groups:
            o = ws[g][2 * c:] + au[g] + ov_scr[cc, g]
            if fused:
                sl = lanes[g]
                o = _rw_output(o + of_ref[rows, sl], r_ref[rows, sl], k_ref[rows, sl],
                               v_ref[rows, sl], a_ref[rows, sl] + ao_ref[rows, sl],
                               gate_ref[rows, sl], lnw_ref[:, sl], lnb_ref[:, sl], ka_ref[:, sl],
                               rk_ref[:, sl], masks[0])
            o_ref[rows, lanes[g]] = o.astype(o_ref.dtype)
            s_scr[g] = st[g] * gl_scr[cc, g] + kv_scr[cc, g] + ub[g]
        return carry

    lax.fori_loop(0, nc, inter, 0)

    @pl.when(j == pl.num_programs(1) - 1)
    def _():
        sf_ref[...] = s_scr[...]


def rw_scan(r, k, v, a_sig, logw, k_k, k_a, s0, direction, fuse=None, casts=(), hg=16, nc=2):
    m, d = r.shape
    npair = d // LANES
    hg = min(hg, npair)
    ng = npair // hg
    nc = min(nc, m // RW_CHUNK)
    bt = nc * RW_CHUNK
    nb = m // bt
    rev = direction == 1

    def tb(j):
        return (nb - 1 - j) if rev else j

    tok = pl.BlockSpec((bt, hg * LANES), lambda g, j: (tb(j), g))
    row = pl.BlockSpec((1, hg * LANES), lambda g, j: (0, g))
    st = pl.BlockSpec((hg, LANES, LANES), lambda g, j: (g, 0, 0))
    args = [r, k, v, a_sig, logw, k_k.reshape(1, d), k_a.reshape(1, d), s0]
    specs = [tok, tok, tok, tok, tok, row, row, st]
    if fuse is not None:
        args += list(fuse[:3]) + [t.reshape(1, d) for t in fuse[3:]]
        specs += [tok, tok, tok, row, row, row]
    c_args, c_in, c_shapes, c_out = _cast_jobs(casts, ng * nb, lambda g, j: g * nb + j)
    res = pl.pallas_call(
        functools.partial(_rw_scan_body, rev=rev, nc=nc, hg=hg, fused=fuse is not None,
                          n_cast=len(casts)),
        out_shape=(jax.ShapeDtypeStruct((m, d), F32 if fuse is None else BF16),
                   jax.ShapeDtypeStruct(s0.shape, F32), *c_shapes),
        grid=(ng, nb), in_specs=specs + c_in, out_specs=(tok, st, *c_out),
        scratch_shapes=[pltpu.VMEM((hg, LANES, LANES), F32),
                        pltpu.VMEM((nc, hg, 3 * RW_CHUNK, LANES), BF16),
                        pltpu.VMEM((nc, hg, 2 * RW_CHUNK, LANES), F32),
                        pltpu.VMEM((nc, hg, RW_CHUNK, 2 * RW_CHUNK), BF16),
                        pltpu.VMEM((nc, hg, RW_CHUNK, LANES), F32),
                        pltpu.VMEM((nc, hg, 2 * RW_CHUNK, LANES), BF16),
                        pltpu.VMEM((nc, hg, LANES, LANES), F32),
                        pltpu.VMEM((nc, hg, 1, LANES), F32)],
        compiler_params=_cparams('parallel', 'arbitrary'),
        name='rw_scan_bwd' if rev else 'rw_scan_fwd')(*args, *c_args)
    return res[0], res[1], list(res[2:])


def _convglu_body(u_ref, prev_ref, next_ref, v_ref, cw_ref, cb_ref, o_ref, *, width):
    i = pl.program_id(0)
    nb = pl.num_programs(0)
    x = u_ref[...].astype(F32)
    bt = x.shape[0]
    up = jnp.where(i > 0, prev_ref[...].astype(F32), 0.0)
    dn = jnp.where(i < nb - 1, next_ref[...].astype(F32), 0.0)
    if bt > width:
        up = jnp.concatenate([up, x[:bt - width]], axis=0)
        dn = jnp.concatenate([x[width:], dn], axis=0)
    col = lax.broadcasted_iota(jnp.int32, x.shape, 0) % width
    rows = (up, x, dn)

    def column(kw):
        return sum(cw_ref[3 * r + kw:3 * r + kw + 1, :] * rows[r] for r in range(3))

    left = jnp.where(col == 0, 0.0, pltpu.roll(column(0), 1, 0))
    right = jnp.where(col == width - 1, 0.0, pltpu.roll(column(2), bt - 1, 0))
    h = column(1) + left + right + cb_ref[...]
    c = math.sqrt(2.0 / math.pi)
    t = jnp.tanh(h * (2.0 * c + (8.0 * 0.044715 * c) * (h * h)))
    o_ref[...] = ((h + h * t) * v_ref[...].astype(F32)).astype(o_ref.dtype)


def conv_glu_gate(uv, conv_w, conv_b, width, f, bt=512, bc=512):
    m = uv.shape[0]
    bt = max(min(bt, m), width)
    bc = _fit(f, bc)
    rw = bt // width
    lastw = m // width - 1
    voff = f // bc
    return pl.pallas_call(
        functools.partial(_convglu_body, width=width),
        out_shape=jax.ShapeDtypeStruct((m, f), BF16), grid=(m // bt, f // bc),
        in_specs=[pl.BlockSpec((bt, bc), lambda i, j: (i, j)),
                  pl.BlockSpec((width, bc), lambda i, j: (jnp.maximum(i * rw - 1, 0), j)),
                  pl.BlockSpec((width, bc), lambda i, j: (jnp.minimum((i + 1) * rw, lastw), j)),
                  pl.BlockSpec((bt, bc), lambda i, j: (i, voff + j)),
                  pl.BlockSpec((9, bc), lambda i, j: (0, j)),
                  pl.BlockSpec((1, bc), lambda i, j: (0, j))],
        out_specs=pl.BlockSpec((bt, bc), lambda i, j: (i, j)),
        compiler_params=_cparams('parallel', 'parallel'), name='conv_glu')(
            uv, uv, uv, uv, 0.5 * conv_w.reshape(9, f), 0.5 * conv_b.reshape(1, f))


def _modulation(conds, down, up, bias):
    low = matmul(conds, down, precise=True, bm=8, bn=512, name='ada_down')
    return matmul(low, up, bias=bias, precise=True, bm=8, bn=2048, name='ada_up')


def _dn_mixer(streams, w, d, last, casts):
    heads = d // DN_DK
    w_in = w['w_in']
    if w_in.dtype == BF16:
        w_main = w_in
        ab = dict(w=w_in, w_cols=(4 * d, 4 * heads))
    else:
        w_main = w_in[:, :4 * d].astype(BF16)
        ab = dict(w=w_in[:, 4 * d:].astype(BF16))
    gain = jnp.tile(w['norm_g'], heads)
    prepped = []
    for h in streams:
        q, k, v = (dn_proj_conv(h, w_main, 0, w['conv_w'], part, d, mode)
                   for part, mode in enumerate('qkv'))
        z = matmul(h, w_main, w_cols=(3 * d, d), name='dn_in_z')
        pab = matmul(h, bn=4 * heads, name='dn_in_ab', **ab)
        gb = dn_gates(pab, w['a_log'], w['dt_bias'])
        prepped.append((z, q, k, v, gb.T))
    states = [jnp.zeros((heads, DN_DK, DN_DK), F32)] * 2
    outs, done = [], [[], []]
    for si, (z, q, k, v, gbt) in enumerate(prepped):
        jobs = casts if si == 1 else ((), ())
        o_f, states[0], done[0] = dn_scan(q, k, v, gbt, states[0], 0, casts=jobs[0])
        y, states[1], done[1] = dn_scan(q, k, v, gbt, states[1], 1, fuse=(o_f, z, 0, gain),
                                        casts=jobs[1])
        outs.append(None if si == 0 and last else y)
    return outs, done[0], done[1]


def _hg_mixer(streams, w, d, last, casts):
    heads = d // HG_DK
    states = [jnp.zeros((heads, HG_DK, HG_DK), F32)] * 2
    outs, done = [], [[], []]
    for si, h in enumerate(streams):
        jobs = casts if si == 1 else ((), ())
        p = matmul(h, w['w_in'], name='hg_in')
        o_f, states[0], done[0] = hg_scan(p, w['lower'], states[0], 0, d, casts=jobs[0])
        y, states[1], done[1] = hg_scan(p, w['lower'], states[1], 1, d,
                                        fuse=(o_f, w['norm_g']), casts=jobs[1])
        outs.append(None if si == 0 and last else y)
    return outs, done[0], done[1]


def _rw_mixer(streams, widths, w, d, last, casts):
    w_rkv = w['w_rkv']
    w1 = jnp.concatenate([w['w1'][0], w['w1'][1]], axis=1).astype(BF16)
    a1 = jnp.concatenate([w['a1'][0], w['a1'][1]], axis=1).astype(BF16)
    w2 = w['w2'].astype(BF16)
    a2 = w['a2'].astype(BF16)
    lora = w['w1'].shape[-1]
    gl = w['g1'].shape[-1]
    glp = -(-gl // LANES) * LANES
    g1 = jnp.pad(w['g1'], ((0, 0), (0, glp - gl))).astype(BF16)
    g2 = jnp.pad(w['g2'], ((0, glp - gl), (0, 0))).astype(BF16)
    states = [jnp.zeros((d // LANES, LANES, LANES), F32)] * 2
    outs, done = [], [[], []]
    for si, (h, width) in enumerate(zip(streams, widths)):
        jobs = casts if si == 1 else ((), ())
        xr, xw, xk, xv, xa, xg = shift_mix(h, w['mu'], width)
        r = matmul(xr, w_rkv[0], name='rw_r')
        k = matmul(xk, w_rkv[1], name='rw_k')
        v = matmul(xv, w_rkv[2], name='rw_v')
        tw = matmul(xw, w1, epi='tanh', out_dtype=BF16, bn=2 * lora, name='rw_w1')
        ta = matmul(xa, a1, out_dtype=BF16, bn=2 * lora, name='rw_a1')
        tg = matmul(xg, g1, epi='sigmoid', out_dtype=BF16, name='rw_g1')
        gate = matmul(tg, g2, name='rw_g2')
        logw, a_sig = [], []
        for direction in range(2):
            logw.append(matmul(tw, w2[direction], bias=w['w0'][direction], epi='logw',
                               a_koff=direction, name='rw_w2'))
            a_sig.append(matmul(ta, a2[direction], bias=w['a0'][direction], epi='sigmoid',
                                a_koff=direction, name='rw_a2'))
        o_f, states[0], done[0] = rw_scan(r, k, v, a_sig[0], logw[0], w['k_k'], w['k_a'],
                                          states[0], 0, casts=jobs[0])
        y, states[1], done[1] = rw_scan(r, k, v, a_sig[1], logw[1], w['k_k'], w['k_a'],
                                        states[1], 1, casts=jobs[1],
                                        fuse=(o_f, a_sig[0], gate, w['lnx_w'], w['lnx_b'],
                                              w['r_k'].reshape(-1)))
        outs.append(None if si == 0 and last else y)
    return outs, done[0], done[1]


def kernel(x, c, ctx, c_ctx, ada_down, ada_up, ada_b, norm1_g, norm2_g, ffn_w_up, ffn_conv_w, ffn_conv_b, ffn_w_down, dn_w_in, dn_conv_w, dn_a_log, dn_dt_bias, dn_norm_g, dn_w_out, hg_w_in, hg_lower, hg_norm_g, hg_w_out, rw_mu, rw_w_rkv, rw_w0, rw_w1, rw_w2, rw_a0, rw_a1, rw_a2, rw_g1, rw_g2, rw_k_k, rw_k_a, rw_r_k, rw_lnx_w, rw_lnx_b, rw_w_out, final_g):
    _, seq, d = x.shape
    depth = ada_down.shape[0]
    n_ctx = ctx.shape[1]
    f = ffn_w_down.shape[1]
    xl, xc = x[0], ctx[0]
    sm = jax.nn.softmax(hg_lower.astype(F32), axis=0)
    lower_bounds = jnp.cumsum(sm, axis=0) - sm[0]
    conds = jnp.zeros((8, d), F32).at[0].set(jax.nn.silu(c[0])).at[1].set(jax.nn.silu(c_ctx))
    widths = (n_ctx, GRID_W)
    w_outs = {0: dn_w_out, 1: hg_w_out, 2: rw_w_out}
    w_ins = {0: dn_w_in, 1: hg_w_in, 2: rw_w_rkv.reshape(rw_w_rkv.shape[0], 3 * d, d)}

    def in_proj(i):
        return w_ins[i % 3], i // 3

    first, _ = in_proj(0)
    w_in_bf = first[0] if first.shape[2] % LANES else first[0].astype(BF16)
    for i in range(depth):
        kind, j = i % 3, i // 3
        last = i == depth - 1
        mod = _modulation(conds, ada_down[i], ada_up[i], ada_b[i])
        mods = [[mod[row, n * d:(n + 1) * d] for n in range(6)] for row in (1, 0)]
        xs = [xc, xl]
        h_dtype = F32 if kind == 2 else BF16
        hs = [normmod(t, norm1_g[i], mm[1], mm[0], h_dtype) for t, mm in zip(xs, mods)]
        nxt = in_proj(i + 1) if not last else None
        hide_next = nxt is not None and nxt[0].shape[2] % LANES == 0
        casts = ([(w_outs[kind], j), (ffn_w_up, i)],
                 [(ffn_w_down, i)] + ([nxt] if hide_next else []))
        if kind == 0:
            w = dict(w_in=w_in_bf, conv_w=dn_conv_w[j], a_log=dn_a_log[j],
                     dt_bias=dn_dt_bias[j], norm_g=dn_norm_g[j])
            ys, done_f, done_b = _dn_mixer(hs, w, d, last, casts)
        elif kind == 1:
            w = dict(w_in=w_in_bf, lower=lower_bounds[i], norm_g=hg_norm_g[j])
            ys, done_f, done_b = _hg_mixer(hs, w, d, last, casts)
        else:
            w = dict(mu=rw_mu[j], w_rkv=w_in_bf.reshape(3, d, d), w0=rw_w0[j], w1=rw_w1[j],
                     w2=rw_w2[j], a0=rw_a0[j], a1=rw_a1[j], a2=rw_a2[j], g1=rw_g1[j],
                     g2=rw_g2[j], k_k=rw_k_k[j], k_a=rw_k_a[j], r_k=rw_r_k[j],
                     lnx_w=rw_lnx_w[j], lnx_b=rw_lnx_b[j])
            ys, done_f, done_b = _rw_mixer(hs, widths, w, d, last, casts)
        w_out, w_up = done_f
        w_down = done_b[0]
        if hide_next:
            w_in_bf = done_b[1]
        elif nxt is not None:
            w_in_bf = nxt[0][nxt[1]]
        new = []
        for t, y, mm, width in zip(xs, ys, mods, widths):
            if y is None:
                new.append(t)
                continue
            t = matmul(y, w_out, gate=mm[2], resid=t, name='mix_out')
            h2 = normmod(t, norm2_g[i], mm[4], mm[3], BF16)
            uv = matmul(h2, w_up, out_dtype=BF16, name='ffn_up')
            gated = conv_glu_gate(uv, ffn_conv_w[i], ffn_conv_b[i], width, f)
            t = matmul(gated, w_down, gate=mm[5], resid=t, bm=512, bn=1024, w_single=True,
                       name='ffn_down')
            new.append(t)
        xc, xl = new
    zeros = jnp.zeros((d,), F32)
    return normmod(xl, final_g, zeros, zeros, F32)[None]
```

```python
import functools
import math

import jax
import jax.numpy as jnp
from jax import lax
from jax.experimental import pallas as pl
from jax.experimental.pallas import tpu as pltpu

F32 = jnp.float32
BF16 = jnp.bfloat16

EPS = 1e-6
RW_LN_EPS = 64e-5
GRID_W = 64
DN_DK = 128
DN_CHUNK = 128
HG_DK = 128
HG_CHUNK = 32
RW_HEAD = 64
RW_CHUNK = 64
LANES = 128
VMEM_LIMIT = 60 * 1024 * 1024


def _fit(n, b, unit=LANES):
    if n <= b:
        return n
    for cand in range(b - b % unit, 0, -unit):
        if n % cand == 0:
            return cand
    raise ValueError((n, b, unit))


def _cparams(*sem):
    return pltpu.CompilerParams(dimension_semantics=sem, vmem_limit_bytes=VMEM_LIMIT)


def _sigmoid(x):
    return 1.0 / (1.0 + jnp.exp(-x))


def _silu(x):
    return x * _sigmoid(x)


def _softplus(x):
    return jnp.maximum(x, 0.0) + jnp.log(1.0 + jnp.exp(-jnp.abs(x)))


def _dot(a, b, dims=(((1,), (0,)), ((), ()))):
    return lax.dot_general(a.astype(BF16), b.astype(BF16), dims, preferred_element_type=F32)


def _dot_nt(a, b):
    return _dot(a, b, (((1,), (1,)), ((), ())))


def _dot_tn(a, b):
    return _dot(a, b, (((0,), (0,)), ((), ())))


def _split2(x):
    hi = x.astype(BF16)
    lo = (x - hi.astype(F32)).astype(BF16)
    return hi, lo


def _dot3(a, b, dims=(((1,), (0,)), ((), ()))):
    ah, al = _split2(a)
    bh, bl = _split2(b)
    d = functools.partial(lax.dot_general, dimension_numbers=dims, preferred_element_type=F32)
    return d(ah, bh) + (d(ah, bl) + d(al, bh))


def _tri_levels(c, size):
    ii = lax.broadcasted_iota(jnp.int32, (c, c), 0)
    jj = lax.broadcasted_iota(jnp.int32, (c, c), 1)
    levels = []
    s = 1
    while (1 << s) < size:
        lo = (ii >> s) != (jj >> s)
        hi = (ii >> (s + 1)) == (jj >> (s + 1))
        levels.append(lo & hi)
        s += 1
    return (ii >> 1) == (jj >> 1), ii == jj, levels


def _unit_tri_solve_many(ns, rhss, tri):
    pair, eye, levels = tri
    units = range(len(ns))
    ts = [jnp.where(pair, jnp.where(eye, 1.0, n), 0.0) for n in ns]
    for lmask in levels:
        ls = [jnp.where(lmask, n, 0.0) for n in ns]
        lt = [_dot(ls[i], ts[i]) for i in units]
        ts = [ts[i] + _dot(ts[i], lt[i]) for i in units]
    return [_dot(ts[i], rhss[i]) for i in units]


def _cumsum_rows(x, rev):
    c = x.shape[0]
    row = lax.broadcasted_iota(jnp.int32, x.shape, 0)
    s = 1
    while s < c:
        if rev:
            x = x + jnp.where(row < c - s, pltpu.roll(x, c - s, 0), 0.0)
        else:
            x = x + jnp.where(row >= s, pltpu.roll(x, s, 0), 0.0)
        s *= 2
    return x


def _epi_none(y):
    return y


def _epi_logw(y):
    return -jnp.exp(-_softplus(-y) - 0.5)


_EPILOGUES = {'none': _epi_none, 'tanh': jnp.tanh, 'sigmoid': _sigmoid, 'logw': _epi_logw}


def _mm_body(*refs, nk, epi, has_bias, has_gate, has_resid, precise):
    it = iter(refs)
    a_ref, w_ref = next(it), next(it)
    bias_ref = next(it) if has_bias else None
    gate_ref = next(it) if has_gate else None
    resid_ref = next(it) if has_resid else None
    o_ref = next(it)
    acc_ref = next(it) if nk > 1 else None

    def finish(y):
        if has_bias:
            y = y + bias_ref[...]
        y = _EPILOGUES[epi](y)
        if has_gate:
            y = y * gate_ref[...]
        if has_resid:
            y = resid_ref[...] + y
        o_ref[...] = y.astype(o_ref.dtype)

    if precise:
        part = _dot3(a_ref[...], w_ref[...])
    else:
        part = _dot(a_ref[...], w_ref[...])
    if nk == 1:
        finish(part)
    else:
        k = pl.program_id(2)

        @pl.when(k == 0)
        def _():
            acc_ref[...] = part

        @pl.when(k > 0)
        def _():
            acc_ref[...] += part

        @pl.when(k == nk - 1)
        def _():
            finish(acc_ref[...])


def matmul(a, w, *, bias=None, gate=None, resid=None, epi='none', out_dtype=F32,
           bm=1024, bn=1024, bk=None, a_koff=0, precise=False, w_single=False, w_cols=None,
           name='matmul'):
    m = a.shape[0]
    k, n = w.shape
    c0, n = (0, n) if w_cols is None else w_cols
    bm, bn = _fit(m, bm, 8), _fit(n, bn)
    assert c0 % bn == 0
    c0 //= bn
    bk = k if bk is None else _fit(k, bk)
    nk = k // bk
    if nk == 1:
        grid = (n // bn, m // bm)
        a_spec = pl.BlockSpec((bm, bk), lambda j, i: (i, a_koff))
        w_mode = dict(pipeline_mode=pl.Buffered(1)) if w_single else {}
        w_spec = pl.BlockSpec((bk, bn), lambda j, i: (0, j + c0), **w_mode)
        row_spec = pl.BlockSpec((1, bn), lambda j, i: (0, j))
        o_spec = pl.BlockSpec((bm, bn), lambda j, i: (i, j))
        sem = ('parallel', 'parallel')
        scratch = []
    else:
        grid = (n // bn, m // bm, nk)
        a_spec = pl.BlockSpec((bm, bk), lambda j, i, kk: (i, kk + a_koff * nk))
        w_spec = pl.BlockSpec((bk, bn), lambda j, i, kk: (kk, j + c0))
        row_spec = pl.BlockSpec((1, bn), lambda j, i, kk: (0, j))
        o_spec = pl.BlockSpec((bm, bn), lambda j, i, kk: (i, j))
        sem = ('parallel', 'parallel', 'arbitrary')
        scratch = [pltpu.VMEM((bm, bn), F32)]
    args, specs = [a, w], [a_spec, w_spec]
    for extra in (bias, gate):
        if extra is not None:
            args.append(extra.reshape(1, n).astype(F32))
            specs.append(row_spec)
    if resid is not None:
        args.append(resid)
        specs.append(o_spec)
    body = functools.partial(_mm_body, nk=nk, epi=epi, has_bias=bias is not None,
                             has_gate=gate is not None, has_resid=resid is not None,
                             precise=precise)
    return pl.pallas_call(
        body, out_shape=jax.ShapeDtypeStruct((m, n), out_dtype), grid=grid,
        in_specs=specs, out_specs=o_spec, scratch_shapes=scratch,
        compiler_params=_cparams(*sem), name=name)(*args)


def _normmod_body(x_ref, g_ref, sc_ref, sh_ref, o_ref):
    x = x_ref[...]
    y = x * lax.rsqrt(jnp.mean(x * x, axis=-1, keepdims=True) + EPS) * g_ref[...]
    o_ref[...] = (y * (1.0 + sc_ref[...]) + sh_ref[...]).astype(o_ref.dtype)


def normmod(x, g, sc, sh, out_dtype, bt=512):
    m, d = x.shape
    bt = min(bt, m)
    row = pl.BlockSpec((1, d), lambda i: (0, 0))
    blk = pl.BlockSpec((bt, d), lambda i: (i, 0))
    return pl.pallas_call(
        _normmod_body, out_shape=jax.ShapeDtypeStruct((m, d), out_dtype), grid=(m // bt,),
        in_specs=[blk, row, row, row], out_specs=blk,
        compiler_params=_cparams('parallel'), name='normmod')(
            x, g.reshape(1, d), sc.reshape(1, d), sh.reshape(1, d))


HALO = 16


def _dnproj_body(a_ref, ap_ref, an_ref, w_ref, cw_ref, o_ref, *, mode):
    i = pl.program_id(1)
    nb = pl.num_programs(1)
    bm = a_ref.shape[0]
    lhs = jnp.concatenate([ap_ref[...], a_ref[...], an_ref[...]], axis=0)
    y = jnp.dot(lhs, w_ref[...], preferred_element_type=F32)
    x = y[HALO:HALO + bm]
    row = lax.broadcasted_iota(jnp.int32, x.shape, 0)
    prev_row = jnp.where(i > 0, y[HALO - 1:HALO], 0.0)
    next_row = jnp.where(i < nb - 1, y[HALO + bm:HALO + bm + 1], 0.0)
    xm = jnp.where(row == 0, prev_row, pltpu.roll(x, 1, 0))
    xp = jnp.where(row == bm - 1, next_row, pltpu.roll(x, bm - 1, 0))
    y = _silu(cw_ref[0:1, :] * xm + cw_ref[1:2, :] * x + cw_ref[2:3, :] * xp)
    if mode == 'v':
        o_ref[...] = y.astype(o_ref.dtype)
        return
    scale = DN_DK ** -0.5 if mode == 'q' else 1.0
    for h in range(x.shape[1] // DN_DK):
        sl = slice(h * DN_DK, (h + 1) * DN_DK)
        yh = y[:, sl]
        inv = lax.rsqrt(jnp.sum(yh * yh, axis=-1, keepdims=True) + EPS)
        o_ref[:, sl] = (yh * (inv * scale)).astype(o_ref.dtype)


def dn_proj_conv(a, w, w_col0, conv_w, part, d, mode, bm=1024, bn=1024):
    m, k = a.shape
    bm, bn = _fit(m, bm, HALO), _fit(d, bn)
    assert w_col0 % bn == 0 and bm % HALO == 0
    c0 = (w_col0 + part * d) // bn
    cw0 = part * d // bn
    rh = bm // HALO
    lasth = m // HALO - 1
    return pl.pallas_call(
        functools.partial(_dnproj_body, mode=mode),
        out_shape=jax.ShapeDtypeStruct((m, d), F32), grid=(d // bn, m // bm),
        in_specs=[pl.BlockSpec((bm, k), lambda j, i: (i, 0)),
                  pl.BlockSpec((HALO, k), lambda j, i: (jnp.maximum(i * rh - 1, 0), 0)),
                  pl.BlockSpec((HALO, k), lambda j, i: (jnp.minimum((i + 1) * rh, lasth), 0)),
                  pl.BlockSpec((k, bn), lambda j, i: (0, c0 + j)),
                  pl.BlockSpec((3, bn), lambda j, i: (0, cw0 + j))],
        out_specs=pl.BlockSpec((bm, bn), lambda j, i: (i, j)),
        compiler_params=_cparams('parallel', 'parallel'), name='dn_proj_' + mode)(
            a, a, a, w, conv_w)


def _dnab_body(x_ref, alog_ref, dtb_ref, o_ref):
    x = x_ref[...]
    lane = lax.broadcasted_iota(jnp.int32, x.shape, 1)
    g = -jnp.exp(alog_ref[...]) * _softplus(x + dtb_ref[...])
    o_ref[...] = jnp.where(lane < x.shape[1] // 2, g, _sigmoid(x))


def dn_gates(pab, a_log, dt_bias, bt=1024):
    m, w = pab.shape
    bt = min(bt, m)
    zeros = jnp.zeros((w // 2,), F32)
    alog = jnp.concatenate([a_log.reshape(-1), zeros]).reshape(1, w)
    dtb = jnp.concatenate([dt_bias.reshape(-1), zeros]).reshape(1, w)
    row = pl.BlockSpec((1, w), lambda i: (0, 0))
    blk = pl.BlockSpec((bt, w), lambda i: (i, 0))
    return pl.pallas_call(
        _dnab_body, out_shape=jax.ShapeDtypeStruct((m, w), F32), grid=(m // bt,),
        in_specs=[blk, row, row], out_specs=blk,
        compiler_params=_cparams('parallel'), name='dn_gates')(pab, alog, dtb)


def _cast_jobs(ws, steps, step_index):
    args, in_specs, out_shapes, out_specs = [], [], [], []
    for stack, layer in ws:
        _, rows, cols = stack.shape
        ratio = next(r for r in (1, 2, 4, 8, 16) if steps % r == 0
                     and rows % (steps // r) == 0 and (rows // (steps // r)) % 16 == 0)
        blk = rows // (steps // ratio)
        args.append(stack)
        in_specs.append(pl.BlockSpec(
            (None, blk, cols),
            lambda g, j, ratio=ratio, layer=layer: (layer, step_index(g, j) // ratio, 0)))
        out_shapes.append(jax.ShapeDtypeStruct((rows, cols), BF16))
        out_specs.append(pl.BlockSpec(
            (blk, cols), lambda g, j, ratio=ratio: (step_index(g, j) // ratio, 0)))
    return args, in_specs, out_shapes, out_specs


def _split_refs(rest, n_fused, n_cast):
    a, b, c, d = n_fused, n_fused + n_cast, n_fused + n_cast + 2, n_fused + 2 * n_cast + 2
    return rest[:a], rest[a:b], rest[b:c], rest[c:d], rest[d:]


def _run_casts(cast_in, cast_out):
    for wi, wo in zip(cast_in, cast_out):
        wo[...] = wi[...].astype(BF16)


def _gated_head_norm(o, z, gain):
    y = o * lax.rsqrt(jnp.mean(o * o, axis=-1, keepdims=True) + EPS) * gain
    return y * _silu(z)


def _tri_masks(c, rev):
    ii = lax.broadcasted_iota(jnp.int32, (c, c), 0)
    jj = lax.broadcasted_iota(jnp.int32, (c, c), 1)
    if rev:
        return ii == jj, ii <= jj, ii < jj
    return ii == jj, ii >= jj, ii > jj


def _dn_intra(q, k, v, g_row, beta_row, masks, tri):
    eye, incl, strict = masks
    c = q[0].shape[0]
    heads = range(len(q))
    gc_col, beta_col, g_tot, decay = [], [], [], []
    for h in heads:
        g_bc = jnp.broadcast_to(g_row[h], (c, c))
        gcc = jnp.sum(jnp.where(incl, g_bc, 0.0), axis=1, keepdims=True)
        gcr = jnp.sum(jnp.where(eye, jnp.broadcast_to(gcc, (c, c)), 0.0), axis=0, keepdims=True)
        gc_col.append(gcc)
        beta_col.append(jnp.sum(jnp.where(eye, jnp.broadcast_to(beta_row[h], (c, c)), 0.0),
                                axis=1, keepdims=True))
        g_tot.append(jnp.sum(g_row[h], axis=1, keepdims=True))
        decay.append(jnp.where(incl, jnp.exp(jnp.where(incl, gcc - gcr, 0.0)), 0.0))
    kb = [k[h] * beta_col[h] for h in heads]
    kq = [_dot_nt(jnp.concatenate([kb[h], q[h]], axis=0), k[h]) for h in heads]
    kk = [kq[h][:c] for h in heads]
    qk = [kq[h][c:] for h in heads]
    n = [-jnp.where(strict, kk[h] * decay[h], 0.0) for h in heads]
    egc = [jnp.exp(gc_col[h]) for h in heads]
    rhs = [jnp.concatenate([kb[h] * egc[h], v[h] * beta_col[h]], axis=1) for h in heads]
    sol = _unit_tri_solve_many(n, rhs, tri)
    wq = [jnp.concatenate([sol[h][:, :DN_DK], q[h] * egc[h]], axis=0) for h in heads]
    u = [sol[h][:, DN_DK:] for h in heads]
    attn = [jnp.where(incl, qk[h] * decay[h], 0.0) for h in heads]
    kd = [k[h] * jnp.exp(g_tot[h] - gc_col[h]) for h in heads]
    gl = [jnp.exp(g_tot[h]) for h in heads]
    return wq, u, attn, kd, gl


def _dn_scan_body(q_ref, k_ref, v_ref, g_ref, b_ref, s0_ref, *rest, rev, nc, hg, fused, n_cast):
    fused_refs, cast_in, (o_ref, sf_ref), cast_out, scratch = _split_refs(
        rest, 3 if fused else 0, n_cast)
    if fused:
        of_ref, z_ref, gain_ref = fused_refs
    s_scr, wq_scr, u_scr, at_scr, kd_scr, gl_scr = scratch
    _run_casts(cast_in, cast_out)
    j = pl.program_id(1)

    @pl.when(j == 0)
    def _():
        s_scr[...] = s0_ref[...]

    c = DN_CHUNK
    masks = _tri_masks(c, rev)
    tri = _tri_levels(c, c)
    heads = range(hg)
    lanes = [slice(h * DN_DK, (h + 1) * DN_DK) for h in heads]

    def intra(ci, carry):
        rows = pl.ds(pl.multiple_of(ci * c, c), c)
        wq, u, attn, kd, gl = _dn_intra(
            [q_ref[rows, sl] for sl in lanes], [k_ref[rows, sl] for sl in lanes],
            [v_ref[rows, sl] for sl in lanes], [g_ref[ci, h:h + 1, :] for h in heads],
            [b_ref[ci, h:h + 1, :] for h in heads], masks, tri)
        for h in heads:
            wq_scr[ci, h] = wq[h].astype(BF16)
            u_scr[ci, h] = u[h]
            at_scr[ci, h] = attn[h].astype(BF16)
            kd_scr[ci, h] = kd[h].astype(BF16)
            gl_scr[ci, h] = jnp.broadcast_to(gl[h], (1, DN_DK))
        return carry

    lax.fori_loop(0, nc, intra, 0)

    def inter(ci, carry):
        cc = (nc - 1 - ci) if rev else ci
        rows = pl.ds(pl.multiple_of(cc * c, c), c)
        s = [s_scr[h] for h in heads]
        ws = [_dot(wq_scr[cc, h], s[h]) for h in heads]
        v_new = [u_scr[cc, h] - ws[h][:c] for h in heads]
        av = [_dot(at_scr[cc, h], v_new[h]) for h in heads]
        kv = [_dot_tn(kd_scr[cc, h], v_new[h]) for h in heads]
        for h in heads:
            o = ws[h][c:] + av[h]
            if fused:
                o = _gated_head_norm(o + of_ref[rows, lanes[h]], z_ref[rows, lanes[h]],
                                     gain_ref[:, lanes[h]])
            o_ref[rows, lanes[h]] = o.astype(o_ref.dtype)
            s_scr[h] = s[h] * gl_scr[cc, h] + kv[h]
        return carry

    lax.fori_loop(0, nc, inter, 0)

    @pl.when(j == pl.num_programs(1) - 1)
    def _():
        sf_ref[...] = s_scr[...]


def dn_scan(q, k, v, gbt, s0, direction, fuse=None, casts=(), hg=16, nc=2):
    m, d = q.shape
    h = d // DN_DK
    hg = min(hg, h)
    ng = h // hg
    nc = min(nc, m // DN_CHUNK)
    bt = nc * DN_CHUNK
    nb = m // bt
    rev = direction == 1
    gb4 = gbt.reshape(4 * ng, hg, m // DN_CHUNK, DN_CHUNK).transpose(0, 2, 1, 3)

    def tb(j):
        return (nb - 1 - j) if rev else j

    tok = pl.BlockSpec((bt, hg * DN_DK), lambda g, j: (tb(j), g))
    st = pl.BlockSpec((hg, DN_DK, DN_DK), lambda g, j: (g, 0, 0))
    g_spec = pl.BlockSpec((None, nc, hg, DN_CHUNK),
                          lambda g, j: (direction * ng + g, tb(j), 0, 0))
    b_spec = pl.BlockSpec((None, nc, hg, DN_CHUNK),
                          lambda g, j: ((2 + direction) * ng + g, tb(j), 0, 0))
    args, specs = [q, k, v, gb4, gb4, s0], [tok, tok, tok, g_spec, b_spec, st]
    if fuse is not None:
        o_other, p, z_part, gain = fuse
        args += [o_other, p, gain.reshape(1, d)]
        specs += [tok, pl.BlockSpec((bt, hg * DN_DK), lambda g, j: (tb(j), z_part * ng + g)),
                  pl.BlockSpec((1, hg * DN_DK), lambda g, j: (0, g))]
    c_args, c_in, c_shapes, c_out = _cast_jobs(casts, ng * nb, lambda g, j: g * nb + j)
    res = pl.pallas_call(
        functools.partial(_dn_scan_body, rev=rev, nc=nc, hg=hg, fused=fuse is not None,
                          n_cast=len(casts)),
        out_shape=(jax.ShapeDtypeStruct((m, d), F32 if fuse is None else BF16),
                   jax.ShapeDtypeStruct(s0.shape, F32), *c_shapes),
        grid=(ng, nb), in_specs=specs + c_in, out_specs=(tok, st, *c_out),
        scratch_shapes=[pltpu.VMEM((hg, DN_DK, DN_DK), F32),
                        pltpu.VMEM((nc, hg, 2 * DN_CHUNK, DN_DK), BF16),
                        pltpu.VMEM((nc, hg, DN_CHUNK, DN_DK), F32),
                        pltpu.VMEM((nc, hg, DN_CHUNK, DN_CHUNK), BF16),
                        pltpu.VMEM((nc, hg, DN_CHUNK, DN_DK), BF16),
                        pltpu.VMEM((nc, hg, 1, DN_DK), F32)],
        compiler_params=_cparams('parallel', 'arbitrary'),
        name='dn_scan_bwd' if rev else 'dn_scan_fwd')(*args, *c_args)
    return res[0], res[1], list(res[2:])


def _hg_chunk(q_raw, f_raw, v, lb, st, masks, rev):
    _, incl, _ = masks
    c = q_raw[0].shape[0]
    heads = range(len(q_raw))
    mid = (c - 1 - c // 2) if rev else c // 2
    last = 0 if rev else c - 1
    qm, km, qs, kd, gl = [], [], [], [], []
    for h in heads:
        q = _silu(q_raw[h])
        f = lb[h] + (1.0 - lb[h]) * _sigmoid(f_raw[h])
        k = 1.0 - f
        b = _cumsum_rows(jnp.log(f), rev)
        m = b[mid:mid + 1, :]
        b_last = b[last:last + 1, :]
        qm.append(q * jnp.exp(b - m))
        km.append(k * jnp.exp(m - b))
        qs.append(q * jnp.exp(b))
        kd.append(k * jnp.exp(b_last - b))
        gl.append(jnp.exp(b_last))
    a_qk = [jnp.where(incl, _dot_nt(qm[h], km[h]), 0.0) for h in heads]
    inter = [_dot_nt(qs[h], st[h]) for h in heads]
    kv = [_dot_tn(v[h], kd[h]) for h in heads]
    intra = [_dot(a_qk[h], v[h]) for h in heads]
    o = [inter[h] + intra[h] for h in heads]
    st_new = [st[h] * gl[h] + kv[h] for h in heads]
    return o, st_new


def _hg_scan_body(q_ref, f_ref, v_ref, lb_ref, s0_ref, *rest, rev, nc, hg, fused, n_cast):
    fused_refs, cast_in, (o_ref, sf_ref), cast_out, (s_scr,) = _split_refs(
        rest, 3 if fused else 0, n_cast)
    if fused:
        of_ref, z_ref, gain_ref = fused_refs
    _run_casts(cast_in, cast_out)
    j = pl.program_id(1)

    @pl.when(j == 0)
    def _():
        s_scr[...] = s0_ref[...]

    c = HG_CHUNK
    masks = _tri_masks(c, rev)
    heads = range(hg)
    lanes = [slice(h * HG_DK, (h + 1) * HG_DK) for h in heads]

    def chunk(ci, carry):
        cc = (nc - 1 - ci) if rev else ci
        rows = pl.ds(pl.multiple_of(cc * c, c), c)
        o, s_new = _hg_chunk([q_ref[rows, sl] for sl in lanes], [f_ref[rows, sl] for sl in lanes],
                             [v_ref[rows, sl] for sl in lanes], [lb_ref[:, sl] for sl in lanes],
                             [s_scr[h] for h in heads], masks, rev)
        for h in heads:
            oh = o[h]
            if fused:
                oh = _gated_head_norm(oh + of_ref[rows, lanes[h]], z_ref[rows, lanes[h]],
                                      gain_ref[:, lanes[h]])
            o_ref[rows, lanes[h]] = oh.astype(o_ref.dtype)
            s_scr[h] = s_new[h]
        return carry

    lax.fori_loop(0, nc, chunk, 0)

    @pl.when(j == pl.num_programs(1) - 1)
    def _():
        sf_ref[...] = s_scr[...]


def hg_scan(p, lower, s0, direction, d, fuse=None, casts=(), hg=16, nc=8):
    m = p.shape[0]
    h = d // HG_DK
    hg = min(hg, h)
    ng = h // hg
    nc = min(nc, m // HG_CHUNK)
    bt = nc * HG_CHUNK
    nb = m // bt
    rev = direction == 1

    def tb(j):
        return (nb - 1 - j) if rev else j

    def part(pi):
        return pl.BlockSpec((bt, hg * HG_DK), lambda g, j: (tb(j), pi * ng + g))

    st = pl.BlockSpec((hg, HG_DK, HG_DK), lambda g, j: (g, 0, 0))
    row = pl.BlockSpec((1, hg * HG_DK), lambda g, j: (0, g))
    tok = pl.BlockSpec((bt, hg * HG_DK), lambda g, j: (tb(j), g))
    args = [p, p, p, lower.reshape(1, d), s0]
    specs = [part(0), part(1 + direction), part(3), row, st]
    if fuse is not None:
        o_other, gain = fuse
        args += [o_other, p, gain.reshape(1, d)]
        specs += [tok, part(4), row]
    c_args, c_in, c_shapes, c_out = _cast_jobs(casts, ng * nb, lambda g, j: g * nb + j)
    res = pl.pallas_call(
        functools.partial(_hg_scan_body, rev=rev, nc=nc, hg=hg, fused=fuse is not None,
                          n_cast=len(casts)),
        out_shape=(jax.ShapeDtypeStruct((m, d), F32 if fuse is None else BF16),
                   jax.ShapeDtypeStruct(s0.shape, F32), *c_shapes),
        grid=(ng, nb), in_specs=specs + c_in, out_specs=(tok, st, *c_out),
        scratch_shapes=[pltpu.VMEM((hg, HG_DK, HG_DK), F32)],
        compiler_params=_cparams('parallel', 'arbitrary'),
        name='hg_scan_bwd' if rev else 'hg_scan_fwd')(*args, *c_args)
    return res[0], res[1], list(res[2:])


def _shiftmix_body(x_ref, prev_ref, next_ref, mu_ref, *o_refs, width):
    i = pl.program_id(0)
    nb = pl.num_programs(0)
    bt, d = x_ref.shape
    dq = d // 4
    col = lax.broadcasted_iota(jnp.int32, (bt, dq), 0) % width
    for qi in range(4):
        sl = slice(qi * dq, (qi + 1) * dq)
        x = x_ref[:, sl]
        if qi == 0:
            sh = jnp.where(col == 0, 0.0, pltpu.roll(x, 1, 0))
        elif qi == 1:
            sh = jnp.where(col == width - 1, 0.0, pltpu.roll(x, bt - 1, 0))
        elif qi == 2:
            edge = jnp.where(i > 0, prev_ref[:, sl], 0.0)
            sh = edge if bt == width else jnp.concatenate([edge, x[:bt - width]], axis=0)
        else:
            edge = jnp.where(i < nb - 1, next_ref[:, sl], 0.0)
            sh = edge if bt == width else jnp.concatenate([x[width:], edge], axis=0)
        xx = sh - x
        for n, o_ref in enumerate(o_refs):
            o_ref[:, sl] = (x + xx * mu_ref[n:n + 1, sl]).astype(o_ref.dtype)


def shift_mix(h, mu, width, bt=128):
    m, d = h.shape
    bt = max(min(bt, m), width)
    rw = bt // width
    lastw = m // width - 1
    blk = pl.BlockSpec((bt, d), lambda i: (i, 0))
    out = jax.ShapeDtypeStruct((m, d), BF16)
    return pl.pallas_call(
        functools.partial(_shiftmix_body, width=width),
        out_shape=(out,) * 6, grid=(m // bt,),
        in_specs=[blk,
                  pl.BlockSpec((width, d), lambda i: (jnp.maximum(i * rw - 1, 0), 0)),
                  pl.BlockSpec((width, d), lambda i: (jnp.minimum((i + 1) * rw, lastw), 0)),
                  pl.BlockSpec((6, d), lambda i: (0, 0))],
        out_specs=(blk,) * 6, compiler_params=_cparams('arbitrary'), name='shift_mix')(h, h, h, mu)


def _halves(x, lo_mask):
    return jnp.concatenate([jnp.where(lo_mask, x, 0.0), jnp.where(lo_mask, 0.0, x)], axis=0)


def _head_sum(x, lo_mask):
    s_lo = jnp.sum(jnp.where(lo_mask, x, 0.0), axis=-1, keepdims=True)
    s_hi = jnp.sum(jnp.where(lo_mask, 0.0, x), axis=-1, keepdims=True)
    return jnp.where(lo_mask, s_lo, s_hi)


def _rw_intra(r, k, v, a_sig, logw, k_k, k_a, masks, tri, rev):
    lo, strict2, incl_cat = masks
    c = r[0].shape[0]
    groups = range(len(r))
    mid = (c - 1 - c // 2) if rev else c // 2
    last = 0 if rev else c - 1
    am2, bm2, km2, v2, rm, ae2, rg, bd2, kd2, gl = ([] for _ in range(10))
    for g in groups:
        kq = k[g] * k_k[g]
        kk = kq * lax.rsqrt(_head_sum(kq * kq, lo) + EPS)
        b = kk * a_sig[g]
        kd = k[g] * (1.0 + (a_sig[g] - 1.0) * k_a[g])
        cs = _cumsum_rows(logw[g], rev)
        ce = cs - logw[g]
        m = cs[mid:mid + 1, :]
        c_last = cs[last:last + 1, :]
        e_mc = jnp.exp(m - cs)
        dec = jnp.exp(c_last - cs)
        am2.append(_halves(-kk * jnp.exp(ce - m), lo))
        bm2.append(_halves(b * e_mc, lo))
        km2.append(_halves(kd * e_mc, lo))
        v2.append(_halves(v[g], lo))
        rm.append(r[g] * jnp.exp(cs - m))
        ae2.append(_halves(-kk * jnp.exp(ce), lo))
        rg.append(r[g] * jnp.exp(cs))
        bd2.append(_halves(b * dec, lo))
        kd2.append(_halves(kd * dec, lo))
        gl.append(jnp.exp(c_last))
    pairs = [_dot_nt(jnp.concatenate([am2[g], rm[g]], axis=0),
                     jnp.concatenate([bm2[g], km2[g]], axis=0)) for g in groups]
    n = [jnp.where(strict2, pairs[g][:2 * c, :2 * c], 0.0) for g in groups]
    a_ak = [jnp.where(strict2, pairs[g][:2 * c, 2 * c:], 0.0) for g in groups]
    a_rb = [jnp.where(incl_cat, pairs[g][2 * c:, :2 * c], 0.0) for g in groups]
    a_rk = [jnp.where(incl_cat, pairs[g][2 * c:, 2 * c:], 0.0) for g in groups]
    both = [_dot(jnp.concatenate([a_ak[g], a_rk[g]], axis=0), v2[g]) for g in groups]
    akv = [both[g][:2 * c] for g in groups]
    ov = [both[g][2 * c:] for g in groups]
    kv = [_dot_tn(v2[g], kd2[g]) for g in groups]
    sol = _unit_tri_solve_many(n, [jnp.concatenate([ae2[g], akv[g]], axis=1) for g in groups],
                               tri)
    wr = [jnp.concatenate([sol[g][:, :LANES], rg[g]], axis=0) for g in groups]
    uv2 = [sol[g][:, LANES:] for g in groups]
    return wr, uv2, a_rb, ov, bd2, kv, gl


def _rw_output(o, r, k, v, a_sum, gate, ln_w, ln_b, k_a, r_k, lo):
    inv_n = 1.0 / RW_HEAD
    oc = o - _head_sum(o, lo) * inv_n
    y = oc * lax.rsqrt(_head_sum(oc * oc, lo) * inv_n + RW_LN_EPS) * ln_w + ln_b
    k_sum = k * (2.0 + (a_sum - 2.0) * k_a)
    bonus = _head_sum(r * k_sum * r_k, lo) * v
    return (y + bonus) * gate


def _rw_masks(c, rev):
    lo = lax.broadcasted_iota(jnp.int32, (1, LANES), 1) < RW_HEAD
    ii = lax.broadcasted_iota(jnp.int32, (2 * c, 2 * c), 0)
    jj = lax.broadcasted_iota(jnp.int32, (2 * c, 2 * c), 1)
    same = (ii // c) == (jj // c)
    il, jl = ii % c, jj % c
    strict2 = same & ((il < jl) if rev else (il > jl))
    ic = lax.broadcasted_iota(jnp.int32, (c, 2 * c), 0)
    jc = lax.broadcasted_iota(jnp.int32, (c, 2 * c), 1) % c
    incl_cat = (ic <= jc) if rev else (ic >= jc)
    return lo, strict2, incl_cat


def _rw_scan_body(r_ref, k_ref, v_ref, a_ref, lw_ref, kk_ref, ka_ref, s0_ref, *rest,
                  rev, nc, hg, fused, n_cast):
    fused_refs, cast_in, (o_ref, sf_ref), cast_out, scratch = _split_refs(
        rest, 6 if fused else 0, n_cast)
    if fused:
        of_ref, ao_ref, gate_ref, lnw_ref, lnb_ref, rk_ref = fused_refs
    s_scr, wr_scr, uv_scr, arb_scr, ov_scr, bd_scr, kv_scr, gl_scr = scratch
    _run_casts(cast_in, cast_out)
    j = pl.program_id(1)

    @pl.when(j == 0)
    def _():
        s_scr[...] = s0_ref[...]

    c = RW_CHUNK
    masks = _rw_masks(c, rev)
    tri = _tri_levels(2 * c, c)
    groups = range(hg)
    lanes = [slice(g * LANES, (g + 1) * LANES) for g in groups]

    def intra(ci, carry):
        rows = pl.ds(pl.multiple_of(ci * c, c), c)
        wr, uv2, a_rb, ov, bd2, kv, gl = _rw_intra(
            [r_ref[rows, sl] for sl in lanes], [k_ref[rows, sl] for sl in lanes],
            [v_ref[rows, sl] for sl in lanes], [a_ref[rows, sl] for sl in lanes],
            [lw_ref[rows, sl] for sl in lanes], [kk_ref[:, sl] for sl in lanes],
            [ka_ref[:, sl] for sl in lanes], masks, tri, rev)
        for g in groups:
            wr_scr[ci, g] = wr[g].astype(BF16)
            uv_scr[ci, g] = uv2[g]
            arb_scr[ci, g] = a_rb[g].astype(BF16)
            ov_scr[ci, g] = ov[g]
            bd_scr[ci, g] = bd2[g].astype(BF16)
            kv_scr[ci, g] = kv[g]
            gl_scr[ci, g] = gl[g]
        return carry

    lax.fori_loop(0, nc, intra, 0)

    def inter(ci, carry):
        cc = (nc - 1 - ci) if rev else ci
        rows = pl.ds(pl.multiple_of(cc * c, c), c)
        st = [s_scr[g] for g in groups]
        ws = [_dot_nt(wr_scr[cc, g], st[g]) for g in groups]
        u2 = [ws[g][:2 * c] + uv_scr[cc, g] for g in groups]
        au = [_dot(arb_scr[cc, g], u2[g]) for g in groups]
        ub = [_dot_tn(u2[g], bd_scr[cc, g]) for g in groups]
        for g in groups:
            o = ws[g][2 * c:] + au[g] + ov_scr[cc, g]
            if fused:
                sl = lanes[g]
                o = _rw_output(o + of_ref[rows, sl], r_ref[rows, sl], k_ref[rows, sl],
                               v_ref[rows, sl], a_ref[rows, sl] + ao_ref[rows, sl],
                               gate_ref[rows, sl], lnw_ref[:, sl], lnb_ref[:, sl], ka_ref[:, sl],
                               rk_ref[:, sl], masks[0])
            o_ref[rows, lanes[g]] = o.astype(o_ref.dtype)
            s_scr[g] = st[g] * gl_scr[cc, g] + kv_scr[cc, g] + ub[g]
        return carry

    lax.fori_loop(0, nc, inter, 0)

    @pl.when(j == pl.num_programs(1) - 1)
    def _():
        sf_ref[...] = s_scr[...]


def rw_scan(r, k, v, a_sig, logw, k_k, k_a, s0, direction, fuse=None, casts=(), hg=16, nc=2):
    m, d = r.shape
    npair = d // LANES
    hg = min(hg, npair)
    ng = npair // hg
    nc = min(nc, m // RW_CHUNK)
    bt = nc * RW_CHUNK
    nb = m // bt
    rev = direction == 1

    def tb(j):
        return (nb - 1 - j) if rev else j

    tok = pl.BlockSpec((bt, hg * LANES), lambda g, j: (tb(j), g))
    row = pl.BlockSpec((1, hg * LANES), lambda g, j: (0, g))
    st = pl.BlockSpec((hg, LANES, LANES), lambda g, j: (g, 0, 0))
    args = [r, k, v, a_sig, logw, k_k.reshape(1, d), k_a.reshape(1, d), s0]
    specs = [tok, tok, tok, tok, tok, row, row, st]
    if fuse is not None:
        args += list(fuse[:3]) + [t.reshape(1, d) for t in fuse[3:]]
        specs += [tok, tok, tok, row, row, row]
    c_args, c_in, c_shapes, c_out = _cast_jobs(casts, ng * nb, lambda g, j: g * nb + j)
    res = pl.pallas_call(
        functools.partial(_rw_scan_body, rev=rev, nc=nc, hg=hg, fused=fuse is not None,
                          n_cast=len(casts)),
        out_shape=(jax.ShapeDtypeStruct((m, d), F32 if fuse is None else BF16),
                   jax.ShapeDtypeStruct(s0.shape, F32), *c_shapes),
        grid=(ng, nb), in_specs=specs + c_in, out_specs=(tok, st, *c_out),
        scratch_shapes=[pltpu.VMEM((hg, LANES, LANES), F32),
                        pltpu.VMEM((nc, hg, 3 * RW_CHUNK, LANES), BF16),
                        pltpu.VMEM((nc, hg, 2 * RW_CHUNK, LANES), F32),
                        pltpu.VMEM((nc, hg, RW_CHUNK, 2 * RW_CHUNK), BF16),
                        pltpu.VMEM((nc, hg, RW_CHUNK, LANES), F32),
                        pltpu.VMEM((nc, hg, 2 * RW_CHUNK, LANES), BF16),
                        pltpu.VMEM((nc, hg, LANES, LANES), F32),
                        pltpu.VMEM((nc, hg, 1, LANES), F32)],
        compiler_params=_cparams('parallel', 'arbitrary'),
        name='rw_scan_bwd' if rev else 'rw_scan_fwd')(*args, *c_args)
    return res[0], res[1], list(res[2:])


def _convglu_body(u_ref, prev_ref, next_ref, v_ref, cw_ref, cb_ref, o_ref, *, width):
    i = pl.program_id(0)
    nb = pl.num_programs(0)
    x = u_ref[...].astype(F32)
    bt = x.shape[0]
    up = jnp.where(i > 0, prev_ref[...].astype(F32), 0.0)
    dn = jnp.where(i < nb - 1, next_ref[...].astype(F32), 0.0)
    if bt > width:
        up = jnp.concatenate([up, x[:bt - width]], axis=0)
        dn = jnp.concatenate([x[width:], dn], axis=0)
    col = lax.broadcasted_iota(jnp.int32, x.shape, 0) % width
    rows = (up, x, dn)

    def column(kw):
        return sum(cw_ref[3 * r + kw:3 * r + kw + 1, :] * rows[r] for r in range(3))

    left = jnp.where(col == 0, 0.0, pltpu.roll(column(0), 1, 0))
    right = jnp.where(col == width - 1, 0.0, pltpu.roll(column(2), bt - 1, 0))
    h = column(1) + left + right + cb_ref[...]
    c = math.sqrt(2.0 / math.pi)
    t = jnp.tanh(h * (2.0 * c + (8.0 * 0.044715 * c) * (h * h)))
    o_ref[...] = ((h + h * t) * v_ref[...].astype(F32)).astype(o_ref.dtype)


def conv_glu_gate(uv, conv_w, conv_b, width, f, bt=512, bc=512):
    m = uv.shape[0]
    bt = max(min(bt, m), width)
    bc = _fit(f, bc)
    rw = bt // width
    lastw = m // width - 1
    voff = f // bc
    return pl.pallas_call(
        functools.partial(_convglu_body, width=width),
        out_shape=jax.ShapeDtypeStruct((m, f), BF16), grid=(m // bt, f // bc),
        in_specs=[pl.BlockSpec((bt, bc), lambda i, j: (i, j)),
                  pl.BlockSpec((width, bc), lambda i, j: (jnp.maximum(i * rw - 1, 0), j)),
                  pl.BlockSpec((width, bc), lambda i, j: (jnp.minimum((i + 1) * rw, lastw), j)),
                  pl.BlockSpec((bt, bc), lambda i, j: (i, voff + j)),
                  pl.BlockSpec((9, bc), lambda i, j: (0, j)),
                  pl.BlockSpec((1, bc), lambda i, j: (0, j))],
        out_specs=pl.BlockSpec((bt, bc), lambda i, j: (i, j)),
        compiler_params=_cparams('parallel', 'parallel'), name='conv_glu')(
            uv, uv, uv, uv, 0.5 * conv_w.reshape(9, f), 0.5 * conv_b.reshape(1, f))


def _modulation(conds, down, up, bias):
    low = matmul(conds, down, precise=True, bm=8, bn=512, name='ada_down')
    return matmul(low, up, bias=bias, precise=True, bm=8, bn=2048, name='ada_up')


def _dn_mixer(streams, w, d, last, casts):
    heads = d // DN_DK
    w_in = w['w_in']
    if w_in.dtype == BF16:
        w_main = w_in
        ab = dict(w=w_in, w_cols=(4 * d, 4 * heads))
    else:
        w_main = w_in[:, :4 * d].astype(BF16)
        ab = dict(w=w_in[:, 4 * d:].astype(BF16))
    gain = jnp.tile(w['norm_g'], heads)
    prepped = []
    for h in streams:
        q, k, v = (dn_proj_conv(h, w_main, 0, w['conv_w'], part, d, mode)
                   for part, mode in enumerate('qkv'))
        z = matmul(h, w_main, w_cols=(3 * d, d), name='dn_in_z')
        pab = matmul(h, bn=4 * heads, name='dn_in_ab', **ab)
        gb = dn_gates(pab, w['a_log'], w['dt_bias'])
        prepped.append((z, q, k, v, gb.T))
    states = [jnp.zeros((heads, DN_DK, DN_DK), F32)] * 2
    outs, done = [], [[], []]
    for si, (z, q, k, v, gbt) in enumerate(prepped):
        jobs = casts if si == 1 else ((), ())
        o_f, states[0], done[0] = dn_scan(q, k, v, gbt, states[0], 0, casts=jobs[0])
        y, states[1], done[1] = dn_scan(q, k, v, gbt, states[1], 1, fuse=(o_f, z, 0, gain),
                                        casts=jobs[1])
        outs.append(None if si == 0 and last else y)
    return outs, done[0], done[1]


def _hg_mixer(streams, w, d, last, casts):
    heads = d // HG_DK
    states = [jnp.zeros((heads, HG_DK, HG_DK), F32)] * 2
    outs, done = [], [[], []]
    for si, h in enumerate(streams):
        jobs = casts if si == 1 else ((), ())
        p = matmul(h, w['w_in'], name='hg_in')
        o_f, states[0], done[0] = hg_scan(p, w['lower'], states[0], 0, d, casts=jobs[0])
        y, states[1], done[1] = hg_scan(p, w['lower'], states[1], 1, d,
                                        fuse=(o_f, w['norm_g']), casts=jobs[1])
        outs.append(None if si == 0 and last else y)
    return outs, done[0], done[1]


def _rw_mixer(streams, widths, w, d, last, casts):
    w_rkv = w['w_rkv']
    w1 = jnp.concatenate([w['w1'][0], w['w1'][1]], axis=1).astype(BF16)
    a1 = jnp.concatenate([w['a1'][0], w['a1'][1]], axis=1).astype(BF16)
    w2 = w['w2'].astype(BF16)
    a2 = w['a2'].astype(BF16)
    lora = w['w1'].shape[-1]
    gl = w['g1'].shape[-1]
    glp = -(-gl // LANES) * LANES
    g1 = jnp.pad(w['g1'], ((0, 0), (0, glp - gl))).astype(BF16)
    g2 = jnp.pad(w['g2'], ((0, glp - gl), (0, 0))).astype(BF16)
    states = [jnp.zeros((d // LANES, LANES, LANES), F32)] * 2
    outs, done = [], [[], []]
    for si, (h, width) in enumerate(zip(streams, widths)):
        jobs = casts if si == 1 else ((), ())
        xr, xw, xk, xv, xa, xg = shift_mix(h, w['mu'], width)
        r = matmul(xr, w_rkv[0], name='rw_r')
        k = matmul(xk, w_rkv[1], name='rw_k')
        v = matmul(xv, w_rkv[2], name='rw_v')
        tw = matmul(xw, w1, epi='tanh', out_dtype=BF16, bn=2 * lora, name='rw_w1')
        ta = matmul(xa, a1, out_dtype=BF16, bn=2 * lora, name='rw_a1')
        tg = matmul(xg, g1, epi='sigmoid', out_dtype=BF16, name='rw_g1')
        gate = matmul(tg, g2, name='rw_g2')
        logw, a_sig = [], []
        for direction in range(2):
            logw.append(matmul(tw, w2[direction], bias=w['w0'][direction], epi='logw',
                               a_koff=direction, name='rw_w2'))
            a_sig.append(matmul(ta, a2[direction], bias=w['a0'][direction], epi='sigmoid',
                                a_koff=direction, name='rw_a2'))
        o_f, states[0], done[0] = rw_scan(r, k, v, a_sig[0], logw[0], w['k_k'], w['k_a'],
                                          states[0], 0, casts=jobs[0])
        y, states[1], done[1] = rw_scan(r, k, v, a_sig[1], logw[1], w['k_k'], w['k_a'],
                                        states[1], 1, casts=jobs[1],
                                        fuse=(o_f, a_sig[0], gate, w['lnx_w'], w['lnx_b'],
                                              w['r_k'].reshape(-1)))
        outs.append(None if si == 0 and last else y)
    return outs, done[0], done[1]


def kernel(x, c, ctx, c_ctx, ada_down, ada_up, ada_b, norm1_g, norm2_g, ffn_w_up, ffn_conv_w, ffn_conv_b, ffn_w_down, dn_w_in, dn_conv_w, dn_a_log, dn_dt_bias, dn_norm_g, dn_w_out, hg_w_in, hg_lower, hg_norm_g, hg_w_out, rw_mu, rw_w_rkv, rw_w0, rw_w1, rw_w2, rw_a0, rw_a1, rw_a2, rw_g1, rw_g2, rw_k_k, rw_k_a, rw_r_k, rw_lnx_w, rw_lnx_b, rw_w_out, final_g):
    _, seq, d = x.shape
    depth = ada_down.shape[0]
    n_ctx = ctx.shape[1]
    f = ffn_w_down.shape[1]
    xl, xc = x[0], ctx[0]
    sm = jax.nn.softmax(hg_lower.astype(F32), axis=0)
    lower_bounds = jnp.cumsum(sm, axis=0) - sm[0]
    conds = jnp.zeros((8, d), F32).at[0].set(jax.nn.silu(c[0])).at[1].set(jax.nn.silu(c_ctx))
    widths = (n_ctx, GRID_W)
    w_outs = {0: dn_w_out, 1: hg_w_out, 2: rw_w_out}
    w_ins = {0: dn_w_in, 1: hg_w_in, 2: rw_w_rkv.reshape(rw_w_rkv.shape[0], 3 * d, d)}

    def in_proj(i):
        return w_ins[i % 3], i // 3

    first, _ = in_proj(0)
    w_in_bf = first[0] if first.shape[2] % LANES else first[0].astype(BF16)
    for i in range(depth):
        kind, j = i % 3, i // 3
        last = i == depth - 1
        mod = _modulation(conds, ada_down[i], ada_up[i], ada_b[i])
        mods = [[mod[row, n * d:(n + 1) * d] for n in range(6)] for row in (1, 0)]
        xs = [xc, xl]
        h_dtype = F32 if kind == 2 else BF16
        hs = [normmod(t, norm1_g[i], mm[1], mm[0], h_dtype) for t, mm in zip(xs, mods)]
        nxt = in_proj(i + 1) if not last else None
        hide_next = nxt is not None and nxt[0].shape[2] % LANES == 0
        casts = ([(w_outs[kind], j), (ffn_w_up, i)],
                 [(ffn_w_down, i)] + ([nxt] if hide_next else []))
        if kind == 0:
            w = dict(w_in=w_in_bf, conv_w=dn_conv_w[j], a_log=dn_a_log[j],
                     dt_bias=dn_dt_bias[j], norm_g=dn_norm_g[j])
            ys, done_f, done_b = _dn_mixer(hs, w, d, last, casts)
        elif kind == 1:
            w = dict(w_in=w_in_bf, lower=lower_bounds[i], norm_g=hg_norm_g[j])
            ys, done_f, done_b = _hg_mixer(hs, w, d, last, casts)
        else:
            w = dict(mu=rw_mu[j], w_rkv=w_in_bf.reshape(3, d, d), w0=rw_w0[j], w1=rw_w1[j],
                     w2=rw_w2[j], a0=rw_a0[j], a1=rw_a1[j], a2=rw_a2[j], g1=rw_g1[j],
                     g2=rw_g2[j], k_k=rw_k_k[j], k_a=rw_k_a[j], r_k=rw_r_k[j],
                     lnx_w=rw_lnx_w[j], lnx_b=rw_lnx_b[j])
            ys, done_f, done_b = _rw_mixer(hs, widths, w, d, last, casts)
        w_out, w_up = done_f
        w_down = done_b[0]
        if hide_next:
            w_in_bf = done_b[1]
        elif nxt is not None:
            w_in_bf = nxt[0][nxt[1]]
        new = []
        for t, y, mm, width in zip(xs, ys, mods, widths):
            if y is None:
                new.append(t)
                continue
            t = matmul(y, w_out, gate=mm[2], resid=t, name='mix_out')
            h2 = normmod(t, norm2_g[i], mm[4], mm[3], BF16)
            uv = matmul(h2, w_up, out_dtype=BF16, name='ffn_up')
            gated = conv_glu_gate(uv, ffn_conv_w[i], ffn_conv_b[i], width, f)
            t = matmul(gated, w_down, gate=mm[5], resid=t, bm=512, bn=1024, w_single=True,
                       name='ffn_down')
            new.append(t)
        xc, xl = new
    zeros = jnp.zeros((d,), F32)
    return normmod(xl, final_g, zeros, zeros, F32)[None]
```

```python
import functools
import math

import jax
import jax.numpy as jnp
from jax import lax
from jax.experimental import pallas as pl
from jax.experimental.pallas import tpu as pltpu

F32 = jnp.float32
BF16 = jnp.bfloat16

EPS = 1e-6
RW_LN_EPS = 64e-5
GRID_W = 64
DN_DK = 128
DN_CHUNK = 128
HG_DK = 128
HG_CHUNK = 32
RW_HEAD = 64
RW_CHUNK = 64
LANES = 128
VMEM_LIMIT = 60 * 1024 * 1024


def _fit(n, b, unit=LANES):
    if n <= b:
        return n
    for cand in range(b - b % unit, 0, -unit):
        if n % cand == 0:
            return cand
    raise ValueError((n, b, unit))


def _cparams(*sem):
    return pltpu.CompilerParams(dimension_semantics=sem, vmem_limit_bytes=VMEM_LIMIT)


def _sigmoid(x):
    return 1.0 / (1.0 + jnp.exp(-x))


def _silu(x):
    h = 0.5 * x
    return h + h * jnp.tanh(h)


def _softplus(x):
    return jnp.maximum(x, 0.0) + jnp.log(1.0 + jnp.exp(-jnp.abs(x)))


def _dot(a, b, dims=(((1,), (0,)), ((), ()))):
    return lax.dot_general(a.astype(BF16), b.astype(BF16), dims, preferred_element_type=F32)


def _dot_nt(a, b):
    return _dot(a, b, (((1,), (1,)), ((), ())))


def _dot_tn(a, b):
    return _dot(a, b, (((0,), (0,)), ((), ())))


def _split2(x):
    hi = x.astype(BF16)
    lo = (x - hi.astype(F32)).astype(BF16)
    return hi, lo


def _dot3(a, b, dims=(((1,), (0,)), ((), ()))):
    ah, al = _split2(a)
    bh, bl = _split2(b)
    d = functools.partial(lax.dot_general, dimension_numbers=dims, preferred_element_type=F32)
    return d(ah, bh) + (d(ah, bl) + d(al, bh))


def _tri_levels(c, size):
    ii = lax.broadcasted_iota(jnp.int32, (c, c), 0)
    jj = lax.broadcasted_iota(jnp.int32, (c, c), 1)
    levels = []
    s = 1
    while (1 << s) < size:
        lo = (ii >> s) != (jj >> s)
        hi = (ii >> (s + 1)) == (jj >> (s + 1))
        levels.append(lo & hi)
        s += 1
    return (ii >> 1) == (jj >> 1), ii == jj, levels


def _unit_tri_solve_many(ns, rhss, tri):
    pair, eye, levels = tri
    units = range(len(ns))
    ts = [jnp.where(pair, jnp.where(eye, 1.0, n), 0.0) for n in ns]
    for lmask in levels:
        ls = [jnp.where(lmask, n, 0.0) for n in ns]
        lt = [_dot(ls[i], ts[i]) for i in units]
        ts = [ts[i] + _dot(ts[i], lt[i]) for i in units]
    return [_dot(ts[i], rhss[i]) for i in units]


def _cumsum_rows(x, rev):
    c = x.shape[0]
    row = lax.broadcasted_iota(jnp.int32, x.shape, 0)
    s = 1
    while s < c:
        if rev:
            x = x + jnp.where(row < c - s, pltpu.roll(x, c - s, 0), 0.0)
        else:
            x = x + jnp.where(row >= s, pltpu.roll(x, s, 0), 0.0)
        s *= 2
    return x


def _epi_none(y):
    return y


def _epi_logw(y):
    return -jnp.exp(-_softplus(-y) - 0.5)


_EPILOGUES = {'none': _epi_none, 'tanh': jnp.tanh, 'sigmoid': _sigmoid, 'logw': _epi_logw}


def _mm_body(*refs, nk, epi, has_bias, has_gate, has_resid, precise):
    it = iter(refs)
    a_ref, w_ref = next(it), next(it)
    bias_ref = next(it) if has_bias else None
    gate_ref = next(it) if has_gate else None
    resid_ref = next(it) if has_resid else None
    o_ref = next(it)
    acc_ref = next(it) if nk > 1 else None

    def finish(y):
        if has_bias:
            y = y + bias_ref[...]
        y = _EPILOGUES[epi](y)
        if has_gate:
            y = y * gate_ref[...]
        if has_resid:
            y = resid_ref[...] + y
        o_ref[...] = y.astype(o_ref.dtype)

    if precise:
        part = _dot3(a_ref[...], w_ref[...])
    else:
        part = _dot(a_ref[...], w_ref[...])
    if nk == 1:
        finish(part)
    else:
        k = pl.program_id(2)

        @pl.when(k == 0)
        def _():
            acc_ref[...] = part

        @pl.when(k > 0)
        def _():
            acc_ref[...] += part

        @pl.when(k == nk - 1)
        def _():
            finish(acc_ref[...])


def matmul(a, w, *, bias=None, gate=None, resid=None, epi='none', out_dtype=F32,
           bm=1024, bn=1024, bk=None, a_koff=0, precise=False, w_single=False, w_cols=None,
           name='matmul'):
    m = a.shape[0]
    k, n = w.shape
    c0, n = (0, n) if w_cols is None else w_cols
    bm, bn = _fit(m, bm, 8), _fit(n, bn)
    assert c0 % bn == 0
    c0 //= bn
    bk = k if bk is None else _fit(k, bk)
    nk = k // bk
    if nk == 1:
        grid = (n // bn, m // bm)
        a_spec = pl.BlockSpec((bm, bk), lambda j, i: (i, a_koff))
        w_mode = dict(pipeline_mode=pl.Buffered(1)) if w_single else {}
        w_spec = pl.BlockSpec((bk, bn), lambda j, i: (0, j + c0), **w_mode)
        row_spec = pl.BlockSpec((1, bn), lambda j, i: (0, j))
        o_spec = pl.BlockSpec((bm, bn), lambda j, i: (i, j))
        sem = ('parallel', 'parallel')
        scratch = []
    else:
        grid = (n // bn, m // bm, nk)
        a_spec = pl.BlockSpec((bm, bk), lambda j, i, kk: (i, kk + a_koff * nk))
        w_spec = pl.BlockSpec((bk, bn), lambda j, i, kk: (kk, j + c0))
        row_spec = pl.BlockSpec((1, bn), lambda j, i, kk: (0, j))
        o_spec = pl.BlockSpec((bm, bn), lambda j, i, kk: (i, j))
        sem = ('parallel', 'parallel', 'arbitrary')
        scratch = [pltpu.VMEM((bm, bn), F32)]
    args, specs = [a, w], [a_spec, w_spec]
    for extra in (bias, gate):
        if extra is not None:
            args.append(extra.reshape(1, n).astype(F32))
            specs.append(row_spec)
    if resid is not None:
        args.append(resid)
        specs.append(o_spec)
    body = functools.partial(_mm_body, nk=nk, epi=epi, has_bias=bias is not None,
                             has_gate=gate is not None, has_resid=resid is not None,
                             precise=precise)
    return pl.pallas_call(
        body, out_shape=jax.ShapeDtypeStruct((m, n), out_dtype), grid=grid,
        in_specs=specs, out_specs=o_spec, scratch_shapes=scratch,
        compiler_params=_cparams(*sem), name=name)(*args)


def _normmod_body(x_ref, g_ref, sc_ref, sh_ref, o_ref):
    x = x_ref[...]
    y = x * lax.rsqrt(jnp.mean(x * x, axis=-1, keepdims=True) + EPS) * g_ref[...]
    o_ref[...] = (y * (1.0 + sc_ref[...]) + sh_ref[...]).astype(o_ref.dtype)


def normmod(x, g, sc, sh, out_dtype, bt=512):
    m, d = x.shape
    bt = min(bt, m)
    row = pl.BlockSpec((1, d), lambda i: (0, 0))
    blk = pl.BlockSpec((bt, d), lambda i: (i, 0))
    return pl.pallas_call(
        _normmod_body, out_shape=jax.ShapeDtypeStruct((m, d), out_dtype), grid=(m // bt,),
        in_specs=[blk, row, row, row], out_specs=blk,
        compiler_params=_cparams('parallel'), name='normmod')(
            x, g.reshape(1, d), sc.reshape(1, d), sh.reshape(1, d))


HALO = 16


def _dnproj_body(a_ref, ap_ref, an_ref, w_ref, cw_ref, o_ref, *, mode):
    i = pl.program_id(1)
    nb = pl.num_programs(1)
    bm = a_ref.shape[0]
    lhs = jnp.concatenate([ap_ref[...], a_ref[...], an_ref[...]], axis=0)
    y = jnp.dot(lhs, w_ref[...], preferred_element_type=F32)
    x = y[HALO:HALO + bm]
    row = lax.broadcasted_iota(jnp.int32, x.shape, 0)
    prev_row = jnp.where(i > 0, y[HALO - 1:HALO], 0.0)
    next_row = jnp.where(i < nb - 1, y[HALO + bm:HALO + bm + 1], 0.0)
    xm = jnp.where(row == 0, prev_row, pltpu.roll(x, 1, 0))
    xp = jnp.where(row == bm - 1, next_row, pltpu.roll(x, bm - 1, 0))
    y = _silu(cw_ref[0:1, :] * xm + cw_ref[1:2, :] * x + cw_ref[2:3, :] * xp)
    if mode == 'v':
        o_ref[...] = y.astype(o_ref.dtype)
        return
    scale = DN_DK ** -0.5 if mode == 'q' else 1.0
    for h in range(x.shape[1] // DN_DK):
        sl = slice(h * DN_DK, (h + 1) * DN_DK)
        yh = y[:, sl]
        inv = lax.rsqrt(jnp.sum(yh * yh, axis=-1, keepdims=True) + EPS)
        o_ref[:, sl] = (yh * (inv * scale)).astype(o_ref.dtype)


def dn_proj_conv(a, w, w_col0, conv_w, part, d, mode, bm=1024, bn=1024):
    m, k = a.shape
    bm, bn = _fit(m, bm, HALO), _fit(d, bn)
    assert w_col0 % bn == 0 and bm % HALO == 0
    c0 = (w_col0 + part * d) // bn
    cw0 = part * d // bn
    rh = bm // HALO
    lasth = m // HALO - 1
    return pl.pallas_call(
        functools.partial(_dnproj_body, mode=mode),
        out_shape=jax.ShapeDtypeStruct((m, d), F32), grid=(d // bn, m // bm),
        in_specs=[pl.BlockSpec((bm, k), lambda j, i: (i, 0)),
                  pl.BlockSpec((HALO, k), lambda j, i: (jnp.maximum(i * rh - 1, 0), 0)),
                  pl.BlockSpec((HALO, k), lambda j, i: (jnp.minimum((i + 1) * rh, lasth), 0)),
                  pl.BlockSpec((k, bn), lambda j, i: (0, c0 + j)),
                  pl.BlockSpec((3, bn), lambda j, i: (0, cw0 + j))],
        out_specs=pl.BlockSpec((bm, bn), lambda j, i: (i, j)),
        compiler_params=_cparams('parallel', 'parallel'), name='dn_proj_' + mode)(
            a, a, a, w, conv_w)


def _dnab_body(x_ref, alog_ref, dtb_ref, o_ref):
    x = x_ref[...]
    lane = lax.broadcasted_iota(jnp.int32, x.shape, 1)
    g = -jnp.exp(alog_ref[...]) * _softplus(x + dtb_ref[...])
    o_ref[...] = jnp.where(lane < x.shape[1] // 2, g, _sigmoid(x))


def dn_gates(pab, a_log, dt_bias, bt=1024):
    m, w = pab.shape
    bt = min(bt, m)
    zeros = jnp.zeros((w // 2,), F32)
    alog = jnp.concatenate([a_log.reshape(-1), zeros]).reshape(1, w)
    dtb = jnp.concatenate([dt_bias.reshape(-1), zeros]).reshape(1, w)
    row = pl.BlockSpec((1, w), lambda i: (0, 0))
    blk = pl.BlockSpec((bt, w), lambda i: (i, 0))
    return pl.pallas_call(
        _dnab_body, out_shape=jax.ShapeDtypeStruct((m, w), F32), grid=(m // bt,),
        in_specs=[blk, row, row], out_specs=blk,
        compiler_params=_cparams('parallel'), name='dn_gates')(pab, alog, dtb)


def _cast_jobs(ws, steps, step_index):
    args, in_specs, out_shapes, out_specs = [], [], [], []
    for stack, layer in ws:
        _, rows, cols = stack.shape
        ratio = next(r for r in (1, 2, 4, 8, 16) if steps % r == 0
                     and rows % (steps // r) == 0 and (rows // (steps // r)) % 16 == 0)
        blk = rows // (steps // ratio)
        args.append(stack)
        in_specs.append(pl.BlockSpec(
            (None, blk, cols),
            lambda g, j, ratio=ratio, layer=layer: (layer, step_index(g, j) // ratio, 0)))
        out_shapes.append(jax.ShapeDtypeStruct((rows, cols), BF16))
        out_specs.append(pl.BlockSpec(
            (blk, cols), lambda g, j, ratio=ratio: (step_index(g, j) // ratio, 0)))
    return args, in_specs, out_shapes, out_specs


def _split_refs(rest, n_fused, n_cast):
    a, b, c, d = n_fused, n_fused + n_cast, n_fused + n_cast + 2, n_fused + 2 * n_cast + 2
    return rest[:a], rest[a:b], rest[b:c], rest[c:d], rest[d:]


def _run_casts(cast_in, cast_out):
    for wi, wo in zip(cast_in, cast_out):
        wo[...] = wi[...].astype(BF16)


def _gated_head_norm(o, z, gain):
    y = o * lax.rsqrt(jnp.mean(o * o, axis=-1, keepdims=True) + EPS) * gain
    return y * _silu(z)


def _tri_masks(c, rev):
    ii = lax.broadcasted_iota(jnp.int32, (c, c), 0)
    jj = lax.broadcasted_iota(jnp.int32, (c, c), 1)
    if rev:
        return ii == jj, ii <= jj, ii < jj
    return ii == jj, ii >= jj, ii > jj


def _dn_intra(q, k, v, g_row, beta_row, masks, tri):
    eye, incl, strict = masks
    c = q[0].shape[0]
    heads = range(len(q))
    gc_col, beta_col, g_tot, decay = [], [], [], []
    for h in heads:
        g_bc = jnp.broadcast_to(g_row[h], (c, c))
        gcc = jnp.sum(jnp.where(incl, g_bc, 0.0), axis=1, keepdims=True)
        gcr = jnp.sum(jnp.where(eye, jnp.broadcast_to(gcc, (c, c)), 0.0), axis=0, keepdims=True)
        gc_col.append(gcc)
        beta_col.append(jnp.sum(jnp.where(eye, jnp.broadcast_to(beta_row[h], (c, c)), 0.0),
                                axis=1, keepdims=True))
        g_tot.append(jnp.sum(g_row[h], axis=1, keepdims=True))
        decay.append(jnp.where(incl, jnp.exp(jnp.where(incl, gcc - gcr, 0.0)), 0.0))
    kb = [k[h] * beta_col[h] for h in heads]
    kq = [_dot_nt(jnp.concatenate([kb[h], q[h]], axis=0), k[h]) for h in heads]
    kk = [kq[h][:c] for h in heads]
    qk = [kq[h][c:] for h in heads]
    n = [-jnp.where(strict, kk[h] * decay[h], 0.0) for h in heads]
    egc = [jnp.exp(gc_col[h]) for h in heads]
    rhs = [jnp.concatenate([kb[h] * egc[h], v[h] * beta_col[h]], axis=1) for h in heads]
    sol = _unit_tri_solve_many(n, rhs, tri)
    wq = [jnp.concatenate([sol[h][:, :DN_DK], q[h] * egc[h]], axis=0) for h in heads]
    u = [sol[h][:, DN_DK:] for h in heads]
    attn = [jnp.where(incl, qk[h] * decay[h], 0.0) for h in heads]
    kd = [k[h] * jnp.exp(g_tot[h] - gc_col[h]) for h in heads]
    gl = [jnp.exp(g_tot[h]) for h in heads]
    return wq, u, attn, kd, gl


def _dn_scan_body(q_ref, k_ref, v_ref, g_ref, b_ref, s0_ref, *rest, rev, nc, hg, fused, n_cast):
    fused_refs, cast_in, (o_ref, sf_ref), cast_out, scratch = _split_refs(
        rest, 3 if fused else 0, n_cast)
    if fused:
        of_ref, z_ref, gain_ref = fused_refs
    s_scr, wq_scr, u_scr, at_scr, kd_scr, gl_scr = scratch
    _run_casts(cast_in, cast_out)
    j = pl.program_id(1)

    @pl.when(j == 0)
    def _():
        s_scr[...] = s0_ref[...]

    c = DN_CHUNK
    masks = _tri_masks(c, rev)
    tri = _tri_levels(c, c)
    heads = range(hg)
    lanes = [slice(h * DN_DK, (h + 1) * DN_DK) for h in heads]

    def intra(ci, carry):
        rows = pl.ds(pl.multiple_of(ci * c, c), c)
        wq, u, attn, kd, gl = _dn_intra(
            [q_ref[rows, sl] for sl in lanes], [k_ref[rows, sl] for sl in lanes],
            [v_ref[rows, sl] for sl in lanes], [g_ref[ci, h:h + 1, :] for h in heads],
            [b_ref[ci, h:h + 1, :] for h in heads], masks, tri)
        for h in heads:
            wq_scr[ci, h] = wq[h].astype(BF16)
            u_scr[ci, h] = u[h]
            at_scr[ci, h] = attn[h].astype(BF16)
            kd_scr[ci, h] = kd[h].astype(BF16)
            gl_scr[ci, h] = jnp.broadcast_to(gl[h], (1, DN_DK))
        return carry

    lax.fori_loop(0, nc, intra, 0)

    def inter(ci, carry):
        cc = (nc - 1 - ci) if rev else ci
        rows = pl.ds(pl.multiple_of(cc * c, c), c)
        s = [s_scr[h] for h in heads]
        ws = [_dot(wq_scr[cc, h], s[h]) for h in heads]
        v_new = [u_scr[cc, h] - ws[h][:c] for h in heads]
        av = [_dot(at_scr[cc, h], v_new[h]) for h in heads]
        kv = [_dot_tn(kd_scr[cc, h], v_new[h]) for h in heads]
        for h in heads:
            o = ws[h][c:] + av[h]
            if fused:
                o = _gated_head_norm(o + of_ref[rows, lanes[h]], z_ref[rows, lanes[h]],
                                     gain_ref[:, lanes[h]])
            o_ref[rows, lanes[h]] = o.astype(o_ref.dtype)
            s_scr[h] = s[h] * gl_scr[cc, h] + kv[h]
        return carry

    lax.fori_loop(0, nc, inter, 0)

    @pl.when(j == pl.num_programs(1) - 1)
    def _():
        sf_ref[...] = s_scr[...]


def dn_scan(q, k, v, gbt, s0, direction, fuse=None, casts=(), hg=16, nc=2):
    m, d = q.shape
    h = d // DN_DK
    hg = min(hg, h)
    ng = h // hg
    nc = min(nc, m // DN_CHUNK)
    bt = nc * DN_CHUNK
    nb = m // bt
    rev = direction == 1
    gb4 = gbt.reshape(4 * ng, hg, m // DN_CHUNK, DN_CHUNK).transpose(0, 2, 1, 3)

    def tb(j):
        return (nb - 1 - j) if rev else j

    tok = pl.BlockSpec((bt, hg * DN_DK), lambda g, j: (tb(j), g))
    st = pl.BlockSpec((hg, DN_DK, DN_DK), lambda g, j: (g, 0, 0))
    g_spec = pl.BlockSpec((None, nc, hg, DN_CHUNK),
                          lambda g, j: (direction * ng + g, tb(j), 0, 0))
    b_spec = pl.BlockSpec((None, nc, hg, DN_CHUNK),
                          lambda g, j: ((2 + direction) * ng + g, tb(j), 0, 0))
    args, specs = [q, k, v, gb4, gb4, s0], [tok, tok, tok, g_spec, b_spec, st]
    if fuse is not None:
        o_other, p, z_part, gain = fuse
        args += [o_other, p, gain.reshape(1, d)]
        specs += [tok, pl.BlockSpec((bt, hg * DN_DK), lambda g, j: (tb(j), z_part * ng + g)),
                  pl.BlockSpec((1, hg * DN_DK), lambda g, j: (0, g))]
    c_args, c_in, c_shapes, c_out = _cast_jobs(casts, ng * nb, lambda g, j: g * nb + j)
    res = pl.pallas_call(
        functools.partial(_dn_scan_body, rev=rev, nc=nc, hg=hg, fused=fuse is not None,
                          n_cast=len(casts)),
        out_shape=(jax.ShapeDtypeStruct((m, d), F32 if fuse is None else BF16),
                   jax.ShapeDtypeStruct(s0.shape, F32), *c_shapes),
        grid=(ng, nb), in_specs=specs + c_in, out_specs=(tok, st, *c_out),
        scratch_shapes=[pltpu.VMEM((hg, DN_DK, DN_DK), F32),
                        pltpu.VMEM((nc, hg, 2 * DN_CHUNK, DN_DK), BF16),
                        pltpu.VMEM((nc, hg, DN_CHUNK, DN_DK), F32),
                        pltpu.VMEM((nc, hg, DN_CHUNK, DN_CHUNK), BF16),
                        pltpu.VMEM((nc, hg, DN_CHUNK, DN_DK), BF16),
                        pltpu.VMEM((nc, hg, 1, DN_DK), F32)],
        compiler_params=_cparams('parallel', 'arbitrary'),
        name='dn_scan_bwd' if rev else 'dn_scan_fwd')(*args, *c_args)
    return res[0], res[1], list(res[2:])


def _hg_chunk(q_raw, f_raw, v, lb, st, masks, rev):
    _, incl, _ = masks
    c = q_raw[0].shape[0]
    heads = range(len(q_raw))
    mid = (c - 1 - c // 2) if rev else c // 2
    last = 0 if rev else c - 1
    qm, km, qs, kd, gl = [], [], [], [], []
    for h in heads:
        q = _silu(q_raw[h])
        f = lb[h] + (1.0 - lb[h]) * _sigmoid(f_raw[h])
        k = 1.0 - f
        b = _cumsum_rows(jnp.log(f), rev)
        m = b[mid:mid + 1, :]
        b_last = b[last:last + 1, :]
        qm.append(q * jnp.exp(b - m))
        km.append(k * jnp.exp(m - b))
        qs.append(q * jnp.exp(b))
        kd.append(k * jnp.exp(b_last - b))
        gl.append(jnp.exp(b_last))
    a_qk = [jnp.where(incl, _dot_nt(qm[h], km[h]), 0.0) for h in heads]
    inter = [_dot_nt(qs[h], st[h]) for h in heads]
    kv = [_dot_tn(v[h], kd[h]) for h in heads]
    intra = [_dot(a_qk[h], v[h]) for h in heads]
    o = [inter[h] + intra[h] for h in heads]
    st_new = [st[h] * gl[h] + kv[h] for h in heads]
    return o, st_new


def _hg_scan_body(q_ref, f_ref, v_ref, lb_ref, s0_ref, *rest, rev, nc, hg, fused, n_cast):
    fused_refs, cast_in, (o_ref, sf_ref), cast_out, (s_scr,) = _split_refs(
        rest, 3 if fused else 0, n_cast)
    if fused:
        of_ref, z_ref, gain_ref = fused_refs
    _run_casts(cast_in, cast_out)
    j = pl.program_id(1)

    @pl.when(j == 0)
    def _():
        s_scr[...] = s0_ref[...]

    c = HG_CHUNK
    masks = _tri_masks(c, rev)
    heads = range(hg)
    lanes = [slice(h * HG_DK, (h + 1) * HG_DK) for h in heads]

    def chunk(ci, carry):
        cc = (nc - 1 - ci) if rev else ci
        rows = pl.ds(pl.multiple_of(cc * c, c), c)
        o, s_new = _hg_chunk([q_ref[rows, sl] for sl in lanes], [f_ref[rows, sl] for sl in lanes],
                             [v_ref[rows, sl] for sl in lanes], [lb_ref[:, sl] for sl in lanes],
                             [s_scr[h] for h in heads], masks, rev)
        for h in heads:
            oh = o[h]
            if fused:
                oh = _gated_head_norm(oh + of_ref[rows, lanes[h]], z_ref[rows, lanes[h]],
                                      gain_ref[:, lanes[h]])
            o_ref[rows, lanes[h]] = oh.astype(o_ref.dtype)
            s_scr[h] = s_new[h]
        return carry

    lax.fori_loop(0, nc, chunk, 0)

    @pl.when(j == pl.num_programs(1) - 1)
    def _():
        sf_ref[...] = s_scr[...]


def hg_scan(p, lower, s0, direction, d, fuse=None, casts=(), hg=16, nc=8):
    m = p.shape[0]
    h = d // HG_DK
    hg = min(hg, h)
    ng = h // hg
    nc = min(nc, m // HG_CHUNK)
    bt = nc * HG_CHUNK
    nb = m // bt
    rev = direction == 1

    def tb(j):
        return (nb - 1 - j) if rev else j

    def part(pi):
        return pl.BlockSpec((bt, hg * HG_DK), lambda g, j: (tb(j), pi * ng + g))

    st = pl.BlockSpec((hg, HG_DK, HG_DK), lambda g, j: (g, 0, 0))
    row = pl.BlockSpec((1, hg * HG_DK), lambda g, j: (0, g))
    tok = pl.BlockSpec((bt, hg * HG_DK), lambda g, j: (tb(j), g))
    args = [p, p, p, lower.reshape(1, d), s0]
    specs = [part(0), part(1 + direction), part(3), row, st]
    if fuse is not None:
        o_other, gain = fuse
        args += [o_other, p, gain.reshape(1, d)]
        specs += [tok, part(4), row]
    c_args, c_in, c_shapes, c_out = _cast_jobs(casts, ng * nb, lambda g, j: g * nb + j)
    res = pl.pallas_call(
        functools.partial(_hg_scan_body, rev=rev, nc=nc, hg=hg, fused=fuse is not None,
                          n_cast=len(casts)),
        out_shape=(jax.ShapeDtypeStruct((m, d), F32 if fuse is None else BF16),
                   jax.ShapeDtypeStruct(s0.shape, F32), *c_shapes),
        grid=(ng, nb), in_specs=specs + c_in, out_specs=(tok, st, *c_out),
        scratch_shapes=[pltpu.VMEM((hg, HG_DK, HG_DK), F32)],
        compiler_params=_cparams('parallel', 'arbitrary'),
        name='hg_scan_bwd' if rev else 'hg_scan_fwd')(*args, *c_args)
    return res[0], res[1], list(res[2:])


def _shiftmix_body(x_ref, prev_ref, next_ref, mu_ref, *o_refs, width):
    i = pl.program_id(0)
    nb = pl.num_programs(0)
    bt, d = x_ref.shape
    dq = d // 4
    col = lax.broadcasted_iota(jnp.int32, (bt, dq), 0) % width
    for qi in range(4):
        sl = slice(qi * dq, (qi + 1) * dq)
        x = x_ref[:, sl]
        if qi == 0:
            sh = jnp.where(col == 0, 0.0, pltpu.roll(x, 1, 0))
        elif qi == 1:
            sh = jnp.where(col == width - 1, 0.0, pltpu.roll(x, bt - 1, 0))
        elif qi == 2:
            edge = jnp.where(i > 0, prev_ref[:, sl], 0.0)
            sh = edge if bt == width else jnp.concatenate([edge, x[:bt - width]], axis=0)
        else:
            edge = jnp.where(i < nb - 1, next_ref[:, sl], 0.0)
            sh = edge if bt == width else jnp.concatenate([x[width:], edge], axis=0)
        xx = sh - x
        for n, o_ref in enumerate(o_refs):
            o_ref[:, sl] = (x + xx * mu_ref[n:n + 1, sl]).astype(o_ref.dtype)


def shift_mix(h, mu, width, bt=128):
    m, d = h.shape
    bt = max(min(bt, m), width)
    rw = bt // width
    lastw = m // width - 1
    blk = pl.BlockSpec((bt, d), lambda i: (i, 0))
    out = jax.ShapeDtypeStruct((m, d), BF16)
    return pl.pallas_call(
        functools.partial(_shiftmix_body, width=width),
        out_shape=(out,) * 6, grid=(m // bt,),
        in_specs=[blk,
                  pl.BlockSpec((width, d), lambda i: (jnp.maximum(i * rw - 1, 0), 0)),
                  pl.BlockSpec((width, d), lambda i: (jnp.minimum((i + 1) * rw, lastw), 0)),
                  pl.BlockSpec((6, d), lambda i: (0, 0))],
        out_specs=(blk,) * 6, compiler_params=_cparams('arbitrary'), name='shift_mix')(h, h, h, mu)


def _halves(x, lo_mask):
    return jnp.concatenate([jnp.where(lo_mask, x, 0.0), jnp.where(lo_mask, 0.0, x)], axis=0)


def _head_sum(x, lo_mask):
    s_lo = jnp.sum(jnp.where(lo_mask, x, 0.0), axis=-1, keepdims=True)
    s_hi = jnp.sum(jnp.where(lo_mask, 0.0, x), axis=-1, keepdims=True)
    return jnp.where(lo_mask, s_lo, s_hi)


def _rw_intra(r, k, v, a_sig, logw, k_k, k_a, masks, tri, rev):
    lo, strict2, incl_cat = masks
    c = r[0].shape[0]
    groups = range(len(r))
    mid = (c - 1 - c // 2) if rev else c // 2
    last = 0 if rev else c - 1
    am2, bm2, km2, v2, rm, ae2, rg, bd2, kd2, gl = ([] for _ in range(10))
    for g in groups:
        kq = k[g] * k_k[g]
        kk = kq * lax.rsqrt(_head_sum(kq * kq, lo) + EPS)
        b = kk * a_sig[g]
        kd = k[g] * (1.0 + (a_sig[g] - 1.0) * k_a[g])
        cs = _cumsum_rows(logw[g], rev)
        ce = cs - logw[g]
        m = cs[mid:mid + 1, :]
        c_last = cs[last:last + 1, :]
        e_mc = jnp.exp(m - cs)
        dec = jnp.exp(c_last - cs)
        am2.append(_halves(-kk * jnp.exp(ce - m), lo))
        bm2.append(_halves(b * e_mc, lo))
        km2.append(_halves(kd * e_mc, lo))
        v2.append(_halves(v[g], lo))
        rm.append(r[g] * jnp.exp(cs - m))
        ae2.append(_halves(-kk * jnp.exp(ce), lo))
        rg.append(r[g] * jnp.exp(cs))
        bd2.append(_halves(b * dec, lo))
        kd2.append(_halves(kd * dec, lo))
        gl.append(jnp.exp(c_last))
    pairs = [_dot_nt(jnp.concatenate([am2[g], rm[g]], axis=0),
                     jnp.concatenate([bm2[g], km2[g]], axis=0)) for g in groups]
    n = [jnp.where(strict2, pairs[g][:2 * c, :2 * c], 0.0) for g in groups]
    a_ak = [jnp.where(strict2, pairs[g][:2 * c, 2 * c:], 0.0) for g in groups]
    a_rb = [jnp.where(incl_cat, pairs[g][2 * c:, :2 * c], 0.0) for g in groups]
    a_rk = [jnp.where(incl_cat, pairs[g][2 * c:, 2 * c:], 0.0) for g in groups]
    both = [_dot(jnp.concatenate([a_ak[g], a_rk[g]], axis=0), v2[g]) for g in groups]
    akv = [both[g][:2 * c] for g in groups]
    ov = [both[g][2 * c:] for g in groups]
    kv = [_dot_tn(v2[g], kd2[g]) for g in groups]
    sol = _unit_tri_solve_many(n, [jnp.concatenate([ae2[g], akv[g]], axis=1) for g in groups],
                               tri)
    wr = [jnp.concatenate([sol[g][:, :LANES], rg[g]], axis=0) for g in groups]
    uv2 = [sol[g][:, LANES:] for g in groups]
    return wr, uv2, a_rb, ov, bd2, kv, gl


def _rw_output(o, r, k, v, a_sum, gate, ln_w, ln_b, k_a, r_k, lo):
    inv_n = 1.0 / RW_HEAD
    oc = o - _head_sum(o, lo) * inv_n
    y = oc * lax.rsqrt(_head_sum(oc * oc, lo) * inv_n + RW_LN_EPS) * ln_w + ln_b
    k_sum = k * (2.0 + (a_sum - 2.0) * k_a)
    bonus = _head_sum(r * k_sum * r_k, lo) * v
    return (y + bonus) * gate


def _rw_masks(c, rev):
    lo = lax.broadcasted_iota(jnp.int32, (1, LANES), 1) < RW_HEAD
    ii = lax.broadcasted_iota(jnp.int32, (2 * c, 2 * c), 0)
    jj = lax.broadcasted_iota(jnp.int32, (2 * c, 2 * c), 1)
    same = (ii // c) == (jj // c)
    il, jl = ii % c, jj % c
    strict2 = same & ((il < jl) if rev else (il > jl))
    ic = lax.broadcasted_iota(jnp.int32, (c, 2 * c), 0)
    jc = lax.broadcasted_iota(jnp.int32, (c, 2 * c), 1) % c
    incl_cat = (ic <= jc) if rev else (ic >= jc)
    return lo, strict2, incl_cat


def _rw_scan_body(r_ref, k_ref, v_ref, a_ref, lw_ref, kk_ref, ka_ref, s0_ref, *rest,
                  rev, nc, hg, fused, n_cast):
    fused_refs, cast_in, (o_ref, sf_ref), cast_out, scratch = _split_refs(
        rest, 6 if fused else 0, n_cast)
    if fused:
        of_ref, ao_ref, gate_ref, lnw_ref, lnb_ref, rk_ref = fused_refs
    s_scr, wr_scr, uv_scr, arb_scr, ov_scr, bd_scr, kv_scr, gl_scr = scratch
    _run_casts(cast_in, cast_out)
    j = pl.program_id(1)

    @pl.when(j == 0)
    def _():
        s_scr[...] = s0_ref[...]

    c = RW_CHUNK
    masks = _rw_masks(c, rev)
    tri = _tri_levels(2 * c, c)
    groups = range(hg)
    lanes = [slice(g * LANES, (g + 1) * LANES) for g in groups]

    def intra(ci, carry):
        rows = pl.ds(pl.multiple_of(ci * c, c), c)
        wr, uv2, a_rb, ov, bd2, kv, gl = _rw_intra(
            [r_ref[rows, sl] for sl in lanes], [k_ref[rows, sl] for sl in lanes],
            [v_ref[rows, sl] for sl in lanes], [a_ref[rows, sl] for sl in lanes],
            [lw_ref[rows, sl] for sl in lanes], [kk_ref[:, sl] for sl in lanes],
            [ka_ref[:, sl] for sl in lanes], masks, tri, rev)
        for g in groups:
            wr_scr[ci, g] = wr[g].astype(BF16)
            uv_scr[ci, g] = uv2[g]
            arb_scr[ci, g] = a_rb[g].astype(BF16)
            ov_scr[ci, g] = ov[g]
            bd_scr[ci, g] = bd2[g].astype(BF16)
            kv_scr[ci, g] = kv[g]
            gl_scr[ci, g] = gl[g]
        return carry

    lax.fori_loop(0, nc, intra, 0)

    def inter(ci, carry):
        cc = (nc - 1 - ci) if rev else ci
        rows = pl.ds(pl.multiple_of(cc * c, c), c)
        st = [s_scr[g] for g in groups]
        ws = [_dot_nt(wr_scr[cc, g], st[g]) for g in groups]
        u2 = [ws[g][:2 * c] + uv_scr[cc, g] for g in groups]
        au = [_dot(arb_scr[cc, g], u2[g]) for g in groups]
        ub = [_dot_tn(u2[g], bd_scr[cc, g]) for g in groups]
        for g in groups:
            o = ws[g][2 * c:] + au[g] + ov_scr[cc, g]
            if fused:
                sl = lanes[g]
                o = _rw_output(o + of_ref[rows, sl], r_ref[rows, sl], k_ref[rows, sl],
                               v_ref[rows, sl], a_ref[rows, sl] + ao_ref[rows, sl],
                               gate_ref[rows, sl], lnw_ref[:, sl], lnb_ref[:, sl], ka_ref[:, sl],
                               rk_ref[:, sl], masks[0])
            o_ref[rows, lanes[g]] = o.astype(o_ref.dtype)
            s_scr[g] = st[g] * gl_scr[cc, g] + kv_scr[cc, g] + ub[g]
        return carry

    lax.fori_loop(0, nc, inter, 0)

    @pl.when(j == pl.num_programs(1) - 1)
    def _():
        sf_ref[...] = s_scr[...]


def rw_scan(r, k, v, a_sig, logw, k_k, k_a, s0, direction, fuse=None, casts=(), hg=16, nc=2):
    m, d = r.shape
    npair = d // LANES
    hg = min(hg, npair)
    ng = npair // hg
    nc = min(nc, m // RW_CHUNK)
    bt = nc * RW_CHUNK
    nb = m // bt
    rev = direction == 1

    def tb(j):
        return (nb - 1 - j) if rev else j

    tok = pl.BlockSpec((bt, hg * LANES), lambda g, j: (tb(j), g))
    row = pl.BlockSpec((1, hg * LANES), lambda g, j: (0, g))
    st = pl.BlockSpec((hg, LANES, LANES), lambda g, j: (g, 0, 0))
    args = [r, k, v, a_sig, logw, k_k.reshape(1, d), k_a.reshape(1, d), s0]
    specs = [tok, tok, tok, tok, tok, row, row, st]
    if fuse is not None:
        args += list(fuse[:3]) + [t.reshape(1, d) for t in fuse[3:]]
        specs += [tok, tok, tok, row, row, row]
    c_args, c_in, c_shapes, c_out = _cast_jobs(casts, ng * nb, lambda g, j: g * nb + j)
    res = pl.pallas_call(
        functools.partial(_rw_scan_body, rev=rev, nc=nc, hg=hg, fused=fuse is not None,
                          n_cast=len(casts)),
        out_shape=(jax.ShapeDtypeStruct((m, d), F32 if fuse is None else BF16),
                   jax.ShapeDtypeStruct(s0.shape, F32), *c_shapes),
        grid=(ng, nb), in_specs=specs + c_in, out_specs=(tok, st, *c_out),
        scratch_shapes=[pltpu.VMEM((hg, LANES, LANES), F32),
                        pltpu.VMEM((nc, hg, 3 * RW_CHUNK, LANES), BF16),
                        pltpu.VMEM((nc, hg, 2 * RW_CHUNK, LANES), F32),
                        pltpu.VMEM((nc, hg, RW_CHUNK, 2 * RW_CHUNK), BF16),
                        pltpu.VMEM((nc, hg, RW_CHUNK, LANES), F32),
                        pltpu.VMEM((nc, hg, 2 * RW_CHUNK, LANES), BF16),
                        pltpu.VMEM((nc, hg, LANES, LANES), F32),
                        pltpu.VMEM((nc, hg, 1, LANES), F32)],
        compiler_params=_cparams('parallel', 'arbitrary'),
        name='rw_scan_bwd' if rev else 'rw_scan_fwd')(*args, *c_args)
    return res[0], res[1], list(res[2:])


def _convglu_body(u_ref, prev_ref, next_ref, v_ref, cw_ref, cb_ref, o_ref, *, width):
    i = pl.program_id(0)
    nb = pl.num_programs(0)
    x = u_ref[...].astype(F32)
    bt = x.shape[0]
    up = jnp.where(i > 0, prev_ref[...].astype(F32), 0.0)
    dn = jnp.where(i < nb - 1, next_ref[...].astype(F32), 0.0)
    if bt > width:
        up = jnp.concatenate([up, x[:bt - width]], axis=0)
        dn = jnp.concatenate([x[width:], dn], axis=0)
    col = lax.broadcasted_iota(jnp.int32, x.shape, 0) % width
    rows = (up, x, dn)

    def column(kw):
        return sum(cw_ref[3 * r + kw:3 * r + kw + 1, :] * rows[r] for r in range(3))

    left = jnp.where(col == 0, 0.0, pltpu.roll(column(0), 1, 0))
    right = jnp.where(col == width - 1, 0.0, pltpu.roll(column(2), bt - 1, 0))
    h = column(1) + left + right + cb_ref[...]
    c = math.sqrt(2.0 / math.pi)
    t = jnp.tanh(h * (2.0 * c + (8.0 * 0.044715 * c) * (h * h)))
    o_ref[...] = ((h + h * t) * v_ref[...].astype(F32)).astype(o_ref.dtype)


def conv_glu_gate(uv, conv_w, conv_b, width, f, bt=512, bc=1024):
    m = uv.shape[0]
    bt = max(min(bt, m), width)
    bc = _fit(f, bc)
    rw = bt // width
    lastw = m // width - 1
    voff = f // bc
    return pl.pallas_call(
        functools.partial(_convglu_body, width=width),
        out_shape=jax.ShapeDtypeStruct((m, f), BF16), grid=(m // bt, f // bc),
        in_specs=[pl.BlockSpec((bt, bc), lambda i, j: (i, j)),
                  pl.BlockSpec((width, bc), lambda i, j: (jnp.maximum(i * rw - 1, 0), j)),
                  pl.BlockSpec((width, bc), lambda i, j: (jnp.minimum((i + 1) * rw, lastw), j)),
                  pl.BlockSpec((bt, bc), lambda i, j: (i, voff + j)),
                  pl.BlockSpec((9, bc), lambda i, j: (0, j)),
                  pl.BlockSpec((1, bc), lambda i, j: (0, j))],
        out_specs=pl.BlockSpec((bt, bc), lambda i, j: (i, j)),
        compiler_params=_cparams('parallel', 'parallel'), name='conv_glu')(
            uv, uv, uv, uv, 0.5 * conv_w.reshape(9, f), 0.5 * conv_b.reshape(1, f))


def _modulation(conds, down, up, bias):
    low = matmul(conds, down, precise=True, bm=8, bn=512, name='ada_down')
    return matmul(low, up, bias=bias, precise=True, bm=8, bn=2048, name='ada_up')


def _dn_mixer(streams, w, d, last, casts):
    heads = d // DN_DK
    w_in = w['w_in']
    if w_in.dtype == BF16:
        w_main = w_in
        ab = dict(w=w_in, w_cols=(4 * d, 4 * heads))
    else:
        w_main = w_in[:, :4 * d].astype(BF16)
        ab = dict(w=w_in[:, 4 * d:].astype(BF16))
    gain = jnp.tile(w['norm_g'], heads)
    prepped = []
    for h in streams:
        q, k, v = (dn_proj_conv(h, w_main, 0, w['conv_w'], part, d, mode)
                   for part, mode in enumerate('qkv'))
        z = matmul(h, w_main, w_cols=(3 * d, d), name='dn_in_z')
        pab = matmul(h, bn=4 * heads, name='dn_in_ab', **ab)
        gb = dn_gates(pab, w['a_log'], w['dt_bias'])
        prepped.append((z, q, k, v, gb.T))
    states = [jnp.zeros((heads, DN_DK, DN_DK), F32)] * 2
    outs, done = [], [[], []]
    for si, (z, q, k, v, gbt) in enumerate(prepped):
        jobs = casts if si == 1 else ((), ())
        o_f, states[0], done[0] = dn_scan(q, k, v, gbt, states[0], 0, casts=jobs[0])
        y, states[1], done[1] = dn_scan(q, k, v, gbt, states[1], 1, fuse=(o_f, z, 0, gain),
                                        casts=jobs[1])
        outs.append(None if si == 0 and last else y)
    return outs, done[0], done[1]


def _hg_mixer(streams, w, d, last, casts):
    heads = d // HG_DK
    states = [jnp.zeros((heads, HG_DK, HG_DK), F32)] * 2
    outs, done = [], [[], []]
    for si, h in enumerate(streams):
        jobs = casts if si == 1 else ((), ())
        p = matmul(h, w['w_in'], name='hg_in')
        o_f, states[0], done[0] = hg_scan(p, w['lower'], states[0], 0, d, casts=jobs[0])
        y, states[1], done[1] = hg_scan(p, w['lower'], states[1], 1, d,
                                        fuse=(o_f, w['norm_g']), casts=jobs[1])
        outs.append(None if si == 0 and last else y)
    return outs, done[0], done[1]


def _rw_mixer(streams, widths, w, d, last, casts):
    w_rkv = w['w_rkv']
    w1 = jnp.concatenate([w['w1'][0], w['w1'][1]], axis=1).astype(BF16)
    a1 = jnp.concatenate([w['a1'][0], w['a1'][1]], axis=1).astype(BF16)
    w2 = w['w2'].astype(BF16)
    a2 = w['a2'].astype(BF16)
    lora = w['w1'].shape[-1]
    gl = w['g1'].shape[-1]
    glp = -(-gl // LANES) * LANES
    g1 = jnp.pad(w['g1'], ((0, 0), (0, glp - gl))).astype(BF16)
    g2 = jnp.pad(w['g2'], ((0, glp - gl), (0, 0))).astype(BF16)
    states = [jnp.zeros((d // LANES, LANES, LANES), F32)] * 2
    outs, done = [], [[], []]
    for si, (h, width) in enumerate(zip(streams, widths)):
        jobs = casts if si == 1 else ((), ())
        xr, xw, xk, xv, xa, xg = shift_mix(h, w['mu'], width)
        r = matmul(xr, w_rkv[0], name='rw_r')
        k = matmul(xk, w_rkv[1], name='rw_k')
        v = matmul(xv, w_rkv[2], name='rw_v')
        tw = matmul(xw, w1, epi='tanh', out_dtype=BF16, bn=2 * lora, name='rw_w1')
        ta = matmul(xa, a1, out_dtype=BF16, bn=2 * lora, name='rw_a1')
        tg = matmul(xg, g1, epi='sigmoid', out_dtype=BF16, name='rw_g1')
        gate = matmul(tg, g2, name='rw_g2')
        logw, a_sig = [], []
        for direction in range(2):
            logw.append(matmul(tw, w2[direction], bias=w['w0'][direction], epi='logw',
                               a_koff=direction, name='rw_w2'))
            a_sig.append(matmul(ta, a2[direction], bias=w['a0'][direction], epi='sigmoid',
                                a_koff=direction, name='rw_a2'))
        o_f, states[0], done[0] = rw_scan(r, k, v, a_sig[0], logw[0], w['k_k'], w['k_a'],
                                          states[0], 0, casts=jobs[0])
        y, states[1], done[1] = rw_scan(r, k, v, a_sig[1], logw[1], w['k_k'], w['k_a'],
                                        states[1], 1, casts=jobs[1],
                                        fuse=(o_f, a_sig[0], gate, w['lnx_w'], w['lnx_b'],
                                              w['r_k'].reshape(-1)))
        outs.append(None if si == 0 and last else y)
    return outs, done[0], done[1]


def kernel(x, c, ctx, c_ctx, ada_down, ada_up, ada_b, norm1_g, norm2_g, ffn_w_up, ffn_conv_w, ffn_conv_b, ffn_w_down, dn_w_in, dn_conv_w, dn_a_log, dn_dt_bias, dn_norm_g, dn_w_out, hg_w_in, hg_lower, hg_norm_g, hg_w_out, rw_mu, rw_w_rkv, rw_w0, rw_w1, rw_w2, rw_a0, rw_a1, rw_a2, rw_g1, rw_g2, rw_k_k, rw_k_a, rw_r_k, rw_lnx_w, rw_lnx_b, rw_w_out, final_g):
    _, seq, d = x.shape
    depth = ada_down.shape[0]
    n_ctx = ctx.shape[1]
    f = ffn_w_down.shape[1]
    xl, xc = x[0], ctx[0]
    sm = jax.nn.softmax(hg_lower.astype(F32), axis=0)
    lower_bounds = jnp.cumsum(sm, axis=0) - sm[0]
    conds = jnp.zeros((8, d), F32).at[0].set(jax.nn.silu(c[0])).at[1].set(jax.nn.silu(c_ctx))
    widths = (n_ctx, GRID_W)
    w_outs = {0: dn_w_out, 1: hg_w_out, 2: rw_w_out}
    w_ins = {0: dn_w_in, 1: hg_w_in, 2: rw_w_rkv.reshape(rw_w_rkv.shape[0], 3 * d, d)}

    def in_proj(i):
        return w_ins[i % 3], i // 3

    first, _ = in_proj(0)
    w_in_bf = first[0] if first.shape[2] % LANES else first[0].astype(BF16)
    for i in range(depth):
        kind, j = i % 3, i // 3
        last = i == depth - 1
        mod = _modulation(conds, ada_down[i], ada_up[i], ada_b[i])
        mods = [[mod[row, n * d:(n + 1) * d] for n in range(6)] for row in (1, 0)]
        xs = [xc, xl]
        h_dtype = F32 if kind == 2 else BF16
        hs = [normmod(t, norm1_g[i], mm[1], mm[0], h_dtype) for t, mm in zip(xs, mods)]
        nxt = in_proj(i + 1) if not last else None
        hide_next = nxt is not None and nxt[0].shape[2] % LANES == 0
        casts = ([(w_outs[kind], j), (ffn_w_up, i)],
                 [(ffn_w_down, i)] + ([nxt] if hide_next else []))
        if kind == 0:
            w = dict(w_in=w_in_bf, conv_w=dn_conv_w[j], a_log=dn_a_log[j],
                     dt_bias=dn_dt_bias[j], norm_g=dn_norm_g[j])
            ys, done_f, done_b = _dn_mixer(hs, w, d, last, casts)
        elif kind == 1:
            w = dict(w_in=w_in_bf, lower=lower_bounds[i], norm_g=hg_norm_g[j])
            ys, done_f, done_b = _hg_mixer(hs, w, d, last, casts)
        else:
            w = dict(mu=rw_mu[j], w_rkv=w_in_bf.reshape(3, d, d), w0=rw_w0[j], w1=rw_w1[j],
                     w2=rw_w2[j], a0=rw_a0[j], a1=rw_a1[j], a2=rw_a2[j], g1=rw_g1[j],
                     g2=rw_g2[j], k_k=rw_k_k[j], k_a=rw_k_a[j], r_k=rw_r_k[j],
                     lnx_w=rw_lnx_w[j], lnx_b=rw_lnx_b[j])
            ys, done_f, done_b = _rw_mixer(hs, widths, w, d, last, casts)
        w_out, w_up = done_f
        w_down = done_b[0]
        if hide_next:
            w_in_bf = done_b[1]
        elif nxt is not None:
            w_in_bf = nxt[0][nxt[1]]
        new = []
        for t, y, mm, width in zip(xs, ys, mods, widths):
            if y is None:
                new.append(t)
                continue
            t = matmul(y, w_out, gate=mm[2], resid=t, name='mix_out')
            h2 = normmod(t, norm2_g[i], mm[4], mm[3], BF16)
            uv = matmul(h2, w_up, out_dtype=BF16, name='ffn_up')
            gated = conv_glu_gate(uv, ffn_conv_w[i], ffn_conv_b[i], width, f)
            t = matmul(gated, w_down, gate=mm[5], resid=t, bm=512, bn=1024, w_single=True,
                       name='ffn_down')
            new.append(t)
        xc, xl = new
    zeros = jnp.zeros((d,), F32)
    return normmod(xl, final_g, zeros, zeros, F32)[None]
```

```python
import functools
import math

import jax
import jax.numpy as jnp
from jax import lax
from jax.experimental import pallas as pl
from jax.experimental.pallas import tpu as pltpu

F32 = jnp.float32
BF16 = jnp.bfloat16

EPS = 1e-6
RW_LN_EPS = 64e-5
GRID_W = 64
DN_DK = 128
DN_CHUNK = 128
HG_DK = 128
HG_CHUNK = 32
RW_HEAD = 64
RW_CHUNK = 64
LANES = 128
VMEM_LIMIT = 60 * 1024 * 1024


def _fit(n, b, unit=LANES):
    if n <= b:
        return n
    for cand in range(b - b % unit, 0, -unit):
        if n % cand == 0:
            return cand
    raise ValueError((n, b, unit))


def _cparams(*sem):
    return pltpu.CompilerParams(dimension_semantics=sem, vmem_limit_bytes=VMEM_LIMIT)


def _sigmoid(x):
    return 1.0 / (1.0 + jnp.exp(-x))


def _silu(x):
    h = 0.5 * x
    return h + h * jnp.tanh(h)


def _softplus(x):
    return jnp.maximum(x, 0.0) + jnp.log(1.0 + jnp.exp(-jnp.abs(x)))


def _dot(a, b, dims=(((1,), (0,)), ((), ()))):
    return lax.dot_general(a.astype(BF16), b.astype(BF16), dims, preferred_element_type=F32)


def _dot_nt(a, b):
    return _dot(a, b, (((1,), (1,)), ((), ())))


def _dot_tn(a, b):
    return _dot(a, b, (((0,), (0,)), ((), ())))


def _split2(x):
    hi = x.astype(BF16)
    lo = (x - hi.astype(F32)).astype(BF16)
    return hi, lo


def _dot3(a, b, dims=(((1,), (0,)), ((), ()))):
    ah, al = _split2(a)
    bh, bl = _split2(b)
    d = functools.partial(lax.dot_general, dimension_numbers=dims, preferred_element_type=F32)
    return d(ah, bh) + (d(ah, bl) + d(al, bh))


def _tri_levels(c, size):
    ii = lax.broadcasted_iota(jnp.int32, (c, c), 0)
    jj = lax.broadcasted_iota(jnp.int32, (c, c), 1)
    levels = []
    s = 1
    while (1 << s) < size:
        lo = (ii >> s) != (jj >> s)
        hi = (ii >> (s + 1)) == (jj >> (s + 1))
        levels.append(lo & hi)
        s += 1
    return (ii >> 1) == (jj >> 1), ii == jj, levels


def _unit_tri_solve_many(ns, rhss, tri):
    pair, eye, levels = tri
    units = range(len(ns))
    ts = [jnp.where(pair, jnp.where(eye, 1.0, n), 0.0) for n in ns]
    for lmask in levels:
        ls = [jnp.where(lmask, n, 0.0) for n in ns]
        lt = [_dot(ls[i], ts[i]) for i in units]
        ts = [ts[i] + _dot(ts[i], lt[i]) for i in units]
    return [_dot(ts[i], rhss[i]) for i in units]


def _cumsum_rows(x, rev):
    c = x.shape[0]
    row = lax.broadcasted_iota(jnp.int32, x.shape, 0)
    s = 1
    while s < c:
        if rev:
            x = x + jnp.where(row < c - s, pltpu.roll(x, c - s, 0), 0.0)
        else:
            x = x + jnp.where(row >= s, pltpu.roll(x, s, 0), 0.0)
        s *= 2
    return x


def _epi_none(y):
    return y


def _epi_logw(y):
    return -jnp.exp(-_softplus(-y) - 0.5)


_EPILOGUES = {'none': _epi_none, 'tanh': jnp.tanh, 'sigmoid': _sigmoid, 'logw': _epi_logw}


def _mm_body(*refs, nk, epi, has_bias, has_gate, has_resid, precise):
    it = iter(refs)
    a_ref, w_ref = next(it), next(it)
    bias_ref = next(it) if has_bias else None
    gate_ref = next(it) if has_gate else None
    resid_ref = next(it) if has_resid else None
    o_ref = next(it)
    acc_ref = next(it) if nk > 1 else None

    def finish(y):
        if has_bias:
            y = y + bias_ref[...]
        y = _EPILOGUES[epi](y)
        if has_gate:
            y = y * gate_ref[...]
        if has_resid:
            y = resid_ref[...] + y
        o_ref[...] = y.astype(o_ref.dtype)

    if precise:
        part = _dot3(a_ref[...], w_ref[...])
    else:
        part = _dot(a_ref[...], w_ref[...])
    if nk == 1:
        finish(part)
    else:
        k = pl.program_id(2)

        @pl.when(k == 0)
        def _():
            acc_ref[...] = part

        @pl.when(k > 0)
        def _():
            acc_ref[...] += part

        @pl.when(k == nk - 1)
        def _():
            finish(acc_ref[...])


def matmul(a, w, *, bias=None, gate=None, resid=None, epi='none', out_dtype=F32,
           bm=1024, bn=1024, bk=None, a_koff=0, precise=False, w_single=False, w_cols=None,
           name='matmul'):
    m = a.shape[0]
    k, n = w.shape
    c0, n = (0, n) if w_cols is None else w_cols
    bm, bn = _fit(m, bm, 8), _fit(n, bn)
    assert c0 % bn == 0
    c0 //= bn
    bk = k if bk is None else _fit(k, bk)
    nk = k // bk
    if nk == 1:
        grid = (n // bn, m // bm)
        a_spec = pl.BlockSpec((bm, bk), lambda j, i: (i, a_koff))
        w_mode = dict(pipeline_mode=pl.Buffered(1)) if w_single else {}
        w_spec = pl.BlockSpec((bk, bn), lambda j, i: (0, j + c0), **w_mode)
        row_spec = pl.BlockSpec((1, bn), lambda j, i: (0, j))
        o_spec = pl.BlockSpec((bm, bn), lambda j, i: (i, j))
        sem = ('parallel', 'parallel')
        scratch = []
    else:
        grid = (n // bn, m // bm, nk)
        a_spec = pl.BlockSpec((bm, bk), lambda j, i, kk: (i, kk + a_koff * nk))
        w_spec = pl.BlockSpec((bk, bn), lambda j, i, kk: (kk, j + c0))
        row_spec = pl.BlockSpec((1, bn), lambda j, i, kk: (0, j))
        o_spec = pl.BlockSpec((bm, bn), lambda j, i, kk: (i, j))
        sem = ('parallel', 'parallel', 'arbitrary')
        scratch = [pltpu.VMEM((bm, bn), F32)]
    args, specs = [a, w], [a_spec, w_spec]
    for extra in (bias, gate):
        if extra is not None:
            args.append(extra.reshape(1, n).astype(F32))
            specs.append(row_spec)
    if resid is not None:
        args.append(resid)
        specs.append(o_spec)
    body = functools.partial(_mm_body, nk=nk, epi=epi, has_bias=bias is not None,
                             has_gate=gate is not None, has_resid=resid is not None,
                             precise=precise)
    return pl.pallas_call(
        body, out_shape=jax.ShapeDtypeStruct((m, n), out_dtype), grid=grid,
        in_specs=specs, out_specs=o_spec, scratch_shapes=scratch,
        compiler_params=_cparams(*sem), name=name)(*args)


def _normmod_body(x_ref, g_ref, sc_ref, sh_ref, o_ref):
    x = x_ref[...]
    y = x * lax.rsqrt(jnp.mean(x * x, axis=-1, keepdims=True) + EPS) * g_ref[...]
    o_ref[...] = (y * (1.0 + sc_ref[...]) + sh_ref[...]).astype(o_ref.dtype)


def normmod(x, g, sc, sh, out_dtype, bt=512):
    m, d = x.shape
    bt = min(bt, m)
    row = pl.BlockSpec((1, d), lambda i: (0, 0))
    blk = pl.BlockSpec((bt, d), lambda i: (i, 0))
    return pl.pallas_call(
        _normmod_body, out_shape=jax.ShapeDtypeStruct((m, d), out_dtype), grid=(m // bt,),
        in_specs=[blk, row, row, row], out_specs=blk,
        compiler_params=_cparams('parallel'), name='normmod')(
            x, g.reshape(1, d), sc.reshape(1, d), sh.reshape(1, d))


HALO = 16


def _dnproj_body(a_ref, ap_ref, an_ref, w_ref, cw_ref, o_ref, *, mode):
    i = pl.program_id(1)
    nb = pl.num_programs(1)
    bm = a_ref.shape[0]
    lhs = jnp.concatenate([ap_ref[...], a_ref[...], an_ref[...]], axis=0)
    y = jnp.dot(lhs, w_ref[...], preferred_element_type=F32)
    x = y[HALO:HALO + bm]
    row = lax.broadcasted_iota(jnp.int32, x.shape, 0)
    prev_row = jnp.where(i > 0, y[HALO - 1:HALO], 0.0)
    next_row = jnp.where(i < nb - 1, y[HALO + bm:HALO + bm + 1], 0.0)
    xm = jnp.where(row == 0, prev_row, pltpu.roll(x, 1, 0))
    xp = jnp.where(row == bm - 1, next_row, pltpu.roll(x, bm - 1, 0))
    y = _silu(cw_ref[0:1, :] * xm + cw_ref[1:2, :] * x + cw_ref[2:3, :] * xp)
    if mode == 'v':
        o_ref[...] = y.astype(o_ref.dtype)
        return
    scale = DN_DK ** -0.5 if mode == 'q' else 1.0
    for h in range(x.shape[1] // DN_DK):
        sl = slice(h * DN_DK, (h + 1) * DN_DK)
        yh = y[:, sl]
        inv = lax.rsqrt(jnp.sum(yh * yh, axis=-1, keepdims=True) + EPS)
        o_ref[:, sl] = (yh * (inv * scale)).astype(o_ref.dtype)


def dn_proj_conv(a, w, w_col0, conv_w, part, d, mode, bm=1024, bn=1024):
    m, k = a.shape
    bm, bn = _fit(m, bm, HALO), _fit(d, bn)
    assert w_col0 % bn == 0 and bm % HALO == 0
    c0 = (w_col0 + part * d) // bn
    cw0 = part * d // bn
    rh = bm // HALO
    lasth = m // HALO - 1
    return pl.pallas_call(
        functools.partial(_dnproj_body, mode=mode),
        out_shape=jax.ShapeDtypeStruct((m, d), F32), grid=(d // bn, m // bm),
        in_specs=[pl.BlockSpec((bm, k), lambda j, i: (i, 0)),
                  pl.BlockSpec((HALO, k), lambda j, i: (jnp.maximum(i * rh - 1, 0), 0)),
                  pl.BlockSpec((HALO, k), lambda j, i: (jnp.minimum((i + 1) * rh, lasth), 0)),
                  pl.BlockSpec((k, bn), lambda j, i: (0, c0 + j)),
                  pl.BlockSpec((3, bn), lambda j, i: (0, cw0 + j))],
        out_specs=pl.BlockSpec((bm, bn), lambda j, i: (i, j)),
        compiler_params=_cparams('parallel', 'parallel'), name='dn_proj_' + mode)(
            a, a, a, w, conv_w)


def _dnab_body(x_ref, alog_ref, dtb_ref, o_ref):
    x = x_ref[...]
    lane = lax.broadcasted_iota(jnp.int32, x.shape, 1)
    g = -jnp.exp(alog_ref[...]) * _softplus(x + dtb_ref[...])
    o_ref[...] = jnp.where(lane < x.shape[1] // 2, g, _sigmoid(x))


def dn_gates(pab, a_log, dt_bias, bt=1024):
    m, w = pab.shape
    bt = min(bt, m)
    zeros = jnp.zeros((w // 2,), F32)
    alog = jnp.concatenate([a_log.reshape(-1), zeros]).reshape(1, w)
    dtb = jnp.concatenate([dt_bias.reshape(-1), zeros]).reshape(1, w)
    row = pl.BlockSpec((1, w), lambda i: (0, 0))
    blk = pl.BlockSpec((bt, w), lambda i: (i, 0))
    return pl.pallas_call(
        _dnab_body, out_shape=jax.ShapeDtypeStruct((m, w), F32), grid=(m // bt,),
        in_specs=[blk, row, row], out_specs=blk,
        compiler_params=_cparams('parallel'), name='dn_gates')(pab, alog, dtb)


def _cast_jobs(ws, steps, step_index):
    args, in_specs, out_shapes, out_specs = [], [], [], []
    for stack, layer in ws:
        _, rows, cols = stack.shape
        ratio = next(r for r in (1, 2, 4, 8, 16) if steps % r == 0
                     and rows % (steps // r) == 0 and (rows // (steps // r)) % 16 == 0)
        blk = rows // (steps // ratio)
        args.append(stack)
        in_specs.append(pl.BlockSpec(
            (None, blk, cols),
            lambda g, j, ratio=ratio, layer=layer: (layer, step_index(g, j) // ratio, 0)))
        out_shapes.append(jax.ShapeDtypeStruct((rows, cols), BF16))
        out_specs.append(pl.BlockSpec(
            (blk, cols), lambda g, j, ratio=ratio: (step_index(g, j) // ratio, 0)))
    return args, in_specs, out_shapes, out_specs


def _split_refs(rest, n_fused, n_cast):
    a, b, c, d = n_fused, n_fused + n_cast, n_fused + n_cast + 2, n_fused + 2 * n_cast + 2
    return rest[:a], rest[a:b], rest[b:c], rest[c:d], rest[d:]


def _run_casts(cast_in, cast_out):
    for wi, wo in zip(cast_in, cast_out):
        wo[...] = wi[...].astype(BF16)


def _gated_head_norm(o, z, gain):
    y = o * lax.rsqrt(jnp.mean(o * o, axis=-1, keepdims=True) + EPS) * gain
    return y * _silu(z)


def _tri_masks(c, rev):
    ii = lax.broadcasted_iota(jnp.int32, (c, c), 0)
    jj = lax.broadcasted_iota(jnp.int32, (c, c), 1)
    if rev:
        return ii == jj, ii <= jj, ii < jj
    return ii == jj, ii >= jj, ii > jj


def _dn_intra(q, k, v, g_row, beta_row, masks, tri):
    eye, incl, strict = masks
    c = q[0].shape[0]
    heads = range(len(q))
    gc_col, beta_col, g_tot, decay = [], [], [], []
    for h in heads:
        g_bc = jnp.broadcast_to(g_row[h], (c, c))
        gcc = jnp.sum(jnp.where(incl, g_bc, 0.0), axis=1, keepdims=True)
        gcr = jnp.sum(jnp.where(eye, jnp.broadcast_to(gcc, (c, c)), 0.0), axis=0, keepdims=True)
        gc_col.append(gcc)
        beta_col.append(jnp.sum(jnp.where(eye, jnp.broadcast_to(beta_row[h], (c, c)), 0.0),
                                axis=1, keepdims=True))
        g_tot.append(jnp.sum(g_row[h], axis=1, keepdims=True))
        decay.append(jnp.where(incl, jnp.exp(jnp.where(incl, gcc - gcr, 0.0)), 0.0))
    kb = [k[h] * beta_col[h] for h in heads]
    kq = [_dot_nt(jnp.concatenate([kb[h], q[h]], axis=0), k[h]) for h in heads]
    kk = [kq[h][:c] for h in heads]
    qk = [kq[h][c:] for h in heads]
    n = [-jnp.where(strict, kk[h] * decay[h], 0.0) for h in heads]
    egc = [jnp.exp(gc_col[h]) for h in heads]
    rhs = [jnp.concatenate([kb[h] * egc[h], v[h] * beta_col[h]], axis=1) for h in heads]
    sol = _unit_tri_solve_many(n, rhs, tri)
    wq = [jnp.concatenate([sol[h][:, :DN_DK], q[h] * egc[h]], axis=0) for h in heads]
    u = [sol[h][:, DN_DK:] for h in heads]
    attn = [jnp.where(incl, qk[h] * decay[h], 0.0) for h in heads]
    kd = [k[h] * jnp.exp(g_tot[h] - gc_col[h]) for h in heads]
    gl = [jnp.exp(g_tot[h]) for h in heads]
    return wq, u, attn, kd, gl


def _dn_scan_body(q_ref, k_ref, v_ref, g_ref, b_ref, s0_ref, *rest, rev, nc, hg, fused, n_cast):
    fused_refs, cast_in, (o_ref, sf_ref), cast_out, scratch = _split_refs(
        rest, 3 if fused else 0, n_cast)
    if fused:
        of_ref, z_ref, gain_ref = fused_refs
    s_scr, wq_scr, u_scr, at_scr, kd_scr, gl_scr = scratch
    _run_casts(cast_in, cast_out)
    j = pl.program_id(1)

    @pl.when(j == 0)
    def _():
        s_scr[...] = s0_ref[...]

    c = DN_CHUNK
    masks = _tri_masks(c, rev)
    tri = _tri_levels(c, c)
    heads = range(hg)
    lanes = [slice(h * DN_DK, (h + 1) * DN_DK) for h in heads]

    def intra(ci, carry):
        rows = pl.ds(pl.multiple_of(ci * c, c), c)
        wq, u, attn, kd, gl = _dn_intra(
            [q_ref[rows, sl] for sl in lanes], [k_ref[rows, sl] for sl in lanes],
            [v_ref[rows, sl] for sl in lanes], [g_ref[ci, h:h + 1, :] for h in heads],
            [b_ref[ci, h:h + 1, :] for h in heads], masks, tri)
        for h in heads:
            wq_scr[ci, h] = wq[h].astype(BF16)
            u_scr[ci, h] = u[h]
            at_scr[ci, h] = attn[h].astype(BF16)
            kd_scr[ci, h] = kd[h].astype(BF16)
            gl_scr[ci, h] = jnp.broadcast_to(gl[h], (1, DN_DK))
        return carry

    lax.fori_loop(0, nc, intra, 0)

    def inter(ci, carry):
        cc = (nc - 1 - ci) if rev else ci
        rows = pl.ds(pl.multiple_of(cc * c, c), c)
        s = [s_scr[h] for h in heads]
        ws = [_dot(wq_scr[cc, h], s[h]) for h in heads]
        v_new = [u_scr[cc, h] - ws[h][:c] for h in heads]
        av = [_dot(at_scr[cc, h], v_new[h]) for h in heads]
        kv = [_dot_tn(kd_scr[cc, h], v_new[h]) for h in heads]
        for h in heads:
            o = ws[h][c:] + av[h]
            if fused:
                o = _gated_head_norm(o + of_ref[rows, lanes[h]], z_ref[rows, lanes[h]],
                                     gain_ref[:, lanes[h]])
            o_ref[rows, lanes[h]] = o.astype(o_ref.dtype)
            s_scr[h] = s[h] * gl_scr[cc, h] + kv[h]
        return carry

    lax.fori_loop(0, nc, inter, 0)

    @pl.when(j == pl.num_programs(1) - 1)
    def _():
        sf_ref[...] = s_scr[...]


def dn_scan(q, k, v, gbt, s0, direction, fuse=None, casts=(), hg=16, nc=2):
    m, d = q.shape
    h = d // DN_DK
    hg = min(hg, h)
    ng = h // hg
    nc = min(nc, m // DN_CHUNK)
    bt = nc * DN_CHUNK
    nb = m // bt
    rev = direction == 1
    gb4 = gbt.reshape(4 * ng, hg, m // DN_CHUNK, DN_CHUNK).transpose(0, 2, 1, 3)

    def tb(j):
        return (nb - 1 - j) if rev else j

    tok = pl.BlockSpec((bt, hg * DN_DK), lambda g, j: (tb(j), g))
    st = pl.BlockSpec((hg, DN_DK, DN_DK), lambda g, j: (g, 0, 0))
    g_spec = pl.BlockSpec((None, nc, hg, DN_CHUNK),
                          lambda g, j: (direction * ng + g, tb(j), 0, 0))
    b_spec = pl.BlockSpec((None, nc, hg, DN_CHUNK),
                          lambda g, j: ((2 + direction) * ng + g, tb(j), 0, 0))
    args, specs = [q, k, v, gb4, gb4, s0], [tok, tok, tok, g_spec, b_spec, st]
    if fuse is not None:
        o_other, p, z_part, gain = fuse
        args += [o_other, p, gain.reshape(1, d)]
        specs += [tok, pl.BlockSpec((bt, hg * DN_DK), lambda g, j: (tb(j), z_part * ng + g)),
                  pl.BlockSpec((1, hg * DN_DK), lambda g, j: (0, g))]
    c_args, c_in, c_shapes, c_out = _cast_jobs(casts, ng * nb, lambda g, j: g * nb + j)
    res = pl.pallas_call(
        functools.partial(_dn_scan_body, rev=rev, nc=nc, hg=hg, fused=fuse is not None,
                          n_cast=len(casts)),
        out_shape=(jax.ShapeDtypeStruct((m, d), F32 if fuse is None else BF16),
                   jax.ShapeDtypeStruct(s0.shape, F32), *c_shapes),
        grid=(ng, nb), in_specs=specs + c_in, out_specs=(tok, st, *c_out),
        scratch_shapes=[pltpu.VMEM((hg, DN_DK, DN_DK), F32),
                        pltpu.VMEM((nc, hg, 2 * DN_CHUNK, DN_DK), BF16),
                        pltpu.VMEM((nc, hg, DN_CHUNK, DN_DK), F32),
                        pltpu.VMEM((nc, hg, DN_CHUNK, DN_CHUNK), BF16),
                        pltpu.VMEM((nc, hg, DN_CHUNK, DN_DK), BF16),
                        pltpu.VMEM((nc, hg, 1, DN_DK), F32)],
        compiler_params=_cparams('parallel', 'arbitrary'),
        name='dn_scan_bwd' if rev else 'dn_scan_fwd')(*args, *c_args)
    return res[0], res[1], list(res[2:])


def _hg_chunk(q_raw, f_raw, v, lb, st, masks, rev):
    _, incl, _ = masks
    c = q_raw[0].shape[0]
    heads = range(len(q_raw))
    mid = (c - 1 - c // 2) if rev else c // 2
    last = 0 if rev else c - 1
    qm, km, qs, kd, gl = [], [], [], [], []
    for h in heads:
        q = _silu(q_raw[h])
        f = lb[h] + (1.0 - lb[h]) * _sigmoid(f_raw[h])
        k = 1.0 - f
        b = _cumsum_rows(jnp.log(f), rev)
        m = b[mid:mid + 1, :]
        b_last = b[last:last + 1, :]
        qm.append(q * jnp.exp(b - m))
        km.append(k * jnp.exp(m - b))
        qs.append(q * jnp.exp(b))
        kd.append(k * jnp.exp(b_last - b))
        gl.append(jnp.exp(b_last))
    a_qk = [jnp.where(incl, _dot_nt(qm[h], km[h]), 0.0) for h in heads]
    inter = [_dot_nt(qs[h], st[h]) for h in heads]
    kv = [_dot_tn(v[h], kd[h]) for h in heads]
    intra = [_dot(a_qk[h], v[h]) for h in heads]
    o = [inter[h] + intra[h] for h in heads]
    st_new = [st[h] * gl[h] + kv[h] for h in heads]
    return o, st_new


def _hg_scan_body(q_ref, f_ref, v_ref, lb_ref, s0_ref, *rest, rev, nc, hg, fused, n_cast):
    fused_refs, cast_in, (o_ref, sf_ref), cast_out, (s_scr,) = _split_refs(
        rest, 3 if fused else 0, n_cast)
    if fused:
        of_ref, z_ref, gain_ref = fused_refs
    _run_casts(cast_in, cast_out)
    j = pl.program_id(1)

    @pl.when(j == 0)
    def _():
        s_scr[...] = s0_ref[...]

    c = HG_CHUNK
    masks = _tri_masks(c, rev)
    heads = range(hg)
    lanes = [slice(h * HG_DK, (h + 1) * HG_DK) for h in heads]

    def chunk(ci, carry):
        cc = (nc - 1 - ci) if rev else ci
        rows = pl.ds(pl.multiple_of(cc * c, c), c)
        o, s_new = _hg_chunk([q_ref[rows, sl] for sl in lanes], [f_ref[rows, sl] for sl in lanes],
                             [v_ref[rows, sl] for sl in lanes], [lb_ref[:, sl] for sl in lanes],
                             [s_scr[h] for h in heads], masks, rev)
        for h in heads:
            oh = o[h]
            if fused:
                oh = _gated_head_norm(oh + of_ref[rows, lanes[h]], z_ref[rows, lanes[h]],
                                      gain_ref[:, lanes[h]])
            o_ref[rows, lanes[h]] = oh.astype(o_ref.dtype)
            s_scr[h] = s_new[h]
        return carry

    lax.fori_loop(0, nc, chunk, 0)

    @pl.when(j == pl.num_programs(1) - 1)
    def _():
        sf_ref[...] = s_scr[...]


def hg_scan(p, lower, s0, direction, d, fuse=None, casts=(), hg=16, nc=8):
    m = p.shape[0]
    h = d // HG_DK
    hg = min(hg, h)
    ng = h // hg
    nc = min(nc, m // HG_CHUNK)
    bt = nc * HG_CHUNK
    nb = m // bt
    rev = direction == 1

    def tb(j):
        return (nb - 1 - j) if rev else j

    def part(pi):
        return pl.BlockSpec((bt, hg * HG_DK), lambda g, j: (tb(j), pi * ng + g))

    st = pl.BlockSpec((hg, HG_DK, HG_DK), lambda g, j: (g, 0, 0))
    row = pl.BlockSpec((1, hg * HG_DK), lambda g, j: (0, g))
    tok = pl.BlockSpec((bt, hg * HG_DK), lambda g, j: (tb(j), g))
    args = [p, p, p, lower.reshape(1, d), s0]
    specs = [part(0), part(1 + direction), part(3), row, st]
    if fuse is not None:
        o_other, gain = fuse
        args += [o_other, p, gain.reshape(1, d)]
        specs += [tok, part(4), row]
    c_args, c_in, c_shapes, c_out = _cast_jobs(casts, ng * nb, lambda g, j: g * nb + j)
    res = pl.pallas_call(
        functools.partial(_hg_scan_body, rev=rev, nc=nc, hg=hg, fused=fuse is not None,
                          n_cast=len(casts)),
        out_shape=(jax.ShapeDtypeStruct((m, d), F32 if fuse is None else BF16),
                   jax.ShapeDtypeStruct(s0.shape, F32), *c_shapes),
        grid=(ng, nb), in_specs=specs + c_in, out_specs=(tok, st, *c_out),
        scratch_shapes=[pltpu.VMEM((hg, HG_DK, HG_DK), F32)],
        compiler_params=_cparams('parallel', 'arbitrary'),
        name='hg_scan_bwd' if rev else 'hg_scan_fwd')(*args, *c_args)
    return res[0], res[1], list(res[2:])


def _shiftmix_body(x_ref, prev_ref, next_ref, mu_ref, *o_refs, width):
    i = pl.program_id(0)
    nb = pl.num_programs(0)
    bt, d = x_ref.shape
    dq = d // 4
    col = lax.broadcasted_iota(jnp.int32, (bt, dq), 0) % width
    for qi in range(4):
        sl = slice(qi * dq, (qi + 1) * dq)
        x = x_ref[:, sl]
        if qi == 0:
            sh = jnp.where(col == 0, 0.0, pltpu.roll(x, 1, 0))
        elif qi == 1:
            sh = jnp.where(col == width - 1, 0.0, pltpu.roll(x, bt - 1, 0))
        elif qi == 2:
            edge = jnp.where(i > 0, prev_ref[:, sl], 0.0)
            sh = edge if bt == width else jnp.concatenate([edge, x[:bt - width]], axis=0)
        else:
            edge = jnp.where(i < nb - 1, next_ref[:, sl], 0.0)
            sh = edge if bt == width else jnp.concatenate([x[width:], edge], axis=0)
        xx = sh - x
        for n, o_ref in enumerate(o_refs):
            o_ref[:, sl] = (x + xx * mu_ref[n:n + 1, sl]).astype(o_ref.dtype)


def shift_mix(h, mu, width, bt=256):
    m, d = h.shape
    bt = max(min(bt, m), width)
    rw = bt // width
    lastw = m // width - 1
    blk = pl.BlockSpec((bt, d), lambda i: (i, 0))
    out = jax.ShapeDtypeStruct((m, d), BF16)
    return pl.pallas_call(
        functools.partial(_shiftmix_body, width=width),
        out_shape=(out,) * 6, grid=(m // bt,),
        in_specs=[blk,
                  pl.BlockSpec((width, d), lambda i: (jnp.maximum(i * rw - 1, 0), 0)),
                  pl.BlockSpec((width, d), lambda i: (jnp.minimum((i + 1) * rw, lastw), 0)),
                  pl.BlockSpec((6, d), lambda i: (0, 0))],
        out_specs=(blk,) * 6, compiler_params=_cparams('arbitrary'), name='shift_mix')(h, h, h, mu)


def _halves(x, lo_mask):
    return jnp.concatenate([jnp.where(lo_mask, x, 0.0), jnp.where(lo_mask, 0.0, x)], axis=0)


def _head_sum(x, lo_mask):
    s_lo = jnp.sum(jnp.where(lo_mask, x, 0.0), axis=-1, keepdims=True)
    s_hi = jnp.sum(jnp.where(lo_mask, 0.0, x), axis=-1, keepdims=True)
    return jnp.where(lo_mask, s_lo, s_hi)


def _rw_intra(r, k, v, a_sig, logw, k_k, k_a, masks, tri, rev):
    lo, strict2, incl_cat = masks
    c = r[0].shape[0]
    groups = range(len(r))
    mid = (c - 1 - c // 2) if rev else c // 2
    last = 0 if rev else c - 1
    am2, bm2, km2, v2, rm, ae2, rg, bd2, kd2, gl = ([] for _ in range(10))
    for g in groups:
        kq = k[g] * k_k[g]
        kk = kq * lax.rsqrt(_head_sum(kq * kq, lo) + EPS)
        b = kk * a_sig[g]
        kd = k[g] * (1.0 + (a_sig[g] - 1.0) * k_a[g])
        cs = _cumsum_rows(logw[g], rev)
        ce = cs - logw[g]
        m = cs[mid:mid + 1, :]
        c_last = cs[last:last + 1, :]
        e_mc = jnp.exp(m - cs)
        dec = jnp.exp(c_last - cs)
        am2.append(_halves(-kk * jnp.exp(ce - m), lo))
        bm2.append(_halves(b * e_mc, lo))
        km2.append(_halves(kd * e_mc, lo))
        v2.append(_halves(v[g], lo))
        rm.append(r[g] * jnp.exp(cs - m))
        ae2.append(_halves(-kk * jnp.exp(ce), lo))
        rg.append(r[g] * jnp.exp(cs))
        bd2.append(_halves(b * dec, lo))
        kd2.append(_halves(kd * dec, lo))
        gl.append(jnp.exp(c_last))
    pairs = [_dot_nt(jnp.concatenate([am2[g], rm[g]], axis=0),
                     jnp.concatenate([bm2[g], km2[g]], axis=0)) for g in groups]
    n = [jnp.where(strict2, pairs[g][:2 * c, :2 * c], 0.0) for g in groups]
    a_ak = [jnp.where(strict2, pairs[g][:2 * c, 2 * c:], 0.0) for g in groups]
    a_rb = [jnp.where(incl_cat, pairs[g][2 * c:, :2 * c], 0.0) for g in groups]
    a_rk = [jnp.where(incl_cat, pairs[g][2 * c:, 2 * c:], 0.0) for g in groups]
    both = [_dot(jnp.concatenate([a_ak[g], a_rk[g]], axis=0), v2[g]) for g in groups]
    akv = [both[g][:2 * c] for g in groups]
    ov = [both[g][2 * c:] for g in groups]
    kv = [_dot_tn(v2[g], kd2[g]) for g in groups]
    sol = _unit_tri_solve_many(n, [jnp.concatenate([ae2[g], akv[g]], axis=1) for g in groups],
                               tri)
    wr = [jnp.concatenate([sol[g][:, :LANES], rg[g]], axis=0) for g in groups]
    uv2 = [sol[g][:, LANES:] for g in groups]
    return wr, uv2, a_rb, ov, bd2, kv, gl


def _rw_output(o, r, k, v, a_sum, gate, ln_w, ln_b, k_a, r_k, lo):
    inv_n = 1.0 / RW_HEAD
    oc = o - _head_sum(o, lo) * inv_n
    y = oc * lax.rsqrt(_head_sum(oc * oc, lo) * inv_n + RW_LN_EPS) * ln_w + ln_b
    k_sum = k * (2.0 + (a_sum - 2.0) * k_a)
    bonus = _head_sum(r * k_sum * r_k, lo) * v
    return (y + bonus) * gate


def _rw_masks(c, rev):
    lo = lax.broadcasted_iota(jnp.int32, (1, LANES), 1) < RW_HEAD
    ii = lax.broadcasted_iota(jnp.int32, (2 * c, 2 * c), 0)
    jj = lax.broadcasted_iota(jnp.int32, (2 * c, 2 * c), 1)
    same = (ii // c) == (jj // c)
    il, jl = ii % c, jj % c
    strict2 = same & ((il < jl) if rev else (il > jl))
    ic = lax.broadcasted_iota(jnp.int32, (c, 2 * c), 0)
    jc = lax.broadcasted_iota(jnp.int32, (c, 2 * c), 1) % c
    incl_cat = (ic <= jc) if rev else (ic >= jc)
    return lo, strict2, incl_cat


def _rw_scan_body(r_ref, k_ref, v_ref, a_ref, lw_ref, kk_ref, ka_ref, s0_ref, *rest,
                  rev, nc, hg, fused, n_cast):
    fused_refs, cast_in, (o_ref, sf_ref), cast_out, scratch = _split_refs(
        rest, 6 if fused else 0, n_cast)
    if fused:
        of_ref, ao_ref, gate_ref, lnw_ref, lnb_ref, rk_ref = fused_refs
    s_scr, wr_scr, uv_scr, arb_scr, ov_scr, bd_scr, kv_scr, gl_scr = scratch
    _run_casts(cast_in, cast_out)
    j = pl.program_id(1)

    @pl.when(j == 0)
    def _():
        s_scr[...] = s0_ref[...]

    c = RW_CHUNK
    masks = _rw_masks(c, rev)
    tri = _tri_levels(2 * c, c)
    groups = range(hg)
    lanes = [slice(g * LANES, (g + 1) * LANES) for g in groups]

    def intra(ci, carry):
        rows = pl.ds(pl.multiple_of(ci * c, c), c)
        wr, uv2, a_rb, ov, bd2, kv, gl = _rw_intra(
            [r_ref[rows, sl] for sl in lanes], [k_ref[rows, sl] for sl in lanes],
            [v_ref[rows, sl] for sl in lanes], [a_ref[rows, sl] for sl in lanes],
            [lw_ref[rows, sl] for sl in lanes], [kk_ref[:, sl] for sl in lanes],
            [ka_ref[:, sl] for sl in lanes], masks, tri, rev)
        for g in groups:
            wr_scr[ci, g] = wr[g].astype(BF16)
            uv_scr[ci, g] = uv2[g]
            arb_scr[ci, g] = a_rb[g].astype(BF16)
            ov_scr[ci, g] = ov[g]
            bd_scr[ci, g] = bd2[g].astype(BF16)
            kv_scr[ci, g] = kv[g]
            gl_scr[ci, g] = gl[g]
        return carry

    lax.fori_loop(0, nc, intra, 0)

    def inter(ci, carry):
        cc = (nc - 1 - ci) if rev else ci
        rows = pl.ds(pl.multiple_of(cc * c, c), c)
        st = [s_scr[g] for g in groups]
        ws = [_dot_nt(wr_scr[cc, g], st[g]) for g in groups]
        u2 = [ws[g][:2 * c] + uv_scr[cc, g] for g in groups]
        au = [_dot(arb_scr[cc, g], u2[g]) for g in groups]
        ub = [_dot_tn(u2[g], bd_scr[cc, g]) for g in groups]
        for g in groups:
            o = ws[g][2 * c:] + au[g] + ov_scr[cc, g]
            if fused:
                sl = lanes[g]
                o = _rw_output(o + of_ref[rows, sl], r_ref[rows, sl], k_ref[rows, sl],
                               v_ref[rows, sl], a_ref[rows, sl] + ao_ref[rows, sl],
                               gate_ref[rows, sl], lnw_ref[:, sl], lnb_ref[:, sl], ka_ref[:, sl],
                               rk_ref[:, sl], masks[0])
            o_ref[rows, lanes[g]] = o.astype(o_ref.dtype)
            s_scr[g] = st[g] * gl_scr[cc, g] + kv_scr[cc, g] + ub[g]
        return carry

    lax.fori_loop(0, nc, inter, 0)

    @pl.when(j == pl.num_programs(1) - 1)
    def _():
        sf_ref[...] = s_scr[...]


def rw_scan(r, k, v, a_sig, logw, k_k, k_a, s0, direction, fuse=None, casts=(), hg=16, nc=2):
    m, d = r.shape
    npair = d // LANES
    hg = min(hg, npair)
    ng = npair // hg
    nc = min(nc, m // RW_CHUNK)
    bt = nc * RW_CHUNK
    nb = m // bt
    rev = direction == 1

    def tb(j):
        return (nb - 1 - j) if rev else j

    tok = pl.BlockSpec((bt, hg * LANES), lambda g, j: (tb(j), g))
    row = pl.BlockSpec((1, hg * LANES), lambda g, j: (0, g))
    st = pl.BlockSpec((hg, LANES, LANES), lambda g, j: (g, 0, 0))
    args = [r, k, v, a_sig, logw, k_k.reshape(1, d), k_a.reshape(1, d), s0]
    specs = [tok, tok, tok, tok, tok, row, row, st]
    if fuse is not None:
        args += list(fuse[:3]) + [t.reshape(1, d) for t in fuse[3:]]
        specs += [tok, tok, tok, row, row, row]
    c_args, c_in, c_shapes, c_out = _cast_jobs(casts, ng * nb, lambda g, j: g * nb + j)
    res = pl.pallas_call(
        functools.partial(_rw_scan_body, rev=rev, nc=nc, hg=hg, fused=fuse is not None,
                          n_cast=len(casts)),
        out_shape=(jax.ShapeDtypeStruct((m, d), F32 if fuse is None else BF16),
                   jax.ShapeDtypeStruct(s0.shape, F32), *c_shapes),
        grid=(ng, nb), in_specs=specs + c_in, out_specs=(tok, st, *c_out),
        scratch_shapes=[pltpu.VMEM((hg, LANES, LANES), F32),
                        pltpu.VMEM((nc, hg, 3 * RW_CHUNK, LANES), BF16),
                        pltpu.VMEM((nc, hg, 2 * RW_CHUNK, LANES), F32),
                        pltpu.VMEM((nc, hg, RW_CHUNK, 2 * RW_CHUNK), BF16),
                        pltpu.VMEM((nc, hg, RW_CHUNK, LANES), F32),
                        pltpu.VMEM((nc, hg, 2 * RW_CHUNK, LANES), BF16),
                        pltpu.VMEM((nc, hg, LANES, LANES), F32),
                        pltpu.VMEM((nc, hg, 1, LANES), F32)],
        compiler_params=_cparams('parallel', 'arbitrary'),
        name='rw_scan_bwd' if rev else 'rw_scan_fwd')(*args, *c_args)
    return res[0], res[1], list(res[2:])


def _convglu_body(u_ref, prev_ref, next_ref, v_ref, cw_ref, cb_ref, o_ref, *, width):
    i = pl.program_id(0)
    nb = pl.num_programs(0)
    x = u_ref[...].astype(F32)
    bt = x.shape[0]
    up = jnp.where(i > 0, prev_ref[...].astype(F32), 0.0)
    dn = jnp.where(i < nb - 1, next_ref[...].astype(F32), 0.0)
    if bt > width:
        up = jnp.concatenate([up, x[:bt - width]], axis=0)
        dn = jnp.concatenate([x[width:], dn], axis=0)
    col = lax.broadcasted_iota(jnp.int32, x.shape, 0) % width
    rows = (up, x, dn)

    def column(kw):
        return sum(cw_ref[3 * r + kw:3 * r + kw + 1, :] * rows[r] for r in range(3))

    left = jnp.where(col == 0, 0.0, pltpu.roll(column(0), 1, 0))
    right = jnp.where(col == width - 1, 0.0, pltpu.roll(column(2), bt - 1, 0))
    h = column(1) + left + right + cb_ref[...]
    c = math.sqrt(2.0 / math.pi)
    t = jnp.tanh(h * (2.0 * c + (8.0 * 0.044715 * c) * (h * h)))
    o_ref[...] = ((h + h * t) * v_ref[...].astype(F32)).astype(o_ref.dtype)


def conv_glu_gate(uv, conv_w, conv_b, width, f, bt=1024, bc=1024):
    m = uv.shape[0]
    bt = max(min(bt, m), width)
    bc = _fit(f, bc)
    rw = bt // width
    lastw = m // width - 1
    voff = f // bc
    return pl.pallas_call(
        functools.partial(_convglu_body, width=width),
        out_shape=jax.ShapeDtypeStruct((m, f), BF16), grid=(m // bt, f // bc),
        in_specs=[pl.BlockSpec((bt, bc), lambda i, j: (i, j)),
                  pl.BlockSpec((width, bc), lambda i, j: (jnp.maximum(i * rw - 1, 0), j)),
                  pl.BlockSpec((width, bc), lambda i, j: (jnp.minimum((i + 1) * rw, lastw), j)),
                  pl.BlockSpec((bt, bc), lambda i, j: (i, voff + j)),
                  pl.BlockSpec((9, bc), lambda i, j: (0, j)),
                  pl.BlockSpec((1, bc), lambda i, j: (0, j))],
        out_specs=pl.BlockSpec((bt, bc), lambda i, j: (i, j)),
        compiler_params=_cparams('parallel', 'parallel'), name='conv_glu')(
            uv, uv, uv, uv, 0.5 * conv_w.reshape(9, f), 0.5 * conv_b.reshape(1, f))


def _modulation(conds, down, up, bias):
    low = matmul(conds, down, precise=True, bm=8, bn=512, name='ada_down')
    return matmul(low, up, bias=bias, precise=True, bm=8, bn=2048, name='ada_up')


def _dn_mixer(streams, w, d, last, casts):
    heads = d // DN_DK
    w_in = w['w_in']
    if w_in.dtype == BF16:
        w_main = w_in
        ab = dict(w=w_in, w_cols=(4 * d, 4 * heads))
    else:
        w_main = w_in[:, :4 * d].astype(BF16)
        ab = dict(w=w_in[:, 4 * d:].astype(BF16))
    gain = jnp.tile(w['norm_g'], heads)
    prepped = []
    for h in streams:
        q, k, v = (dn_proj_conv(h, w_main, 0, w['conv_w'], part, d, mode)
                   for part, mode in enumerate('qkv'))
        z = matmul(h, w_main, w_cols=(3 * d, d), name='dn_in_z')
        pab = matmul(h, bn=4 * heads, name='dn_in_ab', **ab)
        gb = dn_gates(pab, w['a_log'], w['dt_bias'])
        prepped.append((z, q, k, v, gb.T))
    states = [jnp.zeros((heads, DN_DK, DN_DK), F32)] * 2
    outs, done = [], [[], []]
    for si, (z, q, k, v, gbt) in enumerate(prepped):
        jobs = casts if si == 1 else ((), ())
        o_f, states[0], done[0] = dn_scan(q, k, v, gbt, states[0], 0, casts=jobs[0])
        y, states[1], done[1] = dn_scan(q, k, v, gbt, states[1], 1, fuse=(o_f, z, 0, gain),
                                        casts=jobs[1])
        outs.append(None if si == 0 and last else y)
    return outs, done[0], done[1]


def _hg_mixer(streams, w, d, last, casts):
    heads = d // HG_DK
    states = [jnp.zeros((heads, HG_DK, HG_DK), F32)] * 2
    outs, done = [], [[], []]
    for si, h in enumerate(streams):
        jobs = casts if si == 1 else ((), ())
        p = matmul(h, w['w_in'], name='hg_in')
        o_f, states[0], done[0] = hg_scan(p, w['lower'], states[0], 0, d, casts=jobs[0])
        y, states[1], done[1] = hg_scan(p, w['lower'], states[1], 1, d,
                                        fuse=(o_f, w['norm_g']), casts=jobs[1])
        outs.append(None if si == 0 and last else y)
    return outs, done[0], done[1]


def _rw_mixer(streams, widths, w, d, last, casts):
    w_rkv = w['w_rkv']
    w1 = jnp.concatenate([w['w1'][0], w['w1'][1]], axis=1).astype(BF16)
    a1 = jnp.concatenate([w['a1'][0], w['a1'][1]], axis=1).astype(BF16)
    w2 = w['w2'].astype(BF16)
    a2 = w['a2'].astype(BF16)
    lora = w['w1'].shape[-1]
    gl = w['g1'].shape[-1]
    glp = -(-gl // LANES) * LANES
    g1 = jnp.pad(w['g1'], ((0, 0), (0, glp - gl))).astype(BF16)
    g2 = jnp.pad(w['g2'], ((0, glp - gl), (0, 0))).astype(BF16)
    states = [jnp.zeros((d // LANES, LANES, LANES), F32)] * 2
    outs, done = [], [[], []]
    for si, (h, width) in enumerate(zip(streams, widths)):
        jobs = casts if si == 1 else ((), ())
        xr, xw, xk, xv, xa, xg = shift_mix(h, w['mu'], width)
        r = matmul(xr, w_rkv[0], name='rw_r')
        k = matmul(xk, w_rkv[1], name='rw_k')
        v = matmul(xv, w_rkv[2], name='rw_v')
        tw = matmul(xw, w1, epi='tanh', out_dtype=BF16, bn=2 * lora, name='rw_w1')
        ta = matmul(xa, a1, out_dtype=BF16, bn=2 * lora, name='rw_a1')
        tg = matmul(xg, g1, epi='sigmoid', out_dtype=BF16, name='rw_g1')
        gate = matmul(tg, g2, name='rw_g2')
        logw, a_sig = [], []
        for direction in range(2):
            logw.append(matmul(tw, w2[direction], bias=w['w0'][direction], epi='logw',
                               a_koff=direction, name='rw_w2'))
            a_sig.append(matmul(ta, a2[direction], bias=w['a0'][direction], epi='sigmoid',
                                a_koff=direction, name='rw_a2'))
        o_f, states[0], done[0] = rw_scan(r, k, v, a_sig[0], logw[0], w['k_k'], w['k_a'],
                                          states[0], 0, casts=jobs[0])
        y, states[1], done[1] = rw_scan(r, k, v, a_sig[1], logw[1], w['k_k'], w['k_a'],
                                        states[1], 1, casts=jobs[1],
                                        fuse=(o_f, a_sig[0], gate, w['lnx_w'], w['lnx_b'],
                                              w['r_k'].reshape(-1)))
        outs.append(None if si == 0 and last else y)
    return outs, done[0], done[1]


def kernel(x, c, ctx, c_ctx, ada_down, ada_up, ada_b, norm1_g, norm2_g, ffn_w_up, ffn_conv_w, ffn_conv_b, ffn_w_down, dn_w_in, dn_conv_w, dn_a_log, dn_dt_bias, dn_norm_g, dn_w_out, hg_w_in, hg_lower, hg_norm_g, hg_w_out, rw_mu, rw_w_rkv, rw_w0, rw_w1, rw_w2, rw_a0, rw_a1, rw_a2, rw_g1, rw_g2, rw_k_k, rw_k_a, rw_r_k, rw_lnx_w, rw_lnx_b, rw_w_out, final_g):
    _, seq, d = x.shape
    depth = ada_down.shape[0]
    n_ctx = ctx.shape[1]
    f = ffn_w_down.shape[1]
    xl, xc = x[0], ctx[0]
    sm = jax.nn.softmax(hg_lower.astype(F32), axis=0)
    lower_bounds = jnp.cumsum(sm, axis=0) - sm[0]
    conds = jnp.zeros((8, d), F32).at[0].set(jax.nn.silu(c[0])).at[1].set(jax.nn.silu(c_ctx))
    widths = (n_ctx, GRID_W)
    w_outs = {0: dn_w_out, 1: hg_w_out, 2: rw_w_out}
    w_ins = {0: dn_w_in, 1: hg_w_in, 2: rw_w_rkv.reshape(rw_w_rkv.shape[0], 3 * d, d)}

    def in_proj(i):
        return w_ins[i % 3], i // 3

    first, _ = in_proj(0)
    w_in_bf = first[0] if first.shape[2] % LANES else first[0].astype(BF16)
    for i in range(depth):
        kind, j = i % 3, i // 3
        last = i == depth - 1
        mod = _modulation(conds, ada_down[i], ada_up[i], ada_b[i])
        mods = [[mod[row, n * d:(n + 1) * d] for n in range(6)] for row in (1, 0)]
        xs = [xc, xl]
        h_dtype = F32 if kind == 2 else BF16
        hs = [normmod(t, norm1_g[i], mm[1], mm[0], h_dtype) for t, mm in zip(xs, mods)]
        nxt = in_proj(i + 1) if not last else None
        hide_next = nxt is not None and nxt[0].shape[2] % LANES == 0
        casts = ([(w_outs[kind], j), (ffn_w_up, i)],
                 [(ffn_w_down, i)] + ([nxt] if hide_next else []))
        if kind == 0:
            w = dict(w_in=w_in_bf, conv_w=dn_conv_w[j], a_log=dn_a_log[j],
                     dt_bias=dn_dt_bias[j], norm_g=dn_norm_g[j])
            ys, done_f, done_b = _dn_mixer(hs, w, d, last, casts)
        elif kind == 1:
            w = dict(w_in=w_in_bf, lower=lower_bounds[i], norm_g=hg_norm_g[j])
            ys, done_f, done_b = _hg_mixer(hs, w, d, last, casts)
        else:
            w = dict(mu=rw_mu[j], w_rkv=w_in_bf.reshape(3, d, d), w0=rw_w0[j], w1=rw_w1[j],
                     w2=rw_w2[j], a0=rw_a0[j], a1=rw_a1[j], a2=rw_a2[j], g1=rw_g1[j],
                     g2=rw_g2[j], k_k=rw_k_k[j], k_a=rw_k_a[j], r_k=rw_r_k[j],
                     lnx_w=rw_lnx_w[j], lnx_b=rw_lnx_b[j])
            ys, done_f, done_b = _rw_mixer(hs, widths, w, d, last, casts)
        w_out, w_up = done_f
        w_down = done_b[0]
        if hide_next:
            w_in_bf = done_b[1]
        elif nxt is not None:
            w_in_bf = nxt[0][nxt[1]]
        new = []
        for t, y, mm, width in zip(xs, ys, mods, widths):
            if y is None:
                new.append(t)
                continue
            t = matmul(y, w_out, gate=mm[2], resid=t, name='mix_out')
            h2 = normmod(t, norm2_g[i], mm[4], mm[3], BF16)
            uv = matmul(h2, w_up, out_dtype=BF16, name='ffn_up')
            gated = conv_glu_gate(uv, ffn_conv_w[i], ffn_conv_b[i], width, f)
            t = matmul(gated, w_down, gate=mm[5], resid=t, bm=512, bn=1024, w_single=True,
                       name='ffn_down')
            new.append(t)
        xc, xl = new
    zeros = jnp.zeros((d,), F32)
    return normmod(xl, final_g, zeros, zeros, F32)[None]
```
